```python
import functools
import jax, jax.numpy as jnp
from jax import lax
import numpy as np

D_MODEL = 1024
BATCH = 8
SEQ = 2048
DEPTH = 1
DEC_BATCH = 128
DEC_SEQ = 4
PAST_LEN = 2048
PAGE_SIZE = 128

MIX_WIDTH = D_MODEL
HG_HEADS = 4
HG_DIM = (MIX_WIDTH // 2) // HG_HEADS
HG_WIDTH = HG_HEADS * HG_DIM
HG_CHUNK = 32
MB_HEADS = 8
MB_DIM = (MIX_WIDTH - HG_WIDTH) // MB_HEADS
MB_WIDTH = MB_HEADS * MB_DIM
MB_BLOCK = 256
MB_TOPK = 3
MB_QBLK = 64
IN_WIDTHS = (HG_WIDTH, HG_WIDTH, HG_WIDTH, HG_WIDTH, MB_WIDTH, MB_WIDTH, MB_WIDTH)
IN_PROJ = sum(IN_WIDTHS)
MEM_LEN = 256
XA_HEADS = 4
XA_DIM = D_MODEL // XA_HEADS
N_GROUPS = 4
EXP_PER_GROUP = 8
N_EXPERTS = N_GROUPS * EXP_PER_GROUP
EXP_TOPK = 2
D_EXPERT = D_MODEL
MOE_BLK = 128
RMS_EPS = 1e-6

kernel_name = 'hymba_hgrn2_moba_hmoe_step'


def _rmsnorm(x, g):
    xf = x.astype(jnp.float32)
    y = xf * lax.rsqrt(jnp.mean(xf * xf, axis=-1, keepdims=True) + RMS_EPS)
    return (y * g.astype(jnp.float32)).astype(x.dtype)


def _alibi_slopes(n):
    return np.power(2.0, -8.0 * np.arange(1, n + 1) / n).astype(np.float32)


def _hgrn2_chunked(q, k, logf, v, s0):
    B, T, H, K = q.shape
    c = min(HG_CHUNK, T)
    n = -(-T // c)
    pad = n * c - T

    def prep(a):
        a = jnp.pad(a.astype(jnp.float32), ((0, 0), (0, pad), (0, 0), (0, 0)))
        return jnp.moveaxis(a.reshape(B, n, c, H, a.shape[-1]), 1, 0)

    tri = jnp.tril(jnp.ones((c, c), bool))[None, :, :, None, None]

    def step(S, blk):
        qc, kc, lc, vc = blk
        b = jnp.cumsum(lc, axis=1)
        decay = jnp.exp(jnp.where(tri, b[:, :, None] - b[:, None, :], -jnp.inf))
        att = jnp.einsum('bthk,bshk,btshk->bhts', qc, kc, decay)
        o = (jnp.einsum('bhts,bshv->bthv', att, vc)
             + jnp.einsum('bthk,bhkv->bthv', qc * jnp.exp(b), S))
        bl = b[:, -1]
        S = (jnp.exp(bl)[..., None] * S
             + jnp.einsum('bshk,bshv->bhkv', kc * jnp.exp(bl[:, None] - b), vc))
        return S, o

    S, o = lax.scan(step, s0.astype(jnp.float32), (prep(q), prep(k), prep(logf), prep(v)))
    o = jnp.moveaxis(o, 0, 1).reshape(B, n * c, H, v.shape[-1])[:, :T]
    return o, S


def _moba_seq(q, k, v, p0):
    T = q.shape[0]
    L = k.shape[0]
    nb = -(-L // MB_BLOCK)
    padk = nb * MB_BLOCK - L
    kb = jnp.pad(k, ((0, padk), (0, 0), (0, 0))).astype(jnp.float32).reshape(nb, MB_BLOCK, MB_HEADS, MB_DIM)
    vb = jnp.pad(v, ((0, padk), (0, 0), (0, 0))).astype(jnp.float32).reshape(nb, MB_BLOCK, MB_HEADS, MB_DIM)
    kb_h = jnp.moveaxis(kb, 2, 0)
    vb_h = jnp.moveaxis(vb, 2, 0)
    kmean = jnp.mean(kb, axis=1)
    kk = max(1, min(MB_TOPK, nb - 1))
    qb = min(MB_QBLK, T)
    nq = -(-T // qb)
    padq = nq * qb - T
    qp = jnp.pad(q.astype(jnp.float32), ((0, padq), (0, 0), (0, 0))).reshape(nq, qb, MB_HEADS, MB_DIM)
    pos = (p0 + jnp.minimum(jnp.arange(nq * qb), T - 1)).reshape(nq, qb)
    slopes = jnp.asarray(_alibi_slopes(MB_HEADS))
    heads = jnp.arange(MB_HEADS)[:, None, None]
    offs = jnp.arange(MB_BLOCK)
    blk_ids = jnp.arange(nb)
    scale = MB_DIM ** -0.5

    def qstep(args):
        qq, pp = args
        blk = pp // MB_BLOCK
        gate = jnp.einsum('qhd,nhd->hqn', qq, kmean)
        gate = jnp.where(blk_ids[None, None, :] < blk[None, :, None], gate, -jnp.inf)
        _, sel = lax.top_k(gate, kk)
        valid = sel < blk[None, :, None]
        ksel = kb_h[heads, sel]
        vsel = vb_h[heads, sel]
        dist_sel = (pp[None, :, None, None] - (sel[..., None] * MB_BLOCK + offs)).astype(jnp.float32)
        s_sel = jnp.einsum('qhd,hqjsd->hqjs', qq, ksel) * scale - slopes[:, None, None, None] * dist_sel
        s_sel = jnp.where(valid[..., None], s_sel, -jnp.inf).reshape(MB_HEADS, qb, kk * MB_BLOCK)
        kown = kb[blk]
        vown = vb[blk]
        kpos = blk[:, None] * MB_BLOCK + offs
        dist_own = (pp[:, None] - kpos).astype(jnp.float32)
        s_own = jnp.einsum('qhd,qshd->hqs', qq, kown) * scale - slopes[:, None, None] * dist_own[None]
        s_own = jnp.where((kpos <= pp[:, None])[None], s_own, -jnp.inf)
        p = jax.nn.softmax(jnp.concatenate([s_sel, s_own], axis=-1), axis=-1)
        o = (jnp.einsum('hqs,hqsd->qhd', p[..., :kk * MB_BLOCK],
                        vsel.reshape(MB_HEADS, qb, kk * MB_BLOCK, MB_DIM))
             + jnp.einsum('hqs,qshd->qhd', p[..., kk * MB_BLOCK:], vown))
        return o

    o = lax.map(qstep, (qp, pos))
    return o.reshape(nq * qb, MB_HEADS, MB_DIM)[:T].astype(q.dtype)


def _moba_prompt(q, k, v):
    return lax.map(lambda a: _moba_seq(a[0], a[1], a[2], 0), (q, k, v))


def _moba_sample(q, k, v, pool_k, pool_v, page_table):
    def one(a):
        qb, kn, vn, pt = a
        kp = pool_k[pt].reshape(-1, MB_HEADS, MB_DIM)
        vp = pool_v[pt].reshape(-1, MB_HEADS, MB_DIM)
        kf = jnp.concatenate([kp, kn.astype(kp.dtype)], axis=0)
        vf = jnp.concatenate([vp, vn.astype(vp.dtype)], axis=0)
        return _moba_seq(qb, kf, vf, kp.shape[0])
    return lax.map(one, (q, k, v, page_table))


def _mem_kv(mem, g_mem, w_xk, w_xv):
    B, M, _ = mem.shape
    m = _rmsnorm(mem, g_mem)
    return ((m @ w_xk).reshape(B, M, XA_HEADS, XA_DIM), (m @ w_xv).reshape(B, M, XA_HEADS, XA_DIM))


def _cross_attn(h, mk, mv, w_xq, w_xo):
    B, T, _ = h.shape
    q = (h @ w_xq).reshape(B, T, XA_HEADS, XA_DIM)
    s = jnp.einsum('bthd,bmhd->bhtm', q.astype(jnp.float32), mk.astype(jnp.float32)) * XA_DIM ** -0.5
    p = jax.nn.softmax(s, axis=-1)
    o = jnp.einsum('bhtm,bmhd->bthd', p, mv.astype(jnp.float32)).astype(h.dtype)
    return o.reshape(B, T, D_MODEL) @ w_xo


def _grouped_experts(x, eid, gates, w1, w3, w2):
    n, d = x.shape
    a = n * EXP_TOPK
    flat_e = eid.reshape(-1)
    flat_t = jnp.arange(a) // EXP_TOPK
    flat_g = gates.reshape(-1).astype(jnp.float32)
    order = jnp.argsort(flat_e)
    se = flat_e[order]
    counts = jnp.bincount(flat_e, length=N_EXPERTS)
    padc = ((counts + MOE_BLK - 1) // MOE_BLK) * MOE_BLK
    start = jnp.cumsum(counts) - counts
    pend = jnp.cumsum(padc)
    pstart = pend - padc
    dest = pstart[se] + jnp.arange(a) - start[se]
    rows = -(-(a + N_EXPERTS * (MOE_BLK - 1)) // MOE_BLK) * MOE_BLK
    nblk = rows // MOE_BLK
    row_tok = jnp.full((rows,), n, jnp.int32).at[dest].set(flat_t[order].astype(jnp.int32))
    row_g = jnp.zeros((rows,), jnp.float32).at[dest].set(flat_g[order])
    blk_e = jnp.minimum(jnp.searchsorted(pend, jnp.arange(nblk) * MOE_BLK, side='right'), N_EXPERTS - 1)
    xs = jnp.concatenate([x, jnp.zeros((1, d), x.dtype)], axis=0)[row_tok].reshape(nblk, MOE_BLK, d)

    def expert(args):
        xb, e = args
        hmid = jax.nn.silu(xb @ w1[e]) * (xb @ w3[e])
        return hmid @ w2[e]

    outs = lax.map(expert, (xs, blk_e)).reshape(rows, d).astype(jnp.float32)
    y = jnp.zeros((n + 1, d), jnp.float32).at[row_tok].add(outs * row_g[:, None])[:n]
    return y.astype(x.dtype)


def _hmoe(h, w_grp, b_grp, w_exp, b_exp, w1, w3, w2):
    B, T, D = h.shape
    x = h.reshape(B * T, D)
    n = x.shape[0]
    glog = (x @ w_grp).astype(jnp.float32) + b_grp.astype(jnp.float32)
    gw, gsel = lax.top_k(jax.nn.softmax(glog, axis=-1), 1)
    gw, gsel = gw[:, 0], gsel[:, 0]
    elog = ((x @ w_exp).astype(jnp.float32) + b_exp.astype(jnp.float32)).reshape(n, N_GROUPS, EXP_PER_GROUP)
    eprob = jax.nn.softmax(elog[jnp.arange(n), gsel], axis=-1)
    ew, esel = lax.top_k(eprob, EXP_TOPK)
    gates = gw[:, None] * ew / jnp.sum(ew, axis=-1, keepdims=True)
    eid = gsel[:, None] * EXP_PER_GROUP + esel
    return _grouped_experts(x, eid, gates, w1, w3, w2).reshape(B, T, D)


def _layer(x, s0, moba_fn, mem_k, mem_v, g_mix, w_in, lb, hg_norm, w_out, g_cross, w_xq, w_xo,
           g_ffn, w_grp, b_grp, w_exp, b_exp, w1, w3, w2):
    B, T, _ = x.shape
    h = _rmsnorm(x, g_mix)
    proj = h @ w_in
    offs = np.cumsum((0,) + IN_WIDTHS)
    hq, hf, hi, hg, mq, mk, mv = [proj[..., int(offs[j]):int(offs[j + 1])] for j in range(len(IN_WIDTHS))]
    q = jax.nn.silu(hq).reshape(B, T, HG_HEADS, HG_DIM)
    f = lb + (1.0 - lb) * jax.nn.sigmoid(hf.astype(jnp.float32))
    logf = jnp.log(f).reshape(B, T, HG_HEADS, HG_DIM)
    kh = (1.0 - f).reshape(B, T, HG_HEADS, HG_DIM)
    o, s_new = _hgrn2_chunked(q, kh, logf, hi.reshape(B, T, HG_HEADS, HG_DIM), s0)
    o = _rmsnorm(o, hg_norm) * jax.nn.silu(hg.reshape(B, T, HG_HEADS, HG_DIM).astype(jnp.float32))
    mk4 = mk.reshape(B, T, MB_HEADS, MB_DIM)
    mv4 = mv.reshape(B, T, MB_HEADS, MB_DIM)
    att = moba_fn(mq.reshape(B, T, MB_HEADS, MB_DIM), mk4, mv4)
    mix = jnp.concatenate([o.reshape(B, T, HG_WIDTH).astype(x.dtype),
                           att.reshape(B, T, MB_WIDTH).astype(x.dtype)], axis=-1) @ w_out
    x = x + mix
    x = x + _cross_attn(_rmsnorm(x, g_cross), mem_k, mem_v, w_xq, w_xo)
    x = x + _hmoe(_rmsnorm(x, g_ffn), w_grp, b_grp, w_exp, b_exp, w1, w3, w2)
    return x, s_new, mk4, mv4


def setup_inputs(seed: int = 0) -> dict:
    key = jax.random.key(seed)
    ks = jax.random.split(key, 32)
    f32 = jnp.float32
    n_pages = PAST_LEN // PAGE_SIZE
    used = DEC_BATCH * n_pages
    n_phys = used + max(1, used // 4)

    def nrm(i, shape, scale=1.0):
        return jax.random.normal(ks[i], shape, f32) * scale

    def gain(i, shape):
        return 1.0 + 0.02 * jax.random.normal(ks[i], shape, f32)

    page_table = jax.random.permutation(ks[7], n_phys)[:used].reshape(DEC_BATCH, n_pages).astype(jnp.int32)
    return {
        'x_prompt': nrm(0, (BATCH, SEQ, D_MODEL)),
        'x_sample': nrm(1, (DEC_BATCH, DEC_SEQ, D_MODEL)),
        'state_hgrn': nrm(2, (DEPTH, DEC_BATCH, HG_HEADS, HG_DIM, HG_DIM), 0.5),
        'cache_moba_k': nrm(3, (DEPTH, n_phys, PAGE_SIZE, MB_HEADS, MB_DIM)),
        'cache_moba_v': nrm(4, (DEPTH, n_phys, PAGE_SIZE, MB_HEADS, MB_DIM)),
        'cache_mem_k': nrm(5, (DEPTH, DEC_BATCH, MEM_LEN, XA_HEADS, XA_DIM)),
        'cache_mem_v': nrm(6, (DEPTH, DEC_BATCH, MEM_LEN, XA_HEADS, XA_DIM)),
        'page_table': page_table,
        'mem_prompt': nrm(8, (BATCH, MEM_LEN, D_MODEL)),
        'g_mix': gain(9, (DEPTH, D_MODEL)),
        'w_in': nrm(10, (DEPTH, D_MODEL, IN_PROJ), D_MODEL ** -0.5),
        'hg_lb': nrm(11, (DEPTH + 1, HG_WIDTH), 0.5),
        'hg_norm': gain(12, (DEPTH, HG_DIM)),
        'w_out': nrm(13, (DEPTH, MIX_WIDTH, D_MODEL), MIX_WIDTH ** -0.5),
        'g_cross': gain(14, (DEPTH, D_MODEL)),
        'g_mem': gain(15, (DEPTH, D_MODEL)),
        'w_xq': nrm(16, (DEPTH, D_MODEL, D_MODEL), D_MODEL ** -0.5),
        'w_xk': nrm(17, (DEPTH, D_MODEL, D_MODEL), D_MODEL ** -0.5),
        'w_xv': nrm(18, (DEPTH, D_MODEL, D_MODEL), D_MODEL ** -0.5),
        'w_xo': nrm(19, (DEPTH, D_MODEL, D_MODEL), D_MODEL ** -0.5),
        'g_ffn': gain(20, (DEPTH, D_MODEL)),
        'w_grp': nrm(21, (DEPTH, D_MODEL, N_GROUPS), D_MODEL ** -0.5),
        'b_grp': nrm(22, (DEPTH, N_GROUPS), 0.01),
        'w_exp': nrm(23, (DEPTH, D_MODEL, N_EXPERTS), D_MODEL ** -0.5),
        'b_exp': nrm(24, (DEPTH, N_EXPERTS), 0.01),
        'w1': nrm(25, (DEPTH, N_EXPERTS, D_MODEL, D_EXPERT), D_MODEL ** -0.5),
        'w3': nrm(26, (DEPTH, N_EXPERTS, D_MODEL, D_EXPERT), D_MODEL ** -0.5),
        'w2': nrm(27, (DEPTH, N_EXPERTS, D_EXPERT, D_MODEL), D_EXPERT ** -0.5),
        'g_final': gain(28, (D_MODEL,)),
    }


def reference(x_prompt, x_sample, state_hgrn, cache_moba_k, cache_moba_v, cache_mem_k, cache_mem_v,
              page_table, mem_prompt, g_mix, w_in, hg_lb, hg_norm, w_out, g_cross, g_mem, w_xq, w_xk,
              w_xv, w_xo, g_ffn, w_grp, b_grp, w_exp, b_exp, w1, w3, w2, g_final):
    lbs = jnp.cumsum(jax.nn.softmax(hg_lb.astype(jnp.float32), axis=0), axis=0)
    xp, xs = x_prompt, x_sample
    sp_l, kp_l, vp_l, mk_l, mv_l, ss_l, ks_l, vs_l = [], [], [], [], [], [], [], []
    for l in range(DEPTH):
        lw = (g_mix[l], w_in[l], lbs[l], hg_norm[l], w_out[l], g_cross[l], w_xq[l], w_xo[l], g_ffn[l],
              w_grp[l], b_grp[l], w_exp[l], b_exp[l], w1[l], w3[l], w2[l])
        mkp, mvp = _mem_kv(mem_prompt, g_mem[l], w_xk[l], w_xv[l])
        s0 = jnp.zeros((xp.shape[0], HG_HEADS, HG_DIM, HG_DIM), jnp.float32)
        xp, sp, kp, vp = _layer(xp, s0, _moba_prompt, mkp, mvp, *lw)
        moba_s = functools.partial(_moba_sample, pool_k=cache_moba_k[l], pool_v=cache_moba_v[l],
                                   page_table=page_table)
        xs, ss, ks_, vs_ = _layer(xs, state_hgrn[l], moba_s, cache_mem_k[l], cache_mem_v[l], *lw)
        sp_l.append(sp); kp_l.append(kp); vp_l.append(vp); mk_l.append(mkp); mv_l.append(mvp)
        ss_l.append(ss); ks_l.append(ks_); vs_l.append(vs_)
    y_prompt = _rmsnorm(xp, g_final)
    y_sample = _rmsnorm(xs, g_final)
    return (y_prompt, y_sample,
            jnp.stack(sp_l).astype(x_prompt.dtype), jnp.stack(kp_l), jnp.stack(vp_l),
            jnp.stack(mk_l), jnp.stack(mv_l),
            jnp.stack(ss_l).astype(state_hgrn.dtype), jnp.stack(ks_l), jnp.stack(vs_l))
```

```python
import functools

import numpy as np
import jax
import jax.numpy as jnp
from jax import lax
from jax.experimental import pallas as pl
from jax.experimental.pallas import tpu as pltpu

F32 = jnp.float32
BF16 = jnp.bfloat16
I32 = jnp.int32

D_MODEL = 1024
HG_HEADS = 4
HG_DIM = 128
HG_WIDTH = HG_HEADS * HG_DIM
MB_HEADS = 8
MB_DIM = 64
MB_WIDTH = MB_HEADS * MB_DIM
MB_BLOCK = 256
MB_TOPK = 3
XA_HEADS = 4
XA_DIM = 256
N_GROUPS = 4
EXP_PER_GROUP = 8
N_EXPERTS = N_GROUPS * EXP_PER_GROUP
EXP_TOPK = 2
RMS_EPS = 1e-6
NEG_INF = float("-inf")

LANES = 128
SUBLANES = 8
MIB = 1024 * 1024

ROW_TILE = 256
MOE_TILE = 256
SAMPLE_PAD_T = 8

_NT = (((1,), (1,)), ((), ()))


def _cparams(semantics, vmem_mib):
    return pltpu.CompilerParams(dimension_semantics=semantics, vmem_limit_bytes=vmem_mib * MIB)


def _rms(x, g):
    return x * lax.rsqrt(jnp.mean(x * x, axis=-1, keepdims=True) + RMS_EPS) * g


def _silu(x):
    return x * jax.nn.sigmoid(x)


def _rms_proj_kernel(x_ref, g_ref, w_ref, *o_refs, segs):
    h = _rms(x_ref[...], g_ref[...]).astype(BF16)
    k = 0
    for lo, hi, outs in segs:
        r = jnp.dot(h, w_ref[:, lo:hi], preferred_element_type=F32)
        for scale, transposed in outs:
            y = r if scale == 1.0 else r * scale
            if transposed:
                y = y.T
            o_refs[k][...] = y.astype(o_refs[k].dtype)
            k += 1


def _rms_proj(x, g, w_bf, segs, dtypes, tm=ROW_TILE, vmem_mib=48):
    rows, d = x.shape
    assert rows % tm == 0
    out_shape, out_specs = [], []
    k = 0
    for lo, hi, outs in segs:
        n = hi - lo
        for _, transposed in outs:
            if transposed:
                out_shape.append(jax.ShapeDtypeStruct((n, rows), dtypes[k]))
                out_specs.append(pl.BlockSpec((n, tm), lambda i: (0, i)))
            else:
                out_shape.append(jax.ShapeDtypeStruct((rows, n), dtypes[k]))
                out_specs.append(pl.BlockSpec((tm, n), lambda i: (i, 0)))
            k += 1
    return pl.pallas_call(
        functools.partial(_rms_proj_kernel, segs=segs),
        grid=(rows // tm,),
        in_specs=[pl.BlockSpec((tm, d), lambda i: (i, 0)),
                  pl.BlockSpec((1, d), lambda i: (0, 0)),
                  pl.BlockSpec(w_bf.shape, lambda i: (0, 0))],
        out_specs=out_specs,
        out_shape=out_shape,
        compiler_params=_cparams(("parallel",), vmem_mib),
        name="rms_proj",
    )(x, g.reshape(1, d), w_bf)


def _cumsum_rows(x):
    n = x.shape[0]
    row = lax.broadcasted_iota(I32, x.shape, 0)
    s = 1
    while s < n:
        x = x + jnp.where(row >= s, pltpu.roll(x, s, 0), 0.0)
        s *= 2
    return x


def _hgrn_intra(q, kk, b, c, w):
    row = lax.broadcasted_iota(I32, (c, HG_DIM), 0)
    acc = None
    half = c // 2
    while half >= SUBLANES:
        two = 2 * half
        nblk = c // two
        pieces = [jnp.broadcast_to(b[i * two + half - 1:i * two + half, :], (two, HG_DIM)) for i in range(nblk)]
        bm = pieces[0] if nblk == 1 else jnp.concatenate(pieces, axis=0)
        second = (row & (two - 1)) >= half
        qt = jnp.where(second, q * jnp.exp(jnp.where(second, b - bm, 0.0)), 0.0)
        kt = jnp.where(second, 0.0, kk * jnp.exp(jnp.where(second, 0.0, bm - b)))
        al = lax.dot_general(qt.astype(BF16), kt.astype(BF16), _NT, preferred_element_type=F32)
        if nblk > 1:
            shift = two.bit_length() - 1
            rt = lax.broadcasted_iota(I32, (c, c), 0) >> shift
            cs = lax.broadcasted_iota(I32, (c, c), 1) >> shift
            al = jnp.where(rt == cs, al, 0.0)
        acc = al if acc is None else acc + al
        half //= 2
    r8 = lax.broadcasted_iota(I32, (SUBLANES, HG_DIM), 0)
    lane = lax.broadcasted_iota(I32, (SUBLANES, w), 1)
    blocks = []
    for g in range(c // SUBLANES):
        lo = g * SUBLANES
        qg, kg, bg = q[lo:lo + SUBLANES], kk[lo:lo + SUBLANES], b[lo:lo + SUBLANES]
        ag = jnp.zeros((SUBLANES, w), F32)
        for s in range(SUBLANES):
            e = jnp.exp(jnp.where(r8 >= s, bg - bg[s:s + 1, :], NEG_INF))
            p = jnp.sum(qg * kg[s:s + 1, :] * e, axis=1, keepdims=True)
            ag = jnp.where(lane == lo + s, p, ag)
        blocks.append(ag)
    diag = blocks[0] if len(blocks) == 1 else jnp.concatenate(blocks, axis=0)
    return diag if acc is None else acc + diag


def _hgrn_kernel(*refs, c, t_valid, has_s0):
    if has_s0:
        hp_ref, lb_ref, gn_ref, s0_ref, o_ref, sout_ref, st_ref = refs
    else:
        hp_ref, lb_ref, gn_ref, o_ref, sout_ref, st_ref = refs
    ci = pl.program_id(1)
    last = pl.num_programs(1) - 1
    w = max(c, LANES)

    @pl.when(ci == 0)
    def _():
        for h in range(HG_HEADS):
            st_ref[h] = s0_ref[0, h].T if has_s0 else jnp.zeros((HG_DIM, HG_DIM), F32)

    row = lax.broadcasted_iota(I32, (c, HG_DIM), 0)
    for h in range(HG_HEADS):
        lo = h * HG_DIM
        hq = hp_ref[:, lo:lo + HG_DIM]
        hf = hp_ref[:, HG_WIDTH + lo:HG_WIDTH + lo + HG_DIM]
        v = hp_ref[:, 2 * HG_WIDTH + lo:2 * HG_WIDTH + lo + HG_DIM]
        hg = hp_ref[:, 3 * HG_WIDTH + lo:3 * HG_WIDTH + lo + HG_DIM]
        lb = lb_ref[:, lo:lo + HG_DIM]
        q = _silu(hq)
        f = lb + (1.0 - lb) * jax.nn.sigmoid(hf)
        logf = jnp.log(f)
        kk = 1.0 - f
        if t_valid < c:
            valid = row < t_valid
            logf = jnp.where(valid, logf, 0.0)
            kk = jnp.where(valid, kk, 0.0)
            v = jnp.where(valid, v, 0.0)
        b = _cumsum_rows(logf)
        a = _hgrn_intra(q, kk, b, c, w)
        st = st_ref[h]
        bl = b[c - 1:c, :]
        k2 = kk * jnp.exp(bl - b)
        if c < w:
            zpad = jnp.zeros((w - c, HG_DIM), F32)
            vp = jnp.concatenate([v, zpad], axis=0)
            k2 = jnp.concatenate([k2, zpad], axis=0)
        else:
            vp = v
        vb = vp.astype(BF16)
        o = lax.dot_general((q * jnp.exp(b)).astype(BF16), st.astype(BF16), _NT, preferred_element_type=F32)
        o = o + jnp.dot(a.astype(BF16), vb, preferred_element_type=F32)
        o_ref[:, lo:lo + HG_DIM] = (_rms(o, gn_ref[...]) * _silu(hg)).astype(o_ref.dtype)
        st_new = st * jnp.exp(bl) + jnp.dot(vp.T.astype(BF16), k2.astype(BF16), preferred_element_type=F32)
        st_ref[h] = st_new

        @pl.when(ci == last)
        def _():
            sout_ref[0, h] = st_new.T


def _hgrn(hp, lb, gn, s0, batch, t_pad, t_valid, c):
    rows = hp.shape[0]
    nc = t_pad // c
    has_s0 = s0 is not None
    in_specs = [pl.BlockSpec((c, 4 * HG_WIDTH), lambda b, i: (b * nc + i, 0)),
                pl.BlockSpec((1, HG_WIDTH), lambda b, i: (0, 0)),
                pl.BlockSpec((1, HG_DIM), lambda b, i: (0, 0))]
    args = [hp, lb.reshape(1, HG_WIDTH), gn.reshape(1, HG_DIM)]
    if has_s0:
        in_specs.append(pl.BlockSpec((1, HG_HEADS, HG_DIM, HG_DIM), lambda b, i: (b, 0, 0, 0)))
        args.append(s0)
    return pl.pallas_call(
        functools.partial(_hgrn_kernel, c=c, t_valid=min(t_valid, c), has_s0=has_s0),
        grid=(batch, nc),
        in_specs=in_specs,
        out_specs=[pl.BlockSpec((c, HG_WIDTH), lambda b, i: (b * nc + i, 0)),
                   pl.BlockSpec((1, HG_HEADS, HG_DIM, HG_DIM), lambda b, i: (b, 0, 0, 0))],
        out_shape=[jax.ShapeDtypeStruct((rows, HG_WIDTH), BF16),
                   jax.ShapeDtypeStruct((batch, HG_HEADS, HG_DIM, HG_DIM), F32)],
        scratch_shapes=[pltpu.VMEM((HG_HEADS, HG_DIM, HG_DIM), F32)],
        compiler_params=_cparams(("parallel", "arbitrary"), 32),
        name="hgrn2",
    )(*args)


def _topk_block_mask(gate, n_valid, nrow):
    nb = gate.shape[0]
    cnt = jnp.zeros(gate.shape, I32)
    for n2 in range(nb):
        g2 = gate[n2:n2 + 1, :]
        beats = (g2 > gate) | ((g2 == gate) & (n2 < nrow))
        cnt = cnt + jnp.where(beats, (n_valid > n2).astype(I32), 0)
    return (cnt < MB_TOPK) & (nrow < n_valid)


def _moba_prompt_kernel(slope_ref, qt_ref, k_ref, vt_ref, o_ref, kb_ref, vb_ref, km_ref, sel_ref, *, nb):
    hp = pl.program_id(1)
    qi = pl.program_id(2)
    tq = MB_BLOCK

    @pl.when(qi == 0)
    def _():
        kb_ref[...] = k_ref[...].astype(BF16)
        means = [jnp.sum(k_ref[j * MB_BLOCK:(j + 1) * MB_BLOCK, :], axis=0, keepdims=True) * (1.0 / MB_BLOCK)
                 for j in range(nb)]
        km_ref[...] = jnp.concatenate(means, axis=0)
        for j in range(nb):
            vb_ref[j] = vt_ref[:, j * MB_BLOCK:(j + 1) * MB_BLOCK]

    qt2 = qt_ref[...]
    second_head = lax.broadcasted_iota(I32, (2 * MB_DIM, tq), 0) >= MB_DIM
    nrow = lax.broadcasted_iota(I32, (nb, tq), 0)
    lane = lax.broadcasted_iota(I32, (MB_BLOCK, tq), 1)
    rowk = lax.broadcasted_iota(I32, (MB_BLOCK, tq), 0)
    dlr = lane - rowk
    kmb = km_ref[...].astype(BF16)
    outs = []
    for i in range(2):
        slope = slope_ref[hp * 2 + i]
        keep = second_head if i == 1 else jnp.logical_not(second_head)
        qp = jnp.where(keep, qt2, jnp.zeros_like(qt2))
        gate = jnp.dot(kmb, qp, preferred_element_type=F32)
        sel = _topk_block_mask(gate, qi, nrow).astype(F32)
        for n in range(nb):
            sel_ref[n] = jnp.broadcast_to(sel[n:n + 1, :], (SUBLANES, tq))

        k_own = kb_ref[pl.ds(pl.multiple_of(qi * MB_BLOCK, MB_BLOCK), MB_BLOCK), :]
        s = jnp.dot(k_own, qp, preferred_element_type=F32) - slope * dlr.astype(F32)
        s = jnp.where(rowk <= lane, s, NEG_INF)
        m = jnp.max(s, axis=0, keepdims=True)
        p = jnp.exp(s - m)
        l = jnp.sum(p, axis=0, keepdims=True)
        acc = jnp.dot(vb_ref[qi], p.astype(BF16), preferred_element_type=F32)

        def body(j, carry, qp=qp, slope=slope):
            m, l, acc = carry
            kj = kb_ref[pl.ds(pl.multiple_of(j * MB_BLOCK, MB_BLOCK), MB_BLOCK), :]
            dist = (dlr + (qi - j) * MB_BLOCK).astype(F32)
            s = jnp.dot(kj, qp, preferred_element_type=F32) - slope * dist
            s = jnp.where(sel_ref[j][0:1, :] > 0.5, s, NEG_INF)
            m_new = jnp.maximum(m, jnp.max(s, axis=0, keepdims=True))
            alpha = jnp.exp(m - m_new)
            p = jnp.exp(s - m_new)
            l = alpha * l + jnp.sum(p, axis=0, keepdims=True)
            acc = alpha * acc + jnp.dot(vb_ref[j], p.astype(BF16), preferred_element_type=F32)
            return m_new, l, acc

        m, l, acc = lax.fori_loop(0, qi, body, (m, l, acc))
        outs.append(acc / l)
    o_ref[...] = jnp.where(second_head, outs[1], outs[0])


def _moba_prompt(qt, mk, vt, slopes, batch, t):
    nb = t // MB_BLOCK
    rows = batch * t
    return pl.pallas_call(
        functools.partial(_moba_prompt_kernel, nb=nb),
        grid=(batch, MB_HEADS // 2, nb),
        in_specs=[pl.BlockSpec(memory_space=pltpu.SMEM),
                  pl.BlockSpec((2 * MB_DIM, MB_BLOCK), lambda b, h, i: (h, b * nb + i)),
                  pl.BlockSpec((t, 2 * MB_DIM), lambda b, h, i: (b, h)),
                  pl.BlockSpec((2 * MB_DIM, t), lambda b, h, i: (h, b))],
        out_specs=pl.BlockSpec((2 * MB_DIM, MB_BLOCK), lambda b, h, i: (h, b * nb + i)),
        out_shape=jax.ShapeDtypeStruct((MB_WIDTH, rows), F32),
        scratch_shapes=[pltpu.VMEM((t, 2 * MB_DIM), BF16),
                        pltpu.VMEM((nb, 2 * MB_DIM, MB_BLOCK), BF16),
                        pltpu.VMEM((nb, 2 * MB_DIM), F32),
                        pltpu.VMEM((nb, SUBLANES, MB_BLOCK), F32)],
        compiler_params=_cparams(("parallel", "parallel", "arbitrary"), 32),
        name="moba_prompt",
    )(slopes, qt, mk, vt)


def _moba_sample_kernel(pt_ref, slope_ref, q_ref, kn_ref, vn_ref, *rest, npages, page, t_new, past_len):
    del pt_ref
    k_pages = rest[:npages]
    v_pages = rest[npages:2 * npages]
    o_ref = rest[2 * npages]
    s_ref = rest[2 * npages + 1]
    nrows = t_new * MB_HEADS
    per_page = page * MB_HEADS
    per_new = SAMPLE_PAD_T * MB_HEADS
    pages_per_block = MB_BLOCK // page
    nb = npages // pages_per_block

    qm = (q_ref[0, :t_new] * (MB_DIM ** -0.5)).reshape(nrows, MB_DIM).astype(BF16)
    slope = slope_ref[...][:, 0:1]
    row_q = lax.broadcasted_iota(I32, (nrows, 1), 0) >> 3
    row_h = lax.broadcasted_iota(I32, (nrows, 1), 0) & (MB_HEADS - 1)

    ksum = [jnp.sum(k_pages[p][0], axis=0) for p in range(npages)]
    kmean = []
    for n in range(nb):
        tot = ksum[n * pages_per_block]
        for p in range(1, pages_per_block):
            tot = tot + ksum[n * pages_per_block + p]
        kmean.append(tot * (1.0 / MB_BLOCK))
    kmean = jnp.concatenate(kmean, axis=0).astype(BF16)
    g = lax.dot_general(qm, kmean, _NT, preferred_element_type=F32)
    glane = lax.broadcasted_iota(I32, g.shape, 1)
    g = jnp.where((glane & (MB_HEADS - 1)) == row_h, g, 0.0)
    gates = [jnp.sum(jnp.where((glane >> 3) == n, g, 0.0), axis=1, keepdims=True) for n in range(nb)]
    sels = []
    for n in range(nb):
        cnt = jnp.zeros((nrows, 1), I32)
        for n2 in range(nb):
            beats = (gates[n2] > gates[n]) | ((gates[n2] == gates[n]) & (n2 < n))
            cnt = cnt + jnp.where(beats, 1, 0)
        sels.append(jnp.where(cnt < MB_TOPK, 1.0, 0.0))

    lane = lax.broadcasted_iota(I32, (nrows, per_page), 1)
    lane_h_ok = (lane & (MB_HEADS - 1)) == row_h
    lane_tok = lane >> 3
    pos_q = past_len + row_q
    m = jnp.full((nrows, 1), NEG_INF, F32)
    for p in range(npages):
        k2 = k_pages[p][0].reshape(per_page, MB_DIM).astype(BF16)
        st = lax.dot_general(qm, k2, _NT, preferred_element_type=F32)
        dist = (pos_q - (p * page + lane_tok)).astype(F32)
        sel_p = jnp.broadcast_to(sels[p // pages_per_block], (nrows, per_page)) > 0.5
        s = jnp.where(lane_h_ok & sel_p, st - slope * dist, NEG_INF)
        s_ref[:, p * per_page:(p + 1) * per_page] = s
        m = jnp.maximum(m, jnp.max(s, axis=1, keepdims=True))
    lane_n = lax.broadcasted_iota(I32, (nrows, per_new), 1)
    kn2 = kn_ref[0].reshape(per_new, MB_DIM).astype(BF16)
    st = lax.dot_general(qm, kn2, _NT, preferred_element_type=F32)
    tok_n = lane_n >> 3
    ok_n = ((lane_n & (MB_HEADS - 1)) == row_h) & (tok_n <= row_q)
    s_new = jnp.where(ok_n, st - slope * (row_q - tok_n).astype(F32), NEG_INF)
    m = jnp.maximum(m, jnp.max(s_new, axis=1, keepdims=True))

    p_new = jnp.exp(s_new - m)
    l = jnp.sum(p_new, axis=1, keepdims=True)
    vn2 = vn_ref[0].reshape(per_new, MB_DIM).astype(BF16)
    acc = jnp.dot(p_new.astype(BF16), vn2, preferred_element_type=F32)
    for p in range(npages):
        pr = jnp.exp(s_ref[:, p * per_page:(p + 1) * per_page] - m)
        l = l + jnp.sum(pr, axis=1, keepdims=True)
        v2 = v_pages[p][0].reshape(per_page, MB_DIM).astype(BF16)
        acc = acc + jnp.dot(pr.astype(BF16), v2, preferred_element_type=F32)
    o_ref[0] = acc / l


def _moba_sample(q4, kn4, vn4, pool_k, pool_v, page_table, t_new):
    batch = q4.shape[0]
    npages = page_table.shape[1]
    page = pool_k.shape[1]
    past_len = npages * page
    assert past_len % MB_BLOCK == 0 and MB_BLOCK % page == 0 and t_new <= SAMPLE_PAD_T
    nrows = t_new * MB_HEADS
    slopes = np.power(2.0, -8.0 * np.arange(1, MB_HEADS + 1) / MB_HEADS).astype(np.float32)
    slope_rows = jnp.asarray(np.tile(np.tile(slopes, t_new)[:, None], (1, LANES)))
    new_spec = pl.BlockSpec((1, SAMPLE_PAD_T, MB_HEADS, MB_DIM), lambda b, pt: (b, 0, 0, 0))
    page_specs = [pl.BlockSpec((1, page, MB_HEADS, MB_DIM),
                               functools.partial(lambda b, pt, i: (pt[b * npages + i], 0, 0, 0), i=i))
                  for i in range(npages)]
    grid_spec = pltpu.PrefetchScalarGridSpec(
        num_scalar_prefetch=1,
        grid=(batch,),
        in_specs=[pl.BlockSpec((nrows, LANES), lambda b, pt: (0, 0)), new_spec, new_spec, new_spec]
        + page_specs + page_specs,
        out_specs=pl.BlockSpec((1, nrows, MB_DIM), lambda b, pt: (b, 0, 0)),
        scratch_shapes=[pltpu.VMEM((nrows, npages * page * MB_HEADS), F32)],
    )
    return pl.pallas_call(
        functools.partial(_moba_sample_kernel, npages=npages, page=page, t_new=t_new, past_len=past_len),
        grid_spec=grid_spec,
        out_shape=jax.ShapeDtypeStruct((batch, nrows, MB_DIM), F32),
        compiler_params=_cparams(("parallel",), 56),
        name="moba_sample",
    )(page_table.reshape(-1), slope_rows, q4, kn4, vn4, *([pool_k] * npages), *([pool_v] * npages))


def _outproj_kernel(x_ref, o_ref, att_ref, wo_ref, g_ref, wq_ref, x1_ref, q_ref, *, att_transposed):
    att = att_ref[...]
    if att_transposed:
        att = att.T
    mix = jnp.dot(o_ref[...], wo_ref[:HG_WIDTH, :], preferred_element_type=F32)
    mix = mix + jnp.dot(att.astype(BF16), wo_ref[HG_WIDTH:, :], preferred_element_type=F32)
    x1 = x_ref[...] + mix
    x1_ref[...] = x1
    h = _rms(x1, g_ref[...]).astype(BF16)
    q_ref[...] = (jnp.dot(h, wq_ref[...], preferred_element_type=F32) * (XA_DIM ** -0.5)).astype(q_ref.dtype)


def _outproj(x, o, att, wo_bf, g, wq_bf, att_transposed, q_dtype, tm=ROW_TILE):
    rows, d = x.shape
    att_spec = (pl.BlockSpec((MB_WIDTH, tm), lambda i: (0, i)) if att_transposed
                else pl.BlockSpec((tm, MB_WIDTH), lambda i: (i, 0)))
    return pl.pallas_call(
        functools.partial(_outproj_kernel, att_transposed=att_transposed),
        grid=(rows // tm,),
        in_specs=[pl.BlockSpec((tm, d), lambda i: (i, 0)),
                  pl.BlockSpec((tm, HG_WIDTH), lambda i: (i, 0)),
                  att_spec,
                  pl.BlockSpec((d, d), lambda i: (0, 0)),
                  pl.BlockSpec((1, d), lambda i: (0, 0)),
                  pl.BlockSpec((d, d), lambda i: (0, 0))],
        out_specs=[pl.BlockSpec((tm, d), lambda i: (i, 0)), pl.BlockSpec((tm, d), lambda i: (i, 0))],
        out_shape=[jax.ShapeDtypeStruct((rows, d), F32), jax.ShapeDtypeStruct((rows, d), q_dtype)],
        compiler_params=_cparams(("parallel",), 40),
        name="outproj_q",
    )(x, o, att, wo_bf, g.reshape(1, d), wq_bf)


def _xattn_kernel(q_ref, k_ref, v_ref, o_ref):
    q = q_ref[...].astype(BF16)
    for h in range(XA_HEADS):
        lo = h * XA_DIM
        kh = k_ref[:, lo:lo + XA_DIM].astype(BF16)
        vh = v_ref[:, lo:lo + XA_DIM].astype(BF16)
        s = lax.dot_general(q[:, lo:lo + XA_DIM], kh, _NT, preferred_element_type=F32)
        p = jnp.exp(s - jnp.max(s, axis=-1, keepdims=True))
        p = p / jnp.sum(p, axis=-1, keepdims=True)
        o_ref[:, lo:lo + XA_DIM] = jnp.dot(p.astype(BF16), vh, preferred_element_type=F32).astype(o_ref.dtype)


def _xattn(q, mem_k, mem_v, batch, t, tq):
    rows, d = q.shape
    m = mem_k.shape[0] // batch
    nq = t // tq
    return pl.pallas_call(
        _xattn_kernel,
        grid=(batch, nq),
        in_specs=[pl.BlockSpec((tq, d), lambda b, i: (b * nq + i, 0)),
                  pl.BlockSpec((m, d), lambda b, i: (b, 0)),
                  pl.BlockSpec((m, d), lambda b, i: (b, 0))],
        out_specs=pl.BlockSpec((tq, d), lambda b, i: (b * nq + i, 0)),
        out_shape=jax.ShapeDtypeStruct((rows, d), q.dtype),
        compiler_params=_cparams(("parallel", "arbitrary"), 32),
        name="xattn",
    )(q, mem_k, mem_v)


def _xo_router_kernel(x1_ref, o_ref, wo_ref, g_ref, wr_ref, br_ref, x2_ref, h_ref, r_ref):
    x2 = x1_ref[...] + jnp.dot(o_ref[...].astype(BF16), wo_ref[...], preferred_element_type=F32)
    x2_ref[...] = x2
    h = _rms(x2, g_ref[...])
    h_ref[...] = h
    logits = jnp.dot(h.astype(BF16), wr_ref[...], preferred_element_type=F32) + br_ref[...]
    lane = lax.broadcasted_iota(I32, logits.shape, 1)
    big = jnp.int32(LANES)

    def top1(mask):
        mx = jnp.max(jnp.where(mask, logits, NEG_INF), axis=-1, keepdims=True)
        idx = jnp.min(jnp.where(mask & (logits == mx), lane, big), axis=-1, keepdims=True)
        return mx, idx

    gmask = lane < N_GROUPS
    gmx, gsel = top1(gmask)
    gw = 1.0 / jnp.sum(jnp.where(gmask, jnp.exp(logits - gmx), 0.0), axis=-1, keepdims=True)
    elo = N_GROUPS + gsel * EXP_PER_GROUP
    emask = (lane >= elo) & (lane < elo + EXP_PER_GROUP)
    m1, i1 = top1(emask)
    m2, i2 = top1(emask & (lane != i1))
    e2 = jnp.exp(m2 - m1)
    g1 = gw / (1.0 + e2)
    g2 = gw * e2 / (1.0 + e2)
    out = jnp.where(lane == 0, (i1 - N_GROUPS).astype(F32), 0.0)
    out = jnp.where(lane == 1, (i2 - N_GROUPS).astype(F32), out)
    out = jnp.where(lane == 2, g1, out)
    out = jnp.where(lane == 3, g2, out)
    r_ref[...] = out


def _xo_router(x1, o, wxo_bf, g, wr_bf, br, tm=ROW_TILE):
    rows, d = x1.shape
    return pl.pallas_call(
        _xo_router_kernel,
        grid=(rows // tm,),
        in_specs=[pl.BlockSpec((tm, d), lambda i: (i, 0)),
                  pl.BlockSpec((tm, d), lambda i: (i, 0)),
                  pl.BlockSpec((d, d), lambda i: (0, 0)),
                  pl.BlockSpec((1, d), lambda i: (0, 0)),
                  pl.BlockSpec((d, LANES), lambda i: (0, 0)),
                  pl.BlockSpec((1, LANES), lambda i: (0, 0))],
        out_specs=[pl.BlockSpec((tm, d), lambda i: (i, 0)),
                   pl.BlockSpec((tm, d), lambda i: (i, 0)),
                   pl.BlockSpec((tm, LANES), lambda i: (i, 0))],
        out_shape=[jax.ShapeDtypeStruct((rows, d), F32),
                   jax.ShapeDtypeStruct((rows, d), F32),
                   jax.ShapeDtypeStruct((rows, LANES), F32)],
        compiler_params=_cparams(("parallel",), 40),
        name="xo_router",
    )(x1, o, wxo_bf, g.reshape(1, d), wr_bf, br)


def _row_copy(src_hbm, src_row, dst_buf, slot, r, sem):
    return pltpu.make_async_copy(src_hbm.at[pl.ds(src_row, 1)], dst_buf.at[slot, pl.ds(r, 1)], sem.at[slot])


def _moe_kernel(blk_e_ref, row_tok_ref, nused_ref, h_hbm, w1_ref, w3_ref, w2_ref, o_ref, xbuf, sem, *, tm):
    del blk_e_ref
    i = pl.program_id(0)
    nused = nused_ref[0]

    def start_block(blk, slot):
        def body(r, carry):
            _row_copy(h_hbm, row_tok_ref[blk * tm + r], xbuf, slot, r, sem).start()
            return carry
        lax.fori_loop(0, tm, body, 0)

    def wait_block(slot):
        def body(r, carry):
            _row_copy(h_hbm, 0, xbuf, slot, r, sem).wait()
            return carry
        lax.fori_loop(0, tm, body, 0)

    @pl.when(i == 0)
    def _():
        start_block(0, 0)

    @pl.when(i + 1 < nused)
    def _():
        start_block(i + 1, (i + 1) % 2)

    @pl.when(i < nused)
    def _():
        slot = i % 2
        wait_block(slot)
        x = xbuf[slot].astype(BF16)
        a = jnp.dot(x, w1_ref[0], preferred_element_type=F32)
        b = jnp.dot(x, w3_ref[0], preferred_element_type=F32)
        hmid = (_silu(a) * b).astype(BF16)
        o_ref[...] = jnp.dot(hmid, w2_ref[0], preferred_element_type=F32)

    @pl.when(i >= nused)
    def _():
        o_ref[...] = jnp.zeros(o_ref.shape, o_ref.dtype)


def _moe_experts(h, blk_e, row_tok, nused, w1_bf, w3_bf, w2_bf, tm=MOE_TILE):
    nblk = blk_e.shape[0]
    d = h.shape[1]
    wspec = pl.BlockSpec((1, d, d), lambda i, be, rt, nu: (be[i], 0, 0))
    grid_spec = pltpu.PrefetchScalarGridSpec(
        num_scalar_prefetch=3,
        grid=(nblk,),
        in_specs=[pl.BlockSpec(memory_space=pl.ANY), wspec, wspec, wspec],
        out_specs=pl.BlockSpec((tm, d), lambda i, be, rt, nu: (i, 0)),
        scratch_shapes=[pltpu.VMEM((2, tm, d), F32), pltpu.SemaphoreType.DMA((2,))],
    )
    return pl.pallas_call(
        functools.partial(_moe_kernel, tm=tm),
        grid_spec=grid_spec,
        out_shape=jax.ShapeDtypeStruct((nblk * tm, d), F32),
        compiler_params=_cparams(("arbitrary",), 48),
        name="moe_experts",
    )(blk_e, row_tok, nused, h, w1_bf, w3_bf, w2_bf)


def _combine_kernel(dest_ref, x2_ref, r_ref, g_ref, outs_hbm, y_ref, buf0, buf1, sem0, sem1, *, tm):
    i = pl.program_id(0)
    n = pl.num_programs(0)

    def start_tile(tile, slot):
        def body(r, carry):
            base = 2 * (tile * tm + r)
            _row_copy(outs_hbm, dest_ref[base], buf0, slot, r, sem0).start()
            _row_copy(outs_hbm, dest_ref[base + 1], buf1, slot, r, sem1).start()
            return carry
        lax.fori_loop(0, tm, body, 0)

    def wait_tile(slot):
        def body(r, carry):
            _row_copy(outs_hbm, 0, buf0, slot, r, sem0).wait()
            _row_copy(outs_hbm, 0, buf1, slot, r, sem1).wait()
            return carry
        lax.fori_loop(0, tm, body, 0)

    @pl.when(i == 0)
    def _():
        start_tile(0, 0)

    @pl.when(i + 1 < n)
    def _():
        start_tile(i + 1, (i + 1) % 2)

    slot = i % 2
    wait_tile(slot)
    route = r_ref[...]
    y = buf0[slot] * route[:, 2:3] + buf1[slot] * route[:, 3:4]
    y_ref[...] = _rms(x2_ref[...] + y, g_ref[...])


def _combine(x2, route, g_final, outs, dest, tm=ROW_TILE):
    rows, d = x2.shape
    grid_spec = pltpu.PrefetchScalarGridSpec(
        num_scalar_prefetch=1,
        grid=(rows // tm,),
        in_specs=[pl.BlockSpec((tm, d), lambda i, de: (i, 0)),
                  pl.BlockSpec((tm, LANES), lambda i, de: (i, 0)),
                  pl.BlockSpec((1, d), lambda i, de: (0, 0)),
                  pl.BlockSpec(memory_space=pl.ANY)],
        out_specs=pl.BlockSpec((tm, d), lambda i, de: (i, 0)),
        scratch_shapes=[pltpu.VMEM((2, tm, d), F32), pltpu.VMEM((2, tm, d), F32),
                        pltpu.SemaphoreType.DMA((2,)), pltpu.SemaphoreType.DMA((2,))],
    )
    return pl.pallas_call(
        functools.partial(_combine_kernel, tm=tm),
        grid_spec=grid_spec,
        out_shape=jax.ShapeDtypeStruct((rows, d), F32),
        compiler_params=_cparams(("arbitrary",), 32),
        name="moe_combine",
    )(dest, x2, route, g_final.reshape(1, d), outs)


def _route_tables(eid, tm):
    n = eid.shape[0]
    a = n * EXP_TOPK
    flat_e = eid.reshape(-1)
    onehot = (flat_e[:, None] == jnp.arange(N_EXPERTS, dtype=I32)[None, :]).astype(I32)
    csum = jnp.cumsum(onehot, axis=0)
    pos = jnp.take_along_axis(csum, flat_e[:, None], axis=1)[:, 0] - 1
    counts = csum[-1]
    padc = ((counts + tm - 1) // tm) * tm
    pend = jnp.cumsum(padc)
    pstart = pend - padc
    dest = (pstart[flat_e] + pos).astype(I32)
    nblk = -(-(a + N_EXPERTS * (tm - 1)) // tm)
    row_tok = jnp.zeros((nblk * tm,), I32).at[dest].set(jnp.arange(a, dtype=I32) // EXP_TOPK)
    blk_e = jnp.minimum(jnp.searchsorted(pend, jnp.arange(nblk, dtype=I32) * tm, side="right"),
                        N_EXPERTS - 1).astype(I32)
    nused = (pend[-1] // tm).astype(I32).reshape(1)
    return dest, row_tok, blk_e, nused


def _layer_tail(x, o, att, att_transposed, mem_k, mem_v, batch, t, xq_tile, wts):
    q_dtype = BF16 if xq_tile % (2 * SUBLANES) == 0 else F32
    x1, q = _outproj(x, o, att, wts["w_out"], wts["g_cross"], wts["w_xq"], att_transposed, q_dtype)
    xo = _xattn(q, mem_k, mem_v, batch, t, xq_tile)
    x2, h, route = _xo_router(x1, xo, wts["w_xo"], wts["g_ffn"], wts["w_router"], wts["b_router"])
    return x2, h, route


def _moe_and_final(x2, h, route, valid_rows, wts):
    d = x2.shape[1]
    eid = route[:, 0:2].astype(I32)
    if valid_rows is not None:
        batch, t_pad, t_valid = valid_rows
        h_tok = h.reshape(batch, t_pad, d)[:, :t_valid].reshape(batch * t_valid, d)
        eid_tok = eid.reshape(batch, t_pad, EXP_TOPK)[:, :t_valid].reshape(batch * t_valid, EXP_TOPK)
    else:
        h_tok, eid_tok = h, eid
    dest, row_tok, blk_e, nused = _route_tables(eid_tok, MOE_TILE)
    outs = _moe_experts(h_tok, blk_e, row_tok, nused, wts["w1"], wts["w3"], wts["w2"])
    if valid_rows is not None:
        dest = jnp.pad(dest.reshape(batch, t_valid, EXP_TOPK), ((0, 0), (0, t_pad - t_valid), (0, 0))).reshape(-1)
    return _combine(x2, route, wts["g_final"], outs, dest)


def kernel(x_prompt, x_sample, state_hgrn, cache_moba_k, cache_moba_v, cache_mem_k, cache_mem_v, page_table,
           mem_prompt, g_mix, w_in, hg_lb, hg_norm, w_out, g_cross, g_mem, w_xq, w_xk, w_xv, w_xo, g_ffn,
           w_grp, b_grp, w_exp, b_exp, w1, w3, w2, g_final):
    depth = g_mix.shape[0]
    assert depth == 1
    bp, tp, d = x_prompt.shape
    bs, ts, _ = x_sample.shape
    mem_len = mem_prompt.shape[1]
    l = 0

    lb = jnp.cumsum(jax.nn.softmax(hg_lb.astype(F32), axis=0), axis=0)[l]
    slopes = jnp.asarray(np.power(2.0, -8.0 * np.arange(1, MB_HEADS + 1) / MB_HEADS).astype(np.float32))
    n_router = N_GROUPS + N_EXPERTS
    w_router = jnp.pad(jnp.concatenate([w_grp[l], w_exp[l]], axis=1), ((0, 0), (0, LANES - n_router)))
    b_router = jnp.pad(jnp.concatenate([b_grp[l], b_exp[l]]), (0, LANES - n_router)).reshape(1, LANES)
    wts = {
        "w_out": w_out[l].astype(BF16), "g_cross": g_cross[l], "w_xq": w_xq[l].astype(BF16),
        "w_xo": w_xo[l].astype(BF16), "g_ffn": g_ffn[l], "w_router": w_router.astype(BF16),
        "b_router": b_router.astype(F32), "w1": w1[l].astype(BF16), "w3": w3[l].astype(BF16),
        "w2": w2[l].astype(BF16), "g_final": g_final,
    }
    w_in_bf = w_in[l].astype(BF16)
    o_hq, o_mq, o_mk, o_mv = 0, 4 * HG_WIDTH, 4 * HG_WIDTH + MB_WIDTH, 4 * HG_WIDTH + 2 * MB_WIDTH
    o_end = o_mv + MB_WIDTH

    xp = x_prompt.reshape(bp * tp, d)
    w_kv = jnp.concatenate([w_xk[l], w_xv[l]], axis=1).astype(BF16)
    memk_p, memv_p = _rms_proj(mem_prompt.reshape(bp * mem_len, d), g_mem[l], w_kv,
                               [(0, d, [(1.0, False)]), (d, 2 * d, [(1.0, False)])], [F32, F32])
    hp_p, qt_p, mk_p, mv_p, vt_p = _rms_proj(
        xp, g_mix[l], w_in_bf,
        [(o_hq, o_mq, [(1.0, False)]), (o_mq, o_mk, [(MB_DIM ** -0.5, True)]), (o_mk, o_mv, [(1.0, False)]),
         (o_mv, o_end, [(1.0, False), (1.0, True)])],
        [F32, BF16, F32, F32, BF16])
    o_p, s_p = _hgrn(hp_p, lb, hg_norm[l], None, bp, tp, tp, 128)
    att_p = _moba_prompt(qt_p, mk_p, vt_p, slopes, bp, tp)
    x2_p, h_p, route_p = _layer_tail(xp, o_p, att_p, True, memk_p, memv_p, bp, tp, 512, wts)
    y_p = _moe_and_final(x2_p, h_p, route_p, None, wts)

    tpad = SAMPLE_PAD_T
    xs = jnp.pad(x_sample, ((0, 0), (0, tpad - ts), (0, 0))).reshape(bs * tpad, d)
    hp_s, mq_s, mk_s, mv_s = _rms_proj(
        xs, g_mix[l], w_in_bf,
        [(o_hq, o_mq, [(1.0, False)]), (o_mq, o_mk, [(1.0, False)]), (o_mk, o_mv, [(1.0, False)]),
         (o_mv, o_end, [(1.0, False)])],
        [F32, F32, F32, F32])
    o_s, s_s = _hgrn(hp_s, lb, hg_norm[l], state_hgrn[l], bs, tpad, ts, tpad)
    mq4 = mq_s.reshape(bs, tpad, MB_HEADS, MB_DIM)
    mk4 = mk_s.reshape(bs, tpad, MB_HEADS, MB_DIM)
    mv4 = mv_s.reshape(bs, tpad, MB_HEADS, MB_DIM)
    pool_k = cache_moba_k.reshape(cache_moba_k.shape[1:]) if depth == 1 else cache_moba_k[l]
    pool_v = cache_moba_v.reshape(cache_moba_v.shape[1:]) if depth == 1 else cache_moba_v[l]
    att_c = _moba_sample(mq4, mk4, mv4, pool_k, pool_v, page_table, ts)
    att_s = jnp.pad(att_c.reshape(bs, ts, MB_WIDTH), ((0, 0), (0, tpad - ts), (0, 0))).reshape(bs * tpad, MB_WIDTH)
    memk_s = cache_mem_k.reshape(bs * mem_len, d)
    memv_s = cache_mem_v.reshape(bs * mem_len, d)
    x2_s, h_s, route_s = _layer_tail(xs, o_s, att_s, False, memk_s, memv_s, bs, tpad, tpad, wts)
    y_s = _moe_and_final(x2_s, h_s, route_s, (bs, tpad, ts), wts)

    return (y_p.reshape(bp, tp, d),
            y_s.reshape(bs, tpad, d)[:, :ts],
            s_p.reshape(1, bp, HG_HEADS, HG_DIM, HG_DIM),
            mk_p.reshape(1, bp, tp, MB_HEADS, MB_DIM),
            mv_p.reshape(1, bp, tp, MB_HEADS, MB_DIM),
            memk_p.reshape(1, bp, mem_len, XA_HEADS, XA_DIM),
            memv_p.reshape(1, bp, mem_len, XA_HEADS, XA_DIM),
            s_s.reshape(1, bs, HG_HEADS, HG_DIM, HG_DIM),
            mk4[:, :ts].reshape(1, bs, ts, MB_HEADS, MB_DIM),
            mv4[:, :ts].reshape(1, bs, ts, MB_HEADS, MB_DIM))
```

```python
import functools

import numpy as np
import jax
import jax.numpy as jnp
from jax import lax
from jax.experimental import pallas as pl
from jax.experimental.pallas import tpu as pltpu

F32 = jnp.float32
BF16 = jnp.bfloat16
I32 = jnp.int32

D_MODEL = 1024
HG_HEADS = 4
HG_DIM = 128
HG_WIDTH = HG_HEADS * HG_DIM
MB_HEADS = 8
MB_DIM = 64
MB_WIDTH = MB_HEADS * MB_DIM
MB_BLOCK = 256
MB_TOPK = 3
XA_HEADS = 4
XA_DIM = 256
N_GROUPS = 4
EXP_PER_GROUP = 8
N_EXPERTS = N_GROUPS * EXP_PER_GROUP
EXP_TOPK = 2
RMS_EPS = 1e-6
NEG_INF = float("-inf")

LANES = 128
SUBLANES = 8
MIB = 1024 * 1024

ROW_TILE = 256
MOE_TILE = 256
SAMPLE_PAD_T = 8

_NT = (((1,), (1,)), ((), ()))


def _cparams(semantics, vmem_mib):
    return pltpu.CompilerParams(dimension_semantics=semantics, vmem_limit_bytes=vmem_mib * MIB)


def _rms(x, g):
    return x * lax.rsqrt(jnp.mean(x * x, axis=-1, keepdims=True) + RMS_EPS) * g


def _silu(x):
    return x * jax.nn.sigmoid(x)


def _rms_proj_kernel(x_ref, g_ref, w_ref, *o_refs, segs):
    h = _rms(x_ref[...], g_ref[...]).astype(BF16)
    k = 0
    for lo, hi, outs in segs:
        r = jnp.dot(h, w_ref[:, lo:hi], preferred_element_type=F32)
        for scale, transposed in outs:
            y = r if scale == 1.0 else r * scale
            if transposed:
                y = y.T
            o_refs[k][...] = y.astype(o_refs[k].dtype)
            k += 1


def _rms_proj(x, g, w_bf, segs, dtypes, tm=ROW_TILE, vmem_mib=48, seq_len=None):
    rows, d = x.shape
    assert rows % tm == 0
    out_shape, out_specs = [], []
    k = 0
    for lo, hi, outs in segs:
        n = hi - lo
        for _, transposed in outs:
            if transposed == "batched":
                per_seq = seq_len // tm
                out_shape.append(jax.ShapeDtypeStruct((rows // seq_len, n, seq_len), dtypes[k]))
                out_specs.append(pl.BlockSpec((None, n, tm), lambda i: (i // per_seq, 0, i % per_seq)))
            elif transposed:
                out_shape.append(jax.ShapeDtypeStruct((n, rows), dtypes[k]))
                out_specs.append(pl.BlockSpec((n, tm), lambda i: (0, i)))
            else:
                out_shape.append(jax.ShapeDtypeStruct((rows, n), dtypes[k]))
                out_specs.append(pl.BlockSpec((tm, n), lambda i: (i, 0)))
            k += 1
    return pl.pallas_call(
        functools.partial(_rms_proj_kernel, segs=segs),
        grid=(rows // tm,),
        in_specs=[pl.BlockSpec((tm, d), lambda i: (i, 0)),
                  pl.BlockSpec((1, d), lambda i: (0, 0)),
                  pl.BlockSpec(w_bf.shape, lambda i: (0, 0))],
        out_specs=out_specs,
        out_shape=out_shape,
        compiler_params=_cparams(("parallel",), vmem_mib),
        name="rms_proj",
    )(x, g.reshape(1, d), w_bf)


def _cumsum_rows(x):
    n = x.shape[0]
    row = lax.broadcasted_iota(I32, x.shape, 0)
    s = 1
    while s < n:
        x = x + jnp.where(row >= s, pltpu.roll(x, s, 0), 0.0)
        s *= 2
    return x


def _hgrn_intra(q, kk, b, c, w):
    row = lax.broadcasted_iota(I32, (c, HG_DIM), 0)
    acc = None
    half = c // 2
    while half >= SUBLANES:
        two = 2 * half
        nblk = c // two
        pieces = [jnp.broadcast_to(b[i * two + half - 1:i * two + half, :], (two, HG_DIM)) for i in range(nblk)]
        bm = pieces[0] if nblk == 1 else jnp.concatenate(pieces, axis=0)
        second = (row & (two - 1)) >= half
        qt = jnp.where(second, q * jnp.exp(jnp.where(second, b - bm, 0.0)), 0.0)
        kt = jnp.where(second, 0.0, kk * jnp.exp(jnp.where(second, 0.0, bm - b)))
        al = lax.dot_general(qt.astype(BF16), kt.astype(BF16), _NT, preferred_element_type=F32)
        if nblk > 1:
            shift = two.bit_length() - 1
            rt = lax.broadcasted_iota(I32, (c, c), 0) >> shift
            cs = lax.broadcasted_iota(I32, (c, c), 1) >> shift
            al = jnp.where(rt == cs, al, 0.0)
        acc = al if acc is None else acc + al
        half //= 2
    r8 = lax.broadcasted_iota(I32, (SUBLANES, HG_DIM), 0)
    lane = lax.broadcasted_iota(I32, (SUBLANES, w), 1)
    blocks = []
    for g in range(c // SUBLANES):
        lo = g * SUBLANES
        qg, kg, bg = q[lo:lo + SUBLANES], kk[lo:lo + SUBLANES], b[lo:lo + SUBLANES]
        ag = jnp.zeros((SUBLANES, w), F32)
        for s in range(SUBLANES):
            e = jnp.exp(jnp.where(r8 >= s, bg - bg[s:s + 1, :], NEG_INF))
            p = jnp.sum(qg * kg[s:s + 1, :] * e, axis=1, keepdims=True)
            ag = jnp.where(lane == lo + s, p, ag)
        blocks.append(ag)
    diag = blocks[0] if len(blocks) == 1 else jnp.concatenate(blocks, axis=0)
    return diag if acc is None else acc + diag


def _hgrn_kernel(*refs, c, t_valid, has_s0):
    if has_s0:
        hp_ref, lb_ref, gn_ref, s0_ref, o_ref, sout_ref, st_ref = refs
    else:
        hp_ref, lb_ref, gn_ref, o_ref, sout_ref, st_ref = refs
    ci = pl.program_id(1)
    last = pl.num_programs(1) - 1
    w = max(c, LANES)

    @pl.when(ci == 0)
    def _():
        for h in range(HG_HEADS):
            st_ref[h] = s0_ref[0, h].T if has_s0 else jnp.zeros((HG_DIM, HG_DIM), F32)

    row = lax.broadcasted_iota(I32, (c, HG_DIM), 0)
    for h in range(HG_HEADS):
        lo = h * HG_DIM
        hq = hp_ref[:, lo:lo + HG_DIM]
        hf = hp_ref[:, HG_WIDTH + lo:HG_WIDTH + lo + HG_DIM]
        v = hp_ref[:, 2 * HG_WIDTH + lo:2 * HG_WIDTH + lo + HG_DIM]
        hg = hp_ref[:, 3 * HG_WIDTH + lo:3 * HG_WIDTH + lo + HG_DIM]
        lb = lb_ref[:, lo:lo + HG_DIM]
        q = _silu(hq)
        f = lb + (1.0 - lb) * jax.nn.sigmoid(hf)
        logf = jnp.log(f)
        kk = 1.0 - f
        if t_valid < c:
            valid = row < t_valid
            logf = jnp.where(valid, logf, 0.0)
            kk = jnp.where(valid, kk, 0.0)
            v = jnp.where(valid, v, 0.0)
        b = _cumsum_rows(logf)
        a = _hgrn_intra(q, kk, b, c, w)
        st = st_ref[h]
        bl = b[c - 1:c, :]
        k2 = kk * jnp.exp(bl - b)
        if c < w:
            zpad = jnp.zeros((w - c, HG_DIM), F32)
            vp = jnp.concatenate([v, zpad], axis=0)
            k2 = jnp.concatenate([k2, zpad], axis=0)
        else:
            vp = v
        vb = vp.astype(BF16)
        o = lax.dot_general((q * jnp.exp(b)).astype(BF16), st.astype(BF16), _NT, preferred_element_type=F32)
        o = o + jnp.dot(a.astype(BF16), vb, preferred_element_type=F32)
        o_ref[:, lo:lo + HG_DIM] = (_rms(o, gn_ref[...]) * _silu(hg)).astype(o_ref.dtype)
        st_new = st * jnp.exp(bl) + jnp.dot(vp.T.astype(BF16), k2.astype(BF16), preferred_element_type=F32)
        st_ref[h] = st_new

        @pl.when(ci == last)
        def _():
            sout_ref[0, h] = st_new.T


def _hgrn(hp, lb, gn, s0, batch, t_pad, t_valid, c):
    rows = hp.shape[0]
    nc = t_pad // c
    has_s0 = s0 is not None
    in_specs = [pl.BlockSpec((c, 4 * HG_WIDTH), lambda b, i: (b * nc + i, 0)),
                pl.BlockSpec((1, HG_WIDTH), lambda b, i: (0, 0)),
                pl.BlockSpec((1, HG_DIM), lambda b, i: (0, 0))]
    args = [hp, lb.reshape(1, HG_WIDTH), gn.reshape(1, HG_DIM)]
    if has_s0:
        in_specs.append(pl.BlockSpec((1, HG_HEADS, HG_DIM, HG_DIM), lambda b, i: (b, 0, 0, 0)))
        args.append(s0)
    return pl.pallas_call(
        functools.partial(_hgrn_kernel, c=c, t_valid=min(t_valid, c), has_s0=has_s0),
        grid=(batch, nc),
        in_specs=in_specs,
        out_specs=[pl.BlockSpec((c, HG_WIDTH), lambda b, i: (b * nc + i, 0)),
                   pl.BlockSpec((1, HG_HEADS, HG_DIM, HG_DIM), lambda b, i: (b, 0, 0, 0))],
        out_shape=[jax.ShapeDtypeStruct((rows, HG_WIDTH), BF16),
                   jax.ShapeDtypeStruct((batch, HG_HEADS, HG_DIM, HG_DIM), F32)],
        scratch_shapes=[pltpu.VMEM((HG_HEADS, HG_DIM, HG_DIM), F32)],
        compiler_params=_cparams(("parallel", "arbitrary"), 32),
        name="hgrn2",
    )(*args)


def _topk_block_mask(gate, n_valid, nrow):
    nb = gate.shape[0]
    cnt = jnp.zeros(gate.shape, I32)
    for n2 in range(nb):
        g2 = gate[n2:n2 + 1, :]
        beats = (g2 > gate) | ((g2 == gate) & (n2 < nrow))
        cnt = cnt + jnp.where(beats, (n_valid > n2).astype(I32), 0)
    return (cnt < MB_TOPK) & (nrow < n_valid)


def _moba_prompt_kernel(slope_ref, qt_ref, kt_ref, vt_ref, o_ref, kb_ref, vb_ref, km_ref, sel_ref, *, nb):
    hp = pl.program_id(1)
    qi = pl.program_id(2)
    tq = MB_BLOCK

    @pl.when(qi == 0)
    def _():
        means = []
        for j in range(nb):
            kj = kt_ref[:, j * MB_BLOCK:(j + 1) * MB_BLOCK].T
            kb_ref[j * MB_BLOCK:(j + 1) * MB_BLOCK, :] = kj.astype(BF16)
            means.append(jnp.sum(kj, axis=0, keepdims=True) * (1.0 / MB_BLOCK))
            vb_ref[j] = vt_ref[:, j * MB_BLOCK:(j + 1) * MB_BLOCK].astype(BF16)
        km_ref[...] = jnp.concatenate(means, axis=0)

    qt2 = qt_ref[...]
    second_head = lax.broadcasted_iota(I32, (2 * MB_DIM, tq), 0) >= MB_DIM
    nrow = lax.broadcasted_iota(I32, (nb, tq), 0)
    lane = lax.broadcasted_iota(I32, (MB_BLOCK, tq), 1)
    rowk = lax.broadcasted_iota(I32, (MB_BLOCK, tq), 0)
    dlr = lane - rowk
    kmb = km_ref[...].astype(BF16)
    outs = []
    for i in range(2):
        slope = slope_ref[hp * 2 + i]
        keep = second_head if i == 1 else jnp.logical_not(second_head)
        qp = jnp.where(keep, qt2, jnp.zeros_like(qt2))
        gate = jnp.dot(kmb, qp, preferred_element_type=F32)
        sel = _topk_block_mask(gate, qi, nrow).astype(F32)
        for n in range(nb):
            sel_ref[n] = jnp.broadcast_to(sel[n:n + 1, :], (SUBLANES, tq))

        k_own = kb_ref[pl.ds(pl.multiple_of(qi * MB_BLOCK, MB_BLOCK), MB_BLOCK), :]
        s = jnp.dot(k_own, qp, preferred_element_type=F32) - slope * dlr.astype(F32)
        s = jnp.where(rowk <= lane, s, NEG_INF)
        m = jnp.max(s, axis=0, keepdims=True)
        p = jnp.exp(s - m)
        l = jnp.sum(p, axis=0, keepdims=True)
        acc = jnp.dot(vb_ref[qi], p.astype(BF16), preferred_element_type=F32)

        def body(j, carry, qp=qp, slope=slope):
            m, l, acc = carry
            kj = kb_ref[pl.ds(pl.multiple_of(j * MB_BLOCK, MB_BLOCK), MB_BLOCK), :]
            dist = (dlr + (qi - j) * MB_BLOCK).astype(F32)
            s = jnp.dot(kj, qp, preferred_element_type=F32) - slope * dist
            s = jnp.where(sel_ref[j][0:1, :] > 0.5, s, NEG_INF)
            m_new = jnp.maximum(m, jnp.max(s, axis=0, keepdims=True))
            alpha = jnp.exp(m - m_new)
            p = jnp.exp(s - m_new)
            l = alpha * l + jnp.sum(p, axis=0, keepdims=True)
            acc = alpha * acc + jnp.dot(vb_ref[j], p.astype(BF16), preferred_element_type=F32)
            return m_new, l, acc

        m, l, acc = lax.fori_loop(0, qi, body, (m, l, acc))
        outs.append(acc / l)
    o_ref[...] = jnp.where(second_head, outs[1], outs[0])


def _moba_prompt(qt, kt, vt, slopes, batch, t):
    nb = t // MB_BLOCK
    rows = batch * t
    return pl.pallas_call(
        functools.partial(_moba_prompt_kernel, nb=nb),
        grid=(batch, MB_HEADS // 2, nb),
        in_specs=[pl.BlockSpec(memory_space=pltpu.SMEM),
                  pl.BlockSpec((2 * MB_DIM, MB_BLOCK), lambda b, h, i: (h, b * nb + i)),
                  pl.BlockSpec((None, 2 * MB_DIM, t), lambda b, h, i: (b, h, 0)),
                  pl.BlockSpec((None, 2 * MB_DIM, t), lambda b, h, i: (b, h, 0))],
        out_specs=pl.BlockSpec((2 * MB_DIM, MB_BLOCK), lambda b, h, i: (h, b * nb + i)),
        out_shape=jax.ShapeDtypeStruct((MB_WIDTH, rows), F32),
        scratch_shapes=[pltpu.VMEM((t, 2 * MB_DIM), BF16),
                        pltpu.VMEM((nb, 2 * MB_DIM, MB_BLOCK), BF16),
                        pltpu.VMEM((nb, 2 * MB_DIM), F32),
                        pltpu.VMEM((nb, SUBLANES, MB_BLOCK), F32)],
        compiler_params=_cparams(("parallel", "parallel", "arbitrary"), 32),
        name="moba_prompt",
    )(slopes, qt, kt, vt)


def _moba_sample_kernel(pt_ref, slope_ref, hm_ref, q_ref, kn_ref, vn_ref, *rest, npages, page, t_new, past_len):
    del pt_ref
    k_pages = rest[:npages]
    v_pages = rest[npages:2 * npages]
    o_ref = rest[2 * npages]
    nrows = t_new * MB_HEADS
    pages_per_block = MB_BLOCK // page
    nb = npages // pages_per_block
    hm = hm_ref[...]
    slope = slope_ref[...][:, 0:1]
    row_q = lax.broadcasted_iota(I32, (nrows, 1), 0) >> 3
    pos_q = past_len + row_q

    q = q_ref[...] * (MB_DIM ** -0.5)
    qbd = jnp.concatenate([jnp.broadcast_to(q[t:t + 1, :], (MB_HEADS, MB_WIDTH)) * hm for t in range(t_new)],
                          axis=0).astype(BF16)

    lane = lax.broadcasted_iota(I32, (nrows, page), 1)
    scores = []
    gates = [None] * nb
    for p in range(npages):
        s = jnp.dot(qbd, k_pages[p][0].astype(BF16), preferred_element_type=F32)
        rs = jnp.sum(s, axis=1, keepdims=True)
        n = p // pages_per_block
        gates[n] = rs if gates[n] is None else gates[n] + rs
        scores.append(s - slope * (pos_q - (p * page + lane)).astype(F32))
    sels = []
    for n in range(nb):
        cnt = jnp.zeros((nrows, 1), I32)
        for n2 in range(nb):
            beats = (gates[n2] > gates[n]) | ((gates[n2] == gates[n]) & (n2 < n))
            cnt = cnt + jnp.where(beats, 1, 0)
        sels.append(jnp.where(cnt < MB_TOPK, 1.0, 0.0))

    zrows = jnp.zeros((LANES - SAMPLE_PAD_T, MB_WIDTH), F32)
    knp = jnp.concatenate([kn_ref[...], zrows], axis=0).astype(BF16)
    vnp = jnp.concatenate([vn_ref[...], zrows], axis=0).astype(BF16)
    lane_n = lax.broadcasted_iota(I32, (nrows, LANES), 1)
    s_new = lax.dot_general(qbd, knp, _NT, preferred_element_type=F32)
    s_new = jnp.where(lane_n <= row_q, s_new - slope * (row_q - lane_n).astype(F32), NEG_INF)
    m = jnp.max(s_new, axis=1, keepdims=True)
    for p in range(npages):
        sel_p = jnp.broadcast_to(sels[p // pages_per_block], (nrows, page)) > 0.5
        scores[p] = jnp.where(sel_p, scores[p], NEG_INF)
        m = jnp.maximum(m, jnp.max(scores[p], axis=1, keepdims=True))

    p_new = jnp.exp(s_new - m)
    l = jnp.sum(p_new, axis=1, keepdims=True)
    acc = jnp.dot(p_new.astype(BF16), vnp, preferred_element_type=F32)
    for p in range(npages):
        pr = jnp.exp(scores[p] - m)
        l = l + jnp.sum(pr, axis=1, keepdims=True)
        acc = acc + lax.dot_general(pr.astype(BF16), v_pages[p][0].astype(BF16), _NT,
                                    preferred_element_type=F32)
    acc = acc / l
    rows = [jnp.sum(acc[t * MB_HEADS:(t + 1) * MB_HEADS, :] * hm, axis=0, keepdims=True) for t in range(t_new)]
    rows.append(jnp.zeros((SAMPLE_PAD_T - t_new, MB_WIDTH), F32))
    o_ref[...] = jnp.concatenate(rows, axis=0)


def _moba_sample(mq, mk, mv, pool_kt, pool_vt, page_table, t_new):
    batch, npages = page_table.shape
    page = pool_kt.shape[2]
    past_len = npages * page
    assert past_len % MB_BLOCK == 0 and MB_BLOCK % page == 0 and t_new <= SAMPLE_PAD_T and page == LANES
    nrows = t_new * MB_HEADS
    slopes = np.power(2.0, -8.0 * np.arange(1, MB_HEADS + 1) / MB_HEADS).astype(np.float32)
    slope_rows = jnp.asarray(np.tile(np.tile(slopes, t_new)[:, None], (1, LANES)))
    head_mask = jnp.asarray((np.arange(MB_WIDTH)[None, :] // MB_DIM == np.arange(MB_HEADS)[:, None])
                            .astype(np.float32))
    new_spec = pl.BlockSpec((SAMPLE_PAD_T, MB_WIDTH), lambda b, pt: (b, 0))
    page_specs = [pl.BlockSpec((1, MB_WIDTH, page),
                               functools.partial(lambda b, pt, i: (pt[b * npages + i], 0, 0), i=i))
                  for i in range(npages)]
    grid_spec = pltpu.PrefetchScalarGridSpec(
        num_scalar_prefetch=1,
        grid=(batch,),
        in_specs=[pl.BlockSpec((nrows, LANES), lambda b, pt: (0, 0)),
                  pl.BlockSpec((MB_HEADS, MB_WIDTH), lambda b, pt: (0, 0)),
                  new_spec, new_spec, new_spec] + page_specs + page_specs,
        out_specs=new_spec,
    )
    return pl.pallas_call(
        functools.partial(_moba_sample_kernel, npages=npages, page=page, t_new=t_new, past_len=past_len),
        grid_spec=grid_spec,
        out_shape=jax.ShapeDtypeStruct((batch * SAMPLE_PAD_T, MB_WIDTH), F32),
        compiler_params=_cparams(("parallel",), 40),
        name="moba_sample",
    )(page_table.reshape(-1), slope_rows, head_mask, mq, mk, mv, *([pool_kt] * npages), *([pool_vt] * npages))


def _outproj_kernel(x_ref, o_ref, att_ref, wo_ref, g_ref, wq_ref, x1_ref, q_ref, *, att_transposed):
    att = att_ref[...]
    if att_transposed:
        att = att.T
    mix = jnp.dot(o_ref[...], wo_ref[:HG_WIDTH, :], preferred_element_type=F32)
    mix = mix + jnp.dot(att.astype(BF16), wo_ref[HG_WIDTH:, :], preferred_element_type=F32)
    x1 = x_ref[...] + mix
    x1_ref[...] = x1
    h = _rms(x1, g_ref[...]).astype(BF16)
    q_ref[...] = (jnp.dot(h, wq_ref[...], preferred_element_type=F32) * (XA_DIM ** -0.5)).astype(q_ref.dtype)


def _outproj(x, o, att, wo_bf, g, wq_bf, att_transposed, q_dtype, tm=ROW_TILE):
    rows, d = x.shape
    att_spec = (pl.BlockSpec((MB_WIDTH, tm), lambda i: (0, i)) if att_transposed
                else pl.BlockSpec((tm, MB_WIDTH), lambda i: (i, 0)))
    return pl.pallas_call(
        functools.partial(_outproj_kernel, att_transposed=att_transposed),
        grid=(rows // tm,),
        in_specs=[pl.BlockSpec((tm, d), lambda i: (i, 0)),
                  pl.BlockSpec((tm, HG_WIDTH), lambda i: (i, 0)),
                  att_spec,
                  pl.BlockSpec((d, d), lambda i: (0, 0)),
                  pl.BlockSpec((1, d), lambda i: (0, 0)),
                  pl.BlockSpec((d, d), lambda i: (0, 0))],
        out_specs=[pl.BlockSpec((tm, d), lambda i: (i, 0)), pl.BlockSpec((tm, d), lambda i: (i, 0))],
        out_shape=[jax.ShapeDtypeStruct((rows, d), F32), jax.ShapeDtypeStruct((rows, d), q_dtype)],
        compiler_params=_cparams(("parallel",), 40),
        name="outproj_q",
    )(x, o, att, wo_bf, g.reshape(1, d), wq_bf)


def _xattn_kernel(q_ref, k_ref, v_ref, o_ref, *, kv_head_axis):
    q = q_ref[...].astype(BF16)
    for h in range(XA_HEADS):
        lo = h * XA_DIM
        if kv_head_axis:
            kh = k_ref[:, h, :].astype(BF16)
            vh = v_ref[:, h, :].astype(BF16)
        else:
            kh = k_ref[:, lo:lo + XA_DIM].astype(BF16)
            vh = v_ref[:, lo:lo + XA_DIM].astype(BF16)
        s = lax.dot_general(q[:, lo:lo + XA_DIM], kh, _NT, preferred_element_type=F32)
        p = jnp.exp(s - jnp.max(s, axis=-1, keepdims=True))
        p = p / jnp.sum(p, axis=-1, keepdims=True)
        o_ref[:, lo:lo + XA_DIM] = jnp.dot(p.astype(BF16), vh, preferred_element_type=F32).astype(o_ref.dtype)


def _xattn(q, mem_k, mem_v, batch, t, tq):
    rows, d = q.shape
    nq = t // tq
    kv_head_axis = mem_k.ndim == 4
    if kv_head_axis:
        kv_spec = pl.BlockSpec((None,) + mem_k.shape[1:], lambda b, i: (b, 0, 0, 0))
    else:
        m = mem_k.shape[0] // batch
        kv_spec = pl.BlockSpec((m, d), lambda b, i: (b, 0))
    return pl.pallas_call(
        functools.partial(_xattn_kernel, kv_head_axis=kv_head_axis),
        grid=(batch, nq),
        in_specs=[pl.BlockSpec((tq, d), lambda b, i: (b * nq + i, 0)), kv_spec, kv_spec],
        out_specs=pl.BlockSpec((tq, d), lambda b, i: (b * nq + i, 0)),
        out_shape=jax.ShapeDtypeStruct((rows, d), q.dtype),
        compiler_params=_cparams(("parallel", "arbitrary"), 32),
        name="xattn",
    )(q, mem_k, mem_v)


def _xo_router_kernel(x1_ref, o_ref, wo_ref, g_ref, wr_ref, br_ref, x2_ref, h_ref, r_ref):
    x2 = x1_ref[...] + jnp.dot(o_ref[...].astype(BF16), wo_ref[...], preferred_element_type=F32)
    x2_ref[...] = x2
    h = _rms(x2, g_ref[...])
    h_ref[...] = h
    logits = jnp.dot(h.astype(BF16), wr_ref[...], preferred_element_type=F32) + br_ref[...]
    lane = lax.broadcasted_iota(I32, logits.shape, 1)
    big = jnp.int32(LANES)

    def top1(mask):
        mx = jnp.max(jnp.where(mask, logits, NEG_INF), axis=-1, keepdims=True)
        idx = jnp.min(jnp.where(mask & (logits == mx), lane, big), axis=-1, keepdims=True)
        return mx, idx

    gmask = lane < N_GROUPS
    gmx, gsel = top1(gmask)
    gw = 1.0 / jnp.sum(jnp.where(gmask, jnp.exp(logits - gmx), 0.0), axis=-1, keepdims=True)
    elo = N_GROUPS + gsel * EXP_PER_GROUP
    emask = (lane >= elo) & (lane < elo + EXP_PER_GROUP)
    m1, i1 = top1(emask)
    m2, i2 = top1(emask & (lane != i1))
    e2 = jnp.exp(m2 - m1)
    g1 = gw / (1.0 + e2)
    g2 = gw * e2 / (1.0 + e2)
    out = jnp.where(lane == 0, (i1 - N_GROUPS).astype(F32), 0.0)
    out = jnp.where(lane == 1, (i2 - N_GROUPS).astype(F32), out)
    out = jnp.where(lane == 2, g1, out)
    out = jnp.where(lane == 3, g2, out)
    r_ref[...] = out


def _xo_router(x1, o, wxo_bf, g, wr_bf, br, tm=ROW_TILE):
    rows, d = x1.shape
    return pl.pallas_call(
        _xo_router_kernel,
        grid=(rows // tm,),
        in_specs=[pl.BlockSpec((tm, d), lambda i: (i, 0)),
                  pl.BlockSpec((tm, d), lambda i: (i, 0)),
                  pl.BlockSpec((d, d), lambda i: (0, 0)),
                  pl.BlockSpec((1, d), lambda i: (0, 0)),
                  pl.BlockSpec((d, LANES), lambda i: (0, 0)),
                  pl.BlockSpec((1, LANES), lambda i: (0, 0))],
        out_specs=[pl.BlockSpec((tm, d), lambda i: (i, 0)),
                   pl.BlockSpec((tm, d), lambda i: (i, 0)),
                   pl.BlockSpec((tm, LANES), lambda i: (i, 0))],
        out_shape=[jax.ShapeDtypeStruct((rows, d), F32),
                   jax.ShapeDtypeStruct((rows, d), F32),
                   jax.ShapeDtypeStruct((rows, LANES), F32)],
        compiler_params=_cparams(("parallel",), 40),
        name="xo_router",
    )(x1, o, wxo_bf, g.reshape(1, d), wr_bf, br)


def _row_copy(src_hbm, src_row, dst_buf, slot, r, sem):
    return pltpu.make_async_copy(src_hbm.at[pl.ds(src_row, 1)], dst_buf.at[slot, pl.ds(r, 1)], sem.at[slot])


def _moe_kernel(blk_e_ref, row_tok_ref, nused_ref, h_hbm, w1_ref, w3_ref, w2_ref, o_ref, xbuf, sem, *, tm):
    del blk_e_ref
    i = pl.program_id(0)
    nused = nused_ref[0]

    def start_block(blk, slot):
        def body(r, carry):
            _row_copy(h_hbm, row_tok_ref[blk * tm + r], xbuf, slot, r, sem).start()
            return carry
        lax.fori_loop(0, tm, body, 0)

    def wait_block(slot):
        def body(r, carry):
            _row_copy(h_hbm, 0, xbuf, slot, r, sem).wait()
            return carry
        lax.fori_loop(0, tm, body, 0)

    @pl.when(i == 0)
    def _():
        start_block(0, 0)

    @pl.when(i + 1 < nused)
    def _():
        start_block(i + 1, (i + 1) % 2)

    @pl.when(i < nused)
    def _():
        slot = i % 2
        wait_block(slot)
        x = xbuf[slot].astype(BF16)
        a = jnp.dot(x, w1_ref[0], preferred_element_type=F32)
        b = jnp.dot(x, w3_ref[0], preferred_element_type=F32)
        hmid = (_silu(a) * b).astype(BF16)
        o_ref[...] = jnp.dot(hmid, w2_ref[0], preferred_element_type=F32)

    @pl.when(i >= nused)
    def _():
        o_ref[...] = jnp.zeros(o_ref.shape, o_ref.dtype)


def _moe_experts(h, blk_e, row_tok, nused, w1_bf, w3_bf, w2_bf, tm=MOE_TILE):
    nblk = blk_e.shape[0]
    d = h.shape[1]
    wspec = pl.BlockSpec((1, d, d), lambda i, be, rt, nu: (be[i], 0, 0))
    grid_spec = pltpu.PrefetchScalarGridSpec(
        num_scalar_prefetch=3,
        grid=(nblk,),
        in_specs=[pl.BlockSpec(memory_space=pl.ANY), wspec, wspec, wspec],
        out_specs=pl.BlockSpec((tm, d), lambda i, be, rt, nu: (i, 0)),
        scratch_shapes=[pltpu.VMEM((2, tm, d), F32), pltpu.SemaphoreType.DMA((2,))],
    )
    return pl.pallas_call(
        functools.partial(_moe_kernel, tm=tm),
        grid_spec=grid_spec,
        out_shape=jax.ShapeDtypeStruct((nblk * tm, d), F32),
        compiler_params=_cparams(("arbitrary",), 48),
        name="moe_experts",
    )(blk_e, row_tok, nused, h, w1_bf, w3_bf, w2_bf)


def _combine_kernel(dest_ref, x2_ref, r_ref, g_ref, outs_hbm, y_ref, buf0, buf1, sem0, sem1, *, tm):
    i = pl.program_id(0)
    n = pl.num_programs(0)

    def start_tile(tile, slot):
        def body(r, carry):
            base = 2 * (tile * tm + r)
            _row_copy(outs_hbm, dest_ref[base], buf0, slot, r, sem0).start()
            _row_copy(outs_hbm, dest_ref[base + 1], buf1, slot, r, sem1).start()
            return carry
        lax.fori_loop(0, tm, body, 0)

    def wait_tile(slot):
        def body(r, carry):
            _row_copy(outs_hbm, 0, buf0, slot, r, sem0).wait()
            _row_copy(outs_hbm, 0, buf1, slot, r, sem1).wait()
            return carry
        lax.fori_loop(0, tm, body, 0)

    @pl.when(i == 0)
    def _():
        start_tile(0, 0)

    @pl.when(i + 1 < n)
    def _():
        start_tile(i + 1, (i + 1) % 2)

    slot = i % 2
    wait_tile(slot)
    route = r_ref[...]
    y = buf0[slot] * route[:, 2:3] + buf1[slot] * route[:, 3:4]
    y_ref[...] = _rms(x2_ref[...] + y, g_ref[...])


def _combine(x2, route, g_final, outs, dest, tm=ROW_TILE):
    rows, d = x2.shape
    grid_spec = pltpu.PrefetchScalarGridSpec(
        num_scalar_prefetch=1,
        grid=(rows // tm,),
        in_specs=[pl.BlockSpec((tm, d), lambda i, de: (i, 0)),
                  pl.BlockSpec((tm, LANES), lambda i, de: (i, 0)),
                  pl.BlockSpec((1, d), lambda i, de: (0, 0)),
                  pl.BlockSpec(memory_space=pl.ANY)],
        out_specs=pl.BlockSpec((tm, d), lambda i, de: (i, 0)),
        scratch_shapes=[pltpu.VMEM((2, tm, d), F32), pltpu.VMEM((2, tm, d), F32),
                        pltpu.SemaphoreType.DMA((2,)), pltpu.SemaphoreType.DMA((2,))],
    )
    return pl.pallas_call(
        functools.partial(_combine_kernel, tm=tm),
        grid_spec=grid_spec,
        out_shape=jax.ShapeDtypeStruct((rows, d), F32),
        compiler_params=_cparams(("arbitrary",), 32),
        name="moe_combine",
    )(dest, x2, route, g_final.reshape(1, d), outs)


def _route_tables(eid, tm):
    n = eid.shape[0]
    a = n * EXP_TOPK
    flat_e = eid.reshape(-1)
    onehot = (flat_e[:, None] == jnp.arange(N_EXPERTS, dtype=I32)[None, :]).astype(I32)
    csum = jnp.cumsum(onehot, axis=0)
    pos = jnp.take_along_axis(csum, flat_e[:, None], axis=1)[:, 0] - 1
    counts = csum[-1]
    padc = ((counts + tm - 1) // tm) * tm
    pend = jnp.cumsum(padc)
    pstart = pend - padc
    dest = (pstart[flat_e] + pos).astype(I32)
    nblk = -(-(a + N_EXPERTS * (tm - 1)) // tm)
    row_tok = jnp.zeros((nblk * tm,), I32).at[dest].set(jnp.arange(a, dtype=I32) // EXP_TOPK)
    blk_e = jnp.minimum(jnp.searchsorted(pend, jnp.arange(nblk, dtype=I32) * tm, side="right"),
                        N_EXPERTS - 1).astype(I32)
    nused = (pend[-1] // tm).astype(I32).reshape(1)
    return dest, row_tok, blk_e, nused


def _layer_tail(x, o, att, att_transposed, mem_k, mem_v, batch, t, xq_tile, wts):
    q_dtype = BF16 if xq_tile % (2 * SUBLANES) == 0 else F32
    x1, q = _outproj(x, o, att, wts["w_out"], wts["g_cross"], wts["w_xq"], att_transposed, q_dtype)
    xo = _xattn(q, mem_k, mem_v, batch, t, xq_tile)
    x2, h, route = _xo_router(x1, xo, wts["w_xo"], wts["g_ffn"], wts["w_router"], wts["b_router"])
    return x2, h, route


def _moe_and_final(x2, h, route, valid_rows, wts):
    d = x2.shape[1]
    eid = route[:, 0:2].astype(I32)
    if valid_rows is not None:
        batch, t_pad, t_valid = valid_rows
        h_tok = h.reshape(batch, t_pad, d)[:, :t_valid].reshape(batch * t_valid, d)
        eid_tok = eid.reshape(batch, t_pad, EXP_TOPK)[:, :t_valid].reshape(batch * t_valid, EXP_TOPK)
    else:
        h_tok, eid_tok = h, eid
    dest, row_tok, blk_e, nused = _route_tables(eid_tok, MOE_TILE)
    outs = _moe_experts(h_tok, blk_e, row_tok, nused, wts["w1"], wts["w3"], wts["w2"])
    if valid_rows is not None:
        dest = jnp.pad(dest.reshape(batch, t_valid, EXP_TOPK), ((0, 0), (0, t_pad - t_valid), (0, 0))).reshape(-1)
    return _combine(x2, route, wts["g_final"], outs, dest)


def kernel(x_prompt, x_sample, state_hgrn, cache_moba_k, cache_moba_v, cache_mem_k, cache_mem_v, page_table,
           mem_prompt, g_mix, w_in, hg_lb, hg_norm, w_out, g_cross, g_mem, w_xq, w_xk, w_xv, w_xo, g_ffn,
           w_grp, b_grp, w_exp, b_exp, w1, w3, w2, g_final):
    depth = g_mix.shape[0]
    assert depth == 1
    bp, tp, d = x_prompt.shape
    bs, ts, _ = x_sample.shape
    mem_len = mem_prompt.shape[1]
    l = 0

    lb = jnp.cumsum(jax.nn.softmax(hg_lb.astype(F32), axis=0), axis=0)[l]
    slopes = jnp.asarray(np.power(2.0, -8.0 * np.arange(1, MB_HEADS + 1) / MB_HEADS).astype(np.float32))
    n_router = N_GROUPS + N_EXPERTS
    w_router = jnp.pad(jnp.concatenate([w_grp[l], w_exp[l]], axis=1), ((0, 0), (0, LANES - n_router)))
    b_router = jnp.pad(jnp.concatenate([b_grp[l], b_exp[l]]), (0, LANES - n_router)).reshape(1, LANES)
    wts = {
        "w_out": w_out[l].astype(BF16), "g_cross": g_cross[l], "w_xq": w_xq[l].astype(BF16),
        "w_xo": w_xo[l].astype(BF16), "g_ffn": g_ffn[l], "w_router": w_router.astype(BF16),
        "b_router": b_router.astype(F32), "w1": w1[l].astype(BF16), "w3": w3[l].astype(BF16),
        "w2": w2[l].astype(BF16), "g_final": g_final,
    }
    w_in_bf = w_in[l].astype(BF16)
    o_hq, o_mq, o_mk, o_mv = 0, 4 * HG_WIDTH, 4 * HG_WIDTH + MB_WIDTH, 4 * HG_WIDTH + 2 * MB_WIDTH
    o_end = o_mv + MB_WIDTH

    xp = x_prompt.reshape(bp * tp, d)
    w_kv = jnp.concatenate([w_xk[l], w_xv[l]], axis=1).astype(BF16)
    memk_p, memv_p = _rms_proj(mem_prompt.reshape(bp * mem_len, d), g_mem[l], w_kv,
                               [(0, d, [(1.0, False)]), (d, 2 * d, [(1.0, False)])], [F32, F32])
    hp_p, qt_p, kt_p, vt_p = _rms_proj(
        xp, g_mix[l], w_in_bf,
        [(o_hq, o_mq, [(1.0, False)]), (o_mq, o_mk, [(MB_DIM ** -0.5, True)]), (o_mk, o_mv, [(1.0, "batched")]),
         (o_mv, o_end, [(1.0, "batched")])],
        [F32, BF16, F32, F32], seq_len=tp)
    o_p, s_p = _hgrn(hp_p, lb, hg_norm[l], None, bp, tp, tp, 128)
    att_p = _moba_prompt(qt_p, kt_p, vt_p, slopes, bp, tp)
    x2_p, h_p, route_p = _layer_tail(xp, o_p, att_p, True, memk_p, memv_p, bp, tp, 512, wts)
    y_p = _moe_and_final(x2_p, h_p, route_p, None, wts)

    tpad = SAMPLE_PAD_T
    xs = jnp.pad(x_sample, ((0, 0), (0, tpad - ts), (0, 0))).reshape(bs * tpad, d)
    hp_s, mq_s, mk_s, mv_s = _rms_proj(
        xs, g_mix[l], w_in_bf,
        [(o_hq, o_mq, [(1.0, False)]), (o_mq, o_mk, [(1.0, False)]), (o_mk, o_mv, [(1.0, False)]),
         (o_mv, o_end, [(1.0, False)])],
        [F32, F32, F32, F32])
    o_s, s_s = _hgrn(hp_s, lb, hg_norm[l], state_hgrn[l], bs, tpad, ts, tpad)
    mk4 = mk_s.reshape(bs, tpad, MB_HEADS, MB_DIM)
    mv4 = mv_s.reshape(bs, tpad, MB_HEADS, MB_DIM)
    n_phys, page = cache_moba_k.shape[1], cache_moba_k.shape[2]
    pool_kt = jnp.transpose(cache_moba_k[l], (0, 2, 3, 1)).reshape(n_phys, MB_WIDTH, page)
    pool_vt = jnp.transpose(cache_moba_v[l], (0, 2, 3, 1)).reshape(n_phys, MB_WIDTH, page)
    att_s = _moba_sample(mq_s, mk_s, mv_s, pool_kt, pool_vt, page_table, ts)
    memk_s = cache_mem_k.reshape(bs, mem_len, XA_HEADS, XA_DIM)
    memv_s = cache_mem_v.reshape(bs, mem_len, XA_HEADS, XA_DIM)
    x2_s, h_s, route_s = _layer_tail(xs, o_s, att_s, False, memk_s, memv_s, bs, tpad, tpad, wts)
    y_s = _moe_and_final(x2_s, h_s, route_s, (bs, tpad, ts), wts)

    return (y_p.reshape(bp, tp, d),
            y_s.reshape(bs, tpad, d)[:, :ts],
            s_p.reshape(1, bp, HG_HEADS, HG_DIM, HG_DIM),
            jnp.transpose(kt_p.reshape(1, bp, MB_HEADS, MB_DIM, tp), (0, 1, 4, 2, 3)),
            jnp.transpose(vt_p.reshape(1, bp, MB_HEADS, MB_DIM, tp), (0, 1, 4, 2, 3)),
            memk_p.reshape(1, bp, mem_len, XA_HEADS, XA_DIM),
            memv_p.reshape(1, bp, mem_len, XA_HEADS, XA_DIM),
            s_s.reshape(1, bs, HG_HEADS, HG_DIM, HG_DIM),
            mk4[:, :ts].reshape(1, bs, ts, MB_HEADS, MB_DIM),
            mv4[:, :ts].reshape(1, bs, ts, MB_HEADS, MB_DIM))
```

```python
import functools

import numpy as np
import jax
import jax.numpy as jnp
from jax import lax
from jax.experimental import pallas as pl
from jax.experimental.pallas import tpu as pltpu

F32 = jnp.float32
BF16 = jnp.bfloat16
I32 = jnp.int32

D_MODEL = 1024
HG_HEADS = 4
HG_DIM = 128
HG_WIDTH = HG_HEADS * HG_DIM
MB_HEADS = 8
MB_DIM = 64
MB_WIDTH = MB_HEADS * MB_DIM
MB_BLOCK = 256
MB_TOPK = 3
XA_HEADS = 4
XA_DIM = 256
N_GROUPS = 4
EXP_PER_GROUP = 8
N_EXPERTS = N_GROUPS * EXP_PER_GROUP
EXP_TOPK = 2
RMS_EPS = 1e-6
NEG_INF = float("-inf")

LANES = 128
SUBLANES = 8
MIB = 1024 * 1024

ROW_TILE = 256
MOE_TILE = 256
SAMPLE_PAD_T = 8
MOBA_PAIRS = 2

_NT = (((1,), (1,)), ((), ()))


def _cparams(semantics, vmem_mib):
    return pltpu.CompilerParams(dimension_semantics=semantics, vmem_limit_bytes=vmem_mib * MIB)


def _rms(x, g):
    return x * lax.rsqrt(jnp.mean(x * x, axis=-1, keepdims=True) + RMS_EPS) * g


def _silu(x):
    return x * jax.nn.sigmoid(x)


def _rms_proj_kernel(x_ref, g_ref, w_ref, *o_refs, segs):
    h = _rms(x_ref[...], g_ref[...]).astype(BF16)
    k = 0
    for lo, hi, outs in segs:
        r = jnp.dot(h, w_ref[:, lo:hi], preferred_element_type=F32)
        for scale, transposed in outs:
            y = r if scale == 1.0 else r * scale
            if transposed:
                y = y.T
            o_refs[k][...] = y.astype(o_refs[k].dtype)
            k += 1


def _rms_proj(x, g, w_bf, segs, dtypes, tm=ROW_TILE, vmem_mib=48, seq_len=None):
    rows, d = x.shape
    assert rows % tm == 0
    out_shape, out_specs = [], []
    k = 0
    for lo, hi, outs in segs:
        n = hi - lo
        for _, transposed in outs:
            if transposed == "batched":
                per_seq = seq_len // tm
                out_shape.append(jax.ShapeDtypeStruct((rows // seq_len, n, seq_len), dtypes[k]))
                out_specs.append(pl.BlockSpec((None, n, tm), lambda i: (i // per_seq, 0, i % per_seq)))
            elif transposed:
                out_shape.append(jax.ShapeDtypeStruct((n, rows), dtypes[k]))
                out_specs.append(pl.BlockSpec((n, tm), lambda i: (0, i)))
            else:
                out_shape.append(jax.ShapeDtypeStruct((rows, n), dtypes[k]))
                out_specs.append(pl.BlockSpec((tm, n), lambda i: (i, 0)))
            k += 1
    return pl.pallas_call(
        functools.partial(_rms_proj_kernel, segs=segs),
        grid=(rows // tm,),
        in_specs=[pl.BlockSpec((tm, d), lambda i: (i, 0)),
                  pl.BlockSpec((1, d), lambda i: (0, 0)),
                  pl.BlockSpec(w_bf.shape, lambda i: (0, 0))],
        out_specs=out_specs,
        out_shape=out_shape,
        compiler_params=_cparams(("parallel",), vmem_mib),
        name="rms_proj",
    )(x, g.reshape(1, d), w_bf)


def _cumsum_rows(x):
    n = x.shape[0]
    row = lax.broadcasted_iota(I32, x.shape, 0)
    s = 1
    while s < n:
        x = x + jnp.where(row >= s, pltpu.roll(x, s, 0), 0.0)
        s *= 2
    return x


def _hgrn_intra(q, kk, b, c, w):
    row = lax.broadcasted_iota(I32, (c, HG_DIM), 0)
    acc = None
    half = c // 2
    while half >= SUBLANES:
        two = 2 * half
        nblk = c // two
        pieces = [jnp.broadcast_to(b[i * two + half - 1:i * two + half, :], (two, HG_DIM)) for i in range(nblk)]
        bm = pieces[0] if nblk == 1 else jnp.concatenate(pieces, axis=0)
        second = (row & (two - 1)) >= half
        qt = jnp.where(second, q * jnp.exp(jnp.where(second, b - bm, 0.0)), 0.0)
        kt = jnp.where(second, 0.0, kk * jnp.exp(jnp.where(second, 0.0, bm - b)))
        al = lax.dot_general(qt.astype(BF16), kt.astype(BF16), _NT, preferred_element_type=F32)
        if nblk > 1:
            shift = two.bit_length() - 1
            rt = lax.broadcasted_iota(I32, (c, c), 0) >> shift
            cs = lax.broadcasted_iota(I32, (c, c), 1) >> shift
            al = jnp.where(rt == cs, al, 0.0)
        acc = al if acc is None else acc + al
        half //= 2
    r8 = lax.broadcasted_iota(I32, (SUBLANES, HG_DIM), 0)
    lane = lax.broadcasted_iota(I32, (SUBLANES, w), 1)
    blocks = []
    for g in range(c // SUBLANES):
        lo = g * SUBLANES
        qg, kg, bg = q[lo:lo + SUBLANES], kk[lo:lo + SUBLANES], b[lo:lo + SUBLANES]
        ag = jnp.zeros((SUBLANES, w), F32)
        for s in range(SUBLANES):
            e = jnp.exp(jnp.where(r8 >= s, bg - bg[s:s + 1, :], NEG_INF))
            p = jnp.sum(qg * kg[s:s + 1, :] * e, axis=1, keepdims=True)
            ag = jnp.where(lane == lo + s, p, ag)
        blocks.append(ag)
    diag = blocks[0] if len(blocks) == 1 else jnp.concatenate(blocks, axis=0)
    return diag if acc is None else acc + diag


def _hgrn_kernel(*refs, c, t_valid, has_s0):
    if has_s0:
        hp_ref, lb_ref, gn_ref, s0_ref, o_ref, sout_ref, st_ref = refs
    else:
        hp_ref, lb_ref, gn_ref, o_ref, sout_ref, st_ref = refs
    ci = pl.program_id(1)
    last = pl.num_programs(1) - 1
    w = max(c, LANES)

    @pl.when(ci == 0)
    def _():
        for h in range(HG_HEADS):
            st_ref[h] = s0_ref[0, h].T if has_s0 else jnp.zeros((HG_DIM, HG_DIM), F32)

    row = lax.broadcasted_iota(I32, (c, HG_DIM), 0)
    for h in range(HG_HEADS):
        lo = h * HG_DIM
        hq = hp_ref[:, lo:lo + HG_DIM]
        hf = hp_ref[:, HG_WIDTH + lo:HG_WIDTH + lo + HG_DIM]
        v = hp_ref[:, 2 * HG_WIDTH + lo:2 * HG_WIDTH + lo + HG_DIM]
        hg = hp_ref[:, 3 * HG_WIDTH + lo:3 * HG_WIDTH + lo + HG_DIM]
        lb = lb_ref[:, lo:lo + HG_DIM]
        q = _silu(hq)
        f = lb + (1.0 - lb) * jax.nn.sigmoid(hf)
        logf = jnp.log(f)
        kk = 1.0 - f
        if t_valid < c:
            valid = row < t_valid
            logf = jnp.where(valid, logf, 0.0)
            kk = jnp.where(valid, kk, 0.0)
            v = jnp.where(valid, v, 0.0)
        b = _cumsum_rows(logf)
        a = _hgrn_intra(q, kk, b, c, w)
        st = st_ref[h]
        bl = b[c - 1:c, :]
        k2 = kk * jnp.exp(bl - b)
        if c < w:
            zpad = jnp.zeros((w - c, HG_DIM), F32)
            vp = jnp.concatenate([v, zpad], axis=0)
            k2 = jnp.concatenate([k2, zpad], axis=0)
        else:
            vp = v
        vb = vp.astype(BF16)
        o = lax.dot_general((q * jnp.exp(b)).astype(BF16), st.astype(BF16), _NT, preferred_element_type=F32)
        o = o + jnp.dot(a.astype(BF16), vb, preferred_element_type=F32)
        o_ref[:, lo:lo + HG_DIM] = (_rms(o, gn_ref[...]) * _silu(hg)).astype(o_ref.dtype)
        st_new = st * jnp.exp(bl) + jnp.dot(vp.T.astype(BF16), k2.astype(BF16), preferred_element_type=F32)
        st_ref[h] = st_new

        @pl.when(ci == last)
        def _():
            sout_ref[0, h] = st_new.T


def _hgrn(hp, lb, gn, s0, batch, t_pad, t_valid, c):
    rows = hp.shape[0]
    nc = t_pad // c
    has_s0 = s0 is not None
    in_specs = [pl.BlockSpec((c, 4 * HG_WIDTH), lambda b, i: (b * nc + i, 0)),
                pl.BlockSpec((1, HG_WIDTH), lambda b, i: (0, 0)),
                pl.BlockSpec((1, HG_DIM), lambda b, i: (0, 0))]
    args = [hp, lb.reshape(1, HG_WIDTH), gn.reshape(1, HG_DIM)]
    if has_s0:
        in_specs.append(pl.BlockSpec((1, HG_HEADS, HG_DIM, HG_DIM), lambda b, i: (b, 0, 0, 0)))
        args.append(s0)
    return pl.pallas_call(
        functools.partial(_hgrn_kernel, c=c, t_valid=min(t_valid, c), has_s0=has_s0),
        grid=(batch, nc),
        in_specs=in_specs,
        out_specs=[pl.BlockSpec((c, HG_WIDTH), lambda b, i: (b * nc + i, 0)),
                   pl.BlockSpec((1, HG_HEADS, HG_DIM, HG_DIM), lambda b, i: (b, 0, 0, 0))],
        out_shape=[jax.ShapeDtypeStruct((rows, HG_WIDTH), BF16),
                   jax.ShapeDtypeStruct((batch, HG_HEADS, HG_DIM, HG_DIM), F32)],
        scratch_shapes=[pltpu.VMEM((HG_HEADS, HG_DIM, HG_DIM), F32)],
        compiler_params=_cparams(("parallel", "arbitrary"), 32),
        name="hgrn2",
    )(*args)


def _topk_block_mask(gate, n_valid, nrow):
    nb = gate.shape[0]
    cnt = jnp.zeros(gate.shape, I32)
    for n2 in range(nb):
        g2 = gate[n2:n2 + 1, :]
        beats = (g2 > gate) | ((g2 == gate) & (n2 < nrow))
        cnt = cnt + jnp.where(beats, (n_valid > n2).astype(I32), 0)
    return (cnt < MB_TOPK) & (nrow < n_valid)


def _moba_prompt_kernel(slope_ref, qt_ref, kt_ref, vt_ref, o_ref, kb_ref, vb_ref, km_ref, sel_ref, *, nb, npairs):
    hg = pl.program_id(1)
    qi = pl.program_id(2)
    tq = MB_BLOCK
    pw = 2 * tq
    width = npairs * pw
    pair_rows = 2 * MB_DIM

    @pl.when(qi == 0)
    def _():
        for pp in range(npairs):
            means = []
            for j in range(nb):
                kj = kt_ref[pp * pair_rows:(pp + 1) * pair_rows, j * MB_BLOCK:(j + 1) * MB_BLOCK].T
                kb_ref[pp, j * MB_BLOCK:(j + 1) * MB_BLOCK, :] = kj.astype(BF16)
                means.append(jnp.sum(kj, axis=0, keepdims=True) * (1.0 / MB_BLOCK))
                vb_ref[pp, j] = vt_ref[pp * pair_rows:(pp + 1) * pair_rows,
                                       j * MB_BLOCK:(j + 1) * MB_BLOCK].astype(BF16)
            km_ref[pp] = jnp.concatenate(means, axis=0)

    second_head = lax.broadcasted_iota(I32, (pair_rows, tq), 0) >= MB_DIM
    qcats, gates, slope_parts = [], [], []
    for pp in range(npairs):
        qt2 = qt_ref[pp * pair_rows:(pp + 1) * pair_rows, :]
        zero = jnp.zeros_like(qt2)
        qcat = jnp.concatenate([jnp.where(second_head, zero, qt2), jnp.where(second_head, qt2, zero)], axis=1)
        qcats.append(qcat)
        gates.append(jnp.dot(km_ref[pp].astype(BF16), qcat, preferred_element_type=F32))
        for i in range(2):
            slope_parts.append(jnp.full((1, tq), slope_ref[(hg * npairs + pp) * 2 + i], F32))
    slope_row = jnp.concatenate(slope_parts, axis=1)
    gate = gates[0] if npairs == 1 else jnp.concatenate(gates, axis=1)
    nrow = lax.broadcasted_iota(I32, (nb, width), 0)
    sel = _topk_block_mask(gate, qi, nrow).astype(F32)
    for n in range(nb):
        sel_ref[n] = jnp.broadcast_to(sel[n:n + 1, :], (SUBLANES, width))

    lane = lax.broadcasted_iota(I32, (MB_BLOCK, width), 1) & (tq - 1)
    rowk = lax.broadcasted_iota(I32, (MB_BLOCK, width), 0)
    base = slope_row * (lane - rowk).astype(F32)

    def scores(j):
        start = pl.multiple_of(j * MB_BLOCK, MB_BLOCK)
        parts = [jnp.dot(kb_ref[pp, pl.ds(start, MB_BLOCK), :], qcats[pp], preferred_element_type=F32)
                 for pp in range(npairs)]
        return (parts[0] if npairs == 1 else jnp.concatenate(parts, axis=1)) - base

    def values(j, p):
        pb = p.astype(BF16)
        return [jnp.dot(vb_ref[pp, j], pb[:, pp * pw:(pp + 1) * pw], preferred_element_type=F32)
                for pp in range(npairs)]

    s = jnp.where(rowk <= lane, scores(qi), NEG_INF)
    m = jnp.max(s, axis=0, keepdims=True)
    p = jnp.exp(s - m)
    l = jnp.sum(p, axis=0, keepdims=True)
    accs = values(qi, p)

    def body(j, carry):
        m, l, accs = carry
        off = jnp.full((1, width), (qi - j) * MB_BLOCK, I32).astype(F32) * slope_row
        s = jnp.where(sel_ref[j][0:1, :] > 0.5, scores(j) - off, NEG_INF)
        m_new = jnp.maximum(m, jnp.max(s, axis=0, keepdims=True))
        alpha = jnp.exp(m - m_new)
        p = jnp.exp(s - m_new)
        l = alpha * l + jnp.sum(p, axis=0, keepdims=True)
        pv = values(j, p)
        accs = [alpha[:, pp * pw:(pp + 1) * pw] * accs[pp] + pv[pp] for pp in range(npairs)]
        return m_new, l, accs

    m, l, accs = lax.fori_loop(0, qi, body, (m, l, accs))
    for pp in range(npairs):
        a = accs[pp] / l[:, pp * pw:(pp + 1) * pw]
        o_ref[pp * pair_rows:(pp + 1) * pair_rows, :] = jnp.where(second_head, a[:, tq:], a[:, :tq])


def _moba_prompt(qt, kt, vt, slopes, batch, t, npairs=MOBA_PAIRS):
    nb = t // MB_BLOCK
    rows = batch * t
    gr = 2 * MB_DIM * npairs
    width = npairs * 2 * MB_BLOCK
    return pl.pallas_call(
        functools.partial(_moba_prompt_kernel, nb=nb, npairs=npairs),
        grid=(batch, MB_WIDTH // gr, nb),
        in_specs=[pl.BlockSpec(memory_space=pltpu.SMEM),
                  pl.BlockSpec((gr, MB_BLOCK), lambda b, h, i: (h, b * nb + i)),
                  pl.BlockSpec((None, gr, t), lambda b, h, i: (b, h, 0)),
                  pl.BlockSpec((None, gr, t), lambda b, h, i: (b, h, 0))],
        out_specs=pl.BlockSpec((gr, MB_BLOCK), lambda b, h, i: (h, b * nb + i)),
        out_shape=jax.ShapeDtypeStruct((MB_WIDTH, rows), F32),
        scratch_shapes=[pltpu.VMEM((npairs, t, 2 * MB_DIM), BF16),
                        pltpu.VMEM((npairs, nb, 2 * MB_DIM, MB_BLOCK), BF16),
                        pltpu.VMEM((npairs, nb, 2 * MB_DIM), F32),
                        pltpu.VMEM((nb, SUBLANES, width), F32)],
        compiler_params=_cparams(("parallel", "parallel", "arbitrary"), 48),
        name="moba_prompt",
    )(slopes, qt, kt, vt)


def _moba_sample_kernel(pt_ref, slope_ref, hm_ref, q_ref, kn_ref, vn_ref, *rest, npages, page, t_new, past_len):
    del pt_ref
    k_pages = rest[:npages]
    v_pages = rest[npages:2 * npages]
    o_ref = rest[2 * npages]
    nrows = t_new * MB_HEADS
    pages_per_block = MB_BLOCK // page
    nb = npages // pages_per_block
    hm = hm_ref[...]
    slope = slope_ref[...][:, 0:1]
    row_q = lax.broadcasted_iota(I32, (nrows, 1), 0) >> 3
    pos_q = past_len + row_q

    q = q_ref[...] * (MB_DIM ** -0.5)
    qbd = jnp.concatenate([jnp.broadcast_to(q[t:t + 1, :], (MB_HEADS, MB_WIDTH)) * hm for t in range(t_new)],
                          axis=0).astype(BF16)

    lane = lax.broadcasted_iota(I32, (nrows, page), 1)
    scores = []
    gates = [None] * nb
    for p in range(npages):
        s = jnp.dot(qbd, k_pages[p][0].astype(BF16), preferred_element_type=F32)
        rs = jnp.sum(s, axis=1, keepdims=True)
        n = p // pages_per_block
        gates[n] = rs if gates[n] is None else gates[n] + rs
        scores.append(s - slope * (pos_q - (p * page + lane)).astype(F32))
    sels = []
    for n in range(nb):
        cnt = jnp.zeros((nrows, 1), I32)
        for n2 in range(nb):
            beats = (gates[n2] > gates[n]) | ((gates[n2] == gates[n]) & (n2 < n))
            cnt = cnt + jnp.where(beats, 1, 0)
        sels.append(jnp.where(cnt < MB_TOPK, 1.0, 0.0))

    zrows = jnp.zeros((LANES - SAMPLE_PAD_T, MB_WIDTH), F32)
    knp = jnp.concatenate([kn_ref[...], zrows], axis=0).astype(BF16)
    vnp = jnp.concatenate([vn_ref[...], zrows], axis=0).astype(BF16)
    lane_n = lax.broadcasted_iota(I32, (nrows, LANES), 1)
    s_new = lax.dot_general(qbd, knp, _NT, preferred_element_type=F32)
    s_new = jnp.where(lane_n <= row_q, s_new - slope * (row_q - lane_n).astype(F32), NEG_INF)
    m = jnp.max(s_new, axis=1, keepdims=True)
    for p in range(npages):
        sel_p = jnp.broadcast_to(sels[p // pages_per_block], (nrows, page)) > 0.5
        scores[p] = jnp.where(sel_p, scores[p], NEG_INF)
        m = jnp.maximum(m, jnp.max(scores[p], axis=1, keepdims=True))

    p_new = jnp.exp(s_new - m)
    l = jnp.sum(p_new, axis=1, keepdims=True)
    acc = jnp.dot(p_new.astype(BF16), vnp, preferred_element_type=F32)
    for p in range(npages):
        pr = jnp.exp(scores[p] - m)
        l = l + jnp.sum(pr, axis=1, keepdims=True)
        acc = acc + lax.dot_general(pr.astype(BF16), v_pages[p][0].astype(BF16), _NT,
                                    preferred_element_type=F32)
    acc = acc / l
    rows = [jnp.sum(acc[t * MB_HEADS:(t + 1) * MB_HEADS, :] * hm, axis=0, keepdims=True) for t in range(t_new)]
    rows.append(jnp.zeros((SAMPLE_PAD_T - t_new, MB_WIDTH), F32))
    o_ref[...] = jnp.concatenate(rows, axis=0)


def _moba_sample(mq, mk, mv, pool_kt, pool_vt, page_table, t_new):
    batch, npages = page_table.shape
    page = pool_kt.shape[2]
    past_len = npages * page
    assert past_len % MB_BLOCK == 0 and MB_BLOCK % page == 0 and t_new <= SAMPLE_PAD_T and page == LANES
    nrows = t_new * MB_HEADS
    slopes = np.power(2.0, -8.0 * np.arange(1, MB_HEADS + 1) / MB_HEADS).astype(np.float32)
    slope_rows = jnp.asarray(np.tile(np.tile(slopes, t_new)[:, None], (1, LANES)))
    head_mask = jnp.asarray((np.arange(MB_WIDTH)[None, :] // MB_DIM == np.arange(MB_HEADS)[:, None])
                            .astype(np.float32))
    new_spec = pl.BlockSpec((SAMPLE_PAD_T, MB_WIDTH), lambda b, pt: (b, 0))
    page_specs = [pl.BlockSpec((1, MB_WIDTH, page),
                               functools.partial(lambda b, pt, i: (pt[b * npages + i], 0, 0), i=i))
                  for i in range(npages)]
    grid_spec = pltpu.PrefetchScalarGridSpec(
        num_scalar_prefetch=1,
        grid=(batch,),
        in_specs=[pl.BlockSpec((nrows, LANES), lambda b, pt: (0, 0)),
                  pl.BlockSpec((MB_HEADS, MB_WIDTH), lambda b, pt: (0, 0)),
                  new_spec, new_spec, new_spec] + page_specs + page_specs,
        out_specs=new_spec,
    )
    return pl.pallas_call(
        functools.partial(_moba_sample_kernel, npages=npages, page=page, t_new=t_new, past_len=past_len),
        grid_spec=grid_spec,
        out_shape=jax.ShapeDtypeStruct((batch * SAMPLE_PAD_T, MB_WIDTH), F32),
        compiler_params=_cparams(("parallel",), 40),
        name="moba_sample",
    )(page_table.reshape(-1), slope_rows, head_mask, mq, mk, mv, *([pool_kt] * npages), *([pool_vt] * npages))


def _outproj_kernel(x_ref, o_ref, att_ref, wo_ref, g_ref, wq_ref, x1_ref, q_ref, *, att_transposed):
    att = att_ref[...]
    if att_transposed:
        att = att.T
    mix = jnp.dot(o_ref[...], wo_ref[:HG_WIDTH, :], preferred_element_type=F32)
    mix = mix + jnp.dot(att.astype(BF16), wo_ref[HG_WIDTH:, :], preferred_element_type=F32)
    x1 = x_ref[...] + mix
    x1_ref[...] = x1
    h = _rms(x1, g_ref[...]).astype(BF16)
    q_ref[...] = (jnp.dot(h, wq_ref[...], preferred_element_type=F32) * (XA_DIM ** -0.5)).astype(q_ref.dtype)


def _outproj(x, o, att, wo_bf, g, wq_bf, att_transposed, q_dtype, tm=ROW_TILE):
    rows, d = x.shape
    att_spec = (pl.BlockSpec((MB_WIDTH, tm), lambda i: (0, i)) if att_transposed
                else pl.BlockSpec((tm, MB_WIDTH), lambda i: (i, 0)))
    return pl.pallas_call(
        functools.partial(_outproj_kernel, att_transposed=att_transposed),
        grid=(rows // tm,),
        in_specs=[pl.BlockSpec((tm, d), lambda i: (i, 0)),
                  pl.BlockSpec((tm, HG_WIDTH), lambda i: (i, 0)),
                  att_spec,
                  pl.BlockSpec((d, d), lambda i: (0, 0)),
                  pl.BlockSpec((1, d), lambda i: (0, 0)),
                  pl.BlockSpec((d, d), lambda i: (0, 0))],
        out_specs=[pl.BlockSpec((tm, d), lambda i: (i, 0)), pl.BlockSpec((tm, d), lambda i: (i, 0))],
        out_shape=[jax.ShapeDtypeStruct((rows, d), F32), jax.ShapeDtypeStruct((rows, d), q_dtype)],
        compiler_params=_cparams(("parallel",), 40),
        name="outproj_q",
    )(x, o, att, wo_bf, g.reshape(1, d), wq_bf)


def _xattn_tiled_kernel(q_ref, k_ref, v_ref, o_ref):
    nt = q_ref.shape[0]
    ndt = XA_DIM // LANES
    nc = XA_HEADS * ndt
    half = XA_HEADS * nt
    q = q_ref[...].astype(F32)
    a = jnp.concatenate([q[:, (h * ndt + dt) * LANES:(h * ndt + dt + 1) * LANES]
                         for dt in range(ndt) for h in range(XA_HEADS)], axis=0).astype(BF16)
    s = lax.dot_general(a, k_ref[...].astype(BF16), _NT, preferred_element_type=F32)
    width = s.shape[1]
    lane_c = lax.broadcasted_iota(I32, s.shape, 1) & (nc - 1)
    assert nt & (nt - 1) == 0 and nc & (nc - 1) == 0
    row_c = lax.broadcasted_iota(I32, s.shape, 0) >> (nt.bit_length() - 1)
    s = jnp.where(lane_c == row_c, s, 0.0)
    tot = s[:half]
    for dt in range(1, ndt):
        tot = tot + pltpu.roll(s[dt * half:(dt + 1) * half], width - dt * XA_HEADS, 1)
    ok = ((lax.broadcasted_iota(I32, tot.shape, 1) & (nc - 1))
          == (lax.broadcasted_iota(I32, tot.shape, 0) >> (nt.bit_length() - 1)))
    tot = jnp.where(ok, tot, NEG_INF)
    p = jnp.exp(tot - jnp.max(tot, axis=-1, keepdims=True))
    p = p / jnp.sum(p, axis=-1, keepdims=True)
    pe = jnp.concatenate([p] + [pltpu.roll(p, dt * XA_HEADS, 1) for dt in range(1, ndt)], axis=0)
    o = jnp.dot(pe.astype(BF16), v_ref[...].astype(BF16), preferred_element_type=F32)
    o_ref[...] = jnp.concatenate([o[(dt * XA_HEADS + h) * nt:(dt * XA_HEADS + h + 1) * nt, :]
                                  for h in range(XA_HEADS) for dt in range(ndt)], axis=1).astype(o_ref.dtype)


def _xattn_kernel(q_ref, k_ref, v_ref, o_ref):
    q = q_ref[...].astype(BF16)
    for h in range(XA_HEADS):
        lo = h * XA_DIM
        kh = k_ref[:, lo:lo + XA_DIM].astype(BF16)
        vh = v_ref[:, lo:lo + XA_DIM].astype(BF16)
        s = lax.dot_general(q[:, lo:lo + XA_DIM], kh, _NT, preferred_element_type=F32)
        p = jnp.exp(s - jnp.max(s, axis=-1, keepdims=True))
        p = p / jnp.sum(p, axis=-1, keepdims=True)
        o_ref[:, lo:lo + XA_DIM] = jnp.dot(p.astype(BF16), vh, preferred_element_type=F32).astype(o_ref.dtype)


def _xattn(q, mem_k, mem_v, batch, t, tq):
    rows, d = q.shape
    nq = t // tq
    tiled = mem_k.ndim == 3
    if tiled:
        kv_spec = pl.BlockSpec((None,) + mem_k.shape[1:], lambda b, i: (b, 0, 0))
    else:
        m = mem_k.shape[0] // batch
        kv_spec = pl.BlockSpec((m, d), lambda b, i: (b, 0))
    return pl.pallas_call(
        _xattn_tiled_kernel if tiled else _xattn_kernel,
        grid=(batch, nq),
        in_specs=[pl.BlockSpec((tq, d), lambda b, i: (b * nq + i, 0)), kv_spec, kv_spec],
        out_specs=pl.BlockSpec((tq, d), lambda b, i: (b * nq + i, 0)),
        out_shape=jax.ShapeDtypeStruct((rows, d), q.dtype),
        compiler_params=_cparams(("parallel", "arbitrary"), 32),
        name="xattn",
    )(q, mem_k, mem_v)


def _xo_router_kernel(x1_ref, o_ref, wo_ref, g_ref, wr_ref, br_ref, x2_ref, h_ref, r_ref):
    x2 = x1_ref[...] + jnp.dot(o_ref[...].astype(BF16), wo_ref[...], preferred_element_type=F32)
    x2_ref[...] = x2
    h = _rms(x2, g_ref[...])
    h_ref[...] = h
    logits = jnp.dot(h.astype(BF16), wr_ref[...], preferred_element_type=F32) + br_ref[...]
    lane = lax.broadcasted_iota(I32, logits.shape, 1)
    big = jnp.int32(LANES)

    def top1(mask):
        mx = jnp.max(jnp.where(mask, logits, NEG_INF), axis=-1, keepdims=True)
        idx = jnp.min(jnp.where(mask & (logits == mx), lane, big), axis=-1, keepdims=True)
        return mx, idx

    gmask = lane < N_GROUPS
    gmx, gsel = top1(gmask)
    gw = 1.0 / jnp.sum(jnp.where(gmask, jnp.exp(logits - gmx), 0.0), axis=-1, keepdims=True)
    elo = N_GROUPS + gsel * EXP_PER_GROUP
    emask = (lane >= elo) & (lane < elo + EXP_PER_GROUP)
    m1, i1 = top1(emask)
    m2, i2 = top1(emask & (lane != i1))
    e2 = jnp.exp(m2 - m1)
    g1 = gw / (1.0 + e2)
    g2 = gw * e2 / (1.0 + e2)
    out = jnp.where(lane == 0, (i1 - N_GROUPS).astype(F32), 0.0)
    out = jnp.where(lane == 1, (i2 - N_GROUPS).astype(F32), out)
    out = jnp.where(lane == 2, g1, out)
    out = jnp.where(lane == 3, g2, out)
    r_ref[...] = out


def _xo_router(x1, o, wxo_bf, g, wr_bf, br, tm=ROW_TILE):
    rows, d = x1.shape
    return pl.pallas_call(
        _xo_router_kernel,
        grid=(rows // tm,),
        in_specs=[pl.BlockSpec((tm, d), lambda i: (i, 0)),
                  pl.BlockSpec((tm, d), lambda i: (i, 0)),
                  pl.BlockSpec((d, d), lambda i: (0, 0)),
                  pl.BlockSpec((1, d), lambda i: (0, 0)),
                  pl.BlockSpec((d, LANES), lambda i: (0, 0)),
                  pl.BlockSpec((1, LANES), lambda i: (0, 0))],
        out_specs=[pl.BlockSpec((tm, d), lambda i: (i, 0)),
                   pl.BlockSpec((tm, d), lambda i: (i, 0)),
                   pl.BlockSpec((tm, LANES), lambda i: (i, 0))],
        out_shape=[jax.ShapeDtypeStruct((rows, d), F32),
                   jax.ShapeDtypeStruct((rows, d), F32),
                   jax.ShapeDtypeStruct((rows, LANES), F32)],
        compiler_params=_cparams(("parallel",), 40),
        name="xo_router",
    )(x1, o, wxo_bf, g.reshape(1, d), wr_bf, br)


def _row_copy(src_hbm, src_row, dst_buf, slot, r, sem):
    return pltpu.make_async_copy(src_hbm.at[pl.ds(src_row, 1)], dst_buf.at[slot, pl.ds(r, 1)], sem.at[slot])


def _moe_kernel(blk_e_ref, row_tok_ref, nused_ref, h_hbm, w1_ref, w3_ref, w2_ref, o_ref, xbuf, sem, *, tm):
    del blk_e_ref
    i = pl.program_id(0)
    nused = nused_ref[0]

    def start_block(blk, slot):
        def body(r, carry):
            _row_copy(h_hbm, row_tok_ref[blk * tm + r], xbuf, slot, r, sem).start()
            return carry
        lax.fori_loop(0, tm, body, 0)

    def wait_block(slot):
        def body(r, carry):
            _row_copy(h_hbm, 0, xbuf, slot, r, sem).wait()
            return carry
        lax.fori_loop(0, tm, body, 0)

    @pl.when(i == 0)
    def _():
        start_block(0, 0)

    @pl.when(i + 1 < nused)
    def _():
        start_block(i + 1, (i + 1) % 2)

    @pl.when(i < nused)
    def _():
        slot = i % 2
        wait_block(slot)
        x = xbuf[slot].astype(BF16)
        a = jnp.dot(x, w1_ref[0], preferred_element_type=F32)
        b = jnp.dot(x, w3_ref[0], preferred_element_type=F32)
        hmid = (_silu(a) * b).astype(BF16)
        o_ref[...] = jnp.dot(hmid, w2_ref[0], preferred_element_type=F32)

    @pl.when(i >= nused)
    def _():
        o_ref[...] = jnp.zeros(o_ref.shape, o_ref.dtype)


def _moe_experts(h, blk_e, row_tok, nused, w1_bf, w3_bf, w2_bf, tm=MOE_TILE):
    nblk = blk_e.shape[0]
    d = h.shape[1]
    wspec = pl.BlockSpec((1, d, d), lambda i, be, rt, nu: (be[i], 0, 0))
    grid_spec = pltpu.PrefetchScalarGridSpec(
        num_scalar_prefetch=3,
        grid=(nblk,),
        in_specs=[pl.BlockSpec(memory_space=pl.ANY), wspec, wspec, wspec],
        out_specs=pl.BlockSpec((tm, d), lambda i, be, rt, nu: (i, 0)),
        scratch_shapes=[pltpu.VMEM((2, tm, d), F32), pltpu.SemaphoreType.DMA((2,))],
    )
    return pl.pallas_call(
        functools.partial(_moe_kernel, tm=tm),
        grid_spec=grid_spec,
        out_shape=jax.ShapeDtypeStruct((nblk * tm, d), F32),
        compiler_params=_cparams(("arbitrary",), 48),
        name="moe_experts",
    )(blk_e, row_tok, nused, h, w1_bf, w3_bf, w2_bf)


def _combine_kernel(dest_ref, x2_ref, r_ref, g_ref, outs_hbm, y_ref, buf0, buf1, sem0, sem1, *, tm):
    i = pl.program_id(0)
    n = pl.num_programs(0)

    def start_tile(tile, slot):
        def body(r, carry):
            base = 2 * (tile * tm + r)
            _row_copy(outs_hbm, dest_ref[base], buf0, slot, r, sem0).start()
            _row_copy(outs_hbm, dest_ref[base + 1], buf1, slot, r, sem1).start()
            return carry
        lax.fori_loop(0, tm, body, 0)

    def wait_tile(slot):
        def body(r, carry):
            _row_copy(outs_hbm, 0, buf0, slot, r, sem0).wait()
            _row_copy(outs_hbm, 0, buf1, slot, r, sem1).wait()
            return carry
        lax.fori_loop(0, tm, body, 0)

    @pl.when(i == 0)
    def _():
        start_tile(0, 0)

    @pl.when(i + 1 < n)
    def _():
        start_tile(i + 1, (i + 1) % 2)

    slot = i % 2
    wait_tile(slot)
    route = r_ref[...]
    y = buf0[slot] * route[:, 2:3] + buf1[slot] * route[:, 3:4]
    y_ref[...] = _rms(x2_ref[...] + y, g_ref[...])


def _combine(x2, route, g_final, outs, dest, tm=ROW_TILE):
    rows, d = x2.shape
    grid_spec = pltpu.PrefetchScalarGridSpec(
        num_scalar_prefetch=1,
        grid=(rows // tm,),
        in_specs=[pl.BlockSpec((tm, d), lambda i, de: (i, 0)),
                  pl.BlockSpec((tm, LANES), lambda i, de: (i, 0)),
                  pl.BlockSpec((1, d), lambda i, de: (0, 0)),
                  pl.BlockSpec(memory_space=pl.ANY)],
        out_specs=pl.BlockSpec((tm, d), lambda i, de: (i, 0)),
        scratch_shapes=[pltpu.VMEM((2, tm, d), F32), pltpu.VMEM((2, tm, d), F32),
                        pltpu.SemaphoreType.DMA((2,)), pltpu.SemaphoreType.DMA((2,))],
    )
    return pl.pallas_call(
        functools.partial(_combine_kernel, tm=tm),
        grid_spec=grid_spec,
        out_shape=jax.ShapeDtypeStruct((rows, d), F32),
        compiler_params=_cparams(("arbitrary",), 32),
        name="moe_combine",
    )(dest, x2, route, g_final.reshape(1, d), outs)


def _route_tables(eid, tm):
    n = eid.shape[0]
    a = n * EXP_TOPK
    flat_e = eid.reshape(-1)
    onehot = (flat_e[:, None] == jnp.arange(N_EXPERTS, dtype=I32)[None, :]).astype(I32)
    csum = jnp.cumsum(onehot, axis=0)
    pos = jnp.take_along_axis(csum, flat_e[:, None], axis=1)[:, 0] - 1
    counts = csum[-1]
    padc = ((counts + tm - 1) // tm) * tm
    pend = jnp.cumsum(padc)
    pstart = pend - padc
    dest = (pstart[flat_e] + pos).astype(I32)
    nblk = -(-(a + N_EXPERTS * (tm - 1)) // tm)
    row_tok = jnp.zeros((nblk * tm,), I32).at[dest].set(jnp.arange(a, dtype=I32) // EXP_TOPK)
    blk_start = jnp.arange(nblk, dtype=I32) * tm
    blk_e = jnp.minimum(jnp.sum((pend[None, :] <= blk_start[:, None]).astype(I32), axis=1), N_EXPERTS - 1)
    nused = (pend[-1] // tm).astype(I32).reshape(1)
    return dest, row_tok, blk_e, nused


def _layer_tail(x, o, att, att_transposed, mem_k, mem_v, batch, t, xq_tile, wts):
    q_dtype = BF16 if xq_tile % (2 * SUBLANES) == 0 else F32
    x1, q = _outproj(x, o, att, wts["w_out"], wts["g_cross"], wts["w_xq"], att_transposed, q_dtype)
    xo = _xattn(q, mem_k, mem_v, batch, t, xq_tile)
    x2, h, route = _xo_router(x1, xo, wts["w_xo"], wts["g_ffn"], wts["w_router"], wts["b_router"])
    return x2, h, route


def _moe_and_final(x2, h, route, valid_rows, wts):
    d = x2.shape[1]
    eid = route[:, 0:2].astype(I32)
    if valid_rows is not None:
        batch, t_pad, t_valid = valid_rows
        h_tok = h.reshape(batch, t_pad, d)[:, :t_valid].reshape(batch * t_valid, d)
        eid_tok = eid.reshape(batch, t_pad, EXP_TOPK)[:, :t_valid].reshape(batch * t_valid, EXP_TOPK)
    else:
        h_tok, eid_tok = h, eid
    dest, row_tok, blk_e, nused = _route_tables(eid_tok, MOE_TILE)
    outs = _moe_experts(h_tok, blk_e, row_tok, nused, wts["w1"], wts["w3"], wts["w2"])
    if valid_rows is not None:
        dest = jnp.pad(dest.reshape(batch, t_valid, EXP_TOPK), ((0, 0), (0, t_pad - t_valid), (0, 0))).reshape(-1)
    return _combine(x2, route, wts["g_final"], outs, dest)


def kernel(x_prompt, x_sample, state_hgrn, cache_moba_k, cache_moba_v, cache_mem_k, cache_mem_v, page_table,
           mem_prompt, g_mix, w_in, hg_lb, hg_norm, w_out, g_cross, g_mem, w_xq, w_xk, w_xv, w_xo, g_ffn,
           w_grp, b_grp, w_exp, b_exp, w1, w3, w2, g_final):
    depth = g_mix.shape[0]
    assert depth == 1
    bp, tp, d = x_prompt.shape
    bs, ts, _ = x_sample.shape
    mem_len = mem_prompt.shape[1]
    l = 0

    lb = jnp.cumsum(jax.nn.softmax(hg_lb.astype(F32), axis=0), axis=0)[l]
    slopes = jnp.asarray(np.power(2.0, -8.0 * np.arange(1, MB_HEADS + 1) / MB_HEADS).astype(np.float32))
    n_router = N_GROUPS + N_EXPERTS
    w_router = jnp.pad(jnp.concatenate([w_grp[l], w_exp[l]], axis=1), ((0, 0), (0, LANES - n_router)))
    b_router = jnp.pad(jnp.concatenate([b_grp[l], b_exp[l]]), (0, LANES - n_router)).reshape(1, LANES)
    wts = {
        "w_out": w_out[l].astype(BF16), "g_cross": g_cross[l], "w_xq": w_xq[l].astype(BF16),
        "w_xo": w_xo[l].astype(BF16), "g_ffn": g_ffn[l], "w_router": w_router.astype(BF16),
        "b_router": b_router.astype(F32), "w1": w1[l].astype(BF16), "w3": w3[l].astype(BF16),
        "w2": w2[l].astype(BF16), "g_final": g_final,
    }
    w_in_bf = w_in[l].astype(BF16)
    o_hq, o_mq, o_mk, o_mv = 0, 4 * HG_WIDTH, 4 * HG_WIDTH + MB_WIDTH, 4 * HG_WIDTH + 2 * MB_WIDTH
    o_end = o_mv + MB_WIDTH

    xp = x_prompt.reshape(bp * tp, d)
    w_kv = jnp.concatenate([w_xk[l], w_xv[l]], axis=1).astype(BF16)
    memk_p, memv_p = _rms_proj(mem_prompt.reshape(bp * mem_len, d), g_mem[l], w_kv,
                               [(0, d, [(1.0, False)]), (d, 2 * d, [(1.0, False)])], [F32, F32])
    hp_p, qt_p, kt_p, vt_p = _rms_proj(
        xp, g_mix[l], w_in_bf,
        [(o_hq, o_mq, [(1.0, False)]), (o_mq, o_mk, [(MB_DIM ** -0.5, True)]), (o_mk, o_mv, [(1.0, "batched")]),
         (o_mv, o_end, [(1.0, "batched")])],
        [F32, BF16, F32, F32], seq_len=tp)
    o_p, s_p = _hgrn(hp_p, lb, hg_norm[l], None, bp, tp, tp, 128)
    att_p = _moba_prompt(qt_p, kt_p, vt_p, slopes, bp, tp)
    x2_p, h_p, route_p = _layer_tail(xp, o_p, att_p, True, memk_p, memv_p, bp, tp, 512, wts)
    y_p = _moe_and_final(x2_p, h_p, route_p, None, wts)

    tpad = SAMPLE_PAD_T
    xs = jnp.pad(x_sample, ((0, 0), (0, tpad - ts), (0, 0))).reshape(bs * tpad, d)
    hp_s, mq_s, mk_s, mv_s = _rms_proj(
        xs, g_mix[l], w_in_bf,
        [(o_hq, o_mq, [(1.0, False)]), (o_mq, o_mk, [(1.0, False)]), (o_mk, o_mv, [(1.0, False)]),
         (o_mv, o_end, [(1.0, False)])],
        [F32, F32, F32, F32])
    o_s, s_s = _hgrn(hp_s, lb, hg_norm[l], state_hgrn[l], bs, tpad, ts, tpad)
    mk4 = mk_s.reshape(bs, tpad, MB_HEADS, MB_DIM)
    mv4 = mv_s.reshape(bs, tpad, MB_HEADS, MB_DIM)
    n_phys, page = cache_moba_k.shape[1], cache_moba_k.shape[2]
    pool_kt = jnp.transpose(cache_moba_k[l], (0, 2, 3, 1)).reshape(n_phys, MB_WIDTH, page)
    pool_vt = jnp.transpose(cache_moba_v[l], (0, 2, 3, 1)).reshape(n_phys, MB_WIDTH, page)
    att_s = _moba_sample(mq_s, mk_s, mv_s, pool_kt, pool_vt, page_table, ts)

    def mem_rows(c):
        c = c[l].reshape(bs, mem_len, XA_HEADS, XA_DIM // LANES, LANES)
        return jnp.transpose(c, (0, 1, 3, 2, 4)).reshape(bs, mem_len * d // LANES, LANES)

    memk_s, memv_s = mem_rows(cache_mem_k), mem_rows(cache_mem_v)
    x2_s, h_s, route_s = _layer_tail(xs, o_s, att_s, False, memk_s, memv_s, bs, tpad, tpad, wts)
    y_s = _moe_and_final(x2_s, h_s, route_s, (bs, tpad, ts), wts)

    return (y_p.reshape(bp, tp, d),
            y_s.reshape(bs, tpad, d)[:, :ts],
            s_p.reshape(1, bp, HG_HEADS, HG_DIM, HG_DIM),
            jnp.transpose(kt_p.reshape(1, bp, MB_HEADS, MB_DIM, tp), (0, 1, 4, 2, 3)),
            jnp.transpose(vt_p.reshape(1, bp, MB_HEADS, MB_DIM, tp), (0, 1, 4, 2, 3)),
            memk_p.reshape(1, bp, mem_len, XA_HEADS, XA_DIM),
            memv_p.reshape(1, bp, mem_len, XA_HEADS, XA_DIM),
            s_s.reshape(1, bs, HG_HEADS, HG_DIM, HG_DIM),
            mk4[:, :ts].reshape(1, bs, ts, MB_HEADS, MB_DIM),
            mv4[:, :ts].reshape(1, bs, ts, MB_HEADS, MB_DIM))
```

```python
import functools

import numpy as np
import jax
import jax.numpy as jnp
from jax import lax
from jax.experimental import pallas as pl
from jax.experimental.pallas import tpu as pltpu

F32 = jnp.float32
BF16 = jnp.bfloat16
I32 = jnp.int32

D_MODEL = 1024
HG_HEADS = 4
HG_DIM = 128
HG_WIDTH = HG_HEADS * HG_DIM
MB_HEADS = 8
MB_DIM = 64
MB_WIDTH = MB_HEADS * MB_DIM
MB_BLOCK = 256
MB_TOPK = 3
XA_HEADS = 4
XA_DIM = 256
N_GROUPS = 4
EXP_PER_GROUP = 8
N_EXPERTS = N_GROUPS * EXP_PER_GROUP
EXP_TOPK = 2
RMS_EPS = 1e-6
NEG_INF = float("-inf")

LANES = 128
SUBLANES = 8
MIB = 1024 * 1024

ROW_TILE = 256
MOE_TILE = 256
SAMPLE_PAD_T = 8
MOBA_PAIRS = 2

_NT = (((1,), (1,)), ((), ()))


def _cparams(semantics, vmem_mib):
    return pltpu.CompilerParams(dimension_semantics=semantics, vmem_limit_bytes=vmem_mib * MIB)


def _rms(x, g):
    return x * lax.rsqrt(jnp.mean(x * x, axis=-1, keepdims=True) + RMS_EPS) * g


def _silu(x):
    return x * jax.nn.sigmoid(x)


def _rms_proj_kernel(x_ref, g_ref, w_ref, *o_refs, segs):
    h = _rms(x_ref[...], g_ref[...]).astype(BF16)
    k = 0
    for lo, hi, outs in segs:
        r = jnp.dot(h, w_ref[:, lo:hi], preferred_element_type=F32)
        for scale, transposed in outs:
            y = r if scale == 1.0 else r * scale
            if transposed:
                y = y.T
            o_refs[k][...] = y.astype(o_refs[k].dtype)
            k += 1


def _rms_proj(x, g, w_bf, segs, dtypes, tm=ROW_TILE, vmem_mib=48, seq_len=None):
    rows, d = x.shape
    assert rows % tm == 0
    out_shape, out_specs = [], []
    k = 0
    for lo, hi, outs in segs:
        n = hi - lo
        for _, transposed in outs:
            if transposed == "batched":
                per_seq = seq_len // tm
                out_shape.append(jax.ShapeDtypeStruct((rows // seq_len, n, seq_len), dtypes[k]))
                out_specs.append(pl.BlockSpec((None, n, tm), lambda i: (i // per_seq, 0, i % per_seq)))
            elif transposed:
                out_shape.append(jax.ShapeDtypeStruct((n, rows), dtypes[k]))
                out_specs.append(pl.BlockSpec((n, tm), lambda i: (0, i)))
            else:
                out_shape.append(jax.ShapeDtypeStruct((rows, n), dtypes[k]))
                out_specs.append(pl.BlockSpec((tm, n), lambda i: (i, 0)))
            k += 1
    return pl.pallas_call(
        functools.partial(_rms_proj_kernel, segs=segs),
        grid=(rows // tm,),
        in_specs=[pl.BlockSpec((tm, d), lambda i: (i, 0)),
                  pl.BlockSpec((1, d), lambda i: (0, 0)),
                  pl.BlockSpec(w_bf.shape, lambda i: (0, 0))],
        out_specs=out_specs,
        out_shape=out_shape,
        compiler_params=_cparams(("parallel",), vmem_mib),
        name="rms_proj",
    )(x, g.reshape(1, d), w_bf)


def _cumsum_rows(x):
    n = x.shape[0]
    row = lax.broadcasted_iota(I32, x.shape, 0)
    s = 1
    while s < n:
        x = x + jnp.where(row >= s, pltpu.roll(x, s, 0), 0.0)
        s *= 2
    return x


def _hgrn_intra(q, kk, b, c, w):
    row = lax.broadcasted_iota(I32, (c, HG_DIM), 0)
    acc = None
    half = c // 2
    while half >= SUBLANES:
        two = 2 * half
        nblk = c // two
        pieces = [jnp.broadcast_to(b[i * two + half - 1:i * two + half, :], (two, HG_DIM)) for i in range(nblk)]
        bm = pieces[0] if nblk == 1 else jnp.concatenate(pieces, axis=0)
        second = (row & (two - 1)) >= half
        qt = jnp.where(second, q * jnp.exp(jnp.where(second, b - bm, 0.0)), 0.0)
        kt = jnp.where(second, 0.0, kk * jnp.exp(jnp.where(second, 0.0, bm - b)))
        al = lax.dot_general(qt.astype(BF16), kt.astype(BF16), _NT, preferred_element_type=F32)
        if nblk > 1:
            shift = two.bit_length() - 1
            rt = lax.broadcasted_iota(I32, (c, c), 0) >> shift
            cs = lax.broadcasted_iota(I32, (c, c), 1) >> shift
            al = jnp.where(rt == cs, al, 0.0)
        acc = al if acc is None else acc + al
        half //= 2
    r8 = lax.broadcasted_iota(I32, (SUBLANES, HG_DIM), 0)
    lane = lax.broadcasted_iota(I32, (SUBLANES, w), 1)
    blocks = []
    for g in range(c // SUBLANES):
        lo = g * SUBLANES
        qg, kg, bg = q[lo:lo + SUBLANES], kk[lo:lo + SUBLANES], b[lo:lo + SUBLANES]
        ag = jnp.zeros((SUBLANES, w), F32)
        for s in range(SUBLANES):
            e = jnp.exp(jnp.where(r8 >= s, bg - bg[s:s + 1, :], NEG_INF))
            p = jnp.sum(qg * kg[s:s + 1, :] * e, axis=1, keepdims=True)
            ag = jnp.where(lane == lo + s, p, ag)
        blocks.append(ag)
    diag = blocks[0] if len(blocks) == 1 else jnp.concatenate(blocks, axis=0)
    return diag if acc is None else acc + diag


def _hgrn_kernel(*refs, c, t_valid, has_s0):
    if has_s0:
        hp_ref, lb_ref, gn_ref, s0_ref, o_ref, sout_ref, st_ref = refs
    else:
        hp_ref, lb_ref, gn_ref, o_ref, sout_ref, st_ref = refs
    ci = pl.program_id(1)
    last = pl.num_programs(1) - 1
    w = max(c, LANES)

    @pl.when(ci == 0)
    def _():
        for h in range(HG_HEADS):
            st_ref[h] = s0_ref[0, h].T if has_s0 else jnp.zeros((HG_DIM, HG_DIM), F32)

    row = lax.broadcasted_iota(I32, (c, HG_DIM), 0)
    for h in range(HG_HEADS):
        lo = h * HG_DIM
        hq = hp_ref[:, lo:lo + HG_DIM]
        hf = hp_ref[:, HG_WIDTH + lo:HG_WIDTH + lo + HG_DIM]
        v = hp_ref[:, 2 * HG_WIDTH + lo:2 * HG_WIDTH + lo + HG_DIM]
        hg = hp_ref[:, 3 * HG_WIDTH + lo:3 * HG_WIDTH + lo + HG_DIM]
        lb = lb_ref[:, lo:lo + HG_DIM]
        q = _silu(hq)
        f = lb + (1.0 - lb) * jax.nn.sigmoid(hf)
        logf = jnp.log(f)
        kk = 1.0 - f
        if t_valid < c:
            valid = row < t_valid
            logf = jnp.where(valid, logf, 0.0)
            kk = jnp.where(valid, kk, 0.0)
            v = jnp.where(valid, v, 0.0)
        b = _cumsum_rows(logf)
        a = _hgrn_intra(q, kk, b, c, w)
        st = st_ref[h]
        bl = b[c - 1:c, :]
        k2 = kk * jnp.exp(bl - b)
        if c < w:
            zpad = jnp.zeros((w - c, HG_DIM), F32)
            vp = jnp.concatenate([v, zpad], axis=0)
            k2 = jnp.concatenate([k2, zpad], axis=0)
        else:
            vp = v
        vb = vp.astype(BF16)
        o = lax.dot_general((q * jnp.exp(b)).astype(BF16), st.astype(BF16), _NT, preferred_element_type=F32)
        o = o + jnp.dot(a.astype(BF16), vb, preferred_element_type=F32)
        o_ref[:, lo:lo + HG_DIM] = (_rms(o, gn_ref[...]) * _silu(hg)).astype(o_ref.dtype)
        st_new = st * jnp.exp(bl) + jnp.dot(vp.T.astype(BF16), k2.astype(BF16), preferred_element_type=F32)
        st_ref[h] = st_new

        @pl.when(ci == last)
        def _():
            sout_ref[0, h] = st_new.T


def _hgrn(hp, lb, gn, s0, batch, t_pad, t_valid, c):
    rows = hp.shape[0]
    nc = t_pad // c
    has_s0 = s0 is not None
    in_specs = [pl.BlockSpec((c, 4 * HG_WIDTH), lambda b, i: (b * nc + i, 0)),
                pl.BlockSpec((1, HG_WIDTH), lambda b, i: (0, 0)),
                pl.BlockSpec((1, HG_DIM), lambda b, i: (0, 0))]
    args = [hp, lb.reshape(1, HG_WIDTH), gn.reshape(1, HG_DIM)]
    if has_s0:
        in_specs.append(pl.BlockSpec((1, HG_HEADS, HG_DIM, HG_DIM), lambda b, i: (b, 0, 0, 0)))
        args.append(s0)
    return pl.pallas_call(
        functools.partial(_hgrn_kernel, c=c, t_valid=min(t_valid, c), has_s0=has_s0),
        grid=(batch, nc),
        in_specs=in_specs,
        out_specs=[pl.BlockSpec((c, HG_WIDTH), lambda b, i: (b * nc + i, 0)),
                   pl.BlockSpec((1, HG_HEADS, HG_DIM, HG_DIM), lambda b, i: (b, 0, 0, 0))],
        out_shape=[jax.ShapeDtypeStruct((rows, HG_WIDTH), BF16),
                   jax.ShapeDtypeStruct((batch, HG_HEADS, HG_DIM, HG_DIM), F32)],
        scratch_shapes=[pltpu.VMEM((HG_HEADS, HG_DIM, HG_DIM), F32)],
        compiler_params=_cparams(("parallel", "arbitrary"), 32),
        name="hgrn2",
    )(*args)


def _topk_block_mask(gate, n_valid, nrow):
    nb = gate.shape[0]
    cnt = jnp.zeros(gate.shape, I32)
    for n2 in range(nb):
        g2 = gate[n2:n2 + 1, :]
        beats = (g2 > gate) | ((g2 == gate) & (n2 < nrow))
        cnt = cnt + jnp.where(beats, (n_valid > n2).astype(I32), 0)
    return (cnt < MB_TOPK) & (nrow < n_valid)


def _moba_prompt_kernel(slope_ref, qt_ref, kt_ref, vt_ref, o_ref, kb_ref, vb_ref, km_ref, sel_ref, *, nb, npairs):
    hg = pl.program_id(1)
    qi = pl.program_id(2)
    tq = MB_BLOCK
    pw = 2 * tq
    width = npairs * pw
    pair_rows = 2 * MB_DIM

    @pl.when(qi == 0)
    def _():
        for pp in range(npairs):
            means = []
            for j in range(nb):
                kj = kt_ref[pp * pair_rows:(pp + 1) * pair_rows, j * MB_BLOCK:(j + 1) * MB_BLOCK].T
                kb_ref[pp, j * MB_BLOCK:(j + 1) * MB_BLOCK, :] = kj.astype(BF16)
                means.append(jnp.sum(kj, axis=0, keepdims=True) * (1.0 / MB_BLOCK))
                vb_ref[pp, j] = vt_ref[pp * pair_rows:(pp + 1) * pair_rows,
                                       j * MB_BLOCK:(j + 1) * MB_BLOCK].astype(BF16)
            km_ref[pp] = jnp.concatenate(means, axis=0)

    second_head = lax.broadcasted_iota(I32, (pair_rows, tq), 0) >= MB_DIM
    qcats, gates, slope_parts = [], [], []
    for pp in range(npairs):
        qt2 = qt_ref[pp * pair_rows:(pp + 1) * pair_rows, :]
        zero = jnp.zeros_like(qt2)
        qcat = jnp.concatenate([jnp.where(second_head, zero, qt2), jnp.where(second_head, qt2, zero)], axis=1)
        qcats.append(qcat)
        gates.append(jnp.dot(km_ref[pp].astype(BF16), qcat, preferred_element_type=F32))
        for i in range(2):
            slope_parts.append(jnp.full((1, tq), slope_ref[(hg * npairs + pp) * 2 + i], F32))
    slope_row = jnp.concatenate(slope_parts, axis=1)
    gate = gates[0] if npairs == 1 else jnp.concatenate(gates, axis=1)
    nrow = lax.broadcasted_iota(I32, (nb, width), 0)
    sel = _topk_block_mask(gate, qi, nrow).astype(F32)
    for n in range(nb):
        sel_ref[n] = jnp.broadcast_to(sel[n:n + 1, :], (SUBLANES, width))

    lane = lax.broadcasted_iota(I32, (MB_BLOCK, width), 1) & (tq - 1)
    rowk = lax.broadcasted_iota(I32, (MB_BLOCK, width), 0)
    base = slope_row * (lane - rowk).astype(F32)

    def scores(j):
        start = pl.multiple_of(j * MB_BLOCK, MB_BLOCK)
        parts = [jnp.dot(kb_ref[pp, pl.ds(start, MB_BLOCK), :], qcats[pp], preferred_element_type=F32)
                 for pp in range(npairs)]
        return (parts[0] if npairs == 1 else jnp.concatenate(parts, axis=1)) - base

    def values(j, p):
        pb = p.astype(BF16)
        return [jnp.dot(vb_ref[pp, j], pb[:, pp * pw:(pp + 1) * pw], preferred_element_type=F32)
                for pp in range(npairs)]

    s = jnp.where(rowk <= lane, scores(qi), NEG_INF)
    m = jnp.max(s, axis=0, keepdims=True)
    p = jnp.exp(s - m)
    l = jnp.sum(p, axis=0, keepdims=True)
    accs = values(qi, p)

    def body(j, carry):
        m, l, accs = carry
        off = jnp.full((1, width), (qi - j) * MB_BLOCK, I32).astype(F32) * slope_row
        s = jnp.where(sel_ref[j][0:1, :] > 0.5, scores(j) - off, NEG_INF)
        m_new = jnp.maximum(m, jnp.max(s, axis=0, keepdims=True))
        alpha = jnp.exp(m - m_new)
        p = jnp.exp(s - m_new)
        l = alpha * l + jnp.sum(p, axis=0, keepdims=True)
        pv = values(j, p)
        accs = [alpha[:, pp * pw:(pp + 1) * pw] * accs[pp] + pv[pp] for pp in range(npairs)]
        return m_new, l, accs

    m, l, accs = lax.fori_loop(0, qi, body, (m, l, accs))
    for pp in range(npairs):
        a = accs[pp] / l[:, pp * pw:(pp + 1) * pw]
        o_ref[pp * pair_rows:(pp + 1) * pair_rows, :] = jnp.where(second_head, a[:, tq:], a[:, :tq])


def _moba_prompt(qt, kt, vt, slopes, batch, t, npairs=MOBA_PAIRS):
    nb = t // MB_BLOCK
    rows = batch * t
    gr = 2 * MB_DIM * npairs
    width = npairs * 2 * MB_BLOCK
    return pl.pallas_call(
        functools.partial(_moba_prompt_kernel, nb=nb, npairs=npairs),
        grid=(batch, MB_WIDTH // gr, nb),
        in_specs=[pl.BlockSpec(memory_space=pltpu.SMEM),
                  pl.BlockSpec((gr, MB_BLOCK), lambda b, h, i: (h, b * nb + i)),
                  pl.BlockSpec((None, gr, t), lambda b, h, i: (b, h, 0)),
                  pl.BlockSpec((None, gr, t), lambda b, h, i: (b, h, 0))],
        out_specs=pl.BlockSpec((gr, MB_BLOCK), lambda b, h, i: (h, b * nb + i)),
        out_shape=jax.ShapeDtypeStruct((MB_WIDTH, rows), F32),
        scratch_shapes=[pltpu.VMEM((npairs, t, 2 * MB_DIM), BF16),
                        pltpu.VMEM((npairs, nb, 2 * MB_DIM, MB_BLOCK), BF16),
                        pltpu.VMEM((npairs, nb, 2 * MB_DIM), F32),
                        pltpu.VMEM((nb, SUBLANES, width), F32)],
        compiler_params=_cparams(("parallel", "parallel", "arbitrary"), 48),
        name="moba_prompt",
    )(slopes, qt, kt, vt)


def _moba_sample_kernel(pt_ref, slope_ref, hm_ref, q_ref, kn_ref, vn_ref, *rest, npages, page, t_new, past_len):
    del pt_ref
    k_pages = rest[:npages]
    v_pages = rest[npages:2 * npages]
    o_ref = rest[2 * npages]
    nrows = t_new * MB_HEADS
    pages_per_block = MB_BLOCK // page
    nb = npages // pages_per_block
    hm = hm_ref[...]
    slope = slope_ref[...][:, 0:1]
    row_q = lax.broadcasted_iota(I32, (nrows, 1), 0) >> 3
    pos_q = past_len + row_q

    q = q_ref[...] * (MB_DIM ** -0.5)
    qbd = jnp.concatenate([jnp.broadcast_to(q[t:t + 1, :], (MB_HEADS, MB_WIDTH)) * hm for t in range(t_new)],
                          axis=0).astype(BF16)

    lane = lax.broadcasted_iota(I32, (nrows, page), 1)
    scores = []
    gates = [None] * nb
    for p in range(npages):
        s = jnp.dot(qbd, k_pages[p][0].astype(BF16), preferred_element_type=F32)
        rs = jnp.sum(s, axis=1, keepdims=True)
        n = p // pages_per_block
        gates[n] = rs if gates[n] is None else gates[n] + rs
        scores.append(s - slope * (pos_q - (p * page + lane)).astype(F32))
    sels = []
    for n in range(nb):
        cnt = jnp.zeros((nrows, 1), I32)
        for n2 in range(nb):
            beats = (gates[n2] > gates[n]) | ((gates[n2] == gates[n]) & (n2 < n))
            cnt = cnt + jnp.where(beats, 1, 0)
        sels.append(jnp.where(cnt < MB_TOPK, 1.0, 0.0))

    zrows = jnp.zeros((LANES - SAMPLE_PAD_T, MB_WIDTH), F32)
    knp = jnp.concatenate([kn_ref[...], zrows], axis=0).astype(BF16)
    vnp = jnp.concatenate([vn_ref[...], zrows], axis=0).astype(BF16)
    lane_n = lax.broadcasted_iota(I32, (nrows, LANES), 1)
    s_new = lax.dot_general(qbd, knp, _NT, preferred_element_type=F32)
    s_new = jnp.where(lane_n <= row_q, s_new - slope * (row_q - lane_n).astype(F32), NEG_INF)
    m = jnp.max(s_new, axis=1, keepdims=True)
    for p in range(npages):
        sel_p = jnp.broadcast_to(sels[p // pages_per_block], (nrows, page)) > 0.5
        scores[p] = jnp.where(sel_p, scores[p], NEG_INF)
        m = jnp.maximum(m, jnp.max(scores[p], axis=1, keepdims=True))

    p_new = jnp.exp(s_new - m)
    l = jnp.sum(p_new, axis=1, keepdims=True)
    acc = jnp.dot(p_new.astype(BF16), vnp, preferred_element_type=F32)
    for p in range(npages):
        pr = jnp.exp(scores[p] - m)
        l = l + jnp.sum(pr, axis=1, keepdims=True)
        acc = acc + lax.dot_general(pr.astype(BF16), v_pages[p][0].astype(BF16), _NT,
                                    preferred_element_type=F32)
    acc = acc / l
    rows = [jnp.sum(acc[t * MB_HEADS:(t + 1) * MB_HEADS, :] * hm, axis=0, keepdims=True) for t in range(t_new)]
    rows.append(jnp.zeros((SAMPLE_PAD_T - t_new, MB_WIDTH), F32))
    o_ref[...] = jnp.concatenate(rows, axis=0)


def _moba_sample(mq, mk, mv, pool_kt, pool_vt, page_table, t_new):
    batch, npages = page_table.shape
    page = pool_kt.shape[2]
    past_len = npages * page
    assert past_len % MB_BLOCK == 0 and MB_BLOCK % page == 0 and t_new <= SAMPLE_PAD_T and page == LANES
    nrows = t_new * MB_HEADS
    slopes = np.power(2.0, -8.0 * np.arange(1, MB_HEADS + 1) / MB_HEADS).astype(np.float32)
    slope_rows = jnp.asarray(np.tile(np.tile(slopes, t_new)[:, None], (1, LANES)))
    head_mask = jnp.asarray((np.arange(MB_WIDTH)[None, :] // MB_DIM == np.arange(MB_HEADS)[:, None])
                            .astype(np.float32))
    new_spec = pl.BlockSpec((SAMPLE_PAD_T, MB_WIDTH), lambda b, pt: (b, 0))
    page_specs = [pl.BlockSpec((1, MB_WIDTH, page),
                               functools.partial(lambda b, pt, i: (pt[b * npages + i], 0, 0), i=i))
                  for i in range(npages)]
    grid_spec = pltpu.PrefetchScalarGridSpec(
        num_scalar_prefetch=1,
        grid=(batch,),
        in_specs=[pl.BlockSpec((nrows, LANES), lambda b, pt: (0, 0)),
                  pl.BlockSpec((MB_HEADS, MB_WIDTH), lambda b, pt: (0, 0)),
                  new_spec, new_spec, new_spec] + page_specs + page_specs,
        out_specs=new_spec,
    )
    return pl.pallas_call(
        functools.partial(_moba_sample_kernel, npages=npages, page=page, t_new=t_new, past_len=past_len),
        grid_spec=grid_spec,
        out_shape=jax.ShapeDtypeStruct((batch * SAMPLE_PAD_T, MB_WIDTH), F32),
        compiler_params=_cparams(("parallel",), 40),
        name="moba_sample",
    )(page_table.reshape(-1), slope_rows, head_mask, mq, mk, mv, *([pool_kt] * npages), *([pool_vt] * npages))


def _outproj_kernel(x_ref, o_ref, att_ref, wo_ref, g_ref, wq_ref, x1_ref, q_ref, *, att_transposed):
    att = att_ref[...]
    if att_transposed:
        att = att.T
    mix = jnp.dot(o_ref[...], wo_ref[:HG_WIDTH, :], preferred_element_type=F32)
    mix = mix + jnp.dot(att.astype(BF16), wo_ref[HG_WIDTH:, :], preferred_element_type=F32)
    x1 = x_ref[...] + mix
    x1_ref[...] = x1
    h = _rms(x1, g_ref[...]).astype(BF16)
    q_ref[...] = (jnp.dot(h, wq_ref[...], preferred_element_type=F32) * (XA_DIM ** -0.5)).astype(q_ref.dtype)


def _outproj(x, o, att, wo_bf, g, wq_bf, att_transposed, q_dtype, tm=ROW_TILE):
    rows, d = x.shape
    att_spec = (pl.BlockSpec((MB_WIDTH, tm), lambda i: (0, i)) if att_transposed
                else pl.BlockSpec((tm, MB_WIDTH), lambda i: (i, 0)))
    return pl.pallas_call(
        functools.partial(_outproj_kernel, att_transposed=att_transposed),
        grid=(rows // tm,),
        in_specs=[pl.BlockSpec((tm, d), lambda i: (i, 0)),
                  pl.BlockSpec((tm, HG_WIDTH), lambda i: (i, 0)),
                  att_spec,
                  pl.BlockSpec((d, d), lambda i: (0, 0)),
                  pl.BlockSpec((1, d), lambda i: (0, 0)),
                  pl.BlockSpec((d, d), lambda i: (0, 0))],
        out_specs=[pl.BlockSpec((tm, d), lambda i: (i, 0)), pl.BlockSpec((tm, d), lambda i: (i, 0))],
        out_shape=[jax.ShapeDtypeStruct((rows, d), F32), jax.ShapeDtypeStruct((rows, d), q_dtype)],
        compiler_params=_cparams(("parallel",), 40),
        name="outproj_q",
    )(x, o, att, wo_bf, g.reshape(1, d), wq_bf)


def _xattn_tiled_kernel(q_ref, k_ref, v_ref, o_ref):
    nt = q_ref.shape[0]
    ndt = XA_DIM // LANES
    nc = XA_HEADS * ndt
    half = XA_HEADS * nt
    q = q_ref[...].astype(F32)
    a = jnp.concatenate([q[:, (h * ndt + dt) * LANES:(h * ndt + dt + 1) * LANES]
                         for dt in range(ndt) for h in range(XA_HEADS)], axis=0).astype(BF16)
    s = lax.dot_general(a, k_ref[...].astype(BF16), _NT, preferred_element_type=F32)
    width = s.shape[1]
    lane_c = lax.broadcasted_iota(I32, s.shape, 1) & (nc - 1)
    assert nt & (nt - 1) == 0 and nc & (nc - 1) == 0
    row_c = lax.broadcasted_iota(I32, s.shape, 0) >> (nt.bit_length() - 1)
    s = jnp.where(lane_c == row_c, s, 0.0)
    tot = s[:half]
    for dt in range(1, ndt):
        tot = tot + pltpu.roll(s[dt * half:(dt + 1) * half], width - dt * XA_HEADS, 1)
    ok = ((lax.broadcasted_iota(I32, tot.shape, 1) & (nc - 1))
          == (lax.broadcasted_iota(I32, tot.shape, 0) >> (nt.bit_length() - 1)))
    tot = jnp.where(ok, tot, NEG_INF)
    p = jnp.exp(tot - jnp.max(tot, axis=-1, keepdims=True))
    p = p / jnp.sum(p, axis=-1, keepdims=True)
    pe = jnp.concatenate([p] + [pltpu.roll(p, dt * XA_HEADS, 1) for dt in range(1, ndt)], axis=0)
    o = jnp.dot(pe.astype(BF16), v_ref[...].astype(BF16), preferred_element_type=F32)
    o_ref[...] = jnp.concatenate([o[(dt * XA_HEADS + h) * nt:(dt * XA_HEADS + h + 1) * nt, :]
                                  for h in range(XA_HEADS) for dt in range(ndt)], axis=1).astype(o_ref.dtype)


def _xattn_kernel(q_ref, k_ref, v_ref, o_ref):
    q = q_ref[...].astype(BF16)
    for h in range(XA_HEADS):
        lo = h * XA_DIM
        kh = k_ref[:, lo:lo + XA_DIM].astype(BF16)
        vh = v_ref[:, lo:lo + XA_DIM].astype(BF16)
        s = lax.dot_general(q[:, lo:lo + XA_DIM], kh, _NT, preferred_element_type=F32)
        p = jnp.exp(s - jnp.max(s, axis=-1, keepdims=True))
        p = p / jnp.sum(p, axis=-1, keepdims=True)
        o_ref[:, lo:lo + XA_DIM] = jnp.dot(p.astype(BF16), vh, preferred_element_type=F32).astype(o_ref.dtype)


def _xattn(q, mem_k, mem_v, batch, t, tq):
    rows, d = q.shape
    nq = t // tq
    tiled = mem_k.ndim == 3
    if tiled:
        kv_spec = pl.BlockSpec((None,) + mem_k.shape[1:], lambda b, i: (b, 0, 0))
    else:
        m = mem_k.shape[0] // batch
        kv_spec = pl.BlockSpec((m, d), lambda b, i: (b, 0))
    return pl.pallas_call(
        _xattn_tiled_kernel if tiled else _xattn_kernel,
        grid=(batch, nq),
        in_specs=[pl.BlockSpec((tq, d), lambda b, i: (b * nq + i, 0)), kv_spec, kv_spec],
        out_specs=pl.BlockSpec((tq, d), lambda b, i: (b * nq + i, 0)),
        out_shape=jax.ShapeDtypeStruct((rows, d), q.dtype),
        compiler_params=_cparams(("parallel", "arbitrary"), 32),
        name="xattn",
    )(q, mem_k, mem_v)


def _xo_router_kernel(x1_ref, o_ref, wo_ref, g_ref, wr_ref, br_ref, x2_ref, h_ref, r_ref):
    x2 = x1_ref[...] + jnp.dot(o_ref[...].astype(BF16), wo_ref[...], preferred_element_type=F32)
    x2_ref[...] = x2
    h = _rms(x2, g_ref[...])
    h_ref[...] = h
    logits = jnp.dot(h.astype(BF16), wr_ref[...], preferred_element_type=F32) + br_ref[...]
    lane = lax.broadcasted_iota(I32, logits.shape, 1)
    big = jnp.int32(LANES)

    def top1(mask):
        mx = jnp.max(jnp.where(mask, logits, NEG_INF), axis=-1, keepdims=True)
        idx = jnp.min(jnp.where(mask & (logits == mx), lane, big), axis=-1, keepdims=True)
        return mx, idx

    gmask = lane < N_GROUPS
    gmx, gsel = top1(gmask)
    gw = 1.0 / jnp.sum(jnp.where(gmask, jnp.exp(logits - gmx), 0.0), axis=-1, keepdims=True)
    elo = N_GROUPS + gsel * EXP_PER_GROUP
    emask = (lane >= elo) & (lane < elo + EXP_PER_GROUP)
    m1, i1 = top1(emask)
    m2, i2 = top1(emask & (lane != i1))
    e2 = jnp.exp(m2 - m1)
    g1 = gw / (1.0 + e2)
    g2 = gw * e2 / (1.0 + e2)
    out = jnp.where(lane == 0, (i1 - N_GROUPS).astype(F32), 0.0)
    out = jnp.where(lane == 1, (i2 - N_GROUPS).astype(F32), out)
    out = jnp.where(lane == 2, g1, out)
    out = jnp.where(lane == 3, g2, out)
    r_ref[...] = out


def _xo_router(x1, o, wxo_bf, g, wr_bf, br, tm=ROW_TILE):
    rows, d = x1.shape
    return pl.pallas_call(
        _xo_router_kernel,
        grid=(rows // tm,),
        in_specs=[pl.BlockSpec((tm, d), lambda i: (i, 0)),
                  pl.BlockSpec((tm, d), lambda i: (i, 0)),
                  pl.BlockSpec((d, d), lambda i: (0, 0)),
                  pl.BlockSpec((1, d), lambda i: (0, 0)),
                  pl.BlockSpec((d, LANES), lambda i: (0, 0)),
                  pl.BlockSpec((1, LANES), lambda i: (0, 0))],
        out_specs=[pl.BlockSpec((tm, d), lambda i: (i, 0)),
                   pl.BlockSpec((tm, d), lambda i: (i, 0)),
                   pl.BlockSpec((tm, LANES), lambda i: (i, 0))],
        out_shape=[jax.ShapeDtypeStruct((rows, d), F32),
                   jax.ShapeDtypeStruct((rows, d), F32),
                   jax.ShapeDtypeStruct((rows, LANES), F32)],
        compiler_params=_cparams(("parallel",), 40),
        name="xo_router",
    )(x1, o, wxo_bf, g.reshape(1, d), wr_bf, br)


def _lane_cumsum(x):
    lane = lax.broadcasted_iota(I32, x.shape, 1)
    s = 1
    while s < LANES:
        x = x + jnp.where(lane >= s, pltpu.roll(x, s, 1), 0.0)
        s *= 2
    return x


def _route_dest_kernel(rp_ref, rs_ref, dest_ref, tab_ref, r_buf, cnt_ref, carry_ref, pstart_ref, *,
                       n_prompt_chunks, t_pad, t_valid, tm, rows_cap, nblk_lanes):
    ph = pl.program_id(0)
    c = pl.program_id(1)
    chunk = r_buf.shape[0]
    is_prompt = c < n_prompt_chunks

    @pl.when(is_prompt)
    def _():
        r_buf[...] = rp_ref[...]

    @pl.when(jnp.logical_not(is_prompt))
    def _():
        r_buf[...] = rs_ref[...]

    @pl.when((ph == 0) & (c == 0))
    def _():
        cnt_ref[...] = jnp.zeros(cnt_ref.shape, F32)

    route = r_buf[...]
    lane = lax.broadcasted_iota(I32, (chunk, LANES), 1)
    row = lax.broadcasted_iota(I32, (chunk, LANES), 0)
    lanef = lane.astype(F32)
    valid = (((row & (t_pad - 1)) < t_valid).astype(I32) | is_prompt.astype(I32)) > 0
    oh0 = jnp.where((lanef == route[:, 0:1]) & valid, 1.0, 0.0)
    oh1 = jnp.where((lanef == route[:, 1:2]) & valid, 1.0, 0.0)
    cmat = oh0 + oh1
    csum = jnp.sum(cmat, axis=0, keepdims=True)

    @pl.when(ph == 0)
    def _():
        cnt_ref[...] = cnt_ref[...] + csum

    @pl.when((ph == 1) & (c == 0))
    def _():
        cnt = jnp.broadcast_to(cnt_ref[...], (SUBLANES, LANES))
        padc = jnp.floor((cnt + (tm - 1)) * (1.0 / tm)) * tm
        pend = _lane_cumsum(padc)
        pstart_ref[...] = (pend - padc)[0:1, :]
        carry_ref[...] = jnp.zeros(carry_ref.shape, F32)
        starts = lax.broadcasted_iota(I32, (SUBLANES, nblk_lanes), 1).astype(F32) * tm
        blk = jnp.zeros((SUBLANES, nblk_lanes), F32)
        for e in range(N_EXPERTS):
            blk = blk + jnp.where(pend[:, e:e + 1] <= starts, 1.0, 0.0)
        blk = jnp.minimum(blk, N_EXPERTS - 1.0)
        nused = jnp.broadcast_to(pend[:, N_EXPERTS - 1:N_EXPERTS] * (1.0 / tm), (SUBLANES, nblk_lanes))
        sub = lax.broadcasted_iota(I32, (SUBLANES, nblk_lanes), 0)
        tab_ref[...] = jnp.where(sub == 0, blk, nused)

    @pl.when(ph == 1)
    def _():
        rt = lax.broadcasted_iota(I32, (chunk, chunk), 0)
        cs = lax.broadcasted_iota(I32, (chunk, chunk), 1)
        tri = jnp.where(rt > cs, 1.0, 0.0).astype(BF16)
        before = jnp.dot(tri, cmat.astype(BF16), preferred_element_type=F32) + carry_ref[...]
        base = before + pstart_ref[...]
        d0 = jnp.sum(base * oh0, axis=1, keepdims=True)
        d1 = jnp.sum(base * oh1, axis=1, keepdims=True)
        spare = (rows_cap + 2 * ((c - n_prompt_chunks) * chunk + row)).astype(F32)
        out = jnp.where(lane == 0, d0, d1)
        out = jnp.where(valid, out, spare + lanef)
        dest_ref[...] = jnp.where(lane < EXP_TOPK, out, 0.0)
        carry_ref[...] = carry_ref[...] + csum


def _route_dest(route_p, route_s, t_pad, t_valid, tm, nblk, chunk=ROW_TILE):
    rp, rs = route_p.shape[0], route_s.shape[0]
    assert rp % chunk == 0 and rs % chunk == 0 and chunk % t_pad == 0
    npc, nsc = rp // chunk, rs // chunk
    rows_cap = nblk * tm
    nblk_lanes = -(-nblk // LANES) * LANES
    dest_f, tab = pl.pallas_call(
        functools.partial(_route_dest_kernel, n_prompt_chunks=npc, t_pad=t_pad, t_valid=t_valid, tm=tm,
                          rows_cap=rows_cap, nblk_lanes=nblk_lanes),
        grid=(2, npc + nsc),
        in_specs=[pl.BlockSpec((chunk, LANES), lambda ph, c: (jnp.minimum(c, npc - 1), 0)),
                  pl.BlockSpec((chunk, LANES), lambda ph, c: (jnp.maximum(c - npc, 0), 0))],
        out_specs=[pl.BlockSpec((chunk, LANES), lambda ph, c: (c * ph, 0)),
                   pl.BlockSpec((SUBLANES, nblk_lanes), lambda ph, c: (0, 0))],
        out_shape=[jax.ShapeDtypeStruct((rp + rs, LANES), F32),
                   jax.ShapeDtypeStruct((SUBLANES, nblk_lanes), F32)],
        scratch_shapes=[pltpu.VMEM((chunk, LANES), F32), pltpu.VMEM((1, LANES), F32),
                        pltpu.VMEM((1, LANES), F32), pltpu.VMEM((1, LANES), F32)],
        compiler_params=_cparams(("arbitrary", "arbitrary"), 32),
        name="route_dest",
    )(route_p, route_s)
    dest = dest_f[:, :EXP_TOPK].astype(I32).reshape(-1)
    blk_e = tab[0, :nblk].astype(I32)
    nused = tab[1, :1].astype(I32)
    return dest, blk_e, nused


def _scatter_rows_kernel(dest_ref, h_ref, xs_in, xs_ref, sem, *, tm, first_tok):
    del xs_in
    base = EXP_TOPK * (first_tok + pl.program_id(0) * tm)
    for r in range(tm):
        for k in range(EXP_TOPK):
            pltpu.make_async_copy(h_ref.at[pl.ds(r, 1)],
                                  xs_ref.at[pl.ds(dest_ref[base + EXP_TOPK * r + k], 1)], sem).start()
    for k in range(EXP_TOPK):
        pltpu.make_async_copy(h_ref, xs_ref.at[pl.ds(0, tm)], sem).wait()


def _scatter_rows(dest, h, xs, first_tok, tm=ROW_TILE):
    rows, d = h.shape
    grid_spec = pltpu.PrefetchScalarGridSpec(
        num_scalar_prefetch=1,
        grid=(rows // tm,),
        in_specs=[pl.BlockSpec((tm, d), lambda i, de: (i, 0)), pl.BlockSpec(memory_space=pl.ANY)],
        out_specs=pl.BlockSpec(memory_space=pl.ANY),
        scratch_shapes=[pltpu.SemaphoreType.DMA(())],
    )
    return pl.pallas_call(
        functools.partial(_scatter_rows_kernel, tm=tm, first_tok=first_tok),
        grid_spec=grid_spec,
        out_shape=jax.ShapeDtypeStruct(xs.shape, xs.dtype),
        input_output_aliases={2: 0},
        compiler_params=_cparams(("arbitrary",), 32),
        name="scatter_rows",
    )(dest, h, xs)


def _moe_kernel(blk_e_ref, nused_ref, x_ref, w1_ref, w3_ref, w2_ref, o_ref, wb_ref):
    i = pl.program_id(0)
    prev = blk_e_ref[jnp.maximum(i - 1, 0)]

    @pl.when((i == 0) | (blk_e_ref[i] != prev))
    def _():
        wb_ref[0] = w1_ref[0].astype(BF16)
        wb_ref[1] = w3_ref[0].astype(BF16)
        wb_ref[2] = w2_ref[0].astype(BF16)

    @pl.when(i < nused_ref[0])
    def _():
        x = x_ref[...].astype(BF16)
        a = jnp.dot(x, wb_ref[0], preferred_element_type=F32)
        b = jnp.dot(x, wb_ref[1], preferred_element_type=F32)
        hmid = (_silu(a) * b).astype(BF16)
        o_ref[...] = jnp.dot(hmid, wb_ref[2], preferred_element_type=F32)

    @pl.when(i >= nused_ref[0])
    def _():
        o_ref[...] = jnp.zeros(o_ref.shape, o_ref.dtype)


def _moe_experts(xs, blk_e, nused, w1, w3, w2, tm=MOE_TILE):
    nblk = blk_e.shape[0]
    d = xs.shape[1]
    wspec = pl.BlockSpec((1, d, d), lambda i, be, nu: (be[i], 0, 0))
    grid_spec = pltpu.PrefetchScalarGridSpec(
        num_scalar_prefetch=2,
        grid=(nblk,),
        in_specs=[pl.BlockSpec((tm, d), lambda i, be, nu: (i, 0)), wspec, wspec, wspec],
        out_specs=pl.BlockSpec((tm, d), lambda i, be, nu: (i, 0)),
        scratch_shapes=[pltpu.VMEM((3, d, d), BF16)],
    )
    return pl.pallas_call(
        _moe_kernel,
        grid_spec=grid_spec,
        out_shape=jax.ShapeDtypeStruct((nblk * tm, d), F32),
        compiler_params=_cparams(("arbitrary",), 56),
        name="moe_experts",
    )(blk_e, nused, xs, w1, w3, w2)


def _combine_kernel(dest_ref, x2_ref, r_ref, g_ref, outs_hbm, y_ref, buf, sem, *, tm):
    i = pl.program_id(0)
    n = pl.num_programs(0)
    last_row = outs_hbm.shape[0] - 1

    def start_tile(tile, slot):
        base = EXP_TOPK * tile * tm
        for r in range(tm):
            for k in range(EXP_TOPK):
                src = jnp.minimum(dest_ref[base + EXP_TOPK * r + k], last_row)
                pltpu.make_async_copy(outs_hbm.at[pl.ds(src, 1)], buf.at[slot, k, pl.ds(r, 1)],
                                      sem.at[slot]).start()

    @pl.when(i == 0)
    def _():
        start_tile(0, 0)

    @pl.when(i + 1 < n)
    def _():
        start_tile(i + 1, (i + 1) % 2)

    slot = i % 2
    for k in range(EXP_TOPK):
        pltpu.make_async_copy(outs_hbm.at[pl.ds(0, tm)], buf.at[slot, k], sem.at[slot]).wait()
    route = r_ref[...]
    y = buf[slot, 0] * route[:, 2:3] + buf[slot, 1] * route[:, 3:4]
    y_ref[...] = _rms(x2_ref[...] + y, g_ref[...])


def _combine(x2, route, g_final, outs, dest, tm=ROW_TILE):
    rows, d = x2.shape
    grid_spec = pltpu.PrefetchScalarGridSpec(
        num_scalar_prefetch=1,
        grid=(rows // tm,),
        in_specs=[pl.BlockSpec((tm, d), lambda i, de: (i, 0)),
                  pl.BlockSpec((tm, LANES), lambda i, de: (i, 0)),
                  pl.BlockSpec((1, d), lambda i, de: (0, 0)),
                  pl.BlockSpec(memory_space=pl.ANY)],
        out_specs=pl.BlockSpec((tm, d), lambda i, de: (i, 0)),
        scratch_shapes=[pltpu.VMEM((2, EXP_TOPK, tm, d), F32), pltpu.SemaphoreType.DMA((2,))],
    )
    return pl.pallas_call(
        functools.partial(_combine_kernel, tm=tm),
        grid_spec=grid_spec,
        out_shape=jax.ShapeDtypeStruct((rows, d), F32),
        compiler_params=_cparams(("arbitrary",), 32),
        name="moe_combine",
    )(dest, x2, route, g_final.reshape(1, d), outs)


def _layer_tail(x, o, att, att_transposed, mem_k, mem_v, batch, t, xq_tile, wts):
    q_dtype = BF16 if xq_tile % (2 * SUBLANES) == 0 else F32
    x1, q = _outproj(x, o, att, wts["w_out"], wts["g_cross"], wts["w_xq"], att_transposed, q_dtype)
    xo = _xattn(q, mem_k, mem_v, batch, t, xq_tile)
    x2, h, route = _xo_router(x1, xo, wts["w_xo"], wts["g_ffn"], wts["w_router"], wts["b_router"])
    return x2, h, route


def _moe_and_final(x2_p, h_p, route_p, x2_s, h_s, route_s, n_sample_tokens, t_pad, t_valid, wts):
    rp, d = h_p.shape
    rs = h_s.shape[0]
    tm = MOE_TILE
    n_assign = (rp + n_sample_tokens) * EXP_TOPK
    nblk = -(-(n_assign + N_EXPERTS * (tm - 1)) // tm)
    dest, blk_e, nused = _route_dest(route_p, route_s, t_pad, t_valid, tm, nblk)
    xs = jnp.zeros((nblk * tm + EXP_TOPK * rs, d), F32)
    xs = _scatter_rows(dest, h_p, xs, 0)
    xs = _scatter_rows(dest, h_s, xs, rp)
    outs = _moe_experts(xs, blk_e, nused, wts["w1"], wts["w3"], wts["w2"])
    y_p = _combine(x2_p, route_p, wts["g_final"], outs, dest[:EXP_TOPK * rp])
    y_s = _combine(x2_s, route_s, wts["g_final"], outs, dest[EXP_TOPK * rp:])
    return y_p, y_s


def kernel(x_prompt, x_sample, state_hgrn, cache_moba_k, cache_moba_v, cache_mem_k, cache_mem_v, page_table,
           mem_prompt, g_mix, w_in, hg_lb, hg_norm, w_out, g_cross, g_mem, w_xq, w_xk, w_xv, w_xo, g_ffn,
           w_grp, b_grp, w_exp, b_exp, w1, w3, w2, g_final):
    depth = g_mix.shape[0]
    assert depth == 1
    bp, tp, d = x_prompt.shape
    bs, ts, _ = x_sample.shape
    mem_len = mem_prompt.shape[1]
    l = 0

    lb = jnp.cumsum(jax.nn.softmax(hg_lb.astype(F32), axis=0), axis=0)[l]
    slopes = jnp.asarray(np.power(2.0, -8.0 * np.arange(1, MB_HEADS + 1) / MB_HEADS).astype(np.float32))
    n_router = N_GROUPS + N_EXPERTS
    w_router = jnp.pad(jnp.concatenate([w_grp[l], w_exp[l]], axis=1), ((0, 0), (0, LANES - n_router)))
    b_router = jnp.pad(jnp.concatenate([b_grp[l], b_exp[l]]), (0, LANES - n_router)).reshape(1, LANES)
    wts = {
        "w_out": w_out[l].astype(BF16), "g_cross": g_cross[l], "w_xq": w_xq[l].astype(BF16),
        "w_xo": w_xo[l].astype(BF16), "g_ffn": g_ffn[l], "w_router": w_router.astype(BF16),
        "b_router": b_router.astype(F32), "w1": w1[l], "w3": w3[l], "w2": w2[l], "g_final": g_final,
    }
    w_in_bf = w_in[l].astype(BF16)
    o_hq, o_mq, o_mk, o_mv = 0, 4 * HG_WIDTH, 4 * HG_WIDTH + MB_WIDTH, 4 * HG_WIDTH + 2 * MB_WIDTH
    o_end = o_mv + MB_WIDTH

    xp = x_prompt.reshape(bp * tp, d)
    w_kv = jnp.concatenate([w_xk[l], w_xv[l]], axis=1).astype(BF16)
    memk_p, memv_p = _rms_proj(mem_prompt.reshape(bp * mem_len, d), g_mem[l], w_kv,
                               [(0, d, [(1.0, False)]), (d, 2 * d, [(1.0, False)])], [F32, F32])
    hp_p, qt_p, kt_p, vt_p = _rms_proj(
        xp, g_mix[l], w_in_bf,
        [(o_hq, o_mq, [(1.0, False)]), (o_mq, o_mk, [(MB_DIM ** -0.5, True)]), (o_mk, o_mv, [(1.0, "batched")]),
         (o_mv, o_end, [(1.0, "batched")])],
        [F32, BF16, F32, F32], seq_len=tp)
    o_p, s_p = _hgrn(hp_p, lb, hg_norm[l], None, bp, tp, tp, 128)
    att_p = _moba_prompt(qt_p, kt_p, vt_p, slopes, bp, tp)
    x2_p, h_p, route_p = _layer_tail(xp, o_p, att_p, True, memk_p, memv_p, bp, tp, 512, wts)

    tpad = SAMPLE_PAD_T
    xs = jnp.pad(x_sample, ((0, 0), (0, tpad - ts), (0, 0))).reshape(bs * tpad, d)
    hp_s, mq_s, mk_s, mv_s = _rms_proj(
        xs, g_mix[l], w_in_bf,
        [(o_hq, o_mq, [(1.0, False)]), (o_mq, o_mk, [(1.0, False)]), (o_mk, o_mv, [(1.0, False)]),
         (o_mv, o_end, [(1.0, False)])],
        [F32, F32, F32, F32])
    o_s, s_s = _hgrn(hp_s, lb, hg_norm[l], state_hgrn[l], bs, tpad, ts, tpad)
    mk4 = mk_s.reshape(bs, tpad, MB_HEADS, MB_DIM)
    mv4 = mv_s.reshape(bs, tpad, MB_HEADS, MB_DIM)
    n_phys, page = cache_moba_k.shape[1], cache_moba_k.shape[2]
    pool_kt = jnp.transpose(cache_moba_k[l], (0, 2, 3, 1)).reshape(n_phys, MB_WIDTH, page)
    pool_vt = jnp.transpose(cache_moba_v[l], (0, 2, 3, 1)).reshape(n_phys, MB_WIDTH, page)
    att_s = _moba_sample(mq_s, mk_s, mv_s, pool_kt, pool_vt, page_table, ts)

    def mem_rows(c):
        c = c[l].reshape(bs, mem_len, XA_HEADS, XA_DIM // LANES, LANES)
        return jnp.transpose(c, (0, 1, 3, 2, 4)).reshape(bs, mem_len * d // LANES, LANES)

    memk_s, memv_s = mem_rows(cache_mem_k), mem_rows(cache_mem_v)
    x2_s, h_s, route_s = _layer_tail(xs, o_s, att_s, False, memk_s, memv_s, bs, tpad, tpad, wts)
    y_p, y_s = _moe_and_final(x2_p, h_p, route_p, x2_s, h_s, route_s, bs * ts, tpad, ts, wts)

    return (y_p.reshape(bp, tp, d),
            y_s.reshape(bs, tpad, d)[:, :ts],
            s_p.reshape(1, bp, HG_HEADS, HG_DIM, HG_DIM),
            jnp.transpose(kt_p.reshape(1, bp, MB_HEADS, MB_DIM, tp), (0, 1, 4, 2, 3)),
            jnp.transpose(vt_p.reshape(1, bp, MB_HEADS, MB_DIM, tp), (0, 1, 4, 2, 3)),
            memk_p.reshape(1, bp, mem_len, XA_HEADS, XA_DIM),
            memv_p.reshape(1, bp, mem_len, XA_HEADS, XA_DIM),
            s_s.reshape(1, bs, HG_HEADS, HG_DIM, HG_DIM),
            mk4[:, :ts].reshape(1, bs, ts, MB_HEADS, MB_DIM),
            mv4[:, :ts].reshape(1, bs, ts, MB_HEADS, MB_DIM))
```

```python
import functools

import numpy as np
import jax
import jax.numpy as jnp
from jax import lax
from jax.experimental import pallas as pl
from jax.experimental.pallas import tpu as pltpu

F32 = jnp.float32
BF16 = jnp.bfloat16
I32 = jnp.int32

D_MODEL = 1024
HG_HEADS = 4
HG_DIM = 128
HG_WIDTH = HG_HEADS * HG_DIM
MB_HEADS = 8
MB_DIM = 64
MB_WIDTH = MB_HEADS * MB_DIM
MB_BLOCK = 256
MB_TOPK = 3
XA_HEADS = 4
XA_DIM = 256
N_GROUPS = 4
EXP_PER_GROUP = 8
N_EXPERTS = N_GROUPS * EXP_PER_GROUP
EXP_TOPK = 2
RMS_EPS = 1e-6
NEG_INF = float("-inf")

LANES = 128
SUBLANES = 8
MIB = 1024 * 1024

ROW_TILE = 256
MOE_TILE = 256
SAMPLE_PAD_T = 8
MOBA_PAIRS = 4

_NT = (((1,), (1,)), ((), ()))


def _cparams(semantics, vmem_mib):
    return pltpu.CompilerParams(dimension_semantics=semantics, vmem_limit_bytes=vmem_mib * MIB)


def _rms(x, g):
    return x * lax.rsqrt(jnp.mean(x * x, axis=-1, keepdims=True) + RMS_EPS) * g


def _silu(x):
    return x * jax.nn.sigmoid(x)


def _rms_proj_kernel(x_ref, g_ref, w_ref, *o_refs, segs):
    h = _rms(x_ref[...], g_ref[...]).astype(BF16)
    k = 0
    for lo, hi, outs in segs:
        r = jnp.dot(h, w_ref[:, lo:hi], preferred_element_type=F32)
        for scale, transposed in outs:
            y = r if scale == 1.0 else r * scale
            if transposed:
                y = y.T
            o_refs[k][...] = y.astype(o_refs[k].dtype)
            k += 1


def _rms_proj(x, g, w_bf, segs, dtypes, tm=ROW_TILE, vmem_mib=48, seq_len=None):
    rows, d = x.shape
    assert rows % tm == 0
    out_shape, out_specs = [], []
    k = 0
    for lo, hi, outs in segs:
        n = hi - lo
        for _, transposed in outs:
            if transposed == "batched":
                per_seq = seq_len // tm
                out_shape.append(jax.ShapeDtypeStruct((rows // seq_len, n, seq_len), dtypes[k]))
                out_specs.append(pl.BlockSpec((None, n, tm), lambda i: (i // per_seq, 0, i % per_seq)))
            elif transposed:
                out_shape.append(jax.ShapeDtypeStruct((n, rows), dtypes[k]))
                out_specs.append(pl.BlockSpec((n, tm), lambda i: (0, i)))
            else:
                out_shape.append(jax.ShapeDtypeStruct((rows, n), dtypes[k]))
                out_specs.append(pl.BlockSpec((tm, n), lambda i: (i, 0)))
            k += 1
    return pl.pallas_call(
        functools.partial(_rms_proj_kernel, segs=segs),
        grid=(rows // tm,),
        in_specs=[pl.BlockSpec((tm, d), lambda i: (i, 0)),
                  pl.BlockSpec((1, d), lambda i: (0, 0)),
                  pl.BlockSpec(w_bf.shape, lambda i: (0, 0))],
        out_specs=out_specs,
        out_shape=out_shape,
        compiler_params=_cparams(("parallel",), vmem_mib),
        name="rms_proj",
    )(x, g.reshape(1, d), w_bf)


def _cumsum_rows(x):
    n = x.shape[0]
    row = lax.broadcasted_iota(I32, x.shape, 0)
    s = 1
    while s < n:
        x = x + jnp.where(row >= s, pltpu.roll(x, s, 0), 0.0)
        s *= 2
    return x


def _hgrn_intra(q, kk, b, c, w):
    row = lax.broadcasted_iota(I32, (c, HG_DIM), 0)
    acc = None
    half = c // 2
    while half >= SUBLANES:
        two = 2 * half
        nblk = c // two
        pieces = [jnp.broadcast_to(b[i * two + half - 1:i * two + half, :], (two, HG_DIM)) for i in range(nblk)]
        bm = pieces[0] if nblk == 1 else jnp.concatenate(pieces, axis=0)
        second = (row & (two - 1)) >= half
        qt = jnp.where(second, q * jnp.exp(jnp.where(second, b - bm, 0.0)), 0.0)
        kt = jnp.where(second, 0.0, kk * jnp.exp(jnp.where(second, 0.0, bm - b)))
        al = lax.dot_general(qt.astype(BF16), kt.astype(BF16), _NT, preferred_element_type=F32)
        if nblk > 1:
            shift = two.bit_length() - 1
            rt = lax.broadcasted_iota(I32, (c, c), 0) >> shift
            cs = lax.broadcasted_iota(I32, (c, c), 1) >> shift
            al = jnp.where(rt == cs, al, 0.0)
        acc = al if acc is None else acc + al
        half //= 2
    r8 = lax.broadcasted_iota(I32, (SUBLANES, HG_DIM), 0)
    lane = lax.broadcasted_iota(I32, (SUBLANES, w), 1)
    blocks = []
    for g in range(c // SUBLANES):
        lo = g * SUBLANES
        qg, kg, bg = q[lo:lo + SUBLANES], kk[lo:lo + SUBLANES], b[lo:lo + SUBLANES]
        ag = jnp.zeros((SUBLANES, w), F32)
        for s in range(SUBLANES):
            e = jnp.exp(jnp.where(r8 >= s, bg - bg[s:s + 1, :], NEG_INF))
            p = jnp.sum(qg * kg[s:s + 1, :] * e, axis=1, keepdims=True)
            ag = jnp.where(lane == lo + s, p, ag)
        blocks.append(ag)
    diag = blocks[0] if len(blocks) == 1 else jnp.concatenate(blocks, axis=0)
    return diag if acc is None else acc + diag


def _hgrn_kernel(*refs, c, t_valid, has_s0):
    if has_s0:
        hp_ref, lb_ref, gn_ref, s0_ref, o_ref, sout_ref, st_ref = refs
    else:
        hp_ref, lb_ref, gn_ref, o_ref, sout_ref, st_ref = refs
    ci = pl.program_id(1)
    last = pl.num_programs(1) - 1
    w = max(c, LANES)

    @pl.when(ci == 0)
    def _():
        for h in range(HG_HEADS):
            st_ref[h] = s0_ref[0, h].T if has_s0 else jnp.zeros((HG_DIM, HG_DIM), F32)

    row = lax.broadcasted_iota(I32, (c, HG_DIM), 0)
    for h in range(HG_HEADS):
        lo = h * HG_DIM
        hq = hp_ref[:, lo:lo + HG_DIM]
        hf = hp_ref[:, HG_WIDTH + lo:HG_WIDTH + lo + HG_DIM]
        v = hp_ref[:, 2 * HG_WIDTH + lo:2 * HG_WIDTH + lo + HG_DIM]
        hg = hp_ref[:, 3 * HG_WIDTH + lo:3 * HG_WIDTH + lo + HG_DIM]
        lb = lb_ref[:, lo:lo + HG_DIM]
        q = _silu(hq)
        f = lb + (1.0 - lb) * jax.nn.sigmoid(hf)
        logf = jnp.log(f)
        kk = 1.0 - f
        if t_valid < c:
            valid = row < t_valid
            logf = jnp.where(valid, logf, 0.0)
            kk = jnp.where(valid, kk, 0.0)
            v = jnp.where(valid, v, 0.0)
        b = _cumsum_rows(logf)
        a = _hgrn_intra(q, kk, b, c, w)
        st = st_ref[h]
        bl = b[c - 1:c, :]
        k2 = kk * jnp.exp(bl - b)
        if c < w:
            zpad = jnp.zeros((w - c, HG_DIM), F32)
            vp = jnp.concatenate([v, zpad], axis=0)
            k2 = jnp.concatenate([k2, zpad], axis=0)
        else:
            vp = v
        vb = vp.astype(BF16)
        o = lax.dot_general((q * jnp.exp(b)).astype(BF16), st.astype(BF16), _NT, preferred_element_type=F32)
        o = o + jnp.dot(a.astype(BF16), vb, preferred_element_type=F32)
        o_ref[:, lo:lo + HG_DIM] = (_rms(o, gn_ref[...]) * _silu(hg)).astype(o_ref.dtype)
        st_new = st * jnp.exp(bl) + jnp.dot(vp.T.astype(BF16), k2.astype(BF16), preferred_element_type=F32)
        st_ref[h] = st_new

        @pl.when(ci == last)
        def _():
            sout_ref[0, h] = st_new.T


def _hgrn(hp, lb, gn, s0, batch, t_pad, t_valid, c):
    rows = hp.shape[0]
    nc = t_pad // c
    has_s0 = s0 is not None
    in_specs = [pl.BlockSpec((c, 4 * HG_WIDTH), lambda b, i: (b * nc + i, 0)),
                pl.BlockSpec((1, HG_WIDTH), lambda b, i: (0, 0)),
                pl.BlockSpec((1, HG_DIM), lambda b, i: (0, 0))]
    args = [hp, lb.reshape(1, HG_WIDTH), gn.reshape(1, HG_DIM)]
    if has_s0:
        in_specs.append(pl.BlockSpec((1, HG_HEADS, HG_DIM, HG_DIM), lambda b, i: (b, 0, 0, 0)))
        args.append(s0)
    return pl.pallas_call(
        functools.partial(_hgrn_kernel, c=c, t_valid=min(t_valid, c), has_s0=has_s0),
        grid=(batch, nc),
        in_specs=in_specs,
        out_specs=[pl.BlockSpec((c, HG_WIDTH), lambda b, i: (b * nc + i, 0)),
                   pl.BlockSpec((1, HG_HEADS, HG_DIM, HG_DIM), lambda b, i: (b, 0, 0, 0))],
        out_shape=[jax.ShapeDtypeStruct((rows, HG_WIDTH), BF16),
                   jax.ShapeDtypeStruct((batch, HG_HEADS, HG_DIM, HG_DIM), F32)],
        scratch_shapes=[pltpu.VMEM((HG_HEADS, HG_DIM, HG_DIM), F32)],
        compiler_params=_cparams(("parallel", "arbitrary"), 32),
        name="hgrn2",
    )(*args)


def _topk_block_mask(gate, n_valid, nrow):
    nb = gate.shape[0]
    cnt = jnp.zeros(gate.shape, I32)
    for n2 in range(nb):
        g2 = gate[n2:n2 + 1, :]
        beats = (g2 > gate) | ((g2 == gate) & (n2 < nrow))
        cnt = cnt + jnp.where(beats, (n_valid > n2).astype(I32), 0)
    return (cnt < MB_TOPK) & (nrow < n_valid)


def _moba_prompt_kernel(slope_ref, qt_ref, kt_ref, vt_ref, o_ref, kb_ref, vb_ref, km_ref, sel_ref, *, nb, npairs):
    hg = pl.program_id(1)
    qi = pl.program_id(2)
    tq = MB_BLOCK
    pw = 2 * tq
    width = npairs * pw
    pair_rows = 2 * MB_DIM

    @pl.when(qi == 0)
    def _():
        pos = jnp.where(lax.broadcasted_iota(I32, (MB_BLOCK, pair_rows), 1) == 0,
                        lax.broadcasted_iota(I32, (MB_BLOCK, pair_rows), 0), 0).astype(BF16)
        for pp in range(npairs):
            means = []
            for j in range(nb):
                kj = kt_ref[pp * pair_rows:(pp + 1) * pair_rows, j * MB_BLOCK:(j + 1) * MB_BLOCK].T
                kb_ref[pp, j * MB_BLOCK:(j + 1) * MB_BLOCK, :pair_rows] = kj.astype(BF16)
                kb_ref[pp, j * MB_BLOCK:(j + 1) * MB_BLOCK, pair_rows:] = pos
                means.append(jnp.sum(kj, axis=0, keepdims=True) * (1.0 / MB_BLOCK))
                vb_ref[pp, j] = vt_ref[pp * pair_rows:(pp + 1) * pair_rows,
                                       j * MB_BLOCK:(j + 1) * MB_BLOCK].astype(BF16)
            km_ref[pp] = jnp.concatenate(means, axis=0)

    second_head = lax.broadcasted_iota(I32, (pair_rows, tq), 0) >= MB_DIM
    first_row = lax.broadcasted_iota(I32, (pair_rows, pw), 0) == 0
    qaugs, gates, slope_parts = [], [], []
    for pp in range(npairs):
        qt2 = qt_ref[pp * pair_rows:(pp + 1) * pair_rows, :]
        zero = jnp.zeros_like(qt2)
        qcat = jnp.concatenate([jnp.where(second_head, zero, qt2), jnp.where(second_head, qt2, zero)], axis=1)
        gates.append(jnp.dot(km_ref[pp].astype(BF16), qcat, preferred_element_type=F32))
        pair_slopes = jnp.concatenate(
            [jnp.full((1, tq), slope_ref[(hg * npairs + pp) * 2 + i], F32) for i in range(2)], axis=1)
        slope_parts.append(pair_slopes)
        srows = jnp.where(first_row, pair_slopes, 0.0).astype(BF16)
        qaugs.append(jnp.concatenate([qcat, srows], axis=0))
    slope_row = slope_parts[0] if npairs == 1 else jnp.concatenate(slope_parts, axis=1)
    gate = gates[0] if npairs == 1 else jnp.concatenate(gates, axis=1)
    nrow = lax.broadcasted_iota(I32, (nb, width), 0)
    sel = _topk_block_mask(gate, qi, nrow).astype(F32)
    for n in range(nb):
        sel_ref[n] = jnp.broadcast_to(sel[n:n + 1, :], (SUBLANES, width))

    def scores(j):
        start = pl.multiple_of(j * MB_BLOCK, MB_BLOCK)
        parts = [jnp.dot(kb_ref[pp, pl.ds(start, MB_BLOCK), :], qaugs[pp], preferred_element_type=F32)
                 for pp in range(npairs)]
        return parts[0] if npairs == 1 else jnp.concatenate(parts, axis=1)

    def values(j, p):
        pb = p.astype(BF16)
        return [jnp.dot(vb_ref[pp, j], pb[:, pp * pw:(pp + 1) * pw], preferred_element_type=F32)
                for pp in range(npairs)]

    lane = lax.broadcasted_iota(I32, (MB_BLOCK, width), 1) & (tq - 1)
    rowk = lax.broadcasted_iota(I32, (MB_BLOCK, width), 0)
    s = jnp.where(rowk <= lane, scores(qi), NEG_INF)
    m = jnp.max(s, axis=0, keepdims=True)
    p = jnp.exp(s - m)
    l = jnp.sum(p, axis=0, keepdims=True)
    accs = values(qi, p)

    def body(j, carry):
        m, l, accs = carry
        off = jnp.full((1, width), (qi - j) * MB_BLOCK, I32).astype(F32) * slope_row
        picked = sel_ref[j][0:1, :] > 0.5
        s = scores(j)
        m_new = jnp.maximum(m, jnp.where(picked, jnp.max(s, axis=0, keepdims=True) - off, NEG_INF))
        alpha = jnp.exp(m - m_new)
        p = jnp.exp(s - jnp.where(picked, m_new + off, float("inf")))
        l = alpha * l + jnp.sum(p, axis=0, keepdims=True)
        pv = values(j, p)
        accs = [alpha[:, pp * pw:(pp + 1) * pw] * accs[pp] + pv[pp] for pp in range(npairs)]
        return m_new, l, accs

    m, l, accs = lax.fori_loop(0, qi, body, (m, l, accs))
    for pp in range(npairs):
        a = accs[pp] / l[:, pp * pw:(pp + 1) * pw]
        o_ref[pp * pair_rows:(pp + 1) * pair_rows, :] = jnp.where(second_head, a[:, tq:], a[:, :tq])


def _moba_prompt(qt, kt, vt, slopes, batch, t, npairs=MOBA_PAIRS):
    nb = t // MB_BLOCK
    rows = batch * t
    gr = 2 * MB_DIM * npairs
    width = npairs * 2 * MB_BLOCK
    return pl.pallas_call(
        functools.partial(_moba_prompt_kernel, nb=nb, npairs=npairs),
        grid=(batch, MB_WIDTH // gr, nb),
        in_specs=[pl.BlockSpec(memory_space=pltpu.SMEM),
                  pl.BlockSpec((gr, MB_BLOCK), lambda b, h, i: (h, b * nb + i)),
                  pl.BlockSpec((None, gr, t), lambda b, h, i: (b, h, 0)),
                  pl.BlockSpec((None, gr, t), lambda b, h, i: (b, h, 0))],
        out_specs=pl.BlockSpec((gr, MB_BLOCK), lambda b, h, i: (h, b * nb + i)),
        out_shape=jax.ShapeDtypeStruct((MB_WIDTH, rows), F32),
        scratch_shapes=[pltpu.VMEM((npairs, t, 4 * MB_DIM), BF16),
                        pltpu.VMEM((npairs, nb, 2 * MB_DIM, MB_BLOCK), BF16),
                        pltpu.VMEM((npairs, nb, 2 * MB_DIM), F32),
                        pltpu.VMEM((nb, SUBLANES, width), F32)],
        compiler_params=_cparams(("parallel", "parallel", "arbitrary"), 48),
        name="moba_prompt",
    )(slopes, qt, kt, vt)


def _moba_sample_kernel(pt_ref, slope_ref, hm_ref, q_ref, kn_ref, vn_ref, *rest, npages, page, t_new, past_len):
    del pt_ref
    k_pages = rest[:npages]
    v_pages = rest[npages:2 * npages]
    o_ref = rest[2 * npages]
    nrows = t_new * MB_HEADS
    pages_per_block = MB_BLOCK // page
    nb = npages // pages_per_block
    hm = hm_ref[...]
    slope = slope_ref[...][:, 0:1]
    row_q = lax.broadcasted_iota(I32, (nrows, 1), 0) >> 3
    pos_q = past_len + row_q

    q = q_ref[...] * (MB_DIM ** -0.5)
    qbd = jnp.concatenate([jnp.broadcast_to(q[t:t + 1, :], (MB_HEADS, MB_WIDTH)) * hm for t in range(t_new)],
                          axis=0).astype(BF16)

    lane = lax.broadcasted_iota(I32, (nrows, page), 1)
    scores = []
    gates = [None] * nb
    for p in range(npages):
        s = jnp.dot(qbd, k_pages[p][0].astype(BF16), preferred_element_type=F32)
        rs = jnp.sum(s, axis=1, keepdims=True)
        n = p // pages_per_block
        gates[n] = rs if gates[n] is None else gates[n] + rs
        scores.append(s - slope * (pos_q - (p * page + lane)).astype(F32))
    sels = []
    for n in range(nb):
        cnt = jnp.zeros((nrows, 1), I32)
        for n2 in range(nb):
            beats = (gates[n2] > gates[n]) | ((gates[n2] == gates[n]) & (n2 < n))
            cnt = cnt + jnp.where(beats, 1, 0)
        sels.append(jnp.where(cnt < MB_TOPK, 1.0, 0.0))

    zrows = jnp.zeros((LANES - SAMPLE_PAD_T, MB_WIDTH), F32)
    knp = jnp.concatenate([kn_ref[...], zrows], axis=0).astype(BF16)
    vnp = jnp.concatenate([vn_ref[...], zrows], axis=0).astype(BF16)
    lane_n = lax.broadcasted_iota(I32, (nrows, LANES), 1)
    s_new = lax.dot_general(qbd, knp, _NT, preferred_element_type=F32)
    s_new = jnp.where(lane_n <= row_q, s_new - slope * (row_q - lane_n).astype(F32), NEG_INF)
    m = jnp.max(s_new, axis=1, keepdims=True)
    for p in range(npages):
        sel_p = jnp.broadcast_to(sels[p // pages_per_block], (nrows, page)) > 0.5
        scores[p] = jnp.where(sel_p, scores[p], NEG_INF)
        m = jnp.maximum(m, jnp.max(scores[p], axis=1, keepdims=True))

    p_new = jnp.exp(s_new - m)
    l = jnp.sum(p_new, axis=1, keepdims=True)
    acc = jnp.dot(p_new.astype(BF16), vnp, preferred_element_type=F32)
    for p in range(npages):
        pr = jnp.exp(scores[p] - m)
        l = l + jnp.sum(pr, axis=1, keepdims=True)
        acc = acc + lax.dot_general(pr.astype(BF16), v_pages[p][0].astype(BF16), _NT,
                                    preferred_element_type=F32)
    acc = acc / l
    rows = [jnp.sum(acc[t * MB_HEADS:(t + 1) * MB_HEADS, :] * hm, axis=0, keepdims=True) for t in range(t_new)]
    rows.append(jnp.zeros((SAMPLE_PAD_T - t_new, MB_WIDTH), F32))
    o_ref[...] = jnp.concatenate(rows, axis=0)


def _moba_sample(mq, mk, mv, pool_kt, pool_vt, page_table, t_new):
    batch, npages = page_table.shape
    page = pool_kt.shape[2]
    past_len = npages * page
    assert past_len % MB_BLOCK == 0 and MB_BLOCK % page == 0 and t_new <= SAMPLE_PAD_T and page == LANES
    nrows = t_new * MB_HEADS
    slopes = np.power(2.0, -8.0 * np.arange(1, MB_HEADS + 1) / MB_HEADS).astype(np.float32)
    slope_rows = jnp.asarray(np.tile(np.tile(slopes, t_new)[:, None], (1, LANES)))
    head_mask = jnp.asarray((np.arange(MB_WIDTH)[None, :] // MB_DIM == np.arange(MB_HEADS)[:, None])
                            .astype(np.float32))
    new_spec = pl.BlockSpec((SAMPLE_PAD_T, MB_WIDTH), lambda b, pt: (b, 0))
    page_specs = [pl.BlockSpec((1, MB_WIDTH, page),
                               functools.partial(lambda b, pt, i: (pt[b * npages + i], 0, 0), i=i))
                  for i in range(npages)]
    grid_spec = pltpu.PrefetchScalarGridSpec(
        num_scalar_prefetch=1,
        grid=(batch,),
        in_specs=[pl.BlockSpec((nrows, LANES), lambda b, pt: (0, 0)),
                  pl.BlockSpec((MB_HEADS, MB_WIDTH), lambda b, pt: (0, 0)),
                  new_spec, new_spec, new_spec] + page_specs + page_specs,
        out_specs=new_spec,
    )
    return pl.pallas_call(
        functools.partial(_moba_sample_kernel, npages=npages, page=page, t_new=t_new, past_len=past_len),
        grid_spec=grid_spec,
        out_shape=jax.ShapeDtypeStruct((batch * SAMPLE_PAD_T, MB_WIDTH), F32),
        compiler_params=_cparams(("parallel",), 40),
        name="moba_sample",
    )(page_table.reshape(-1), slope_rows, head_mask, mq, mk, mv, *([pool_kt] * npages), *([pool_vt] * npages))


def _outproj_kernel(x_ref, o_ref, att_ref, wo_ref, g_ref, wq_ref, x1_ref, q_ref, *, att_transposed):
    att = att_ref[...]
    if att_transposed:
        att = att.T
    mix = jnp.dot(o_ref[...], wo_ref[:HG_WIDTH, :], preferred_element_type=F32)
    mix = mix + jnp.dot(att.astype(BF16), wo_ref[HG_WIDTH:, :], preferred_element_type=F32)
    x1 = x_ref[...] + mix
    x1_ref[...] = x1
    h = _rms(x1, g_ref[...]).astype(BF16)
    q_ref[...] = (jnp.dot(h, wq_ref[...], preferred_element_type=F32) * (XA_DIM ** -0.5)).astype(q_ref.dtype)


def _outproj(x, o, att, wo_bf, g, wq_bf, att_transposed, q_dtype, tm=ROW_TILE):
    rows, d = x.shape
    att_spec = (pl.BlockSpec((MB_WIDTH, tm), lambda i: (0, i)) if att_transposed
                else pl.BlockSpec((tm, MB_WIDTH), lambda i: (i, 0)))
    return pl.pallas_call(
        functools.partial(_outproj_kernel, att_transposed=att_transposed),
        grid=(rows // tm,),
        in_specs=[pl.BlockSpec((tm, d), lambda i: (i, 0)),
                  pl.BlockSpec((tm, HG_WIDTH), lambda i: (i, 0)),
                  att_spec,
                  pl.BlockSpec((d, d), lambda i: (0, 0)),
                  pl.BlockSpec((1, d), lambda i: (0, 0)),
                  pl.BlockSpec((d, d), lambda i: (0, 0))],
        out_specs=[pl.BlockSpec((tm, d), lambda i: (i, 0)), pl.BlockSpec((tm, d), lambda i: (i, 0))],
        out_shape=[jax.ShapeDtypeStruct((rows, d), F32), jax.ShapeDtypeStruct((rows, d), q_dtype)],
        compiler_params=_cparams(("parallel",), 40),
        name="outproj_q",
    )(x, o, att, wo_bf, g.reshape(1, d), wq_bf)


def _xattn_tiled_kernel(q_ref, k_ref, v_ref, o_ref):
    nt = q_ref.shape[0]
    ndt = XA_DIM // LANES
    nc = XA_HEADS * ndt
    half = XA_HEADS * nt
    q = q_ref[...].astype(F32)
    a = jnp.concatenate([q[:, (h * ndt + dt) * LANES:(h * ndt + dt + 1) * LANES]
                         for dt in range(ndt) for h in range(XA_HEADS)], axis=0).astype(BF16)
    s = lax.dot_general(a, k_ref[...].astype(BF16), _NT, preferred_element_type=F32)
    width = s.shape[1]
    lane_c = lax.broadcasted_iota(I32, s.shape, 1) & (nc - 1)
    assert nt & (nt - 1) == 0 and nc & (nc - 1) == 0
    row_c = lax.broadcasted_iota(I32, s.shape, 0) >> (nt.bit_length() - 1)
    s = jnp.where(lane_c == row_c, s, 0.0)
    tot = s[:half]
    for dt in range(1, ndt):
        tot = tot + pltpu.roll(s[dt * half:(dt + 1) * half], width - dt * XA_HEADS, 1)
    ok = ((lax.broadcasted_iota(I32, tot.shape, 1) & (nc - 1))
          == (lax.broadcasted_iota(I32, tot.shape, 0) >> (nt.bit_length() - 1)))
    tot = jnp.where(ok, tot, NEG_INF)
    p = jnp.exp(tot - jnp.max(tot, axis=-1, keepdims=True))
    p = p / jnp.sum(p, axis=-1, keepdims=True)
    pe = jnp.concatenate([p] + [pltpu.roll(p, dt * XA_HEADS, 1) for dt in range(1, ndt)], axis=0)
    o = jnp.dot(pe.astype(BF16), v_ref[...].astype(BF16), preferred_element_type=F32)
    o_ref[...] = jnp.concatenate([o[(dt * XA_HEADS + h) * nt:(dt * XA_HEADS + h + 1) * nt, :]
                                  for h in range(XA_HEADS) for dt in range(ndt)], axis=1).astype(o_ref.dtype)


def _xattn_kernel(q_ref, k_ref, v_ref, o_ref):
    q = q_ref[...].astype(BF16)
    for h in range(XA_HEADS):
        lo = h * XA_DIM
        kh = k_ref[:, lo:lo + XA_DIM].astype(BF16)
        vh = v_ref[:, lo:lo + XA_DIM].astype(BF16)
        s = lax.dot_general(q[:, lo:lo + XA_DIM], kh, _NT, preferred_element_type=F32)
        p = jnp.exp(s - jnp.max(s, axis=-1, keepdims=True))
        p = p / jnp.sum(p, axis=-1, keepdims=True)
        o_ref[:, lo:lo + XA_DIM] = jnp.dot(p.astype(BF16), vh, preferred_element_type=F32).astype(o_ref.dtype)


def _xattn(q, mem_k, mem_v, batch, t, tq):
    rows, d = q.shape
    nq = t // tq
    tiled = mem_k.ndim == 3
    if tiled:
        kv_spec = pl.BlockSpec((None,) + mem_k.shape[1:], lambda b, i: (b, 0, 0))
    else:
        m = mem_k.shape[0] // batch
        kv_spec = pl.BlockSpec((m, d), lambda b, i: (b, 0))
    return pl.pallas_call(
        _xattn_tiled_kernel if tiled else _xattn_kernel,
        grid=(batch, nq),
        in_specs=[pl.BlockSpec((tq, d), lambda b, i: (b * nq + i, 0)), kv_spec, kv_spec],
        out_specs=pl.BlockSpec((tq, d), lambda b, i: (b * nq + i, 0)),
        out_shape=jax.ShapeDtypeStruct((rows, d), q.dtype),
        compiler_params=_cparams(("parallel", "arbitrary"), 32),
        name="xattn",
    )(q, mem_k, mem_v)


def _xo_router_kernel(x1_ref, o_ref, wo_ref, g_ref, wr_ref, br_ref, x2_ref, h_ref, r_ref):
    x2 = x1_ref[...] + jnp.dot(o_ref[...].astype(BF16), wo_ref[...], preferred_element_type=F32)
    x2_ref[...] = x2
    h = _rms(x2, g_ref[...])
    h_ref[...] = h
    logits = jnp.dot(h.astype(BF16), wr_ref[...], preferred_element_type=F32) + br_ref[...]
    lane = lax.broadcasted_iota(I32, logits.shape, 1)
    big = jnp.int32(LANES)

    def top1(mask):
        mx = jnp.max(jnp.where(mask, logits, NEG_INF), axis=-1, keepdims=True)
        idx = jnp.min(jnp.where(mask & (logits == mx), lane, big), axis=-1, keepdims=True)
        return mx, idx

    gmask = lane < N_GROUPS
    gmx, gsel = top1(gmask)
    gw = 1.0 / jnp.sum(jnp.where(gmask, jnp.exp(logits - gmx), 0.0), axis=-1, keepdims=True)
    elo = N_GROUPS + gsel * EXP_PER_GROUP
    emask = (lane >= elo) & (lane < elo + EXP_PER_GROUP)
    m1, i1 = top1(emask)
    m2, i2 = top1(emask & (lane != i1))
    e2 = jnp.exp(m2 - m1)
    g1 = gw / (1.0 + e2)
    g2 = gw * e2 / (1.0 + e2)
    out = jnp.where(lane == 0, (i1 - N_GROUPS).astype(F32), 0.0)
    out = jnp.where(lane == 1, (i2 - N_GROUPS).astype(F32), out)
    out = jnp.where(lane == 2, g1, out)
    out = jnp.where(lane == 3, g2, out)
    r_ref[...] = out


def _xo_router(x1, o, wxo_bf, g, wr_bf, br, tm=ROW_TILE):
    rows, d = x1.shape
    return pl.pallas_call(
        _xo_router_kernel,
        grid=(rows // tm,),
        in_specs=[pl.BlockSpec((tm, d), lambda i: (i, 0)),
                  pl.BlockSpec((tm, d), lambda i: (i, 0)),
                  pl.BlockSpec((d, d), lambda i: (0, 0)),
                  pl.BlockSpec((1, d), lambda i: (0, 0)),
                  pl.BlockSpec((d, LANES), lambda i: (0, 0)),
                  pl.BlockSpec((1, LANES), lambda i: (0, 0))],
        out_specs=[pl.BlockSpec((tm, d), lambda i: (i, 0)),
                   pl.BlockSpec((tm, d), lambda i: (i, 0)),
                   pl.BlockSpec((tm, LANES), lambda i: (i, 0))],
        out_shape=[jax.ShapeDtypeStruct((rows, d), F32),
                   jax.ShapeDtypeStruct((rows, d), F32),
                   jax.ShapeDtypeStruct((rows, LANES), F32)],
        compiler_params=_cparams(("parallel",), 40),
        name="xo_router",
    )(x1, o, wxo_bf, g.reshape(1, d), wr_bf, br)


def _lane_cumsum(x):
    lane = lax.broadcasted_iota(I32, x.shape, 1)
    s = 1
    while s < LANES:
        x = x + jnp.where(lane >= s, pltpu.roll(x, s, 1), 0.0)
        s *= 2
    return x


def _route_dest_kernel(rp_ref, rs_ref, dest_ref, tab_ref, r_buf, cnt_ref, carry_ref, pstart_ref, *,
                       n_prompt_chunks, t_pad, t_valid, tm, rows_cap, nblk_lanes):
    ph = pl.program_id(0)
    c = pl.program_id(1)
    chunk = r_buf.shape[0]
    is_prompt = c < n_prompt_chunks

    @pl.when(is_prompt)
    def _():
        r_buf[...] = rp_ref[...]

    @pl.when(jnp.logical_not(is_prompt))
    def _():
        r_buf[...] = rs_ref[...]

    @pl.when((ph == 0) & (c == 0))
    def _():
        cnt_ref[...] = jnp.zeros(cnt_ref.shape, F32)

    route = r_buf[...]
    lane = lax.broadcasted_iota(I32, (chunk, LANES), 1)
    row = lax.broadcasted_iota(I32, (chunk, LANES), 0)
    lanef = lane.astype(F32)
    valid = (((row & (t_pad - 1)) < t_valid).astype(I32) | is_prompt.astype(I32)) > 0
    oh0 = jnp.where((lanef == route[:, 0:1]) & valid, 1.0, 0.0)
    oh1 = jnp.where((lanef == route[:, 1:2]) & valid, 1.0, 0.0)
    cmat = oh0 + oh1
    csum = jnp.sum(cmat, axis=0, keepdims=True)

    @pl.when(ph == 0)
    def _():
        cnt_ref[...] = cnt_ref[...] + csum

    @pl.when((ph == 1) & (c == 0))
    def _():
        cnt = jnp.broadcast_to(cnt_ref[...], (SUBLANES, LANES))
        padc = jnp.floor((cnt + (tm - 1)) * (1.0 / tm)) * tm
        pend = _lane_cumsum(padc)
        pstart_ref[...] = (pend - padc)[0:1, :]
        carry_ref[...] = jnp.zeros(carry_ref.shape, F32)
        starts = lax.broadcasted_iota(I32, (SUBLANES, nblk_lanes), 1).astype(F32) * tm
        blk = jnp.zeros((SUBLANES, nblk_lanes), F32)
        for e in range(N_EXPERTS):
            blk = blk + jnp.where(pend[:, e:e + 1] <= starts, 1.0, 0.0)
        blk = jnp.minimum(blk, N_EXPERTS - 1.0)
        nused = jnp.broadcast_to(pend[:, N_EXPERTS - 1:N_EXPERTS] * (1.0 / tm), (SUBLANES, nblk_lanes))
        sub = lax.broadcasted_iota(I32, (SUBLANES, nblk_lanes), 0)
        tab_ref[...] = jnp.where(sub == 0, blk, nused)

    @pl.when(ph == 1)
    def _():
        rt = lax.broadcasted_iota(I32, (chunk, chunk), 0)
        cs = lax.broadcasted_iota(I32, (chunk, chunk), 1)
        tri = jnp.where(rt > cs, 1.0, 0.0).astype(BF16)
        before = jnp.dot(tri, cmat.astype(BF16), preferred_element_type=F32) + carry_ref[...]
        base = before + pstart_ref[...]
        d0 = jnp.sum(base * oh0, axis=1, keepdims=True)
        d1 = jnp.sum(base * oh1, axis=1, keepdims=True)
        spare = (rows_cap + 2 * ((c - n_prompt_chunks) * chunk + row)).astype(F32)
        out = jnp.where(lane == 0, d0, d1)
        out = jnp.where(valid, out, spare + lanef)
        dest_ref[...] = jnp.where(lane < EXP_TOPK, out, 0.0)
        carry_ref[...] = carry_ref[...] + csum


def _route_dest(route_p, route_s, t_pad, t_valid, tm, nblk, chunk=ROW_TILE):
    rp, rs = route_p.shape[0], route_s.shape[0]
    assert rp % chunk == 0 and rs % chunk == 0 and chunk % t_pad == 0
    npc, nsc = rp // chunk, rs // chunk
    rows_cap = nblk * tm
    nblk_lanes = -(-nblk // LANES) * LANES
    dest_f, tab = pl.pallas_call(
        functools.partial(_route_dest_kernel, n_prompt_chunks=npc, t_pad=t_pad, t_valid=t_valid, tm=tm,
                          rows_cap=rows_cap, nblk_lanes=nblk_lanes),
        grid=(2, npc + nsc),
        in_specs=[pl.BlockSpec((chunk, LANES), lambda ph, c: (jnp.minimum(c, npc - 1), 0)),
                  pl.BlockSpec((chunk, LANES), lambda ph, c: (jnp.maximum(c - npc, 0), 0))],
        out_specs=[pl.BlockSpec((chunk, LANES), lambda ph, c: (c * ph, 0)),
                   pl.BlockSpec((SUBLANES, nblk_lanes), lambda ph, c: (0, 0))],
        out_shape=[jax.ShapeDtypeStruct((rp + rs, LANES), F32),
                   jax.ShapeDtypeStruct((SUBLANES, nblk_lanes), F32)],
        scratch_shapes=[pltpu.VMEM((chunk, LANES), F32), pltpu.VMEM((1, LANES), F32),
                        pltpu.VMEM((1, LANES), F32), pltpu.VMEM((1, LANES), F32)],
        compiler_params=_cparams(("arbitrary", "arbitrary"), 32),
        name="route_dest",
    )(route_p, route_s)
    dest = dest_f[:, :EXP_TOPK].astype(I32).reshape(-1)
    blk_e = tab[0, :nblk].astype(I32)
    nused = tab[1, :1].astype(I32)
    return dest, blk_e, nused


def _scatter_rows_kernel(dest_ref, h_ref, xs_in, xs_ref, sem, *, tm, first_tok):
    del xs_in
    base = EXP_TOPK * (first_tok + pl.program_id(0) * tm)
    for r in range(tm):
        for k in range(EXP_TOPK):
            pltpu.make_async_copy(h_ref.at[pl.ds(r, 1)],
                                  xs_ref.at[pl.ds(dest_ref[base + EXP_TOPK * r + k], 1)], sem).start()
    for k in range(EXP_TOPK):
        pltpu.make_async_copy(h_ref, xs_ref.at[pl.ds(0, tm)], sem).wait()


def _scatter_rows(dest, h, xs, first_tok, tm=ROW_TILE):
    rows, d = h.shape
    grid_spec = pltpu.PrefetchScalarGridSpec(
        num_scalar_prefetch=1,
        grid=(rows // tm,),
        in_specs=[pl.BlockSpec((tm, d), lambda i, de: (i, 0)), pl.BlockSpec(memory_space=pl.ANY)],
        out_specs=pl.BlockSpec(memory_space=pl.ANY),
        scratch_shapes=[pltpu.SemaphoreType.DMA(())],
    )
    return pl.pallas_call(
        functools.partial(_scatter_rows_kernel, tm=tm, first_tok=first_tok),
        grid_spec=grid_spec,
        out_shape=jax.ShapeDtypeStruct(xs.shape, xs.dtype),
        input_output_aliases={2: 0},
        compiler_params=_cparams(("arbitrary",), 32),
        name="scatter_rows",
    )(dest, h, xs)


def _moe_kernel(blk_e_ref, nused_ref, x_ref, w1_hbm, w3_hbm, w2_hbm, o_ref, wf_ref, wb_ref, sem):
    i = pl.program_id(0)
    nused = nused_ref[0]
    e = blk_e_ref[i]
    w_hbm = (w1_hbm, w3_hbm, w2_hbm)

    def start_fetch(expert):
        for k in range(3):
            pltpu.make_async_copy(w_hbm[k].at[expert], wf_ref.at[k], sem.at[k]).start()

    @pl.when(i == 0)
    def _():
        start_fetch(e)

    @pl.when((i < nused) & ((i == 0) | (e != blk_e_ref[jnp.maximum(i - 1, 0)])))
    def _():
        for k in range(3):
            pltpu.make_async_copy(w_hbm[k].at[e], wf_ref.at[k], sem.at[k]).wait()
            wb_ref[k] = wf_ref[k].astype(BF16)
        nxt = lax.while_loop(lambda j: (j < nused) & (blk_e_ref[jnp.minimum(j, nused - 1)] == e),
                             lambda j: j + 1, i + 1)

        @pl.when(nxt < nused)
        def _():
            start_fetch(blk_e_ref[nxt])

    @pl.when(i < nused)
    def _():
        x = x_ref[...].astype(BF16)
        a = jnp.dot(x, wb_ref[0], preferred_element_type=F32)
        b = jnp.dot(x, wb_ref[1], preferred_element_type=F32)
        hmid = (_silu(a) * b).astype(BF16)
        o_ref[...] = jnp.dot(hmid, wb_ref[2], preferred_element_type=F32)

    @pl.when(i >= nused)
    def _():
        o_ref[...] = jnp.zeros(o_ref.shape, o_ref.dtype)


def _moe_experts(xs, blk_e, nused, w1, w3, w2, tm=MOE_TILE):
    nblk = blk_e.shape[0]
    d = xs.shape[1]
    wspec = pl.BlockSpec(memory_space=pl.ANY)
    grid_spec = pltpu.PrefetchScalarGridSpec(
        num_scalar_prefetch=2,
        grid=(nblk,),
        in_specs=[pl.BlockSpec((tm, d), lambda i, be, nu: (i, 0)), wspec, wspec, wspec],
        out_specs=pl.BlockSpec((tm, d), lambda i, be, nu: (i, 0)),
        scratch_shapes=[pltpu.VMEM((3, d, d), F32), pltpu.VMEM((3, d, d), BF16), pltpu.SemaphoreType.DMA((3,))],
    )
    return pl.pallas_call(
        _moe_kernel,
        grid_spec=grid_spec,
        out_shape=jax.ShapeDtypeStruct((nblk * tm, d), F32),
        compiler_params=_cparams(("arbitrary",), 40),
        name="moe_experts",
    )(blk_e, nused, xs, w1, w3, w2)


def _combine_kernel(dest_ref, x2_ref, r_ref, g_ref, outs_hbm, y_ref, buf, sem, *, tm):
    i = pl.program_id(0)
    n = pl.num_programs(0)
    n_rows = outs_hbm.shape[0]

    def start_tile(tile, slot):
        base = EXP_TOPK * tile * tm
        for r in range(tm):
            for k in range(EXP_TOPK):
                src = dest_ref[base + EXP_TOPK * r + k]
                src = jnp.where(src >= n_rows, src - n_rows, src)
                pltpu.make_async_copy(outs_hbm.at[pl.ds(src, 1)], buf.at[slot, k, pl.ds(r, 1)],
                                      sem.at[slot]).start()

    @pl.when(i == 0)
    def _():
        start_tile(0, 0)

    @pl.when(i + 1 < n)
    def _():
        start_tile(i + 1, (i + 1) % 2)

    slot = i % 2
    for k in range(EXP_TOPK):
        pltpu.make_async_copy(outs_hbm.at[pl.ds(0, tm)], buf.at[slot, k], sem.at[slot]).wait()
    route = r_ref[...]
    y = buf[slot, 0] * route[:, 2:3] + buf[slot, 1] * route[:, 3:4]
    y_ref[...] = _rms(x2_ref[...] + y, g_ref[...])


def _combine(x2, route, g_final, outs, dest, tm=ROW_TILE):
    rows, d = x2.shape
    grid_spec = pltpu.PrefetchScalarGridSpec(
        num_scalar_prefetch=1,
        grid=(rows // tm,),
        in_specs=[pl.BlockSpec((tm, d), lambda i, de: (i, 0)),
                  pl.BlockSpec((tm, LANES), lambda i, de: (i, 0)),
                  pl.BlockSpec((1, d), lambda i, de: (0, 0)),
                  pl.BlockSpec(memory_space=pl.ANY)],
        out_specs=pl.BlockSpec((tm, d), lambda i, de: (i, 0)),
        scratch_shapes=[pltpu.VMEM((2, EXP_TOPK, tm, d), F32), pltpu.SemaphoreType.DMA((2,))],
    )
    return pl.pallas_call(
        functools.partial(_combine_kernel, tm=tm),
        grid_spec=grid_spec,
        out_shape=jax.ShapeDtypeStruct((rows, d), F32),
        compiler_params=_cparams(("arbitrary",), 32),
        name="moe_combine",
    )(dest, x2, route, g_final.reshape(1, d), outs)


def _layer_tail(x, o, att, att_transposed, mem_k, mem_v, batch, t, xq_tile, wts):
    q_dtype = BF16 if xq_tile % (2 * SUBLANES) == 0 else F32
    x1, q = _outproj(x, o, att, wts["w_out"], wts["g_cross"], wts["w_xq"], att_transposed, q_dtype)
    xo = _xattn(q, mem_k, mem_v, batch, t, xq_tile)
    x2, h, route = _xo_router(x1, xo, wts["w_xo"], wts["g_ffn"], wts["w_router"], wts["b_router"])
    return x2, h, route


def _moe_and_final(x2_p, h_p, route_p, x2_s, h_s, route_s, n_sample_tokens, t_pad, t_valid, wts):
    rp, d = h_p.shape
    rs = h_s.shape[0]
    tm = MOE_TILE
    n_assign = (rp + n_sample_tokens) * EXP_TOPK
    nblk = -(-(n_assign + N_EXPERTS * (tm - 1)) // tm)
    dest, blk_e, nused = _route_dest(route_p, route_s, t_pad, t_valid, tm, nblk)
    xs = jnp.zeros((nblk * tm + EXP_TOPK * rs, d), F32)
    xs = _scatter_rows(dest, h_p, xs, 0)
    xs = _scatter_rows(dest, h_s, xs, rp)
    outs = _moe_experts(xs, blk_e, nused, wts["w1"], wts["w3"], wts["w2"])
    y_p = _combine(x2_p, route_p, wts["g_final"], outs, dest[:EXP_TOPK * rp])
    y_s = _combine(x2_s, route_s, wts["g_final"], outs, dest[EXP_TOPK * rp:])
    return y_p, y_s


def kernel(x_prompt, x_sample, state_hgrn, cache_moba_k, cache_moba_v, cache_mem_k, cache_mem_v, page_table,
           mem_prompt, g_mix, w_in, hg_lb, hg_norm, w_out, g_cross, g_mem, w_xq, w_xk, w_xv, w_xo, g_ffn,
           w_grp, b_grp, w_exp, b_exp, w1, w3, w2, g_final):
    depth = g_mix.shape[0]
    assert depth == 1
    bp, tp, d = x_prompt.shape
    bs, ts, _ = x_sample.shape
    mem_len = mem_prompt.shape[1]
    l = 0

    lb = jnp.cumsum(jax.nn.softmax(hg_lb.astype(F32), axis=0), axis=0)[l]
    slopes = jnp.asarray(np.power(2.0, -8.0 * np.arange(1, MB_HEADS + 1) / MB_HEADS).astype(np.float32))
    n_router = N_GROUPS + N_EXPERTS
    w_router = jnp.pad(jnp.concatenate([w_grp[l], w_exp[l]], axis=1), ((0, 0), (0, LANES - n_router)))
    b_router = jnp.pad(jnp.concatenate([b_grp[l], b_exp[l]]), (0, LANES - n_router)).reshape(1, LANES)
    wts = {
        "w_out": w_out[l].astype(BF16), "g_cross": g_cross[l], "w_xq": w_xq[l].astype(BF16),
        "w_xo": w_xo[l].astype(BF16), "g_ffn": g_ffn[l], "w_router": w_router.astype(BF16),
        "b_router": b_router.astype(F32), "w1": w1[l], "w3": w3[l], "w2": w2[l], "g_final": g_final,
    }
    w_in_bf = w_in[l].astype(BF16)
    o_hq, o_mq, o_mk, o_mv = 0, 4 * HG_WIDTH, 4 * HG_WIDTH + MB_WIDTH, 4 * HG_WIDTH + 2 * MB_WIDTH
    o_end = o_mv + MB_WIDTH

    xp = x_prompt.reshape(bp * tp, d)
    w_kv = jnp.concatenate([w_xk[l], w_xv[l]], axis=1).astype(BF16)
    memk_p, memv_p = _rms_proj(mem_prompt.reshape(bp * mem_len, d), g_mem[l], w_kv,
                               [(0, d, [(1.0, False)]), (d, 2 * d, [(1.0, False)])], [F32, F32])
    hp_p, qt_p, kt_p, vt_p = _rms_proj(
        xp, g_mix[l], w_in_bf,
        [(o_hq, o_mq, [(1.0, False)]), (o_mq, o_mk, [(MB_DIM ** -0.5, True)]), (o_mk, o_mv, [(1.0, "batched")]),
         (o_mv, o_end, [(1.0, "batched")])],
        [F32, BF16, F32, F32], seq_len=tp)
    o_p, s_p = _hgrn(hp_p, lb, hg_norm[l], None, bp, tp, tp, 128)
    att_p = _moba_prompt(qt_p, kt_p, vt_p, slopes, bp, tp)
    x2_p, h_p, route_p = _layer_tail(xp, o_p, att_p, True, memk_p, memv_p, bp, tp, 512, wts)

    tpad = SAMPLE_PAD_T
    xs = jnp.pad(x_sample, ((0, 0), (0, tpad - ts), (0, 0))).reshape(bs * tpad, d)
    hp_s, mq_s, mk_s, mv_s = _rms_proj(
        xs, g_mix[l], w_in_bf,
        [(o_hq, o_mq, [(1.0, False)]), (o_mq, o_mk, [(1.0, False)]), (o_mk, o_mv, [(1.0, False)]),
         (o_mv, o_end, [(1.0, False)])],
        [F32, F32, F32, F32])
    o_s, s_s = _hgrn(hp_s, lb, hg_norm[l], state_hgrn[l], bs, tpad, ts, tpad)
    mk4 = mk_s.reshape(bs, tpad, MB_HEADS, MB_DIM)
    mv4 = mv_s.reshape(bs, tpad, MB_HEADS, MB_DIM)
    n_phys, page = cache_moba_k.shape[1], cache_moba_k.shape[2]
    pool_kt = jnp.transpose(cache_moba_k[l], (0, 2, 3, 1)).reshape(n_phys, MB_WIDTH, page)
    pool_vt = jnp.transpose(cache_moba_v[l], (0, 2, 3, 1)).reshape(n_phys, MB_WIDTH, page)
    att_s = _moba_sample(mq_s, mk_s, mv_s, pool_kt, pool_vt, page_table, ts)

    def mem_rows(c):
        c = c[l].reshape(bs, mem_len, XA_HEADS, XA_DIM // LANES, LANES)
        return jnp.transpose(c, (0, 1, 3, 2, 4)).reshape(bs, mem_len * d // LANES, LANES)

    memk_s, memv_s = mem_rows(cache_mem_k), mem_rows(cache_mem_v)
    x2_s, h_s, route_s = _layer_tail(xs, o_s, att_s, False, memk_s, memv_s, bs, tpad, tpad, wts)
    y_p, y_s = _moe_and_final(x2_p, h_p, route_p, x2_s, h_s, route_s, bs * ts, tpad, ts, wts)

    return (y_p.reshape(bp, tp, d),
            y_s.reshape(bs, tpad, d)[:, :ts],
            s_p.reshape(1, bp, HG_HEADS, HG_DIM, HG_DIM),
            jnp.transpose(kt_p.reshape(1, bp, MB_HEADS, MB_DIM, tp), (0, 1, 4, 2, 3)),
            jnp.transpose(vt_p.reshape(1, bp, MB_HEADS, MB_DIM, tp), (0, 1, 4, 2, 3)),
            memk_p.reshape(1, bp, mem_len, XA_HEADS, XA_DIM),
            memv_p.reshape(1, bp, mem_len, XA_HEADS, XA_DIM),
            s_s.reshape(1, bs, HG_HEADS, HG_DIM, HG_DIM),
            mk4[:, :ts].reshape(1, bs, ts, MB_HEADS, MB_DIM),
            mv4[:, :ts].reshape(1, bs, ts, MB_HEADS, MB_DIM))
```

```python
import functools

import numpy as np
import jax
import jax.numpy as jnp
from jax import lax
from jax.experimental import pallas as pl
from jax.experimental.pallas import tpu as pltpu

F32 = jnp.float32
BF16 = jnp.bfloat16
I32 = jnp.int32

D_MODEL = 1024
HG_HEADS = 4
HG_DIM = 128
HG_WIDTH = HG_HEADS * HG_DIM
MB_HEADS = 8
MB_DIM = 64
MB_WIDTH = MB_HEADS * MB_DIM
MB_BLOCK = 256
MB_TOPK = 3
XA_HEADS = 4
XA_DIM = 256
N_GROUPS = 4
EXP_PER_GROUP = 8
N_EXPERTS = N_GROUPS * EXP_PER_GROUP
EXP_TOPK = 2
RMS_EPS = 1e-6
NEG_INF = float("-inf")

LANES = 128
SUBLANES = 8
MIB = 1024 * 1024

ROW_TILE = 256
MOE_TILE = 256
SAMPLE_PAD_T = 8
MOBA_PAIRS = 4
XATTN_SEQS_PER_STEP = 4

_NT = (((1,), (1,)), ((), ()))


def _cparams(semantics, vmem_mib):
    return pltpu.CompilerParams(dimension_semantics=semantics, vmem_limit_bytes=vmem_mib * MIB)


def _rms(x, g):
    return x * lax.rsqrt(jnp.mean(x * x, axis=-1, keepdims=True) + RMS_EPS) * g


def _silu(x):
    return x * jax.nn.sigmoid(x)


def _rms_proj_kernel(x_ref, g_ref, w_ref, *o_refs, segs):
    h = _rms(x_ref[...], g_ref[...]).astype(BF16)
    k = 0
    for lo, hi, outs in segs:
        r = jnp.dot(h, w_ref[:, lo:hi], preferred_element_type=F32)
        for scale, transposed in outs:
            y = r if scale == 1.0 else r * scale
            if transposed:
                y = y.T
            o_refs[k][...] = y.astype(o_refs[k].dtype)
            k += 1


def _rms_proj(x, g, w_bf, segs, dtypes, tm=ROW_TILE, vmem_mib=48, seq_len=None):
    rows, d = x.shape
    assert rows % tm == 0
    out_shape, out_specs = [], []
    k = 0
    for lo, hi, outs in segs:
        n = hi - lo
        for _, transposed in outs:
            if transposed == "batched":
                per_seq = seq_len // tm
                out_shape.append(jax.ShapeDtypeStruct((rows // seq_len, n, seq_len), dtypes[k]))
                out_specs.append(pl.BlockSpec((None, n, tm), lambda i: (i // per_seq, 0, i % per_seq)))
            elif transposed:
                out_shape.append(jax.ShapeDtypeStruct((n, rows), dtypes[k]))
                out_specs.append(pl.BlockSpec((n, tm), lambda i: (0, i)))
            else:
                out_shape.append(jax.ShapeDtypeStruct((rows, n), dtypes[k]))
                out_specs.append(pl.BlockSpec((tm, n), lambda i: (i, 0)))
            k += 1
    return pl.pallas_call(
        functools.partial(_rms_proj_kernel, segs=segs),
        grid=(rows // tm,),
        in_specs=[pl.BlockSpec((tm, d), lambda i: (i, 0)),
                  pl.BlockSpec((1, d), lambda i: (0, 0)),
                  pl.BlockSpec(w_bf.shape, lambda i: (0, 0))],
        out_specs=out_specs,
        out_shape=out_shape,
        compiler_params=_cparams(("parallel",), vmem_mib),
        name="rms_proj",
    )(x, g.reshape(1, d), w_bf)


def _cumsum_rows(x):
    n = x.shape[0]
    row = lax.broadcasted_iota(I32, x.shape, 0)
    s = 1
    while s < n:
        x = x + jnp.where(row >= s, pltpu.roll(x, s, 0), 0.0)
        s *= 2
    return x


def _hgrn_intra(q, kk, b, c, w):
    row = lax.broadcasted_iota(I32, (c, HG_DIM), 0)
    acc = None
    half = c // 2
    while half >= SUBLANES:
        two = 2 * half
        nblk = c // two
        pieces = [jnp.broadcast_to(b[i * two + half - 1:i * two + half, :], (two, HG_DIM)) for i in range(nblk)]
        bm = pieces[0] if nblk == 1 else jnp.concatenate(pieces, axis=0)
        second = (row & (two - 1)) >= half
        qt = jnp.where(second, q * jnp.exp(jnp.where(second, b - bm, 0.0)), 0.0)
        kt = jnp.where(second, 0.0, kk * jnp.exp(jnp.where(second, 0.0, bm - b)))
        al = lax.dot_general(qt.astype(BF16), kt.astype(BF16), _NT, preferred_element_type=F32)
        if nblk > 1:
            shift = two.bit_length() - 1
            rt = lax.broadcasted_iota(I32, (c, c), 0) >> shift
            cs = lax.broadcasted_iota(I32, (c, c), 1) >> shift
            al = jnp.where(rt == cs, al, 0.0)
        acc = al if acc is None else acc + al
        half //= 2
    r8 = lax.broadcasted_iota(I32, (SUBLANES, HG_DIM), 0)
    lane = lax.broadcasted_iota(I32, (SUBLANES, w), 1)
    blocks = []
    for g in range(c // SUBLANES):
        lo = g * SUBLANES
        qg, kg, bg = q[lo:lo + SUBLANES], kk[lo:lo + SUBLANES], b[lo:lo + SUBLANES]
        ag = jnp.zeros((SUBLANES, w), F32)
        for s in range(SUBLANES):
            e = jnp.exp(jnp.where(r8 >= s, bg - bg[s:s + 1, :], NEG_INF))
            p = jnp.sum(qg * kg[s:s + 1, :] * e, axis=1, keepdims=True)
            ag = jnp.where(lane == lo + s, p, ag)
        blocks.append(ag)
    diag = blocks[0] if len(blocks) == 1 else jnp.concatenate(blocks, axis=0)
    return diag if acc is None else acc + diag


def _hgrn_kernel(*refs, c, t_valid, has_s0):
    if has_s0:
        hp_ref, lb_ref, gn_ref, s0_ref, o_ref, sout_ref, st_ref = refs
    else:
        hp_ref, lb_ref, gn_ref, o_ref, sout_ref, st_ref = refs
    ci = pl.program_id(1)
    last = pl.num_programs(1) - 1
    w = max(c, LANES)

    @pl.when(ci == 0)
    def _():
        for h in range(HG_HEADS):
            st_ref[h] = s0_ref[0, h].T if has_s0 else jnp.zeros((HG_DIM, HG_DIM), F32)

    row = lax.broadcasted_iota(I32, (c, HG_DIM), 0)
    for h in range(HG_HEADS):
        lo = h * HG_DIM
        hq = hp_ref[:, lo:lo + HG_DIM]
        hf = hp_ref[:, HG_WIDTH + lo:HG_WIDTH + lo + HG_DIM]
        v = hp_ref[:, 2 * HG_WIDTH + lo:2 * HG_WIDTH + lo + HG_DIM]
        hg = hp_ref[:, 3 * HG_WIDTH + lo:3 * HG_WIDTH + lo + HG_DIM]
        lb = lb_ref[:, lo:lo + HG_DIM]
        q = _silu(hq)
        f = lb + (1.0 - lb) * jax.nn.sigmoid(hf)
        logf = jnp.log(f)
        kk = 1.0 - f
        if t_valid < c:
            valid = row < t_valid
            logf = jnp.where(valid, logf, 0.0)
            kk = jnp.where(valid, kk, 0.0)
            v = jnp.where(valid, v, 0.0)
        b = _cumsum_rows(logf)
        a = _hgrn_intra(q, kk, b, c, w)
        st = st_ref[h]
        bl = b[c - 1:c, :]
        k2 = kk * jnp.exp(bl - b)
        if c < w:
            zpad = jnp.zeros((w - c, HG_DIM), F32)
            vp = jnp.concatenate([v, zpad], axis=0)
            k2 = jnp.concatenate([k2, zpad], axis=0)
        else:
            vp = v
        vb = vp.astype(BF16)
        o = lax.dot_general((q * jnp.exp(b)).astype(BF16), st.astype(BF16), _NT, preferred_element_type=F32)
        o = o + jnp.dot(a.astype(BF16), vb, preferred_element_type=F32)
        o_ref[:, lo:lo + HG_DIM] = (_rms(o, gn_ref[...]) * _silu(hg)).astype(o_ref.dtype)
        st_new = st * jnp.exp(bl) + jnp.dot(vp.T.astype(BF16), k2.astype(BF16), preferred_element_type=F32)
        st_ref[h] = st_new

        @pl.when(ci == last)
        def _():
            sout_ref[0, h] = st_new.T


def _hgrn(hp, lb, gn, s0, batch, t_pad, t_valid, c):
    rows = hp.shape[0]
    nc = t_pad // c
    has_s0 = s0 is not None
    in_specs = [pl.BlockSpec((c, 4 * HG_WIDTH), lambda b, i: (b * nc + i, 0)),
                pl.BlockSpec((1, HG_WIDTH), lambda b, i: (0, 0)),
                pl.BlockSpec((1, HG_DIM), lambda b, i: (0, 0))]
    args = [hp, lb.reshape(1, HG_WIDTH), gn.reshape(1, HG_DIM)]
    if has_s0:
        in_specs.append(pl.BlockSpec((1, HG_HEADS, HG_DIM, HG_DIM), lambda b, i: (b, 0, 0, 0)))
        args.append(s0)
    return pl.pallas_call(
        functools.partial(_hgrn_kernel, c=c, t_valid=min(t_valid, c), has_s0=has_s0),
        grid=(batch, nc),
        in_specs=in_specs,
        out_specs=[pl.BlockSpec((c, HG_WIDTH), lambda b, i: (b * nc + i, 0)),
                   pl.BlockSpec((1, HG_HEADS, HG_DIM, HG_DIM), lambda b, i: (b, 0, 0, 0))],
        out_shape=[jax.ShapeDtypeStruct((rows, HG_WIDTH), BF16),
                   jax.ShapeDtypeStruct((batch, HG_HEADS, HG_DIM, HG_DIM), F32)],
        scratch_shapes=[pltpu.VMEM((HG_HEADS, HG_DIM, HG_DIM), F32)],
        compiler_params=_cparams(("parallel", "arbitrary"), 32),
        name="hgrn2",
    )(*args)


def _topk_block_mask(gate, n_valid, nrow):
    nb = gate.shape[0]
    cnt = jnp.zeros(gate.shape, I32)
    for n2 in range(nb):
        g2 = gate[n2:n2 + 1, :]
        beats = (g2 > gate) | ((g2 == gate) & (n2 < nrow))
        cnt = cnt + jnp.where(beats, (n_valid > n2).astype(I32), 0)
    return (cnt < MB_TOPK) & (nrow < n_valid)


def _moba_prompt_kernel(slope_ref, qt_ref, kt_ref, vt_ref, o_ref, kb_ref, vb_ref, km_ref, sel_ref, *, nb, npairs):
    hg = pl.program_id(1)
    qi = pl.program_id(2)
    tq = MB_BLOCK
    pw = 2 * tq
    width = npairs * pw
    pair_rows = 2 * MB_DIM

    @pl.when(qi == 0)
    def _():
        pos = jnp.where(lax.broadcasted_iota(I32, (MB_BLOCK, pair_rows), 1) == 0,
                        lax.broadcasted_iota(I32, (MB_BLOCK, pair_rows), 0), 0).astype(BF16)
        for pp in range(npairs):
            means = []
            for j in range(nb):
                kj = kt_ref[pp * pair_rows:(pp + 1) * pair_rows, j * MB_BLOCK:(j + 1) * MB_BLOCK].T
                kb_ref[pp, j * MB_BLOCK:(j + 1) * MB_BLOCK, :pair_rows] = kj.astype(BF16)
                kb_ref[pp, j * MB_BLOCK:(j + 1) * MB_BLOCK, pair_rows:] = pos
                means.append(jnp.sum(kj, axis=0, keepdims=True) * (1.0 / MB_BLOCK))
                vb_ref[pp, j] = vt_ref[pp * pair_rows:(pp + 1) * pair_rows,
                                       j * MB_BLOCK:(j + 1) * MB_BLOCK].astype(BF16)
            km_ref[pp] = jnp.concatenate(means, axis=0)

    second_head = lax.broadcasted_iota(I32, (pair_rows, tq), 0) >= MB_DIM
    first_row = lax.broadcasted_iota(I32, (pair_rows, pw), 0) == 0
    qaugs, gates, slope_parts = [], [], []
    for pp in range(npairs):
        qt2 = qt_ref[pp * pair_rows:(pp + 1) * pair_rows, :]
        zero = jnp.zeros_like(qt2)
        qcat = jnp.concatenate([jnp.where(second_head, zero, qt2), jnp.where(second_head, qt2, zero)], axis=1)
        gates.append(jnp.dot(km_ref[pp].astype(BF16), qcat, preferred_element_type=F32))
        pair_slopes = jnp.concatenate(
            [jnp.full((1, tq), slope_ref[(hg * npairs + pp) * 2 + i], F32) for i in range(2)], axis=1)
        slope_parts.append(pair_slopes)
        srows = jnp.where(first_row, pair_slopes, 0.0).astype(BF16)
        qaugs.append(jnp.concatenate([qcat, srows], axis=0))
    slope_row = slope_parts[0] if npairs == 1 else jnp.concatenate(slope_parts, axis=1)
    gate = gates[0] if npairs == 1 else jnp.concatenate(gates, axis=1)
    nrow = lax.broadcasted_iota(I32, (nb, width), 0)
    sel = _topk_block_mask(gate, qi, nrow).astype(F32)
    for n in range(nb):
        sel_ref[n] = jnp.broadcast_to(sel[n:n + 1, :], (SUBLANES, width))

    def scores(j):
        start = pl.multiple_of(j * MB_BLOCK, MB_BLOCK)
        parts = [jnp.dot(kb_ref[pp, pl.ds(start, MB_BLOCK), :], qaugs[pp], preferred_element_type=F32)
                 for pp in range(npairs)]
        return parts[0] if npairs == 1 else jnp.concatenate(parts, axis=1)

    def values(j, p):
        pb = p.astype(BF16)
        return [jnp.dot(vb_ref[pp, j], pb[:, pp * pw:(pp + 1) * pw], preferred_element_type=F32)
                for pp in range(npairs)]

    lane = lax.broadcasted_iota(I32, (MB_BLOCK, width), 1) & (tq - 1)
    rowk = lax.broadcasted_iota(I32, (MB_BLOCK, width), 0)
    s = jnp.where(rowk <= lane, scores(qi), NEG_INF)
    m = jnp.max(s, axis=0, keepdims=True)
    p = jnp.exp(s - m)
    l = jnp.sum(p, axis=0, keepdims=True)
    accs = values(qi, p)

    def body(j, carry):
        m, l, accs = carry
        off = jnp.full((1, width), (qi - j) * MB_BLOCK, I32).astype(F32) * slope_row
        picked = sel_ref[j][0:1, :] > 0.5
        s = scores(j)
        m_new = jnp.maximum(m, jnp.where(picked, jnp.max(s, axis=0, keepdims=True) - off, NEG_INF))
        alpha = jnp.exp(m - m_new)
        p = jnp.exp(s - jnp.where(picked, m_new + off, float("inf")))
        l = alpha * l + jnp.sum(p, axis=0, keepdims=True)
        pv = values(j, p)
        accs = [alpha[:, pp * pw:(pp + 1) * pw] * accs[pp] + pv[pp] for pp in range(npairs)]
        return m_new, l, accs

    m, l, accs = lax.fori_loop(0, qi, body, (m, l, accs))
    for pp in range(npairs):
        a = accs[pp] / l[:, pp * pw:(pp + 1) * pw]
        o_ref[pp * pair_rows:(pp + 1) * pair_rows, :] = jnp.where(second_head, a[:, tq:], a[:, :tq])


def _moba_prompt(qt, kt, vt, slopes, batch, t, npairs=MOBA_PAIRS):
    nb = t // MB_BLOCK
    rows = batch * t
    gr = 2 * MB_DIM * npairs
    width = npairs * 2 * MB_BLOCK
    return pl.pallas_call(
        functools.partial(_moba_prompt_kernel, nb=nb, npairs=npairs),
        grid=(batch, MB_WIDTH // gr, nb),
        in_specs=[pl.BlockSpec(memory_space=pltpu.SMEM),
                  pl.BlockSpec((gr, MB_BLOCK), lambda b, h, i: (h, b * nb + i)),
                  pl.BlockSpec((None, gr, t), lambda b, h, i: (b, h, 0)),
                  pl.BlockSpec((None, gr, t), lambda b, h, i: (b, h, 0))],
        out_specs=pl.BlockSpec((gr, MB_BLOCK), lambda b, h, i: (h, b * nb + i)),
        out_shape=jax.ShapeDtypeStruct((MB_WIDTH, rows), F32),
        scratch_shapes=[pltpu.VMEM((npairs, t, 4 * MB_DIM), BF16),
                        pltpu.VMEM((npairs, nb, 2 * MB_DIM, MB_BLOCK), BF16),
                        pltpu.VMEM((npairs, nb, 2 * MB_DIM), F32),
                        pltpu.VMEM((nb, SUBLANES, width), F32)],
        compiler_params=_cparams(("parallel", "parallel", "arbitrary"), 48),
        name="moba_prompt",
    )(slopes, qt, kt, vt)


def _moba_sample_kernel(pt_ref, slope_ref, hm_ref, q_ref, kn_ref, vn_ref, *rest, npages, page, t_new, past_len):
    del pt_ref
    k_pages = rest[:npages]
    v_pages = rest[npages:2 * npages]
    o_ref = rest[2 * npages]
    nrows = t_new * MB_HEADS
    pages_per_block = MB_BLOCK // page
    nb = npages // pages_per_block
    hm = hm_ref[...]
    slope = slope_ref[...][:, 0:1]
    row_q = lax.broadcasted_iota(I32, (nrows, 1), 0) >> 3
    pos_q = past_len + row_q

    q = q_ref[...] * (MB_DIM ** -0.5)
    qbd = jnp.concatenate([jnp.broadcast_to(q[t:t + 1, :], (MB_HEADS, MB_WIDTH)) * hm for t in range(t_new)],
                          axis=0).astype(BF16)

    lane = lax.broadcasted_iota(I32, (nrows, page), 1)
    scores = []
    gates = [None] * nb
    for p in range(npages):
        s = jnp.dot(qbd, k_pages[p][0].astype(BF16), preferred_element_type=F32)
        rs = jnp.sum(s, axis=1, keepdims=True)
        n = p // pages_per_block
        gates[n] = rs if gates[n] is None else gates[n] + rs
        scores.append(s - slope * (pos_q - (p * page + lane)).astype(F32))
    sels = []
    for n in range(nb):
        cnt = jnp.zeros((nrows, 1), I32)
        for n2 in range(nb):
            beats = (gates[n2] > gates[n]) | ((gates[n2] == gates[n]) & (n2 < n))
            cnt = cnt + jnp.where(beats, 1, 0)
        sels.append(jnp.where(cnt < MB_TOPK, 1.0, 0.0))

    zrows = jnp.zeros((LANES - SAMPLE_PAD_T, MB_WIDTH), F32)
    knp = jnp.concatenate([kn_ref[...], zrows], axis=0).astype(BF16)
    vnp = jnp.concatenate([vn_ref[...], zrows], axis=0).astype(BF16)
    lane_n = lax.broadcasted_iota(I32, (nrows, LANES), 1)
    s_new = lax.dot_general(qbd, knp, _NT, preferred_element_type=F32)
    s_new = jnp.where(lane_n <= row_q, s_new - slope * (row_q - lane_n).astype(F32), NEG_INF)
    m = jnp.max(s_new, axis=1, keepdims=True)
    for p in range(npages):
        sel_p = jnp.broadcast_to(sels[p // pages_per_block], (nrows, page)) > 0.5
        scores[p] = jnp.where(sel_p, scores[p], NEG_INF)
        m = jnp.maximum(m, jnp.max(scores[p], axis=1, keepdims=True))

    p_new = jnp.exp(s_new - m)
    l = jnp.sum(p_new, axis=1, keepdims=True)
    acc = jnp.dot(p_new.astype(BF16), vnp, preferred_element_type=F32)
    for p in range(npages):
        pr = jnp.exp(scores[p] - m)
        l = l + jnp.sum(pr, axis=1, keepdims=True)
        acc = acc + lax.dot_general(pr.astype(BF16), v_pages[p][0].astype(BF16), _NT,
                                    preferred_element_type=F32)
    acc = acc / l
    rows = [jnp.sum(acc[t * MB_HEADS:(t + 1) * MB_HEADS, :] * hm, axis=0, keepdims=True) for t in range(t_new)]
    rows.append(jnp.zeros((SAMPLE_PAD_T - t_new, MB_WIDTH), F32))
    o_ref[...] = jnp.concatenate(rows, axis=0)


def _moba_sample(mq, mk, mv, pool_kt, pool_vt, page_table, t_new):
    batch, npages = page_table.shape
    page = pool_kt.shape[2]
    past_len = npages * page
    assert past_len % MB_BLOCK == 0 and MB_BLOCK % page == 0 and t_new <= SAMPLE_PAD_T and page == LANES
    nrows = t_new * MB_HEADS
    slopes = np.power(2.0, -8.0 * np.arange(1, MB_HEADS + 1) / MB_HEADS).astype(np.float32)
    slope_rows = jnp.asarray(np.tile(np.tile(slopes, t_new)[:, None], (1, LANES)))
    head_mask = jnp.asarray((np.arange(MB_WIDTH)[None, :] // MB_DIM == np.arange(MB_HEADS)[:, None])
                            .astype(np.float32))
    new_spec = pl.BlockSpec((SAMPLE_PAD_T, MB_WIDTH), lambda b, pt: (b, 0))
    page_specs = [pl.BlockSpec((1, MB_WIDTH, page),
                               functools.partial(lambda b, pt, i: (pt[b * npages + i], 0, 0), i=i))
                  for i in range(npages)]
    grid_spec = pltpu.PrefetchScalarGridSpec(
        num_scalar_prefetch=1,
        grid=(batch,),
        in_specs=[pl.BlockSpec((nrows, LANES), lambda b, pt: (0, 0)),
                  pl.BlockSpec((MB_HEADS, MB_WIDTH), lambda b, pt: (0, 0)),
                  new_spec, new_spec, new_spec] + page_specs + page_specs,
        out_specs=new_spec,
    )
    return pl.pallas_call(
        functools.partial(_moba_sample_kernel, npages=npages, page=page, t_new=t_new, past_len=past_len),
        grid_spec=grid_spec,
        out_shape=jax.ShapeDtypeStruct((batch * SAMPLE_PAD_T, MB_WIDTH), F32),
        compiler_params=_cparams(("parallel",), 40),
        name="moba_sample",
    )(page_table.reshape(-1), slope_rows, head_mask, mq, mk, mv, *([pool_kt] * npages), *([pool_vt] * npages))


def _outproj_kernel(x_ref, o_ref, att_ref, wo_ref, g_ref, wq_ref, x1_ref, q_ref, *, att_transposed):
    att = att_ref[...]
    if att_transposed:
        att = att.T
    mix = jnp.dot(o_ref[...], wo_ref[:HG_WIDTH, :], preferred_element_type=F32)
    mix = mix + jnp.dot(att.astype(BF16), wo_ref[HG_WIDTH:, :], preferred_element_type=F32)
    x1 = x_ref[...] + mix
    x1_ref[...] = x1
    h = _rms(x1, g_ref[...]).astype(BF16)
    q_ref[...] = (jnp.dot(h, wq_ref[...], preferred_element_type=F32) * (XA_DIM ** -0.5)).astype(q_ref.dtype)


def _outproj(x, o, att, wo_bf, g, wq_bf, att_transposed, q_dtype, tm=ROW_TILE):
    rows, d = x.shape
    att_spec = (pl.BlockSpec((MB_WIDTH, tm), lambda i: (0, i)) if att_transposed
                else pl.BlockSpec((tm, MB_WIDTH), lambda i: (i, 0)))
    return pl.pallas_call(
        functools.partial(_outproj_kernel, att_transposed=att_transposed),
        grid=(rows // tm,),
        in_specs=[pl.BlockSpec((tm, d), lambda i: (i, 0)),
                  pl.BlockSpec((tm, HG_WIDTH), lambda i: (i, 0)),
                  att_spec,
                  pl.BlockSpec((d, d), lambda i: (0, 0)),
                  pl.BlockSpec((1, d), lambda i: (0, 0)),
                  pl.BlockSpec((d, d), lambda i: (0, 0))],
        out_specs=[pl.BlockSpec((tm, d), lambda i: (i, 0)), pl.BlockSpec((tm, d), lambda i: (i, 0))],
        out_shape=[jax.ShapeDtypeStruct((rows, d), F32), jax.ShapeDtypeStruct((rows, d), q_dtype)],
        compiler_params=_cparams(("parallel",), 40),
        name="outproj_q",
    )(x, o, att, wo_bf, g.reshape(1, d), wq_bf)


def _xattn_tiled_kernel(q_ref, k_ref, v_ref, o_ref, *, nt):
    for g in range(k_ref.shape[0]):
        o_ref[g * nt:(g + 1) * nt, :] = _xattn_tiled_one(
            q_ref[g * nt:(g + 1) * nt, :], k_ref[g], v_ref[g], nt).astype(o_ref.dtype)


def _xattn_tiled_one(q, k, v, nt):
    ndt = XA_DIM // LANES
    nc = XA_HEADS * ndt
    half = XA_HEADS * nt
    q = q.astype(F32)
    a = jnp.concatenate([q[:, (h * ndt + dt) * LANES:(h * ndt + dt + 1) * LANES]
                         for dt in range(ndt) for h in range(XA_HEADS)], axis=0).astype(BF16)
    s = lax.dot_general(a, k.astype(BF16), _NT, preferred_element_type=F32)
    width = s.shape[1]
    lane_c = lax.broadcasted_iota(I32, s.shape, 1) & (nc - 1)
    assert nt & (nt - 1) == 0 and nc & (nc - 1) == 0
    row_c = lax.broadcasted_iota(I32, s.shape, 0) >> (nt.bit_length() - 1)
    s = jnp.where(lane_c == row_c, s, 0.0)
    tot = s[:half]
    for dt in range(1, ndt):
        tot = tot + pltpu.roll(s[dt * half:(dt + 1) * half], width - dt * XA_HEADS, 1)
    ok = ((lax.broadcasted_iota(I32, tot.shape, 1) & (nc - 1))
          == (lax.broadcasted_iota(I32, tot.shape, 0) >> (nt.bit_length() - 1)))
    tot = jnp.where(ok, tot, NEG_INF)
    p = jnp.exp(tot - jnp.max(tot, axis=-1, keepdims=True))
    p = p / jnp.sum(p, axis=-1, keepdims=True)
    pe = jnp.concatenate([p] + [pltpu.roll(p, dt * XA_HEADS, 1) for dt in range(1, ndt)], axis=0)
    o = jnp.dot(pe.astype(BF16), v.astype(BF16), preferred_element_type=F32)
    return jnp.concatenate([o[(dt * XA_HEADS + h) * nt:(dt * XA_HEADS + h + 1) * nt, :]
                            for h in range(XA_HEADS) for dt in range(ndt)], axis=1)


def _xattn_kernel(q_ref, k_ref, v_ref, o_ref):
    q = q_ref[...].astype(BF16)
    for h in range(XA_HEADS):
        lo = h * XA_DIM
        kh = k_ref[:, lo:lo + XA_DIM].astype(BF16)
        vh = v_ref[:, lo:lo + XA_DIM].astype(BF16)
        s = lax.dot_general(q[:, lo:lo + XA_DIM], kh, _NT, preferred_element_type=F32)
        p = jnp.exp(s - jnp.max(s, axis=-1, keepdims=True))
        p = p / jnp.sum(p, axis=-1, keepdims=True)
        o_ref[:, lo:lo + XA_DIM] = jnp.dot(p.astype(BF16), vh, preferred_element_type=F32).astype(o_ref.dtype)


def _xattn(q, mem_k, mem_v, batch, t, tq):
    rows, d = q.shape
    nq = t // tq
    if mem_k.ndim == 3:
        assert tq == t and batch % XATTN_SEQS_PER_STEP == 0
        g = XATTN_SEQS_PER_STEP
        kv_spec = pl.BlockSpec((g,) + mem_k.shape[1:], lambda b: (b, 0, 0))
        return pl.pallas_call(
            functools.partial(_xattn_tiled_kernel, nt=t),
            grid=(batch // g,),
            in_specs=[pl.BlockSpec((g * t, d), lambda b: (b, 0)), kv_spec, kv_spec],
            out_specs=pl.BlockSpec((g * t, d), lambda b: (b, 0)),
            out_shape=jax.ShapeDtypeStruct((rows, d), q.dtype),
            compiler_params=_cparams(("parallel",), 40),
            name="xattn_tiled",
        )(q, mem_k, mem_v)
    m = mem_k.shape[0] // batch
    kv_spec = pl.BlockSpec((m, d), lambda b, i: (b, 0))
    return pl.pallas_call(
        _xattn_kernel,
        grid=(batch, nq),
        in_specs=[pl.BlockSpec((tq, d), lambda b, i: (b * nq + i, 0)), kv_spec, kv_spec],
        out_specs=pl.BlockSpec((tq, d), lambda b, i: (b * nq + i, 0)),
        out_shape=jax.ShapeDtypeStruct((rows, d), q.dtype),
        compiler_params=_cparams(("parallel", "arbitrary"), 32),
        name="xattn",
    )(q, mem_k, mem_v)


def _xo_router_kernel(x1_ref, o_ref, wo_ref, g_ref, wr_ref, br_ref, x2_ref, h_ref, r_ref):
    x2 = x1_ref[...] + jnp.dot(o_ref[...].astype(BF16), wo_ref[...], preferred_element_type=F32)
    x2_ref[...] = x2
    h = _rms(x2, g_ref[...])
    h_ref[...] = h
    logits = jnp.dot(h.astype(BF16), wr_ref[...], preferred_element_type=F32) + br_ref[...]
    lane = lax.broadcasted_iota(I32, logits.shape, 1)
    big = jnp.int32(LANES)

    def top1(mask):
        mx = jnp.max(jnp.where(mask, logits, NEG_INF), axis=-1, keepdims=True)
        idx = jnp.min(jnp.where(mask & (logits == mx), lane, big), axis=-1, keepdims=True)
        return mx, idx

    gmask = lane < N_GROUPS
    gmx, gsel = top1(gmask)
    gw = 1.0 / jnp.sum(jnp.where(gmask, jnp.exp(logits - gmx), 0.0), axis=-1, keepdims=True)
    elo = N_GROUPS + gsel * EXP_PER_GROUP
    emask = (lane >= elo) & (lane < elo + EXP_PER_GROUP)
    m1, i1 = top1(emask)
    m2, i2 = top1(emask & (lane != i1))
    e2 = jnp.exp(m2 - m1)
    g1 = gw / (1.0 + e2)
    g2 = gw * e2 / (1.0 + e2)
    out = jnp.where(lane == 0, (i1 - N_GROUPS).astype(F32), 0.0)
    out = jnp.where(lane == 1, (i2 - N_GROUPS).astype(F32), out)
    out = jnp.where(lane == 2, g1, out)
    out = jnp.where(lane == 3, g2, out)
    r_ref[...] = out


def _xo_router(x1, o, wxo_bf, g, wr_bf, br, tm=ROW_TILE):
    rows, d = x1.shape
    return pl.pallas_call(
        _xo_router_kernel,
        grid=(rows // tm,),
        in_specs=[pl.BlockSpec((tm, d), lambda i: (i, 0)),
                  pl.BlockSpec((tm, d), lambda i: (i, 0)),
                  pl.BlockSpec((d, d), lambda i: (0, 0)),
                  pl.BlockSpec((1, d), lambda i: (0, 0)),
                  pl.BlockSpec((d, LANES), lambda i: (0, 0)),
                  pl.BlockSpec((1, LANES), lambda i: (0, 0))],
        out_specs=[pl.BlockSpec((tm, d), lambda i: (i, 0)),
                   pl.BlockSpec((tm, d), lambda i: (i, 0)),
                   pl.BlockSpec((tm, LANES), lambda i: (i, 0))],
        out_shape=[jax.ShapeDtypeStruct((rows, d), F32),
                   jax.ShapeDtypeStruct((rows, d), F32),
                   jax.ShapeDtypeStruct((rows, LANES), F32)],
        compiler_params=_cparams(("parallel",), 40),
        name="xo_router",
    )(x1, o, wxo_bf, g.reshape(1, d), wr_bf, br)


def _lane_cumsum(x):
    lane = lax.broadcasted_iota(I32, x.shape, 1)
    s = 1
    while s < LANES:
        x = x + jnp.where(lane >= s, pltpu.roll(x, s, 1), 0.0)
        s *= 2
    return x


def _route_dest_kernel(rp_ref, rs_ref, dest_ref, tab_ref, last_ref, r_buf, cnt_ref, carry_ref, pstart_ref, *,
                       n_prompt_chunks, t_pad, t_valid, tm, rows_cap, nblk_lanes):
    ph = pl.program_id(0)
    c = pl.program_id(1)
    chunk = r_buf.shape[0]
    is_prompt = c < n_prompt_chunks

    @pl.when(is_prompt)
    def _():
        r_buf[...] = rp_ref[...]

    @pl.when(jnp.logical_not(is_prompt))
    def _():
        r_buf[...] = rs_ref[...]

    @pl.when((ph == 0) & (c == 0))
    def _():
        cnt_ref[...] = jnp.zeros(cnt_ref.shape, F32)

    route = r_buf[...]
    lane = lax.broadcasted_iota(I32, (chunk, LANES), 1)
    row = lax.broadcasted_iota(I32, (chunk, LANES), 0)
    lanef = lane.astype(F32)
    valid = (((row & (t_pad - 1)) < t_valid).astype(I32) | is_prompt.astype(I32)) > 0
    oh0 = jnp.where((lanef == route[:, 0:1]) & valid, 1.0, 0.0)
    oh1 = jnp.where((lanef == route[:, 1:2]) & valid, 1.0, 0.0)
    cmat = oh0 + oh1
    csum = jnp.sum(cmat, axis=0, keepdims=True)

    @pl.when(ph == 0)
    def _():
        cnt_ref[...] = cnt_ref[...] + csum

    @pl.when((ph == 1) & (c == 0))
    def _():
        cnt = jnp.broadcast_to(cnt_ref[...], (SUBLANES, LANES))
        padc = jnp.floor((cnt + (tm - 1)) * (1.0 / tm)) * tm
        pend = _lane_cumsum(padc)
        pstart_ref[...] = (pend - padc)[0:1, :]
        carry_ref[...] = jnp.zeros(carry_ref.shape, F32)
        starts = lax.broadcasted_iota(I32, (SUBLANES, nblk_lanes), 1).astype(F32) * tm
        blk = jnp.zeros((SUBLANES, nblk_lanes), F32)
        for e in range(N_EXPERTS):
            blk = blk + jnp.where(pend[:, e:e + 1] <= starts, 1.0, 0.0)
        blk = jnp.minimum(blk, N_EXPERTS - 1.0)
        nused = jnp.broadcast_to(pend[:, N_EXPERTS - 1:N_EXPERTS] * (1.0 / tm), (SUBLANES, nblk_lanes))
        sub = lax.broadcasted_iota(I32, (SUBLANES, nblk_lanes), 0)
        tab_ref[...] = jnp.where(sub == 0, blk, nused)
        last_ref[...] = jnp.where(cnt > 0.0, pend * (1.0 / tm) - 1.0, -1.0)

    @pl.when(ph == 1)
    def _():
        rt = lax.broadcasted_iota(I32, (chunk, chunk), 0)
        cs = lax.broadcasted_iota(I32, (chunk, chunk), 1)
        tri = jnp.where(rt > cs, 1.0, 0.0).astype(BF16)
        before = jnp.dot(tri, cmat.astype(BF16), preferred_element_type=F32) + carry_ref[...]
        base = before + pstart_ref[...]
        d0 = jnp.sum(base * oh0, axis=1, keepdims=True)
        d1 = jnp.sum(base * oh1, axis=1, keepdims=True)
        srow = (c - n_prompt_chunks) * chunk + row
        tshift = t_pad.bit_length() - 1
        padded_idx = (srow >> tshift) * (t_pad - t_valid) + (srow & (t_pad - 1)) - t_valid
        spare = (rows_cap + EXP_TOPK * padded_idx).astype(F32)
        out = jnp.where(lane == 0, d0, d1)
        out = jnp.where(valid, out, spare + lanef)
        dest_ref[...] = jnp.where(lane < EXP_TOPK, out, 0.0)
        carry_ref[...] = carry_ref[...] + csum


def _route_dest(route_p, route_s, t_pad, t_valid, tm, nblk, chunk=ROW_TILE):
    rp, rs = route_p.shape[0], route_s.shape[0]
    assert rp % chunk == 0 and rs % chunk == 0 and chunk % t_pad == 0
    npc, nsc = rp // chunk, rs // chunk
    rows_cap = nblk * tm
    nblk_lanes = -(-nblk // LANES) * LANES
    dest_f, tab, last = pl.pallas_call(
        functools.partial(_route_dest_kernel, n_prompt_chunks=npc, t_pad=t_pad, t_valid=t_valid, tm=tm,
                          rows_cap=rows_cap, nblk_lanes=nblk_lanes),
        grid=(2, npc + nsc),
        in_specs=[pl.BlockSpec((chunk, LANES), lambda ph, c: (jnp.minimum(c, npc - 1), 0)),
                  pl.BlockSpec((chunk, LANES), lambda ph, c: (jnp.maximum(c - npc, 0), 0))],
        out_specs=[pl.BlockSpec((chunk, LANES), lambda ph, c: (c * ph, 0)),
                   pl.BlockSpec((SUBLANES, nblk_lanes), lambda ph, c: (0, 0)),
                   pl.BlockSpec((SUBLANES, LANES), lambda ph, c: (0, 0))],
        out_shape=[jax.ShapeDtypeStruct((rp + rs, LANES), F32),
                   jax.ShapeDtypeStruct((SUBLANES, nblk_lanes), F32),
                   jax.ShapeDtypeStruct((SUBLANES, LANES), F32)],
        scratch_shapes=[pltpu.VMEM((chunk, LANES), F32), pltpu.VMEM((1, LANES), F32),
                        pltpu.VMEM((1, LANES), F32), pltpu.VMEM((1, LANES), F32)],
        compiler_params=_cparams(("arbitrary", "arbitrary"), 32),
        name="route_dest",
    )(route_p, route_s)
    dest = dest_f[:, :EXP_TOPK].astype(I32).reshape(-1)
    blk_e = tab[0, :nblk].astype(I32)
    nused = tab[1, :1].astype(I32)
    last_blk = last[0, :N_EXPERTS].astype(I32)
    return dest, blk_e, nused, last_blk


def _scatter_rows_kernel(dest_ref, last_ref, nused_ref, hp_ref, hs_ref, xs_ref, sbuf, sem, zsem, *,
                         tm, n_prompt_tiles, n_tiles, nblk):
    i = pl.program_id(0)
    slot = i % 2

    def wait_tile(s):
        for _ in range(EXP_TOPK):
            pltpu.make_async_copy(sbuf.at[s], xs_ref.at[pl.ds(0, tm)], sem.at[s]).wait()

    def clear_block(b):
        return pltpu.make_async_copy(sbuf.at[1], xs_ref.at[pl.ds(b * tm, tm)], zsem)

    @pl.when(i == 0)
    def _():
        sbuf[1] = jnp.zeros(sbuf.shape[1:], sbuf.dtype)
        nused = nused_ref[0]
        for e in range(N_EXPERTS):
            @pl.when(last_ref[e] >= 0)
            def _():
                clear_block(last_ref[e]).start()

        def start_unused(b, carry):
            clear_block(b).start()
            return carry

        def wait_one(b, carry):
            clear_block(0).wait()
            return carry

        lax.fori_loop(nused, nblk, start_unused, 0)
        for e in range(N_EXPERTS):
            @pl.when(last_ref[e] >= 0)
            def _():
                clear_block(0).wait()
        lax.fori_loop(nused, nblk, wait_one, 0)

    @pl.when(i >= 2)
    def _():
        wait_tile(slot)

    @pl.when(i < n_prompt_tiles)
    def _():
        sbuf[slot] = hp_ref[...]

    @pl.when(i >= n_prompt_tiles)
    def _():
        sbuf[slot] = hs_ref[...]

    base = EXP_TOPK * i * tm
    for r in range(tm):
        for k in range(EXP_TOPK):
            pltpu.make_async_copy(sbuf.at[slot, pl.ds(r, 1)],
                                  xs_ref.at[pl.ds(dest_ref[base + EXP_TOPK * r + k], 1)], sem.at[slot]).start()

    @pl.when(i == n_tiles - 1)
    def _():
        wait_tile(slot)
        wait_tile(1 - slot)


def _scatter_rows(dest, last_blk, nused, h_p, h_s, nblk, n_spare, tm=ROW_TILE):
    d = h_p.shape[1]
    npt, nst = h_p.shape[0] // tm, h_s.shape[0] // tm
    assert npt + nst >= 2 and MOE_TILE == tm
    grid_spec = pltpu.PrefetchScalarGridSpec(
        num_scalar_prefetch=3,
        grid=(npt + nst,),
        in_specs=[pl.BlockSpec((tm, d), lambda i, de, lb, nu: (jnp.minimum(i, npt - 1), 0)),
                  pl.BlockSpec((tm, d), lambda i, de, lb, nu: (jnp.maximum(i - npt, 0), 0))],
        out_specs=pl.BlockSpec(memory_space=pl.ANY),
        scratch_shapes=[pltpu.VMEM((2, tm, d), F32), pltpu.SemaphoreType.DMA((2,)), pltpu.SemaphoreType.DMA(())],
    )
    return pl.pallas_call(
        functools.partial(_scatter_rows_kernel, tm=tm, n_prompt_tiles=npt, n_tiles=npt + nst, nblk=nblk),
        grid_spec=grid_spec,
        out_shape=jax.ShapeDtypeStruct((nblk * tm + n_spare, d), F32),
        compiler_params=_cparams(("arbitrary",), 32),
        name="scatter_rows",
    )(dest, last_blk, nused, h_p, h_s)


def _moe_kernel(blk_e_ref, nused_ref, x_ref, w1_hbm, w3_hbm, w2_hbm, o_ref, wf_ref, wb_ref, sem):
    i = pl.program_id(0)
    nused = nused_ref[0]
    e = blk_e_ref[i]
    w_hbm = (w1_hbm, w3_hbm, w2_hbm)

    def start_fetch(expert):
        for k in range(3):
            pltpu.make_async_copy(w_hbm[k].at[expert], wf_ref.at[k], sem.at[k]).start()

    @pl.when(i == 0)
    def _():
        start_fetch(e)

    @pl.when((i < nused) & ((i == 0) | (e != blk_e_ref[jnp.maximum(i - 1, 0)])))
    def _():
        for k in range(3):
            pltpu.make_async_copy(w_hbm[k].at[e], wf_ref.at[k], sem.at[k]).wait()
            wb_ref[k] = wf_ref[k].astype(BF16)
        nxt = lax.while_loop(lambda j: (j < nused) & (blk_e_ref[jnp.minimum(j, nused - 1)] == e),
                             lambda j: j + 1, i + 1)

        @pl.when(nxt < nused)
        def _():
            start_fetch(blk_e_ref[nxt])

    @pl.when(i < nused)
    def _():
        x = x_ref[...].astype(BF16)
        a = jnp.dot(x, wb_ref[0], preferred_element_type=F32)
        b = jnp.dot(x, wb_ref[1], preferred_element_type=F32)
        hmid = (_silu(a) * b).astype(BF16)
        o_ref[...] = jnp.dot(hmid, wb_ref[2], preferred_element_type=F32)

    @pl.when(i >= nused)
    def _():
        o_ref[...] = jnp.zeros(o_ref.shape, o_ref.dtype)


def _moe_experts(xs, blk_e, nused, w1, w3, w2, tm=MOE_TILE):
    nblk = blk_e.shape[0]
    d = xs.shape[1]
    wspec = pl.BlockSpec(memory_space=pl.ANY)
    grid_spec = pltpu.PrefetchScalarGridSpec(
        num_scalar_prefetch=2,
        grid=(nblk,),
        in_specs=[pl.BlockSpec((tm, d), lambda i, be, nu: (jnp.minimum(i, nu[0] - 1), 0)), wspec, wspec, wspec],
        out_specs=pl.BlockSpec((tm, d), lambda i, be, nu: (i, 0)),
        scratch_shapes=[pltpu.VMEM((3, d, d), F32), pltpu.VMEM((3, d, d), BF16), pltpu.SemaphoreType.DMA((3,))],
    )
    return pl.pallas_call(
        _moe_kernel,
        grid_spec=grid_spec,
        out_shape=jax.ShapeDtypeStruct((nblk * tm, d), F32),
        compiler_params=_cparams(("arbitrary",), 40),
        name="moe_experts",
    )(blk_e, nused, xs, w1, w3, w2)


def _combine_kernel(dest_ref, x2_ref, r_ref, g_ref, outs_hbm, y_ref, buf, sem, *, tm):
    i = pl.program_id(0)
    n = pl.num_programs(0)
    n_rows = outs_hbm.shape[0]

    def start_tile(tile, slot):
        base = EXP_TOPK * tile * tm
        for r in range(tm):
            for k in range(EXP_TOPK):
                src = dest_ref[base + EXP_TOPK * r + k]
                src = jnp.where(src >= n_rows, src - n_rows, src)
                pltpu.make_async_copy(outs_hbm.at[pl.ds(src, 1)], buf.at[slot, k, pl.ds(r, 1)],
                                      sem.at[slot]).start()

    @pl.when(i == 0)
    def _():
        start_tile(0, 0)

    @pl.when(i + 1 < n)
    def _():
        start_tile(i + 1, (i + 1) % 2)

    slot = i % 2
    for k in range(EXP_TOPK):
        pltpu.make_async_copy(outs_hbm.at[pl.ds(0, tm)], buf.at[slot, k], sem.at[slot]).wait()
    route = r_ref[...]
    y = buf[slot, 0] * route[:, 2:3] + buf[slot, 1] * route[:, 3:4]
    y_ref[...] = _rms(x2_ref[...] + y, g_ref[...])


def _combine(x2, route, g_final, outs, dest, tm=ROW_TILE):
    rows, d = x2.shape
    grid_spec = pltpu.PrefetchScalarGridSpec(
        num_scalar_prefetch=1,
        grid=(rows // tm,),
        in_specs=[pl.BlockSpec((tm, d), lambda i, de: (i, 0)),
                  pl.BlockSpec((tm, LANES), lambda i, de: (i, 0)),
                  pl.BlockSpec((1, d), lambda i, de: (0, 0)),
                  pl.BlockSpec(memory_space=pl.ANY)],
        out_specs=pl.BlockSpec((tm, d), lambda i, de: (i, 0)),
        scratch_shapes=[pltpu.VMEM((2, EXP_TOPK, tm, d), F32), pltpu.SemaphoreType.DMA((2,))],
    )
    return pl.pallas_call(
        functools.partial(_combine_kernel, tm=tm),
        grid_spec=grid_spec,
        out_shape=jax.ShapeDtypeStruct((rows, d), F32),
        compiler_params=_cparams(("arbitrary",), 32),
        name="moe_combine",
    )(dest, x2, route, g_final.reshape(1, d), outs)


def _layer_tail(x, o, att, att_transposed, mem_k, mem_v, batch, t, xq_tile, wts):
    q_dtype = BF16 if xq_tile % (2 * SUBLANES) == 0 else F32
    x1, q = _outproj(x, o, att, wts["w_out"], wts["g_cross"], wts["w_xq"], att_transposed, q_dtype)
    xo = _xattn(q, mem_k, mem_v, batch, t, xq_tile)
    x2, h, route = _xo_router(x1, xo, wts["w_xo"], wts["g_ffn"], wts["w_router"], wts["b_router"])
    return x2, h, route


def _moe_and_final(x2_p, h_p, route_p, x2_s, h_s, route_s, n_sample_tokens, t_pad, t_valid, wts):
    rp, d = h_p.shape
    rs = h_s.shape[0]
    tm = MOE_TILE
    n_assign = (rp + n_sample_tokens) * EXP_TOPK
    nblk = -(-(n_assign + N_EXPERTS * (tm - 1)) // tm)
    dest, blk_e, nused, last_blk = _route_dest(route_p, route_s, t_pad, t_valid, tm, nblk)
    n_spare = EXP_TOPK * (rs // t_pad) * (t_pad - t_valid)
    xs = _scatter_rows(dest, last_blk, nused, h_p, h_s, nblk, n_spare)
    outs = _moe_experts(xs, blk_e, nused, wts["w1"], wts["w3"], wts["w2"])
    y_p = _combine(x2_p, route_p, wts["g_final"], outs, dest[:EXP_TOPK * rp])
    y_s = _combine(x2_s, route_s, wts["g_final"], outs, dest[EXP_TOPK * rp:])
    return y_p, y_s


def kernel(x_prompt, x_sample, state_hgrn, cache_moba_k, cache_moba_v, cache_mem_k, cache_mem_v, page_table,
           mem_prompt, g_mix, w_in, hg_lb, hg_norm, w_out, g_cross, g_mem, w_xq, w_xk, w_xv, w_xo, g_ffn,
           w_grp, b_grp, w_exp, b_exp, w1, w3, w2, g_final):
    depth = g_mix.shape[0]
    assert depth == 1
    bp, tp, d = x_prompt.shape
    bs, ts, _ = x_sample.shape
    mem_len = mem_prompt.shape[1]
    l = 0

    lb = jnp.cumsum(jax.nn.softmax(hg_lb.astype(F32), axis=0), axis=0)[l]
    slopes = jnp.asarray(np.power(2.0, -8.0 * np.arange(1, MB_HEADS + 1) / MB_HEADS).astype(np.float32))
    n_router = N_GROUPS + N_EXPERTS
    w_router = jnp.pad(jnp.concatenate([w_grp[l], w_exp[l]], axis=1), ((0, 0), (0, LANES - n_router)))
    b_router = jnp.pad(jnp.concatenate([b_grp[l], b_exp[l]]), (0, LANES - n_router)).reshape(1, LANES)
    wts = {
        "w_out": w_out[l].astype(BF16), "g_cross": g_cross[l], "w_xq": w_xq[l].astype(BF16),
        "w_xo": w_xo[l].astype(BF16), "g_ffn": g_ffn[l], "w_router": w_router.astype(BF16),
        "b_router": b_router.astype(F32), "w1": w1[l], "w3": w3[l], "w2": w2[l], "g_final": g_final,
    }
    w_in_bf = w_in[l].astype(BF16)
    o_hq, o_mq, o_mk, o_mv = 0, 4 * HG_WIDTH, 4 * HG_WIDTH + MB_WIDTH, 4 * HG_WIDTH + 2 * MB_WIDTH
    o_end = o_mv + MB_WIDTH

    xp = x_prompt.reshape(bp * tp, d)
    w_kv = jnp.concatenate([w_xk[l], w_xv[l]], axis=1).astype(BF16)
    memk_p, memv_p = _rms_proj(mem_prompt.reshape(bp * mem_len, d), g_mem[l], w_kv,
                               [(0, d, [(1.0, False)]), (d, 2 * d, [(1.0, False)])], [F32, F32])
    hp_p, qt_p, kt_p, vt_p = _rms_proj(
        xp, g_mix[l], w_in_bf,
        [(o_hq, o_mq, [(1.0, False)]), (o_mq, o_mk, [(MB_DIM ** -0.5, True)]), (o_mk, o_mv, [(1.0, "batched")]),
         (o_mv, o_end, [(1.0, "batched")])],
        [F32, BF16, F32, F32], seq_len=tp)
    o_p, s_p = _hgrn(hp_p, lb, hg_norm[l], None, bp, tp, tp, 128)
    att_p = _moba_prompt(qt_p, kt_p, vt_p, slopes, bp, tp)
    x2_p, h_p, route_p = _layer_tail(xp, o_p, att_p, True, memk_p, memv_p, bp, tp, 512, wts)

    tpad = SAMPLE_PAD_T
    xs = jnp.pad(x_sample, ((0, 0), (0, tpad - ts), (0, 0))).reshape(bs * tpad, d)
    hp_s, mq_s, mk_s, mv_s = _rms_proj(
        xs, g_mix[l], w_in_bf,
        [(o_hq, o_mq, [(1.0, False)]), (o_mq, o_mk, [(1.0, False)]), (o_mk, o_mv, [(1.0, False)]),
         (o_mv, o_end, [(1.0, False)])],
        [F32, F32, F32, F32])
    o_s, s_s = _hgrn(hp_s, lb, hg_norm[l], state_hgrn[l], bs, tpad, ts, tpad)
    mk4 = mk_s.reshape(bs, tpad, MB_HEADS, MB_DIM)
    mv4 = mv_s.reshape(bs, tpad, MB_HEADS, MB_DIM)
    n_phys, page = cache_moba_k.shape[1], cache_moba_k.shape[2]
    pool_kt = jnp.transpose(cache_moba_k[l], (0, 2, 3, 1)).reshape(n_phys, MB_WIDTH, page)
    pool_vt = jnp.transpose(cache_moba_v[l], (0, 2, 3, 1)).reshape(n_phys, MB_WIDTH, page)
    att_s = _moba_sample(mq_s, mk_s, mv_s, pool_kt, pool_vt, page_table, ts)

    def mem_rows(c):
        c = c[l].reshape(bs, mem_len, XA_HEADS, XA_DIM // LANES, LANES)
        return jnp.transpose(c, (0, 1, 3, 2, 4)).reshape(bs, mem_len * d // LANES, LANES)

    memk_s, memv_s = mem_rows(cache_mem_k), mem_rows(cache_mem_v)
    x2_s, h_s, route_s = _layer_tail(xs, o_s, att_s, False, memk_s, memv_s, bs, tpad, tpad, wts)
    y_p, y_s = _moe_and_final(x2_p, h_p, route_p, x2_s, h_s, route_s, bs * ts, tpad, ts, wts)

    return (y_p.reshape(bp, tp, d),
            y_s.reshape(bs, tpad, d)[:, :ts],
            s_p.reshape(1, bp, HG_HEADS, HG_DIM, HG_DIM),
            jnp.transpose(kt_p.reshape(1, bp, MB_HEADS, MB_DIM, tp), (0, 1, 4, 2, 3)),
            jnp.transpose(vt_p.reshape(1, bp, MB_HEADS, MB_DIM, tp), (0, 1, 4, 2, 3)),
            memk_p.reshape(1, bp, mem_len, XA_HEADS, XA_DIM),
            memv_p.reshape(1, bp, mem_len, XA_HEADS, XA_DIM),
            s_s.reshape(1, bs, HG_HEADS, HG_DIM, HG_DIM),
            mk4[:, :ts].reshape(1, bs, ts, MB_HEADS, MB_DIM),
            mv4[:, :ts].reshape(1, bs, ts, MB_HEADS, MB_DIM))
```

```python
import functools

import numpy as np
import jax
import jax.numpy as jnp
from jax import lax
from jax.experimental import pallas as pl
from jax.experimental.pallas import tpu as pltpu

F32 = jnp.float32
BF16 = jnp.bfloat16
I32 = jnp.int32

D_MODEL = 1024
HG_HEADS = 4
HG_DIM = 128
HG_WIDTH = HG_HEADS * HG_DIM
MB_HEADS = 8
MB_DIM = 64
MB_WIDTH = MB_HEADS * MB_DIM
MB_BLOCK = 256
MB_TOPK = 3
XA_HEADS = 4
XA_DIM = 256
N_GROUPS = 4
EXP_PER_GROUP = 8
N_EXPERTS = N_GROUPS * EXP_PER_GROUP
EXP_TOPK = 2
RMS_EPS = 1e-6
NEG_INF = float("-inf")

LANES = 128
SUBLANES = 8
MIB = 1024 * 1024

ROW_TILE = 256
MOE_TILE = 256
SAMPLE_PAD_T = 8
MOBA_PAIRS = 4
XATTN_SEQS_PER_STEP = 4

_NT = (((1,), (1,)), ((), ()))


def _cparams(semantics, vmem_mib):
    return pltpu.CompilerParams(dimension_semantics=semantics, vmem_limit_bytes=vmem_mib * MIB)


def _rms(x, g):
    return x * lax.rsqrt(jnp.mean(x * x, axis=-1, keepdims=True) + RMS_EPS) * g


def _silu(x):
    return x * jax.nn.sigmoid(x)


def _rms_proj_kernel(x_ref, g_ref, w_ref, *o_refs, segs):
    h = _rms(x_ref[...], g_ref[...]).astype(BF16)
    k = 0
    for lo, hi, outs in segs:
        r = jnp.dot(h, w_ref[:, lo:hi], preferred_element_type=F32)
        for scale, transposed in outs:
            y = r if scale == 1.0 else r * scale
            if transposed:
                y = y.T
            o_refs[k][...] = y.astype(o_refs[k].dtype)
            k += 1


def _rms_proj(x, g, w_bf, segs, dtypes, tm=ROW_TILE, vmem_mib=48, seq_len=None):
    rows, d = x.shape
    assert rows % tm == 0
    out_shape, out_specs = [], []
    k = 0
    for lo, hi, outs in segs:
        n = hi - lo
        for _, transposed in outs:
            if transposed == "batched":
                per_seq = seq_len // tm
                out_shape.append(jax.ShapeDtypeStruct((rows // seq_len, n, seq_len), dtypes[k]))
                out_specs.append(pl.BlockSpec((None, n, tm), lambda i: (i // per_seq, 0, i % per_seq)))
            elif transposed:
                out_shape.append(jax.ShapeDtypeStruct((n, rows), dtypes[k]))
                out_specs.append(pl.BlockSpec((n, tm), lambda i: (0, i)))
            else:
                out_shape.append(jax.ShapeDtypeStruct((rows, n), dtypes[k]))
                out_specs.append(pl.BlockSpec((tm, n), lambda i: (i, 0)))
            k += 1
    return pl.pallas_call(
        functools.partial(_rms_proj_kernel, segs=segs),
        grid=(rows // tm,),
        in_specs=[pl.BlockSpec((tm, d), lambda i: (i, 0)),
                  pl.BlockSpec((1, d), lambda i: (0, 0)),
                  pl.BlockSpec(w_bf.shape, lambda i: (0, 0))],
        out_specs=out_specs,
        out_shape=out_shape,
        compiler_params=_cparams(("parallel",), vmem_mib),
        name="rms_proj",
    )(x, g.reshape(1, d), w_bf)


def _cumsum_rows(x):
    n = x.shape[0]
    row = lax.broadcasted_iota(I32, x.shape, 0)
    s = 1
    while s < n:
        x = x + jnp.where(row >= s, pltpu.roll(x, s, 0), 0.0)
        s *= 2
    return x


def _hgrn_intra(q, kk, b, c, w):
    row = lax.broadcasted_iota(I32, (c, HG_DIM), 0)
    acc = None
    half = c // 2
    while half >= SUBLANES:
        two = 2 * half
        nblk = c // two
        pieces = [jnp.broadcast_to(b[i * two + half - 1:i * two + half, :], (two, HG_DIM)) for i in range(nblk)]
        bm = pieces[0] if nblk == 1 else jnp.concatenate(pieces, axis=0)
        second = (row & (two - 1)) >= half
        qt = jnp.where(second, q * jnp.exp(jnp.where(second, b - bm, 0.0)), 0.0)
        kt = jnp.where(second, 0.0, kk * jnp.exp(jnp.where(second, 0.0, bm - b)))
        al = lax.dot_general(qt.astype(BF16), kt.astype(BF16), _NT, preferred_element_type=F32)
        if nblk > 1:
            shift = two.bit_length() - 1
            rt = lax.broadcasted_iota(I32, (c, c), 0) >> shift
            cs = lax.broadcasted_iota(I32, (c, c), 1) >> shift
            al = jnp.where(rt == cs, al, 0.0)
        acc = al if acc is None else acc + al
        half //= 2
    r8 = lax.broadcasted_iota(I32, (SUBLANES, HG_DIM), 0)
    lane = lax.broadcasted_iota(I32, (SUBLANES, w), 1)
    blocks = []
    for g in range(c // SUBLANES):
        lo = g * SUBLANES
        qg, kg, bg = q[lo:lo + SUBLANES], kk[lo:lo + SUBLANES], b[lo:lo + SUBLANES]
        ag = jnp.zeros((SUBLANES, w), F32)
        for s in range(SUBLANES):
            e = jnp.exp(jnp.where(r8 >= s, bg - bg[s:s + 1, :], NEG_INF))
            p = jnp.sum(qg * kg[s:s + 1, :] * e, axis=1, keepdims=True)
            ag = jnp.where(lane == lo + s, p, ag)
        blocks.append(ag)
    diag = blocks[0] if len(blocks) == 1 else jnp.concatenate(blocks, axis=0)
    return diag if acc is None else acc + diag


def _hgrn_kernel(*refs, c, t_valid, has_s0):
    if has_s0:
        hp_ref, lb_ref, gn_ref, s0_ref, o_ref, sout_ref, st_ref = refs
    else:
        hp_ref, lb_ref, gn_ref, o_ref, sout_ref, st_ref = refs
    ci = pl.program_id(1)
    last = pl.num_programs(1) - 1
    w = max(c, LANES)

    @pl.when(ci == 0)
    def _():
        for h in range(HG_HEADS):
            st_ref[h] = s0_ref[0, h].T if has_s0 else jnp.zeros((HG_DIM, HG_DIM), F32)

    row = lax.broadcasted_iota(I32, (c, HG_DIM), 0)
    for h in range(HG_HEADS):
        lo = h * HG_DIM
        hq = hp_ref[:, lo:lo + HG_DIM]
        hf = hp_ref[:, HG_WIDTH + lo:HG_WIDTH + lo + HG_DIM]
        v = hp_ref[:, 2 * HG_WIDTH + lo:2 * HG_WIDTH + lo + HG_DIM]
        hg = hp_ref[:, 3 * HG_WIDTH + lo:3 * HG_WIDTH + lo + HG_DIM]
        lb = lb_ref[:, lo:lo + HG_DIM]
        q = _silu(hq)
        f = lb + (1.0 - lb) * jax.nn.sigmoid(hf)
        logf = jnp.log(f)
        kk = 1.0 - f
        if t_valid < c:
            valid = row < t_valid
            logf = jnp.where(valid, logf, 0.0)
            kk = jnp.where(valid, kk, 0.0)
            v = jnp.where(valid, v, 0.0)
        b = _cumsum_rows(logf)
        a = _hgrn_intra(q, kk, b, c, w)
        st = st_ref[h]
        bl = b[c - 1:c, :]
        k2 = kk * jnp.exp(bl - b)
        if c < w:
            zpad = jnp.zeros((w - c, HG_DIM), F32)
            vp = jnp.concatenate([v, zpad], axis=0)
            k2 = jnp.concatenate([k2, zpad], axis=0)
        else:
            vp = v
        vb = vp.astype(BF16)
        o = lax.dot_general((q * jnp.exp(b)).astype(BF16), st.astype(BF16), _NT, preferred_element_type=F32)
        o = o + jnp.dot(a.astype(BF16), vb, preferred_element_type=F32)
        o_ref[:, lo:lo + HG_DIM] = (_rms(o, gn_ref[...]) * _silu(hg)).astype(o_ref.dtype)
        st_new = st * jnp.exp(bl) + jnp.dot(vp.T.astype(BF16), k2.astype(BF16), preferred_element_type=F32)
        st_ref[h] = st_new

        @pl.when(ci == last)
        def _():
            sout_ref[0, h] = st_new.T


def _hgrn(hp, lb, gn, s0, batch, t_pad, t_valid, c):
    rows = hp.shape[0]
    nc = t_pad // c
    has_s0 = s0 is not None
    in_specs = [pl.BlockSpec((c, 4 * HG_WIDTH), lambda b, i: (b * nc + i, 0)),
                pl.BlockSpec((1, HG_WIDTH), lambda b, i: (0, 0)),
                pl.BlockSpec((1, HG_DIM), lambda b, i: (0, 0))]
    args = [hp, lb.reshape(1, HG_WIDTH), gn.reshape(1, HG_DIM)]
    if has_s0:
        in_specs.append(pl.BlockSpec((1, HG_HEADS, HG_DIM, HG_DIM), lambda b, i: (b, 0, 0, 0)))
        args.append(s0)
    return pl.pallas_call(
        functools.partial(_hgrn_kernel, c=c, t_valid=min(t_valid, c), has_s0=has_s0),
        grid=(batch, nc),
        in_specs=in_specs,
        out_specs=[pl.BlockSpec((c, HG_WIDTH), lambda b, i: (b * nc + i, 0)),
                   pl.BlockSpec((1, HG_HEADS, HG_DIM, HG_DIM), lambda b, i: (b, 0, 0, 0))],
        out_shape=[jax.ShapeDtypeStruct((rows, HG_WIDTH), BF16),
                   jax.ShapeDtypeStruct((batch, HG_HEADS, HG_DIM, HG_DIM), F32)],
        scratch_shapes=[pltpu.VMEM((HG_HEADS, HG_DIM, HG_DIM), F32)],
        compiler_params=_cparams(("parallel", "arbitrary"), 32),
        name="hgrn2",
    )(*args)


def _topk_block_mask(gate, n_valid, nrow):
    nb = gate.shape[0]
    cnt = jnp.zeros(gate.shape, I32)
    for n2 in range(nb):
        g2 = gate[n2:n2 + 1, :]
        beats = (g2 > gate) | ((g2 == gate) & (n2 < nrow))
        cnt = cnt + jnp.where(beats, (n_valid > n2).astype(I32), 0)
    return (cnt < MB_TOPK) & (nrow < n_valid)


def _moba_prompt_kernel(slope_ref, qt_ref, kt_ref, vt_ref, o_ref, kb_ref, vb_ref, km_ref, sel_ref, *, nb, npairs):
    hg = pl.program_id(1)
    qi = pl.program_id(2)
    tq = MB_BLOCK
    pw = 2 * tq
    width = npairs * pw
    pair_rows = 2 * MB_DIM

    @pl.when(qi == 0)
    def _():
        pos = jnp.where(lax.broadcasted_iota(I32, (MB_BLOCK, pair_rows), 1) == 0,
                        lax.broadcasted_iota(I32, (MB_BLOCK, pair_rows), 0), 0).astype(BF16)
        for pp in range(npairs):
            means = []
            for j in range(nb):
                kj = kt_ref[pp * pair_rows:(pp + 1) * pair_rows, j * MB_BLOCK:(j + 1) * MB_BLOCK].T
                kb_ref[pp, j * MB_BLOCK:(j + 1) * MB_BLOCK, :pair_rows] = kj.astype(BF16)
                kb_ref[pp, j * MB_BLOCK:(j + 1) * MB_BLOCK, pair_rows:] = pos
                means.append(jnp.sum(kj, axis=0, keepdims=True) * (1.0 / MB_BLOCK))
                vb_ref[pp, j] = vt_ref[pp * pair_rows:(pp + 1) * pair_rows,
                                       j * MB_BLOCK:(j + 1) * MB_BLOCK].astype(BF16)
            km_ref[pp] = jnp.concatenate(means, axis=0)

    second_head = lax.broadcasted_iota(I32, (pair_rows, tq), 0) >= MB_DIM
    first_row = lax.broadcasted_iota(I32, (pair_rows, pw), 0) == 0
    qaugs, gates, slope_parts = [], [], []
    for pp in range(npairs):
        qt2 = qt_ref[pp * pair_rows:(pp + 1) * pair_rows, :]
        zero = jnp.zeros_like(qt2)
        qcat = jnp.concatenate([jnp.where(second_head, zero, qt2), jnp.where(second_head, qt2, zero)], axis=1)
        gates.append(jnp.dot(km_ref[pp].astype(BF16), qcat, preferred_element_type=F32))
        pair_slopes = jnp.concatenate(
            [jnp.full((1, tq), slope_ref[(hg * npairs + pp) * 2 + i], F32) for i in range(2)], axis=1)
        slope_parts.append(pair_slopes)
        srows = jnp.where(first_row, pair_slopes, 0.0).astype(BF16)
        qaugs.append(jnp.concatenate([qcat, srows], axis=0))
    slope_row = slope_parts[0] if npairs == 1 else jnp.concatenate(slope_parts, axis=1)
    gate = gates[0] if npairs == 1 else jnp.concatenate(gates, axis=1)
    nrow = lax.broadcasted_iota(I32, (nb, width), 0)
    sel = _topk_block_mask(gate, qi, nrow).astype(F32)
    for n in range(nb):
        sel_ref[n] = jnp.broadcast_to(sel[n:n + 1, :], (SUBLANES, width))

    def scores(j):
        start = pl.multiple_of(j * MB_BLOCK, MB_BLOCK)
        parts = [jnp.dot(kb_ref[pp, pl.ds(start, MB_BLOCK), :], qaugs[pp], preferred_element_type=F32)
                 for pp in range(npairs)]
        return parts[0] if npairs == 1 else jnp.concatenate(parts, axis=1)

    def values(j, p):
        pb = p.astype(BF16)
        return [jnp.dot(vb_ref[pp, j], pb[:, pp * pw:(pp + 1) * pw], preferred_element_type=F32)
                for pp in range(npairs)]

    lane = lax.broadcasted_iota(I32, (MB_BLOCK, width), 1) & (tq - 1)
    rowk = lax.broadcasted_iota(I32, (MB_BLOCK, width), 0)
    s = jnp.where(rowk <= lane, scores(qi), NEG_INF)
    m = jnp.max(s, axis=0, keepdims=True)
    p = jnp.exp(s - m)
    l = jnp.sum(p, axis=0, keepdims=True)
    accs = values(qi, p)

    def body(j, carry):
        m, l, accs = carry
        off = jnp.full((1, width), (qi - j) * MB_BLOCK, I32).astype(F32) * slope_row
        picked = sel_ref[j][0:1, :] > 0.5
        s = scores(j)
        m_new = jnp.maximum(m, jnp.where(picked, jnp.max(s, axis=0, keepdims=True) - off, NEG_INF))
        alpha = jnp.exp(m - m_new)
        p = jnp.exp(s - jnp.where(picked, m_new + off, float("inf")))
        l = alpha * l + jnp.sum(p, axis=0, keepdims=True)
        pv = values(j, p)
        accs = [alpha[:, pp * pw:(pp + 1) * pw] * accs[pp] + pv[pp] for pp in range(npairs)]
        return m_new, l, accs

    m, l, accs = lax.fori_loop(0, qi, body, (m, l, accs))
    for pp in range(npairs):
        a = accs[pp] / l[:, pp * pw:(pp + 1) * pw]
        o_ref[pp * pair_rows:(pp + 1) * pair_rows, :] = jnp.where(second_head, a[:, tq:], a[:, :tq])


def _moba_prompt(qt, kt, vt, slopes, batch, t, npairs=MOBA_PAIRS):
    nb = t // MB_BLOCK
    rows = batch * t
    gr = 2 * MB_DIM * npairs
    width = npairs * 2 * MB_BLOCK
    return pl.pallas_call(
        functools.partial(_moba_prompt_kernel, nb=nb, npairs=npairs),
        grid=(batch, MB_WIDTH // gr, nb),
        in_specs=[pl.BlockSpec(memory_space=pltpu.SMEM),
                  pl.BlockSpec((gr, MB_BLOCK), lambda b, h, i: (h, b * nb + i)),
                  pl.BlockSpec((None, gr, t), lambda b, h, i: (b, h, 0)),
                  pl.BlockSpec((None, gr, t), lambda b, h, i: (b, h, 0))],
        out_specs=pl.BlockSpec((gr, MB_BLOCK), lambda b, h, i: (h, b * nb + i)),
        out_shape=jax.ShapeDtypeStruct((MB_WIDTH, rows), F32),
        scratch_shapes=[pltpu.VMEM((npairs, t, 4 * MB_DIM), BF16),
                        pltpu.VMEM((npairs, nb, 2 * MB_DIM, MB_BLOCK), BF16),
                        pltpu.VMEM((npairs, nb, 2 * MB_DIM), F32),
                        pltpu.VMEM((nb, SUBLANES, width), F32)],
        compiler_params=_cparams(("parallel", "parallel", "arbitrary"), 48),
        name="moba_prompt",
    )(slopes, qt, kt, vt)


def _moba_sample_kernel(pt_ref, slope_ref, hm_ref, q_ref, kn_ref, vn_ref, *rest, npages, page, t_new, past_len):
    del pt_ref
    k_pages = rest[:npages]
    v_pages = rest[npages:2 * npages]
    o_ref = rest[2 * npages]
    nrows = t_new * MB_HEADS
    pages_per_block = MB_BLOCK // page
    nb = npages // pages_per_block
    hm = hm_ref[...]
    slope = slope_ref[...][:, 0:1]
    row_q = lax.broadcasted_iota(I32, (nrows, 1), 0) >> 3
    pos_q = past_len + row_q

    q = q_ref[...] * (MB_DIM ** -0.5)
    qbd = jnp.concatenate([jnp.broadcast_to(q[t:t + 1, :], (MB_HEADS, MB_WIDTH)) * hm for t in range(t_new)],
                          axis=0).astype(BF16)

    lane = lax.broadcasted_iota(I32, (nrows, page), 1)
    scores = []
    gates = [None] * nb
    for p in range(npages):
        s = jnp.dot(qbd, k_pages[p][0].astype(BF16), preferred_element_type=F32)
        rs = jnp.sum(s, axis=1, keepdims=True)
        n = p // pages_per_block
        gates[n] = rs if gates[n] is None else gates[n] + rs
        scores.append(s - slope * (pos_q - (p * page + lane)).astype(F32))
    sels = []
    for n in range(nb):
        cnt = jnp.zeros((nrows, 1), I32)
        for n2 in range(nb):
            beats = (gates[n2] > gates[n]) | ((gates[n2] == gates[n]) & (n2 < n))
            cnt = cnt + jnp.where(beats, 1, 0)
        sels.append(jnp.where(cnt < MB_TOPK, 1.0, 0.0))

    zrows = jnp.zeros((LANES - SAMPLE_PAD_T, MB_WIDTH), F32)
    knp = jnp.concatenate([kn_ref[...], zrows], axis=0).astype(BF16)
    vnp = jnp.concatenate([vn_ref[...], zrows], axis=0).astype(BF16)
    lane_n = lax.broadcasted_iota(I32, (nrows, LANES), 1)
    s_new = lax.dot_general(qbd, knp, _NT, preferred_element_type=F32)
    s_new = jnp.where(lane_n <= row_q, s_new - slope * (row_q - lane_n).astype(F32), NEG_INF)
    m = jnp.max(s_new, axis=1, keepdims=True)
    for p in range(npages):
        sel_p = jnp.broadcast_to(sels[p // pages_per_block], (nrows, page)) > 0.5
        scores[p] = jnp.where(sel_p, scores[p], NEG_INF)
        m = jnp.maximum(m, jnp.max(scores[p], axis=1, keepdims=True))

    p_new = jnp.exp(s_new - m)
    l = jnp.sum(p_new, axis=1, keepdims=True)
    acc = jnp.dot(p_new.astype(BF16), vnp, preferred_element_type=F32)
    for p in range(npages):
        pr = jnp.exp(scores[p] - m)
        l = l + jnp.sum(pr, axis=1, keepdims=True)
        acc = acc + lax.dot_general(pr.astype(BF16), v_pages[p][0].astype(BF16), _NT,
                                    preferred_element_type=F32)
    acc = acc / l
    rows = [jnp.sum(acc[t * MB_HEADS:(t + 1) * MB_HEADS, :] * hm, axis=0, keepdims=True) for t in range(t_new)]
    rows.append(jnp.zeros((SAMPLE_PAD_T - t_new, MB_WIDTH), F32))
    o_ref[...] = jnp.concatenate(rows, axis=0)


def _moba_sample(mq, mk, mv, pool_kt, pool_vt, page_table, t_new):
    batch, npages = page_table.shape
    page = pool_kt.shape[2]
    past_len = npages * page
    assert past_len % MB_BLOCK == 0 and MB_BLOCK % page == 0 and t_new <= SAMPLE_PAD_T and page == LANES
    nrows = t_new * MB_HEADS
    slopes = np.power(2.0, -8.0 * np.arange(1, MB_HEADS + 1) / MB_HEADS).astype(np.float32)
    slope_rows = jnp.asarray(np.tile(np.tile(slopes, t_new)[:, None], (1, LANES)))
    head_mask = jnp.asarray((np.arange(MB_WIDTH)[None, :] // MB_DIM == np.arange(MB_HEADS)[:, None])
                            .astype(np.float32))
    new_spec = pl.BlockSpec((SAMPLE_PAD_T, MB_WIDTH), lambda b, pt: (b, 0))
    page_specs = [pl.BlockSpec((1, MB_WIDTH, page),
                               functools.partial(lambda b, pt, i: (pt[b * npages + i], 0, 0), i=i))
                  for i in range(npages)]
    grid_spec = pltpu.PrefetchScalarGridSpec(
        num_scalar_prefetch=1,
        grid=(batch,),
        in_specs=[pl.BlockSpec((nrows, LANES), lambda b, pt: (0, 0)),
                  pl.BlockSpec((MB_HEADS, MB_WIDTH), lambda b, pt: (0, 0)),
                  new_spec, new_spec, new_spec] + page_specs + page_specs,
        out_specs=new_spec,
    )
    return pl.pallas_call(
        functools.partial(_moba_sample_kernel, npages=npages, page=page, t_new=t_new, past_len=past_len),
        grid_spec=grid_spec,
        out_shape=jax.ShapeDtypeStruct((batch * SAMPLE_PAD_T, MB_WIDTH), F32),
        compiler_params=_cparams(("parallel",), 40),
        name="moba_sample",
    )(page_table.reshape(-1), slope_rows, head_mask, mq, mk, mv, *([pool_kt] * npages), *([pool_vt] * npages))


def _outproj_math(x, o, att, wo_ref, g, wq_ref):
    mix = jnp.dot(o, wo_ref[:HG_WIDTH, :], preferred_element_type=F32)
    mix = mix + jnp.dot(att.astype(BF16), wo_ref[HG_WIDTH:, :], preferred_element_type=F32)
    x1 = x + mix
    h = _rms(x1, g).astype(BF16)
    return x1, jnp.dot(h, wq_ref[...], preferred_element_type=F32) * (XA_DIM ** -0.5)


def _outproj_kernel(x_ref, o_ref, att_ref, wo_ref, g_ref, wq_ref, x1_ref, q_ref, *, att_transposed):
    att = att_ref[...]
    if att_transposed:
        att = att.T
    x1, q = _outproj_math(x_ref[...], o_ref[...], att, wo_ref, g_ref[...], wq_ref)
    x1_ref[...] = x1
    q_ref[...] = q.astype(q_ref.dtype)


def _outproj(x, o, att, wo_bf, g, wq_bf, att_transposed, q_dtype, tm=ROW_TILE):
    rows, d = x.shape
    att_spec = (pl.BlockSpec((MB_WIDTH, tm), lambda i: (0, i)) if att_transposed
                else pl.BlockSpec((tm, MB_WIDTH), lambda i: (i, 0)))
    return pl.pallas_call(
        functools.partial(_outproj_kernel, att_transposed=att_transposed),
        grid=(rows // tm,),
        in_specs=[pl.BlockSpec((tm, d), lambda i: (i, 0)),
                  pl.BlockSpec((tm, HG_WIDTH), lambda i: (i, 0)),
                  att_spec,
                  pl.BlockSpec((d, d), lambda i: (0, 0)),
                  pl.BlockSpec((1, d), lambda i: (0, 0)),
                  pl.BlockSpec((d, d), lambda i: (0, 0))],
        out_specs=[pl.BlockSpec((tm, d), lambda i: (i, 0)), pl.BlockSpec((tm, d), lambda i: (i, 0))],
        out_shape=[jax.ShapeDtypeStruct((rows, d), F32), jax.ShapeDtypeStruct((rows, d), q_dtype)],
        compiler_params=_cparams(("parallel",), 40),
        name="outproj_q",
    )(x, o, att, wo_bf, g.reshape(1, d), wq_bf)


def _xattn_tiled_kernel(q_ref, k_ref, v_ref, o_ref, *, nt):
    for g in range(k_ref.shape[0]):
        o_ref[g * nt:(g + 1) * nt, :] = _xattn_tiled_one(
            q_ref[g * nt:(g + 1) * nt, :], k_ref[g], v_ref[g], nt).astype(o_ref.dtype)


def _xattn_tiled_one(q, k, v, nt):
    ndt = XA_DIM // LANES
    nc = XA_HEADS * ndt
    half = XA_HEADS * nt
    q = q.astype(F32)
    a = jnp.concatenate([q[:, (h * ndt + dt) * LANES:(h * ndt + dt + 1) * LANES]
                         for dt in range(ndt) for h in range(XA_HEADS)], axis=0).astype(BF16)
    s = lax.dot_general(a, k.astype(BF16), _NT, preferred_element_type=F32)
    width = s.shape[1]
    lane_c = lax.broadcasted_iota(I32, s.shape, 1) & (nc - 1)
    assert nt & (nt - 1) == 0 and nc & (nc - 1) == 0
    row_c = lax.broadcasted_iota(I32, s.shape, 0) >> (nt.bit_length() - 1)
    s = jnp.where(lane_c == row_c, s, 0.0)
    tot = s[:half]
    for dt in range(1, ndt):
        tot = tot + pltpu.roll(s[dt * half:(dt + 1) * half], width - dt * XA_HEADS, 1)
    ok = ((lax.broadcasted_iota(I32, tot.shape, 1) & (nc - 1))
          == (lax.broadcasted_iota(I32, tot.shape, 0) >> (nt.bit_length() - 1)))
    tot = jnp.where(ok, tot, NEG_INF)
    p = jnp.exp(tot - jnp.max(tot, axis=-1, keepdims=True))
    p = p / jnp.sum(p, axis=-1, keepdims=True)
    pe = jnp.concatenate([p] + [pltpu.roll(p, dt * XA_HEADS, 1) for dt in range(1, ndt)], axis=0)
    o = jnp.dot(pe.astype(BF16), v.astype(BF16), preferred_element_type=F32)
    return jnp.concatenate([o[(dt * XA_HEADS + h) * nt:(dt * XA_HEADS + h + 1) * nt, :]
                            for h in range(XA_HEADS) for dt in range(ndt)], axis=1)


def _xattn_math(q, k_ref, v_ref):
    outs = []
    for h in range(XA_HEADS):
        lo = h * XA_DIM
        kh = k_ref[:, lo:lo + XA_DIM].astype(BF16)
        vh = v_ref[:, lo:lo + XA_DIM].astype(BF16)
        s = lax.dot_general(q[:, lo:lo + XA_DIM], kh, _NT, preferred_element_type=F32)
        p = jnp.exp(s - jnp.max(s, axis=-1, keepdims=True))
        p = p / jnp.sum(p, axis=-1, keepdims=True)
        outs.append(jnp.dot(p.astype(BF16), vh, preferred_element_type=F32))
    return outs


def _xattn_kernel(q_ref, k_ref, v_ref, o_ref):
    outs = _xattn_math(q_ref[...].astype(BF16), k_ref, v_ref)
    for h in range(XA_HEADS):
        o_ref[:, h * XA_DIM:(h + 1) * XA_DIM] = outs[h].astype(o_ref.dtype)


def _xattn(q, mem_k, mem_v, batch, t, tq):
    rows, d = q.shape
    nq = t // tq
    if mem_k.ndim == 3:
        assert tq == t and batch % XATTN_SEQS_PER_STEP == 0
        g = XATTN_SEQS_PER_STEP
        kv_spec = pl.BlockSpec((g,) + mem_k.shape[1:], lambda b: (b, 0, 0))
        return pl.pallas_call(
            functools.partial(_xattn_tiled_kernel, nt=t),
            grid=(batch // g,),
            in_specs=[pl.BlockSpec((g * t, d), lambda b: (b, 0)), kv_spec, kv_spec],
            out_specs=pl.BlockSpec((g * t, d), lambda b: (b, 0)),
            out_shape=jax.ShapeDtypeStruct((rows, d), q.dtype),
            compiler_params=_cparams(("parallel",), 40),
            name="xattn_tiled",
        )(q, mem_k, mem_v)
    m = mem_k.shape[0] // batch
    kv_spec = pl.BlockSpec((m, d), lambda b, i: (b, 0))
    return pl.pallas_call(
        _xattn_kernel,
        grid=(batch, nq),
        in_specs=[pl.BlockSpec((tq, d), lambda b, i: (b * nq + i, 0)), kv_spec, kv_spec],
        out_specs=pl.BlockSpec((tq, d), lambda b, i: (b * nq + i, 0)),
        out_shape=jax.ShapeDtypeStruct((rows, d), q.dtype),
        compiler_params=_cparams(("parallel", "arbitrary"), 32),
        name="xattn",
    )(q, mem_k, mem_v)


def _xo_router_kernel(x1_ref, o_ref, wo_ref, g_ref, wr_ref, br_ref, x2_ref, h_ref, r_ref):
    x2 = x1_ref[...] + jnp.dot(o_ref[...].astype(BF16), wo_ref[...], preferred_element_type=F32)
    x2_ref[...] = x2
    h = _rms(x2, g_ref[...])
    h_ref[...] = h
    r_ref[...] = _router_math(h, wr_ref, br_ref)


def _router_math(h, wr_ref, br_ref):
    logits = jnp.dot(h.astype(BF16), wr_ref[...], preferred_element_type=F32) + br_ref[...]
    lane = lax.broadcasted_iota(I32, logits.shape, 1)
    big = jnp.int32(LANES)

    def top1(mask):
        mx = jnp.max(jnp.where(mask, logits, NEG_INF), axis=-1, keepdims=True)
        idx = jnp.min(jnp.where(mask & (logits == mx), lane, big), axis=-1, keepdims=True)
        return mx, idx

    gmask = lane < N_GROUPS
    gmx, gsel = top1(gmask)
    gw = 1.0 / jnp.sum(jnp.where(gmask, jnp.exp(logits - gmx), 0.0), axis=-1, keepdims=True)
    elo = N_GROUPS + gsel * EXP_PER_GROUP
    emask = (lane >= elo) & (lane < elo + EXP_PER_GROUP)
    m1, i1 = top1(emask)
    m2, i2 = top1(emask & (lane != i1))
    e2 = jnp.exp(m2 - m1)
    g1 = gw / (1.0 + e2)
    g2 = gw * e2 / (1.0 + e2)
    out = jnp.where(lane == 0, (i1 - N_GROUPS).astype(F32), 0.0)
    out = jnp.where(lane == 1, (i2 - N_GROUPS).astype(F32), out)
    out = jnp.where(lane == 2, g1, out)
    return jnp.where(lane == 3, g2, out)


def _tail_fused_kernel(x_ref, o_ref, att_ref, wo_ref, gc_ref, wq_ref, k_ref, v_ref, wxo_ref, gf_ref, wr_ref,
                       br_ref, x2_ref, h_ref, r_ref):
    x1, q = _outproj_math(x_ref[...], o_ref[...], att_ref[...].T, wo_ref, gc_ref[...], wq_ref)
    outs = _xattn_math(q.astype(BF16), k_ref, v_ref)
    x2 = x1
    for h in range(XA_HEADS):
        x2 = x2 + jnp.dot(outs[h].astype(BF16), wxo_ref[h * XA_DIM:(h + 1) * XA_DIM, :],
                          preferred_element_type=F32)
    x2_ref[...] = x2
    hn = _rms(x2, gf_ref[...])
    h_ref[...] = hn
    r_ref[...] = _router_math(hn, wr_ref, br_ref)


def _tail_fused(x, o, att_t, mem_k, mem_v, batch, t, wts, tm=512):
    rows, d = x.shape
    nq = t // tm
    m = mem_k.shape[0] // batch
    row_spec = pl.BlockSpec((tm, d), lambda b, i: (b * nq + i, 0))
    w_spec = pl.BlockSpec((d, d), lambda b, i: (0, 0))
    g_spec = pl.BlockSpec((1, d), lambda b, i: (0, 0))
    kv_spec = pl.BlockSpec((m, d), lambda b, i: (b, 0))
    return pl.pallas_call(
        _tail_fused_kernel,
        grid=(batch, nq),
        in_specs=[row_spec,
                  pl.BlockSpec((tm, HG_WIDTH), lambda b, i: (b * nq + i, 0)),
                  pl.BlockSpec((MB_WIDTH, tm), lambda b, i: (0, b * nq + i)),
                  w_spec, g_spec, w_spec, kv_spec, kv_spec, w_spec, g_spec,
                  pl.BlockSpec((d, LANES), lambda b, i: (0, 0)),
                  pl.BlockSpec((1, LANES), lambda b, i: (0, 0))],
        out_specs=[row_spec, row_spec, pl.BlockSpec((tm, LANES), lambda b, i: (b * nq + i, 0))],
        out_shape=[jax.ShapeDtypeStruct((rows, d), F32),
                   jax.ShapeDtypeStruct((rows, d), F32),
                   jax.ShapeDtypeStruct((rows, LANES), F32)],
        compiler_params=_cparams(("parallel", "arbitrary"), 56),
        name="tail_fused",
    )(x, o, att_t, wts["w_out"], wts["g_cross"].reshape(1, d), wts["w_xq"], mem_k, mem_v, wts["w_xo"],
      wts["g_ffn"].reshape(1, d), wts["w_router"], wts["b_router"])


def _xo_router(x1, o, wxo_bf, g, wr_bf, br, tm=ROW_TILE):
    rows, d = x1.shape
    return pl.pallas_call(
        _xo_router_kernel,
        grid=(rows // tm,),
        in_specs=[pl.BlockSpec((tm, d), lambda i: (i, 0)),
                  pl.BlockSpec((tm, d), lambda i: (i, 0)),
                  pl.BlockSpec((d, d), lambda i: (0, 0)),
                  pl.BlockSpec((1, d), lambda i: (0, 0)),
                  pl.BlockSpec((d, LANES), lambda i: (0, 0)),
                  pl.BlockSpec((1, LANES), lambda i: (0, 0))],
        out_specs=[pl.BlockSpec((tm, d), lambda i: (i, 0)),
                   pl.BlockSpec((tm, d), lambda i: (i, 0)),
                   pl.BlockSpec((tm, LANES), lambda i: (i, 0))],
        out_shape=[jax.ShapeDtypeStruct((rows, d), F32),
                   jax.ShapeDtypeStruct((rows, d), F32),
                   jax.ShapeDtypeStruct((rows, LANES), F32)],
        compiler_params=_cparams(("parallel",), 40),
        name="xo_router",
    )(x1, o, wxo_bf, g.reshape(1, d), wr_bf, br)


def _lane_cumsum(x):
    lane = lax.broadcasted_iota(I32, x.shape, 1)
    s = 1
    while s < LANES:
        x = x + jnp.where(lane >= s, pltpu.roll(x, s, 1), 0.0)
        s *= 2
    return x


def _route_dest_kernel(rp_ref, rs_ref, dest_ref, tab_ref, last_ref, r_buf, cnt_ref, carry_ref, pstart_ref, *,
                       n_prompt_chunks, t_pad, t_valid, tm, rows_cap, nblk_lanes):
    ph = pl.program_id(0)
    c = pl.program_id(1)
    chunk = r_buf.shape[0]
    is_prompt = c < n_prompt_chunks

    @pl.when(is_prompt)
    def _():
        r_buf[...] = rp_ref[...]

    @pl.when(jnp.logical_not(is_prompt))
    def _():
        r_buf[...] = rs_ref[...]

    @pl.when((ph == 0) & (c == 0))
    def _():
        cnt_ref[...] = jnp.zeros(cnt_ref.shape, F32)

    route = r_buf[...]
    lane = lax.broadcasted_iota(I32, (chunk, LANES), 1)
    row = lax.broadcasted_iota(I32, (chunk, LANES), 0)
    lanef = lane.astype(F32)
    valid = (((row & (t_pad - 1)) < t_valid).astype(I32) | is_prompt.astype(I32)) > 0
    oh0 = jnp.where((lanef == route[:, 0:1]) & valid, 1.0, 0.0)
    oh1 = jnp.where((lanef == route[:, 1:2]) & valid, 1.0, 0.0)
    cmat = oh0 + oh1
    csum = jnp.sum(cmat, axis=0, keepdims=True)

    @pl.when(ph == 0)
    def _():
        cnt_ref[...] = cnt_ref[...] + csum

    @pl.when((ph == 1) & (c == 0))
    def _():
        cnt = jnp.broadcast_to(cnt_ref[...], (SUBLANES, LANES))
        padc = jnp.floor((cnt + (tm - 1)) * (1.0 / tm)) * tm
        pend = _lane_cumsum(padc)
        pstart_ref[...] = (pend - padc)[0:1, :]
        carry_ref[...] = jnp.zeros(carry_ref.shape, F32)
        starts = lax.broadcasted_iota(I32, (SUBLANES, nblk_lanes), 1).astype(F32) * tm
        blk = jnp.zeros((SUBLANES, nblk_lanes), F32)
        for e in range(N_EXPERTS):
            blk = blk + jnp.where(pend[:, e:e + 1] <= starts, 1.0, 0.0)
        blk = jnp.minimum(blk, N_EXPERTS - 1.0)
        nused = jnp.broadcast_to(pend[:, N_EXPERTS - 1:N_EXPERTS] * (1.0 / tm), (SUBLANES, nblk_lanes))
        sub = lax.broadcasted_iota(I32, (SUBLANES, nblk_lanes), 0)
        tab_ref[...] = jnp.where(sub == 0, blk, nused)
        last_ref[...] = jnp.where(cnt > 0.0, pend * (1.0 / tm) - 1.0, -1.0)

    @pl.when(ph == 1)
    def _():
        rt = lax.broadcasted_iota(I32, (chunk, chunk), 0)
        cs = lax.broadcasted_iota(I32, (chunk, chunk), 1)
        tri = jnp.where(rt > cs, 1.0, 0.0).astype(BF16)
        before = jnp.dot(tri, cmat.astype(BF16), preferred_element_type=F32) + carry_ref[...]
        base = before + pstart_ref[...]
        d0 = jnp.sum(base * oh0, axis=1, keepdims=True)
        d1 = jnp.sum(base * oh1, axis=1, keepdims=True)
        srow = (c - n_prompt_chunks) * chunk + row
        tshift = t_pad.bit_length() - 1
        padded_idx = (srow >> tshift) * (t_pad - t_valid) + (srow & (t_pad - 1)) - t_valid
        spare = (rows_cap + EXP_TOPK * padded_idx).astype(F32)
        out = jnp.where(lane == 0, d0, d1)
        out = jnp.where(valid, out, spare + lanef)
        dest_ref[...] = jnp.where(lane < EXP_TOPK, out, 0.0)
        carry_ref[...] = carry_ref[...] + csum


def _route_dest(route_p, route_s, t_pad, t_valid, tm, nblk, chunk=ROW_TILE):
    rp, rs = route_p.shape[0], route_s.shape[0]
    assert rp % chunk == 0 and rs % chunk == 0 and chunk % t_pad == 0
    npc, nsc = rp // chunk, rs // chunk
    rows_cap = nblk * tm
    nblk_lanes = -(-nblk // LANES) * LANES
    dest_f, tab, last = pl.pallas_call(
        functools.partial(_route_dest_kernel, n_prompt_chunks=npc, t_pad=t_pad, t_valid=t_valid, tm=tm,
                          rows_cap=rows_cap, nblk_lanes=nblk_lanes),
        grid=(2, npc + nsc),
        in_specs=[pl.BlockSpec((chunk, LANES), lambda ph, c: (jnp.minimum(c, npc - 1), 0)),
                  pl.BlockSpec((chunk, LANES), lambda ph, c: (jnp.maximum(c - npc, 0), 0))],
        out_specs=[pl.BlockSpec((chunk, LANES), lambda ph, c: (c * ph, 0)),
                   pl.BlockSpec((SUBLANES, nblk_lanes), lambda ph, c: (0, 0)),
                   pl.BlockSpec((SUBLANES, LANES), lambda ph, c: (0, 0))],
        out_shape=[jax.ShapeDtypeStruct((rp + rs, LANES), F32),
                   jax.ShapeDtypeStruct((SUBLANES, nblk_lanes), F32),
                   jax.ShapeDtypeStruct((SUBLANES, LANES), F32)],
        scratch_shapes=[pltpu.VMEM((chunk, LANES), F32), pltpu.VMEM((1, LANES), F32),
                        pltpu.VMEM((1, LANES), F32), pltpu.VMEM((1, LANES), F32)],
        compiler_params=_cparams(("arbitrary", "arbitrary"), 32),
        name="route_dest",
    )(route_p, route_s)
    dest = dest_f[:, :EXP_TOPK].astype(I32).reshape(-1)
    blk_e = tab[0, :nblk].astype(I32)
    nused = tab[1, :1].astype(I32)
    last_blk = last[0, :N_EXPERTS].astype(I32)
    return dest, blk_e, nused, last_blk


def _scatter_rows_kernel(dest_ref, last_ref, nused_ref, hp_ref, hs_ref, xs_ref, sbuf, sem, zsem, *,
                         tm, n_prompt_tiles, n_tiles, nblk):
    i = pl.program_id(0)
    slot = i % 2

    def wait_tile(s):
        for _ in range(EXP_TOPK):
            pltpu.make_async_copy(sbuf.at[s], xs_ref.at[pl.ds(0, tm)], sem.at[s]).wait()

    def clear_block(b):
        return pltpu.make_async_copy(sbuf.at[1], xs_ref.at[pl.ds(b * tm, tm)], zsem)

    @pl.when(i == 0)
    def _():
        sbuf[1] = jnp.zeros(sbuf.shape[1:], sbuf.dtype)
        nused = nused_ref[0]
        for e in range(N_EXPERTS):
            @pl.when(last_ref[e] >= 0)
            def _():
                clear_block(last_ref[e]).start()

        def start_unused(b, carry):
            clear_block(b).start()
            return carry

        def wait_one(b, carry):
            clear_block(0).wait()
            return carry

        lax.fori_loop(nused, nblk, start_unused, 0)
        for e in range(N_EXPERTS):
            @pl.when(last_ref[e] >= 0)
            def _():
                clear_block(0).wait()
        lax.fori_loop(nused, nblk, wait_one, 0)

    @pl.when(i >= 2)
    def _():
        wait_tile(slot)

    @pl.when(i < n_prompt_tiles)
    def _():
        sbuf[slot] = hp_ref[...]

    @pl.when(i >= n_prompt_tiles)
    def _():
        sbuf[slot] = hs_ref[...]

    base = EXP_TOPK * i * tm
    for r in range(tm):
        for k in range(EXP_TOPK):
            pltpu.make_async_copy(sbuf.at[slot, pl.ds(r, 1)],
                                  xs_ref.at[pl.ds(dest_ref[base + EXP_TOPK * r + k], 1)], sem.at[slot]).start()

    @pl.when(i == n_tiles - 1)
    def _():
        wait_tile(slot)
        wait_tile(1 - slot)


def _scatter_rows(dest, last_blk, nused, h_p, h_s, nblk, n_spare, tm=ROW_TILE):
    d = h_p.shape[1]
    npt, nst = h_p.shape[0] // tm, h_s.shape[0] // tm
    assert npt + nst >= 2 and MOE_TILE == tm
    grid_spec = pltpu.PrefetchScalarGridSpec(
        num_scalar_prefetch=3,
        grid=(npt + nst,),
        in_specs=[pl.BlockSpec((tm, d), lambda i, de, lb, nu: (jnp.minimum(i, npt - 1), 0)),
                  pl.BlockSpec((tm, d), lambda i, de, lb, nu: (jnp.maximum(i - npt, 0), 0))],
        out_specs=pl.BlockSpec(memory_space=pl.ANY),
        scratch_shapes=[pltpu.VMEM((2, tm, d), F32), pltpu.SemaphoreType.DMA((2,)), pltpu.SemaphoreType.DMA(())],
    )
    return pl.pallas_call(
        functools.partial(_scatter_rows_kernel, tm=tm, n_prompt_tiles=npt, n_tiles=npt + nst, nblk=nblk),
        grid_spec=grid_spec,
        out_shape=jax.ShapeDtypeStruct((nblk * tm + n_spare, d), F32),
        compiler_params=_cparams(("arbitrary",), 32),
        name="scatter_rows",
    )(dest, last_blk, nused, h_p, h_s)


def _moe_kernel(blk_e_ref, nused_ref, x_ref, w1_hbm, w3_hbm, w2_hbm, o_ref, wf_ref, wb_ref, sem):
    i = pl.program_id(0)
    nused = nused_ref[0]
    e = blk_e_ref[i]
    w_hbm = (w1_hbm, w3_hbm, w2_hbm)

    def start_fetch(expert):
        for k in range(3):
            pltpu.make_async_copy(w_hbm[k].at[expert], wf_ref.at[k], sem.at[k]).start()

    @pl.when(i == 0)
    def _():
        start_fetch(e)

    @pl.when((i < nused) & ((i == 0) | (e != blk_e_ref[jnp.maximum(i - 1, 0)])))
    def _():
        for k in range(3):
            pltpu.make_async_copy(w_hbm[k].at[e], wf_ref.at[k], sem.at[k]).wait()
            wb_ref[k] = wf_ref[k].astype(BF16)
        nxt = lax.while_loop(lambda j: (j < nused) & (blk_e_ref[jnp.minimum(j, nused - 1)] == e),
                             lambda j: j + 1, i + 1)

        @pl.when(nxt < nused)
        def _():
            start_fetch(blk_e_ref[nxt])

    @pl.when(i < nused)
    def _():
        x = x_ref[...].astype(BF16)
        a = jnp.dot(x, wb_ref[0], preferred_element_type=F32)
        b = jnp.dot(x, wb_ref[1], preferred_element_type=F32)
        hmid = (_silu(a) * b).astype(BF16)
        o_ref[...] = jnp.dot(hmid, wb_ref[2], preferred_element_type=F32)

    @pl.when(i >= nused)
    def _():
        o_ref[...] = jnp.zeros(o_ref.shape, o_ref.dtype)


def _moe_experts(xs, blk_e, nused, w1, w3, w2, tm=MOE_TILE):
    nblk = blk_e.shape[0]
    d = xs.shape[1]
    wspec = pl.BlockSpec(memory_space=pl.ANY)
    grid_spec = pltpu.PrefetchScalarGridSpec(
        num_scalar_prefetch=2,
        grid=(nblk,),
        in_specs=[pl.BlockSpec((tm, d), lambda i, be, nu: (jnp.minimum(i, nu[0] - 1), 0)), wspec, wspec, wspec],
        out_specs=pl.BlockSpec((tm, d), lambda i, be, nu: (i, 0)),
        scratch_shapes=[pltpu.VMEM((3, d, d), F32), pltpu.VMEM((3, d, d), BF16), pltpu.SemaphoreType.DMA((3,))],
    )
    return pl.pallas_call(
        _moe_kernel,
        grid_spec=grid_spec,
        out_shape=jax.ShapeDtypeStruct((nblk * tm, d), F32),
        compiler_params=_cparams(("arbitrary",), 40),
        name="moe_experts",
    )(blk_e, nused, xs, w1, w3, w2)


def _combine_kernel(dest_ref, x2_ref, r_ref, g_ref, outs_hbm, y_ref, buf, sem, *, tm):
    i = pl.program_id(0)
    n = pl.num_programs(0)
    n_rows = outs_hbm.shape[0]

    def start_tile(tile, slot):
        base = EXP_TOPK * tile * tm
        for r in range(tm):
            for k in range(EXP_TOPK):
                src = dest_ref[base + EXP_TOPK * r + k]
                src = jnp.where(src >= n_rows, src - n_rows, src)
                pltpu.make_async_copy(outs_hbm.at[pl.ds(src, 1)], buf.at[slot, k, pl.ds(r, 1)],
                                      sem.at[slot]).start()

    @pl.when(i == 0)
    def _():
        start_tile(0, 0)

    @pl.when(i + 1 < n)
    def _():
        start_tile(i + 1, (i + 1) % 2)

    slot = i % 2
    for k in range(EXP_TOPK):
        pltpu.make_async_copy(outs_hbm.at[pl.ds(0, tm)], buf.at[slot, k], sem.at[slot]).wait()
    route = r_ref[...]
    y = buf[slot, 0] * route[:, 2:3] + buf[slot, 1] * route[:, 3:4]
    y_ref[...] = _rms(x2_ref[...] + y, g_ref[...])


def _combine(x2, route, g_final, outs, dest, tm=ROW_TILE):
    rows, d = x2.shape
    grid_spec = pltpu.PrefetchScalarGridSpec(
        num_scalar_prefetch=1,
        grid=(rows // tm,),
        in_specs=[pl.BlockSpec((tm, d), lambda i, de: (i, 0)),
                  pl.BlockSpec((tm, LANES), lambda i, de: (i, 0)),
                  pl.BlockSpec((1, d), lambda i, de: (0, 0)),
                  pl.BlockSpec(memory_space=pl.ANY)],
        out_specs=pl.BlockSpec((tm, d), lambda i, de: (i, 0)),
        scratch_shapes=[pltpu.VMEM((2, EXP_TOPK, tm, d), F32), pltpu.SemaphoreType.DMA((2,))],
    )
    return pl.pallas_call(
        functools.partial(_combine_kernel, tm=tm),
        grid_spec=grid_spec,
        out_shape=jax.ShapeDtypeStruct((rows, d), F32),
        compiler_params=_cparams(("arbitrary",), 32),
        name="moe_combine",
    )(dest, x2, route, g_final.reshape(1, d), outs)


def _layer_tail(x, o, att, att_transposed, mem_k, mem_v, batch, t, xq_tile, wts):
    q_dtype = BF16 if xq_tile % (2 * SUBLANES) == 0 else F32
    x1, q = _outproj(x, o, att, wts["w_out"], wts["g_cross"], wts["w_xq"], att_transposed, q_dtype)
    xo = _xattn(q, mem_k, mem_v, batch, t, xq_tile)
    x2, h, route = _xo_router(x1, xo, wts["w_xo"], wts["g_ffn"], wts["w_router"], wts["b_router"])
    return x2, h, route


def _moe_and_final(x2_p, h_p, route_p, x2_s, h_s, route_s, n_sample_tokens, t_pad, t_valid, wts):
    rp, d = h_p.shape
    rs = h_s.shape[0]
    tm = MOE_TILE
    n_assign = (rp + n_sample_tokens) * EXP_TOPK
    nblk = -(-(n_assign + N_EXPERTS * (tm - 1)) // tm)
    dest, blk_e, nused, last_blk = _route_dest(route_p, route_s, t_pad, t_valid, tm, nblk)
    n_spare = EXP_TOPK * (rs // t_pad) * (t_pad - t_valid)
    xs = _scatter_rows(dest, last_blk, nused, h_p, h_s, nblk, n_spare)
    outs = _moe_experts(xs, blk_e, nused, wts["w1"], wts["w3"], wts["w2"])
    y_p = _combine(x2_p, route_p, wts["g_final"], outs, dest[:EXP_TOPK * rp])
    y_s = _combine(x2_s, route_s, wts["g_final"], outs, dest[EXP_TOPK * rp:])
    return y_p, y_s


def kernel(x_prompt, x_sample, state_hgrn, cache_moba_k, cache_moba_v, cache_mem_k, cache_mem_v, page_table,
           mem_prompt, g_mix, w_in, hg_lb, hg_norm, w_out, g_cross, g_mem, w_xq, w_xk, w_xv, w_xo, g_ffn,
           w_grp, b_grp, w_exp, b_exp, w1, w3, w2, g_final):
    depth = g_mix.shape[0]
    assert depth == 1
    bp, tp, d = x_prompt.shape
    bs, ts, _ = x_sample.shape
    mem_len = mem_prompt.shape[1]
    l = 0

    lb = jnp.cumsum(jax.nn.softmax(hg_lb.astype(F32), axis=0), axis=0)[l]
    slopes = jnp.asarray(np.power(2.0, -8.0 * np.arange(1, MB_HEADS + 1) / MB_HEADS).astype(np.float32))
    n_router = N_GROUPS + N_EXPERTS
    w_router = jnp.pad(jnp.concatenate([w_grp[l], w_exp[l]], axis=1), ((0, 0), (0, LANES - n_router)))
    b_router = jnp.pad(jnp.concatenate([b_grp[l], b_exp[l]]), (0, LANES - n_router)).reshape(1, LANES)
    wts = {
        "w_out": w_out[l].astype(BF16), "g_cross": g_cross[l], "w_xq": w_xq[l].astype(BF16),
        "w_xo": w_xo[l].astype(BF16), "g_ffn": g_ffn[l], "w_router": w_router.astype(BF16),
        "b_router": b_router.astype(F32), "w1": w1[l], "w3": w3[l], "w2": w2[l], "g_final": g_final,
    }
    w_in_bf = w_in[l].astype(BF16)
    o_hq, o_mq, o_mk, o_mv = 0, 4 * HG_WIDTH, 4 * HG_WIDTH + MB_WIDTH, 4 * HG_WIDTH + 2 * MB_WIDTH
    o_end = o_mv + MB_WIDTH

    xp = x_prompt.reshape(bp * tp, d)
    w_kv = jnp.concatenate([w_xk[l], w_xv[l]], axis=1).astype(BF16)
    memk_p, memv_p = _rms_proj(mem_prompt.reshape(bp * mem_len, d), g_mem[l], w_kv,
                               [(0, d, [(1.0, False)]), (d, 2 * d, [(1.0, False)])], [F32, F32])
    hp_p, qt_p, kt_p, vt_p = _rms_proj(
        xp, g_mix[l], w_in_bf,
        [(o_hq, o_mq, [(1.0, False)]), (o_mq, o_mk, [(MB_DIM ** -0.5, True)]), (o_mk, o_mv, [(1.0, "batched")]),
         (o_mv, o_end, [(1.0, "batched")])],
        [F32, BF16, F32, F32], seq_len=tp)
    o_p, s_p = _hgrn(hp_p, lb, hg_norm[l], None, bp, tp, tp, 128)
    att_p = _moba_prompt(qt_p, kt_p, vt_p, slopes, bp, tp)
    x2_p, h_p, route_p = _tail_fused(xp, o_p, att_p, memk_p, memv_p, bp, tp, wts)

    tpad = SAMPLE_PAD_T
    xs = jnp.pad(x_sample, ((0, 0), (0, tpad - ts), (0, 0))).reshape(bs * tpad, d)
    hp_s, mq_s, mk_s, mv_s = _rms_proj(
        xs, g_mix[l], w_in_bf,
        [(o_hq, o_mq, [(1.0, False)]), (o_mq, o_mk, [(1.0, False)]), (o_mk, o_mv, [(1.0, False)]),
         (o_mv, o_end, [(1.0, False)])],
        [F32, F32, F32, F32])
    o_s, s_s = _hgrn(hp_s, lb, hg_norm[l], state_hgrn[l], bs, tpad, ts, tpad)
    mk4 = mk_s.reshape(bs, tpad, MB_HEADS, MB_DIM)
    mv4 = mv_s.reshape(bs, tpad, MB_HEADS, MB_DIM)
    n_phys, page = cache_moba_k.shape[1], cache_moba_k.shape[2]
    pool_kt = jnp.transpose(cache_moba_k[l], (0, 2, 3, 1)).reshape(n_phys, MB_WIDTH, page)
    pool_vt = jnp.transpose(cache_moba_v[l], (0, 2, 3, 1)).reshape(n_phys, MB_WIDTH, page)
    att_s = _moba_sample(mq_s, mk_s, mv_s, pool_kt, pool_vt, page_table, ts)

    def mem_rows(c):
        c = c[l].reshape(bs, mem_len, XA_HEADS, XA_DIM // LANES, LANES)
        return jnp.transpose(c, (0, 1, 3, 2, 4)).reshape(bs, mem_len * d // LANES, LANES)

    memk_s, memv_s = mem_rows(cache_mem_k), mem_rows(cache_mem_v)
    x2_s, h_s, route_s = _layer_tail(xs, o_s, att_s, False, memk_s, memv_s, bs, tpad, tpad, wts)
    y_p, y_s = _moe_and_final(x2_p, h_p, route_p, x2_s, h_s, route_s, bs * ts, tpad, ts, wts)

    return (y_p.reshape(bp, tp, d),
            y_s.reshape(bs, tpad, d)[:, :ts],
            s_p.reshape(1, bp, HG_HEADS, HG_DIM, HG_DIM),
            jnp.transpose(kt_p.reshape(1, bp, MB_HEADS, MB_DIM, tp), (0, 1, 4, 2, 3)),
            jnp.transpose(vt_p.reshape(1, bp, MB_HEADS, MB_DIM, tp), (0, 1, 4, 2, 3)),
            memk_p.reshape(1, bp, mem_len, XA_HEADS, XA_DIM),
            memv_p.reshape(1, bp, mem_len, XA_HEADS, XA_DIM),
            s_s.reshape(1, bs, HG_HEADS, HG_DIM, HG_DIM),
            mk4[:, :ts].reshape(1, bs, ts, MB_HEADS, MB_DIM),
            mv4[:, :ts].reshape(1, bs, ts, MB_HEADS, MB_DIM))
```

```python
import functools

import numpy as np
import jax
import jax.numpy as jnp
from jax import lax
from jax.experimental import pallas as pl
from jax.experimental.pallas import tpu as pltpu

F32 = jnp.float32
BF16 = jnp.bfloat16
I32 = jnp.int32

D_MODEL = 1024
HG_HEADS = 4
HG_DIM = 128
HG_WIDTH = HG_HEADS * HG_DIM
MB_HEADS = 8
MB_DIM = 64
MB_WIDTH = MB_HEADS * MB_DIM
MB_BLOCK = 256
MB_TOPK = 3
XA_HEADS = 4
XA_DIM = 256
N_GROUPS = 4
EXP_PER_GROUP = 8
N_EXPERTS = N_GROUPS * EXP_PER_GROUP
EXP_TOPK = 2
RMS_EPS = 1e-6
NEG_INF = float("-inf")

LANES = 128
SUBLANES = 8
MIB = 1024 * 1024

ROW_TILE = 256
MOE_TILE = 256
SAMPLE_PAD_T = 8
MOBA_PAIRS = 4
XATTN_SEQS_PER_STEP = 4

_NT = (((1,), (1,)), ((), ()))


def _cparams(semantics, vmem_mib):
    return pltpu.CompilerParams(dimension_semantics=semantics, vmem_limit_bytes=vmem_mib * MIB)


def _rms(x, g):
    return x * lax.rsqrt(jnp.mean(x * x, axis=-1, keepdims=True) + RMS_EPS) * g


def _sigmoid(x):
    return 0.5 * jnp.tanh(0.5 * x) + 0.5


def _silu(x):
    return x * _sigmoid(x)


def _rms_proj_kernel(x_ref, g_ref, w_ref, *o_refs, segs):
    h = _rms(x_ref[...], g_ref[...]).astype(BF16)
    k = 0
    for lo, hi, outs in segs:
        r = jnp.dot(h, w_ref[:, lo:hi], preferred_element_type=F32)
        for scale, transposed in outs:
            y = r if scale == 1.0 else r * scale
            if transposed:
                y = y.T
            o_refs[k][...] = y.astype(o_refs[k].dtype)
            k += 1


def _rms_proj(x, g, w_bf, segs, dtypes, tm=ROW_TILE, vmem_mib=48, seq_len=None):
    rows, d = x.shape
    assert rows % tm == 0
    out_shape, out_specs = [], []
    k = 0
    for lo, hi, outs in segs:
        n = hi - lo
        for _, transposed in outs:
            if transposed == "batched":
                per_seq = seq_len // tm
                out_shape.append(jax.ShapeDtypeStruct((rows // seq_len, n, seq_len), dtypes[k]))
                out_specs.append(pl.BlockSpec((None, n, tm), lambda i: (i // per_seq, 0, i % per_seq)))
            elif transposed:
                out_shape.append(jax.ShapeDtypeStruct((n, rows), dtypes[k]))
                out_specs.append(pl.BlockSpec((n, tm), lambda i: (0, i)))
            else:
                out_shape.append(jax.ShapeDtypeStruct((rows, n), dtypes[k]))
                out_specs.append(pl.BlockSpec((tm, n), lambda i: (i, 0)))
            k += 1
    return pl.pallas_call(
        functools.partial(_rms_proj_kernel, segs=segs),
        grid=(rows // tm,),
        in_specs=[pl.BlockSpec((tm, d), lambda i: (i, 0)),
                  pl.BlockSpec((1, d), lambda i: (0, 0)),
                  pl.BlockSpec(w_bf.shape, lambda i: (0, 0))],
        out_specs=out_specs,
        out_shape=out_shape,
        compiler_params=_cparams(("parallel",), vmem_mib),
        name="rms_proj",
    )(x, g.reshape(1, d), w_bf)


def _cumsum_rows(x):
    n = x.shape[0]
    row = lax.broadcasted_iota(I32, x.shape, 0)
    s = 1
    while s < n:
        x = x + jnp.where(row >= s, pltpu.roll(x, s, 0), 0.0)
        s *= 2
    return x


def _hgrn_intra(q, kk, b, c, w):
    row = lax.broadcasted_iota(I32, (c, HG_DIM), 0)
    acc = None
    half = c // 2
    while half >= SUBLANES:
        two = 2 * half
        nblk = c // two
        pieces = [jnp.broadcast_to(b[i * two + half - 1:i * two + half, :], (two, HG_DIM)) for i in range(nblk)]
        bm = pieces[0] if nblk == 1 else jnp.concatenate(pieces, axis=0)
        second = (row & (two - 1)) >= half
        qt = jnp.where(second, q * jnp.exp(jnp.where(second, b - bm, 0.0)), 0.0)
        kt = jnp.where(second, 0.0, kk * jnp.exp(jnp.where(second, 0.0, bm - b)))
        al = lax.dot_general(qt.astype(BF16), kt.astype(BF16), _NT, preferred_element_type=F32)
        if nblk > 1:
            shift = two.bit_length() - 1
            rt = lax.broadcasted_iota(I32, (c, c), 0) >> shift
            cs = lax.broadcasted_iota(I32, (c, c), 1) >> shift
            al = jnp.where(rt == cs, al, 0.0)
        acc = al if acc is None else acc + al
        half //= 2
    r8 = lax.broadcasted_iota(I32, (SUBLANES, HG_DIM), 0)
    lane = lax.broadcasted_iota(I32, (SUBLANES, w), 1)
    blocks = []
    for g in range(c // SUBLANES):
        lo = g * SUBLANES
        qg, kg, bg = q[lo:lo + SUBLANES], kk[lo:lo + SUBLANES], b[lo:lo + SUBLANES]
        ag = jnp.zeros((SUBLANES, w), F32)
        for s in range(SUBLANES):
            e = jnp.exp(jnp.where(r8 >= s, bg - bg[s:s + 1, :], NEG_INF))
            p = jnp.sum(qg * kg[s:s + 1, :] * e, axis=1, keepdims=True)
            ag = jnp.where(lane == lo + s, p, ag)
        blocks.append(ag)
    diag = blocks[0] if len(blocks) == 1 else jnp.concatenate(blocks, axis=0)
    return diag if acc is None else acc + diag


def _hgrn_kernel(*refs, c, t_valid, has_s0):
    if has_s0:
        hp_ref, lb_ref, gn_ref, s0_ref, o_ref, sout_ref, st_ref = refs
    else:
        hp_ref, lb_ref, gn_ref, o_ref, sout_ref, st_ref = refs
    ci = pl.program_id(1)
    last = pl.num_programs(1) - 1
    w = max(c, LANES)

    @pl.when(ci == 0)
    def _():
        for h in range(HG_HEADS):
            st_ref[h] = s0_ref[0, h].T if has_s0 else jnp.zeros((HG_DIM, HG_DIM), F32)

    row = lax.broadcasted_iota(I32, (c, HG_DIM), 0)
    for h in range(HG_HEADS):
        lo = h * HG_DIM
        hq = hp_ref[:, lo:lo + HG_DIM]
        hf = hp_ref[:, HG_WIDTH + lo:HG_WIDTH + lo + HG_DIM]
        v = hp_ref[:, 2 * HG_WIDTH + lo:2 * HG_WIDTH + lo + HG_DIM]
        hg = hp_ref[:, 3 * HG_WIDTH + lo:3 * HG_WIDTH + lo + HG_DIM]
        lb = lb_ref[:, lo:lo + HG_DIM]
        q = _silu(hq)
        f = lb + (1.0 - lb) * _sigmoid(hf)
        logf = jnp.log(f)
        kk = 1.0 - f
        if t_valid < c:
            valid = row < t_valid
            logf = jnp.where(valid, logf, 0.0)
            kk = jnp.where(valid, kk, 0.0)
            v = jnp.where(valid, v, 0.0)
        b = _cumsum_rows(logf)
        a = _hgrn_intra(q, kk, b, c, w)
        st = st_ref[h]
        bl = b[c - 1:c, :]
        k2 = kk * jnp.exp(bl - b)
        if c < w:
            zpad = jnp.zeros((w - c, HG_DIM), F32)
            vp = jnp.concatenate([v, zpad], axis=0)
            k2 = jnp.concatenate([k2, zpad], axis=0)
        else:
            vp = v
        vb = vp.astype(BF16)
        o = lax.dot_general((q * jnp.exp(b)).astype(BF16), st.astype(BF16), _NT, preferred_element_type=F32)
        o = o + jnp.dot(a.astype(BF16), vb, preferred_element_type=F32)
        o_ref[:, lo:lo + HG_DIM] = (_rms(o, gn_ref[...]) * _silu(hg)).astype(o_ref.dtype)
        st_new = st * jnp.exp(bl) + jnp.dot(vp.T.astype(BF16), k2.astype(BF16), preferred_element_type=F32)
        st_ref[h] = st_new

        @pl.when(ci == last)
        def _():
            sout_ref[0, h] = st_new.T


def _hgrn(hp, lb, gn, s0, batch, t_pad, t_valid, c):
    rows = hp.shape[0]
    nc = t_pad // c
    has_s0 = s0 is not None
    in_specs = [pl.BlockSpec((c, 4 * HG_WIDTH), lambda b, i: (b * nc + i, 0)),
                pl.BlockSpec((1, HG_WIDTH), lambda b, i: (0, 0)),
                pl.BlockSpec((1, HG_DIM), lambda b, i: (0, 0))]
    args = [hp, lb.reshape(1, HG_WIDTH), gn.reshape(1, HG_DIM)]
    if has_s0:
        in_specs.append(pl.BlockSpec((1, HG_HEADS, HG_DIM, HG_DIM), lambda b, i: (b, 0, 0, 0)))
        args.append(s0)
    return pl.pallas_call(
        functools.partial(_hgrn_kernel, c=c, t_valid=min(t_valid, c), has_s0=has_s0),
        grid=(batch, nc),
        in_specs=in_specs,
        out_specs=[pl.BlockSpec((c, HG_WIDTH), lambda b, i: (b * nc + i, 0)),
                   pl.BlockSpec((1, HG_HEADS, HG_DIM, HG_DIM), lambda b, i: (b, 0, 0, 0))],
        out_shape=[jax.ShapeDtypeStruct((rows, HG_WIDTH), BF16),
                   jax.ShapeDtypeStruct((batch, HG_HEADS, HG_DIM, HG_DIM), F32)],
        scratch_shapes=[pltpu.VMEM((HG_HEADS, HG_DIM, HG_DIM), F32)],
        compiler_params=_cparams(("parallel", "arbitrary"), 32),
        name="hgrn2",
    )(*args)


def _topk_block_mask(gate, n_valid, nrow):
    nb = gate.shape[0]
    cnt = jnp.zeros(gate.shape, I32)
    for n2 in range(nb):
        g2 = gate[n2:n2 + 1, :]
        beats = (g2 > gate) | ((g2 == gate) & (n2 < nrow))
        cnt = cnt + jnp.where(beats, (n_valid > n2).astype(I32), 0)
    return (cnt < MB_TOPK) & (nrow < n_valid)


def _moba_prompt_kernel(slope_ref, qt_ref, kt_ref, vt_ref, o_ref, kb_ref, vb_ref, km_ref, sel_ref, *, nb, npairs):
    hg = pl.program_id(1)
    qi = pl.program_id(2)
    tq = MB_BLOCK
    pw = 2 * tq
    width = npairs * pw
    pair_rows = 2 * MB_DIM

    @pl.when(qi == 0)
    def _():
        pos = jnp.where(lax.broadcasted_iota(I32, (MB_BLOCK, pair_rows), 1) == 0,
                        lax.broadcasted_iota(I32, (MB_BLOCK, pair_rows), 0), 0).astype(BF16)
        for pp in range(npairs):
            means = []
            for j in range(nb):
                kj = kt_ref[pp * pair_rows:(pp + 1) * pair_rows, j * MB_BLOCK:(j + 1) * MB_BLOCK].T
                kb_ref[pp, j * MB_BLOCK:(j + 1) * MB_BLOCK, :pair_rows] = kj.astype(BF16)
                kb_ref[pp, j * MB_BLOCK:(j + 1) * MB_BLOCK, pair_rows:] = pos
                means.append(jnp.sum(kj, axis=0, keepdims=True) * (1.0 / MB_BLOCK))
                vb_ref[pp, j] = vt_ref[pp * pair_rows:(pp + 1) * pair_rows,
                                       j * MB_BLOCK:(j + 1) * MB_BLOCK].astype(BF16)
            km_ref[pp] = jnp.concatenate(means, axis=0)

    second_head = lax.broadcasted_iota(I32, (pair_rows, tq), 0) >= MB_DIM
    first_row = lax.broadcasted_iota(I32, (pair_rows, pw), 0) == 0
    qaugs, gates, slope_parts = [], [], []
    for pp in range(npairs):
        qt2 = qt_ref[pp * pair_rows:(pp + 1) * pair_rows, :]
        zero = jnp.zeros_like(qt2)
        qcat = jnp.concatenate([jnp.where(second_head, zero, qt2), jnp.where(second_head, qt2, zero)], axis=1)
        gates.append(jnp.dot(km_ref[pp].astype(BF16), qcat, preferred_element_type=F32))
        pair_slopes = jnp.concatenate(
            [jnp.full((1, tq), slope_ref[(hg * npairs + pp) * 2 + i], F32) for i in range(2)], axis=1)
        slope_parts.append(pair_slopes)
        srows = jnp.where(first_row, pair_slopes, 0.0).astype(BF16)
        qaugs.append(jnp.concatenate([qcat, srows], axis=0))
    slope_row = slope_parts[0] if npairs == 1 else jnp.concatenate(slope_parts, axis=1)
    gate = gates[0] if npairs == 1 else jnp.concatenate(gates, axis=1)
    nrow = lax.broadcasted_iota(I32, (nb, width), 0)
    sel = _topk_block_mask(gate, qi, nrow).astype(F32)
    for n in range(nb):
        sel_ref[n] = jnp.broadcast_to(sel[n:n + 1, :], (SUBLANES, width))

    def scores(j):
        start = pl.multiple_of(j * MB_BLOCK, MB_BLOCK)
        parts = [jnp.dot(kb_ref[pp, pl.ds(start, MB_BLOCK), :], qaugs[pp], preferred_element_type=F32)
                 for pp in range(npairs)]
        return parts[0] if npairs == 1 else jnp.concatenate(parts, axis=1)

    def values(j, p):
        pb = p.astype(BF16)
        return [jnp.dot(vb_ref[pp, j], pb[:, pp * pw:(pp + 1) * pw], preferred_element_type=F32)
                for pp in range(npairs)]

    lane = lax.broadcasted_iota(I32, (MB_BLOCK, width), 1) & (tq - 1)
    rowk = lax.broadcasted_iota(I32, (MB_BLOCK, width), 0)
    s = jnp.where(rowk <= lane, scores(qi), NEG_INF)
    m = jnp.max(s, axis=0, keepdims=True)
    p = jnp.exp(s - m)
    l = jnp.sum(p, axis=0, keepdims=True)
    accs = values(qi, p)

    def body(j, carry):
        m, l, accs = carry
        off = jnp.full((1, width), (qi - j) * MB_BLOCK, I32).astype(F32) * slope_row
        picked = sel_ref[j][0:1, :] > 0.5
        s = scores(j)
        m_new = jnp.maximum(m, jnp.where(picked, jnp.max(s, axis=0, keepdims=True) - off, NEG_INF))
        alpha = jnp.exp(m - m_new)
        p = jnp.exp(s - jnp.where(picked, m_new + off, float("inf")))
        l = alpha * l + jnp.sum(p, axis=0, keepdims=True)
        pv = values(j, p)
        accs = [alpha[:, pp * pw:(pp + 1) * pw] * accs[pp] + pv[pp] for pp in range(npairs)]
        return m_new, l, accs

    m, l, accs = lax.fori_loop(0, qi, body, (m, l, accs))
    for pp in range(npairs):
        a = accs[pp] / l[:, pp * pw:(pp + 1) * pw]
        o_ref[pp * pair_rows:(pp + 1) * pair_rows, :] = jnp.where(second_head, a[:, tq:], a[:, :tq])


def _moba_prompt(qt, kt, vt, slopes, batch, t, npairs=MOBA_PAIRS):
    nb = t // MB_BLOCK
    rows = batch * t
    gr = 2 * MB_DIM * npairs
    width = npairs * 2 * MB_BLOCK
    return pl.pallas_call(
        functools.partial(_moba_prompt_kernel, nb=nb, npairs=npairs),
        grid=(batch, MB_WIDTH // gr, nb),
        in_specs=[pl.BlockSpec(memory_space=pltpu.SMEM),
                  pl.BlockSpec((gr, MB_BLOCK), lambda b, h, i: (h, b * nb + i)),
                  pl.BlockSpec((None, gr, t), lambda b, h, i: (b, h, 0)),
                  pl.BlockSpec((None, gr, t), lambda b, h, i: (b, h, 0))],
        out_specs=pl.BlockSpec((gr, MB_BLOCK), lambda b, h, i: (h, b * nb + i)),
        out_shape=jax.ShapeDtypeStruct((MB_WIDTH, rows), F32),
        scratch_shapes=[pltpu.VMEM((npairs, t, 4 * MB_DIM), BF16),
                        pltpu.VMEM((npairs, nb, 2 * MB_DIM, MB_BLOCK), BF16),
                        pltpu.VMEM((npairs, nb, 2 * MB_DIM), F32),
                        pltpu.VMEM((nb, SUBLANES, width), F32)],
        compiler_params=_cparams(("parallel", "parallel", "arbitrary"), 48),
        name="moba_prompt",
    )(slopes, qt, kt, vt)


def _moba_sample_kernel(pt_ref, slope_ref, hm_ref, q_ref, kn_ref, vn_ref, *rest, npages, page, t_new, past_len):
    del pt_ref
    k_pages = rest[:npages]
    v_pages = rest[npages:2 * npages]
    o_ref = rest[2 * npages]
    nrows = t_new * MB_HEADS
    pages_per_block = MB_BLOCK // page
    nb = npages // pages_per_block
    hm = hm_ref[...]
    slope = slope_ref[...][:, 0:1]
    row_q = lax.broadcasted_iota(I32, (nrows, 1), 0) >> 3
    pos_q = past_len + row_q

    q = q_ref[...] * (MB_DIM ** -0.5)
    qbd = jnp.concatenate([jnp.broadcast_to(q[t:t + 1, :], (MB_HEADS, MB_WIDTH)) * hm for t in range(t_new)],
                          axis=0).astype(BF16)

    lane = lax.broadcasted_iota(I32, (nrows, page), 1)
    scores = []
    gates = [None] * nb
    for p in range(npages):
        s = jnp.dot(qbd, k_pages[p][0].astype(BF16), preferred_element_type=F32)
        rs = jnp.sum(s, axis=1, keepdims=True)
        n = p // pages_per_block
        gates[n] = rs if gates[n] is None else gates[n] + rs
        scores.append(s - slope * (pos_q - (p * page + lane)).astype(F32))
    sels = []
    for n in range(nb):
        cnt = jnp.zeros((nrows, 1), I32)
        for n2 in range(nb):
            beats = (gates[n2] > gates[n]) | ((gates[n2] == gates[n]) & (n2 < n))
            cnt = cnt + jnp.where(beats, 1, 0)
        sels.append(jnp.where(cnt < MB_TOPK, 1.0, 0.0))

    zrows = jnp.zeros((LANES - SAMPLE_PAD_T, MB_WIDTH), F32)
    knp = jnp.concatenate([kn_ref[...], zrows], axis=0).astype(BF16)
    vnp = jnp.concatenate([vn_ref[...], zrows], axis=0).astype(BF16)
    lane_n = lax.broadcasted_iota(I32, (nrows, LANES), 1)
    s_new = lax.dot_general(qbd, knp, _NT, preferred_element_type=F32)
    s_new = jnp.where(lane_n <= row_q, s_new - slope * (row_q - lane_n).astype(F32), NEG_INF)
    m = jnp.max(s_new, axis=1, keepdims=True)
    for p in range(npages):
        sel_p = jnp.broadcast_to(sels[p // pages_per_block], (nrows, page)) > 0.5
        scores[p] = jnp.where(sel_p, scores[p], NEG_INF)
        m = jnp.maximum(m, jnp.max(scores[p], axis=1, keepdims=True))

    p_new = jnp.exp(s_new - m)
    l = jnp.sum(p_new, axis=1, keepdims=True)
    acc = jnp.dot(p_new.astype(BF16), vnp, preferred_element_type=F32)
    for p in range(npages):
        pr = jnp.exp(scores[p] - m)
        l = l + jnp.sum(pr, axis=1, keepdims=True)
        acc = acc + lax.dot_general(pr.astype(BF16), v_pages[p][0].astype(BF16), _NT,
                                    preferred_element_type=F32)
    acc = acc / l
    rows = [jnp.sum(acc[t * MB_HEADS:(t + 1) * MB_HEADS, :] * hm, axis=0, keepdims=True) for t in range(t_new)]
    rows.append(jnp.zeros((SAMPLE_PAD_T - t_new, MB_WIDTH), F32))
    o_ref[...] = jnp.concatenate(rows, axis=0)


def _moba_sample(mq, mk, mv, pool_kt, pool_vt, page_table, t_new):
    batch, npages = page_table.shape
    page = pool_kt.shape[2]
    past_len = npages * page
    assert past_len % MB_BLOCK == 0 and MB_BLOCK % page == 0 and t_new <= SAMPLE_PAD_T and page == LANES
    nrows = t_new * MB_HEADS
    slopes = np.power(2.0, -8.0 * np.arange(1, MB_HEADS + 1) / MB_HEADS).astype(np.float32)
    slope_rows = jnp.asarray(np.tile(np.tile(slopes, t_new)[:, None], (1, LANES)))
    head_mask = jnp.asarray((np.arange(MB_WIDTH)[None, :] // MB_DIM == np.arange(MB_HEADS)[:, None])
                            .astype(np.float32))
    new_spec = pl.BlockSpec((SAMPLE_PAD_T, MB_WIDTH), lambda b, pt: (b, 0))
    page_specs = [pl.BlockSpec((1, MB_WIDTH, page),
                               functools.partial(lambda b, pt, i: (pt[b * npages + i], 0, 0), i=i))
                  for i in range(npages)]
    grid_spec = pltpu.PrefetchScalarGridSpec(
        num_scalar_prefetch=1,
        grid=(batch,),
        in_specs=[pl.BlockSpec((nrows, LANES), lambda b, pt: (0, 0)),
                  pl.BlockSpec((MB_HEADS, MB_WIDTH), lambda b, pt: (0, 0)),
                  new_spec, new_spec, new_spec] + page_specs + page_specs,
        out_specs=new_spec,
    )
    return pl.pallas_call(
        functools.partial(_moba_sample_kernel, npages=npages, page=page, t_new=t_new, past_len=past_len),
        grid_spec=grid_spec,
        out_shape=jax.ShapeDtypeStruct((batch * SAMPLE_PAD_T, MB_WIDTH), F32),
        compiler_params=_cparams(("parallel",), 40),
        name="moba_sample",
    )(page_table.reshape(-1), slope_rows, head_mask, mq, mk, mv, *([pool_kt] * npages), *([pool_vt] * npages))


def _outproj_math(x, o, att, wo_ref, g, wq_ref):
    mix = jnp.dot(o, wo_ref[:HG_WIDTH, :], preferred_element_type=F32)
    mix = mix + jnp.dot(att.astype(BF16), wo_ref[HG_WIDTH:, :], preferred_element_type=F32)
    x1 = x + mix
    h = _rms(x1, g).astype(BF16)
    return x1, jnp.dot(h, wq_ref[...], preferred_element_type=F32) * (XA_DIM ** -0.5)


def _outproj_kernel(x_ref, o_ref, att_ref, wo_ref, g_ref, wq_ref, x1_ref, q_ref, *, att_transposed):
    att = att_ref[...]
    if att_transposed:
        att = att.T
    x1, q = _outproj_math(x_ref[...], o_ref[...], att, wo_ref, g_ref[...], wq_ref)
    x1_ref[...] = x1
    q_ref[...] = q.astype(q_ref.dtype)


def _outproj(x, o, att, wo_bf, g, wq_bf, att_transposed, q_dtype, tm=ROW_TILE):
    rows, d = x.shape
    att_spec = (pl.BlockSpec((MB_WIDTH, tm), lambda i: (0, i)) if att_transposed
                else pl.BlockSpec((tm, MB_WIDTH), lambda i: (i, 0)))
    return pl.pallas_call(
        functools.partial(_outproj_kernel, att_transposed=att_transposed),
        grid=(rows // tm,),
        in_specs=[pl.BlockSpec((tm, d), lambda i: (i, 0)),
                  pl.BlockSpec((tm, HG_WIDTH), lambda i: (i, 0)),
                  att_spec,
                  pl.BlockSpec((d, d), lambda i: (0, 0)),
                  pl.BlockSpec((1, d), lambda i: (0, 0)),
                  pl.BlockSpec((d, d), lambda i: (0, 0))],
        out_specs=[pl.BlockSpec((tm, d), lambda i: (i, 0)), pl.BlockSpec((tm, d), lambda i: (i, 0))],
        out_shape=[jax.ShapeDtypeStruct((rows, d), F32), jax.ShapeDtypeStruct((rows, d), q_dtype)],
        compiler_params=_cparams(("parallel",), 40),
        name="outproj_q",
    )(x, o, att, wo_bf, g.reshape(1, d), wq_bf)


def _xattn_tiled_kernel(q_ref, k_ref, v_ref, o_ref, *, nt):
    for g in range(k_ref.shape[0]):
        o_ref[g * nt:(g + 1) * nt, :] = _xattn_tiled_one(
            q_ref[g * nt:(g + 1) * nt, :], k_ref[g], v_ref[g], nt).astype(o_ref.dtype)


def _xattn_tiled_one(q, k, v, nt):
    ndt = XA_DIM // LANES
    nc = XA_HEADS * ndt
    half = XA_HEADS * nt
    q = q.astype(F32)
    a = jnp.concatenate([q[:, (h * ndt + dt) * LANES:(h * ndt + dt + 1) * LANES]
                         for dt in range(ndt) for h in range(XA_HEADS)], axis=0).astype(BF16)
    s = lax.dot_general(a, k.astype(BF16), _NT, preferred_element_type=F32)
    width = s.shape[1]
    lane_c = lax.broadcasted_iota(I32, s.shape, 1) & (nc - 1)
    assert nt & (nt - 1) == 0 and nc & (nc - 1) == 0
    row_c = lax.broadcasted_iota(I32, s.shape, 0) >> (nt.bit_length() - 1)
    s = jnp.where(lane_c == row_c, s, 0.0)
    tot = s[:half]
    for dt in range(1, ndt):
        tot = tot + pltpu.roll(s[dt * half:(dt + 1) * half], width - dt * XA_HEADS, 1)
    ok = ((lax.broadcasted_iota(I32, tot.shape, 1) & (nc - 1))
          == (lax.broadcasted_iota(I32, tot.shape, 0) >> (nt.bit_length() - 1)))
    tot = jnp.where(ok, tot, NEG_INF)
    p = jnp.exp(tot - jnp.max(tot, axis=-1, keepdims=True))
    p = p / jnp.sum(p, axis=-1, keepdims=True)
    pe = jnp.concatenate([p] + [pltpu.roll(p, dt * XA_HEADS, 1) for dt in range(1, ndt)], axis=0)
    o = jnp.dot(pe.astype(BF16), v.astype(BF16), preferred_element_type=F32)
    return jnp.concatenate([o[(dt * XA_HEADS + h) * nt:(dt * XA_HEADS + h + 1) * nt, :]
                            for h in range(XA_HEADS) for dt in range(ndt)], axis=1)


def _xattn_math(q, k_ref, v_ref):
    outs = []
    for h in range(XA_HEADS):
        lo = h * XA_DIM
        kh = k_ref[:, lo:lo + XA_DIM].astype(BF16)
        vh = v_ref[:, lo:lo + XA_DIM].astype(BF16)
        s = lax.dot_general(q[:, lo:lo + XA_DIM], kh, _NT, preferred_element_type=F32)
        p = jnp.exp(s - jnp.max(s, axis=-1, keepdims=True))
        p = p / jnp.sum(p, axis=-1, keepdims=True)
        outs.append(jnp.dot(p.astype(BF16), vh, preferred_element_type=F32))
    return outs


def _xattn_kernel(q_ref, k_ref, v_ref, o_ref):
    outs = _xattn_math(q_ref[...].astype(BF16), k_ref, v_ref)
    for h in range(XA_HEADS):
        o_ref[:, h * XA_DIM:(h + 1) * XA_DIM] = outs[h].astype(o_ref.dtype)


def _xattn(q, mem_k, mem_v, batch, t, tq):
    rows, d = q.shape
    nq = t // tq
    if mem_k.ndim == 3:
        assert tq == t and batch % XATTN_SEQS_PER_STEP == 0
        g = XATTN_SEQS_PER_STEP
        kv_spec = pl.BlockSpec((g,) + mem_k.shape[1:], lambda b: (b, 0, 0))
        return pl.pallas_call(
            functools.partial(_xattn_tiled_kernel, nt=t),
            grid=(batch // g,),
            in_specs=[pl.BlockSpec((g * t, d), lambda b: (b, 0)), kv_spec, kv_spec],
            out_specs=pl.BlockSpec((g * t, d), lambda b: (b, 0)),
            out_shape=jax.ShapeDtypeStruct((rows, d), q.dtype),
            compiler_params=_cparams(("parallel",), 40),
            name="xattn_tiled",
        )(q, mem_k, mem_v)
    m = mem_k.shape[0] // batch
    kv_spec = pl.BlockSpec((m, d), lambda b, i: (b, 0))
    return pl.pallas_call(
        _xattn_kernel,
        grid=(batch, nq),
        in_specs=[pl.BlockSpec((tq, d), lambda b, i: (b * nq + i, 0)), kv_spec, kv_spec],
        out_specs=pl.BlockSpec((tq, d), lambda b, i: (b * nq + i, 0)),
        out_shape=jax.ShapeDtypeStruct((rows, d), q.dtype),
        compiler_params=_cparams(("parallel", "arbitrary"), 32),
        name="xattn",
    )(q, mem_k, mem_v)


def _xo_router_kernel(x1_ref, o_ref, wo_ref, g_ref, wr_ref, br_ref, x2_ref, h_ref, r_ref):
    x2 = x1_ref[...] + jnp.dot(o_ref[...].astype(BF16), wo_ref[...], preferred_element_type=F32)
    x2_ref[...] = x2
    h = _rms(x2, g_ref[...])
    h_ref[...] = h
    r_ref[...] = _router_math(h, wr_ref, br_ref)


def _router_math(h, wr_ref, br_ref):
    logits = jnp.dot(h.astype(BF16), wr_ref[...], preferred_element_type=F32) + br_ref[...]
    lane = lax.broadcasted_iota(I32, logits.shape, 1)
    big = jnp.int32(LANES)

    def top1(mask):
        mx = jnp.max(jnp.where(mask, logits, NEG_INF), axis=-1, keepdims=True)
        idx = jnp.min(jnp.where(mask & (logits == mx), lane, big), axis=-1, keepdims=True)
        return mx, idx

    gmask = lane < N_GROUPS
    gmx, gsel = top1(gmask)
    gw = 1.0 / jnp.sum(jnp.where(gmask, jnp.exp(logits - gmx), 0.0), axis=-1, keepdims=True)
    elo = N_GROUPS + gsel * EXP_PER_GROUP
    emask = (lane >= elo) & (lane < elo + EXP_PER_GROUP)
    m1, i1 = top1(emask)
    m2, i2 = top1(emask & (lane != i1))
    e2 = jnp.exp(m2 - m1)
    g1 = gw / (1.0 + e2)
    g2 = gw * e2 / (1.0 + e2)
    out = jnp.where(lane == 0, (i1 - N_GROUPS).astype(F32), 0.0)
    out = jnp.where(lane == 1, (i2 - N_GROUPS).astype(F32), out)
    out = jnp.where(lane == 2, g1, out)
    return jnp.where(lane == 3, g2, out)


def _tail_fused_kernel(x_ref, o_ref, att_ref, wo_ref, gc_ref, wq_ref, k_ref, v_ref, wxo_ref, gf_ref, wr_ref,
                       br_ref, x2_ref, h_ref, r_ref):
    x1, q = _outproj_math(x_ref[...], o_ref[...], att_ref[...].T, wo_ref, gc_ref[...], wq_ref)
    outs = _xattn_math(q.astype(BF16), k_ref, v_ref)
    x2 = x1
    for h in range(XA_HEADS):
        x2 = x2 + jnp.dot(outs[h].astype(BF16), wxo_ref[h * XA_DIM:(h + 1) * XA_DIM, :],
                          preferred_element_type=F32)
    x2_ref[...] = x2
    hn = _rms(x2, gf_ref[...])
    h_ref[...] = hn
    r_ref[...] = _router_math(hn, wr_ref, br_ref)


def _tail_fused(x, o, att_t, mem_k, mem_v, batch, t, wts, tm=512):
    rows, d = x.shape
    nq = t // tm
    m = mem_k.shape[0] // batch
    row_spec = pl.BlockSpec((tm, d), lambda b, i: (b * nq + i, 0))
    w_spec = pl.BlockSpec((d, d), lambda b, i: (0, 0))
    g_spec = pl.BlockSpec((1, d), lambda b, i: (0, 0))
    kv_spec = pl.BlockSpec((m, d), lambda b, i: (b, 0))
    return pl.pallas_call(
        _tail_fused_kernel,
        grid=(batch, nq),
        in_specs=[row_spec,
                  pl.BlockSpec((tm, HG_WIDTH), lambda b, i: (b * nq + i, 0)),
                  pl.BlockSpec((MB_WIDTH, tm), lambda b, i: (0, b * nq + i)),
                  w_spec, g_spec, w_spec, kv_spec, kv_spec, w_spec, g_spec,
                  pl.BlockSpec((d, LANES), lambda b, i: (0, 0)),
                  pl.BlockSpec((1, LANES), lambda b, i: (0, 0))],
        out_specs=[row_spec, row_spec, pl.BlockSpec((tm, LANES), lambda b, i: (b * nq + i, 0))],
        out_shape=[jax.ShapeDtypeStruct((rows, d), F32),
                   jax.ShapeDtypeStruct((rows, d), F32),
                   jax.ShapeDtypeStruct((rows, LANES), F32)],
        compiler_params=_cparams(("parallel", "arbitrary"), 56),
        name="tail_fused",
    )(x, o, att_t, wts["w_out"], wts["g_cross"].reshape(1, d), wts["w_xq"], mem_k, mem_v, wts["w_xo"],
      wts["g_ffn"].reshape(1, d), wts["w_router"], wts["b_router"])


def _xo_router(x1, o, wxo_bf, g, wr_bf, br, tm=ROW_TILE):
    rows, d = x1.shape
    return pl.pallas_call(
        _xo_router_kernel,
        grid=(rows // tm,),
        in_specs=[pl.BlockSpec((tm, d), lambda i: (i, 0)),
                  pl.BlockSpec((tm, d), lambda i: (i, 0)),
                  pl.BlockSpec((d, d), lambda i: (0, 0)),
                  pl.BlockSpec((1, d), lambda i: (0, 0)),
                  pl.BlockSpec((d, LANES), lambda i: (0, 0)),
                  pl.BlockSpec((1, LANES), lambda i: (0, 0))],
        out_specs=[pl.BlockSpec((tm, d), lambda i: (i, 0)),
                   pl.BlockSpec((tm, d), lambda i: (i, 0)),
                   pl.BlockSpec((tm, LANES), lambda i: (i, 0))],
        out_shape=[jax.ShapeDtypeStruct((rows, d), F32),
                   jax.ShapeDtypeStruct((rows, d), F32),
                   jax.ShapeDtypeStruct((rows, LANES), F32)],
        compiler_params=_cparams(("parallel",), 40),
        name="xo_router",
    )(x1, o, wxo_bf, g.reshape(1, d), wr_bf, br)


def _lane_cumsum(x):
    lane = lax.broadcasted_iota(I32, x.shape, 1)
    s = 1
    while s < LANES:
        x = x + jnp.where(lane >= s, pltpu.roll(x, s, 1), 0.0)
        s *= 2
    return x


def _route_dest_kernel(rp_ref, rs_ref, dest_ref, tab_ref, last_ref, r_buf, cnt_ref, carry_ref, pstart_ref, *,
                       n_prompt_chunks, t_pad, t_valid, tm, rows_cap, nblk_lanes):
    ph = pl.program_id(0)
    c = pl.program_id(1)
    chunk = r_buf.shape[0]
    is_prompt = c < n_prompt_chunks

    @pl.when(is_prompt)
    def _():
        r_buf[...] = rp_ref[...]

    @pl.when(jnp.logical_not(is_prompt))
    def _():
        r_buf[...] = rs_ref[...]

    @pl.when((ph == 0) & (c == 0))
    def _():
        cnt_ref[...] = jnp.zeros(cnt_ref.shape, F32)

    route = r_buf[...]
    lane = lax.broadcasted_iota(I32, (chunk, LANES), 1)
    row = lax.broadcasted_iota(I32, (chunk, LANES), 0)
    lanef = lane.astype(F32)
    valid = (((row & (t_pad - 1)) < t_valid).astype(I32) | is_prompt.astype(I32)) > 0
    oh0 = jnp.where((lanef == route[:, 0:1]) & valid, 1.0, 0.0)
    oh1 = jnp.where((lanef == route[:, 1:2]) & valid, 1.0, 0.0)
    cmat = oh0 + oh1
    csum = jnp.sum(cmat, axis=0, keepdims=True)

    @pl.when(ph == 0)
    def _():
        cnt_ref[...] = cnt_ref[...] + csum

    @pl.when((ph == 1) & (c == 0))
    def _():
        cnt = jnp.broadcast_to(cnt_ref[...], (SUBLANES, LANES))
        padc = jnp.floor((cnt + (tm - 1)) * (1.0 / tm)) * tm
        pend = _lane_cumsum(padc)
        pstart_ref[...] = (pend - padc)[0:1, :]
        carry_ref[...] = jnp.zeros(carry_ref.shape, F32)
        starts = lax.broadcasted_iota(I32, (SUBLANES, nblk_lanes), 1).astype(F32) * tm
        blk = jnp.zeros((SUBLANES, nblk_lanes), F32)
        for e in range(N_EXPERTS):
            blk = blk + jnp.where(pend[:, e:e + 1] <= starts, 1.0, 0.0)
        blk = jnp.minimum(blk, N_EXPERTS - 1.0)
        nused = jnp.broadcast_to(pend[:, N_EXPERTS - 1:N_EXPERTS] * (1.0 / tm), (SUBLANES, nblk_lanes))
        sub = lax.broadcasted_iota(I32, (SUBLANES, nblk_lanes), 0)
        tab_ref[...] = jnp.where(sub == 0, blk, nused)
        last_ref[...] = jnp.where(cnt > 0.0, pend * (1.0 / tm) - 1.0, -1.0)

    @pl.when(ph == 1)
    def _():
        rt = lax.broadcasted_iota(I32, (chunk, chunk), 0)
        cs = lax.broadcasted_iota(I32, (chunk, chunk), 1)
        tri = jnp.where(rt > cs, 1.0, 0.0).astype(BF16)
        before = jnp.dot(tri, cmat.astype(BF16), preferred_element_type=F32) + carry_ref[...]
        base = before + pstart_ref[...]
        d0 = jnp.sum(base * oh0, axis=1, keepdims=True)
        d1 = jnp.sum(base * oh1, axis=1, keepdims=True)
        srow = (c - n_prompt_chunks) * chunk + row
        tshift = t_pad.bit_length() - 1
        padded_idx = (srow >> tshift) * (t_pad - t_valid) + (srow & (t_pad - 1)) - t_valid
        spare = (rows_cap + EXP_TOPK * padded_idx).astype(F32)
        out = jnp.where(lane == 0, d0, d1)
        out = jnp.where(valid, out, spare + lanef)
        dest_ref[...] = jnp.where(lane < EXP_TOPK, out, 0.0)
        carry_ref[...] = carry_ref[...] + csum


def _route_dest(route_p, route_s, t_pad, t_valid, tm, nblk, chunk=ROW_TILE):
    rp, rs = route_p.shape[0], route_s.shape[0]
    assert rp % chunk == 0 and rs % chunk == 0 and chunk % t_pad == 0
    npc, nsc = rp // chunk, rs // chunk
    rows_cap = nblk * tm
    nblk_lanes = -(-nblk // LANES) * LANES
    dest_f, tab, last = pl.pallas_call(
        functools.partial(_route_dest_kernel, n_prompt_chunks=npc, t_pad=t_pad, t_valid=t_valid, tm=tm,
                          rows_cap=rows_cap, nblk_lanes=nblk_lanes),
        grid=(2, npc + nsc),
        in_specs=[pl.BlockSpec((chunk, LANES), lambda ph, c: (jnp.minimum(c, npc - 1), 0)),
                  pl.BlockSpec((chunk, LANES), lambda ph, c: (jnp.maximum(c - npc, 0), 0))],
        out_specs=[pl.BlockSpec((chunk, LANES), lambda ph, c: (c * ph, 0)),
                   pl.BlockSpec((SUBLANES, nblk_lanes), lambda ph, c: (0, 0)),
                   pl.BlockSpec((SUBLANES, LANES), lambda ph, c: (0, 0))],
        out_shape=[jax.ShapeDtypeStruct((rp + rs, LANES), F32),
                   jax.ShapeDtypeStruct((SUBLANES, nblk_lanes), F32),
                   jax.ShapeDtypeStruct((SUBLANES, LANES), F32)],
        scratch_shapes=[pltpu.VMEM((chunk, LANES), F32), pltpu.VMEM((1, LANES), F32),
                        pltpu.VMEM((1, LANES), F32), pltpu.VMEM((1, LANES), F32)],
        compiler_params=_cparams(("arbitrary", "arbitrary"), 32),
        name="route_dest",
    )(route_p, route_s)
    dest = dest_f[:, :EXP_TOPK].astype(I32).reshape(-1)
    blk_e = tab[0, :nblk].astype(I32)
    nused = tab[1, :1].astype(I32)
    last_blk = last[0, :N_EXPERTS].astype(I32)
    return dest, blk_e, nused, last_blk


def _scatter_rows_kernel(dest_ref, last_ref, nused_ref, hp_ref, hs_ref, xs_ref, sbuf, sem, zsem, *,
                         tm, n_prompt_tiles, n_tiles, nblk):
    i = pl.program_id(0)
    slot = i % 2

    def wait_tile(s):
        for _ in range(EXP_TOPK):
            pltpu.make_async_copy(sbuf.at[s], xs_ref.at[pl.ds(0, tm)], sem.at[s]).wait()

    def clear_block(b):
        return pltpu.make_async_copy(sbuf.at[1], xs_ref.at[pl.ds(b * tm, tm)], zsem)

    @pl.when(i == 0)
    def _():
        sbuf[1] = jnp.zeros(sbuf.shape[1:], sbuf.dtype)
        nused = nused_ref[0]
        for e in range(N_EXPERTS):
            @pl.when(last_ref[e] >= 0)
            def _():
                clear_block(last_ref[e]).start()

        def start_unused(b, carry):
            clear_block(b).start()
            return carry

        def wait_one(b, carry):
            clear_block(0).wait()
            return carry

        lax.fori_loop(nused, nblk, start_unused, 0)
        for e in range(N_EXPERTS):
            @pl.when(last_ref[e] >= 0)
            def _():
                clear_block(0).wait()
        lax.fori_loop(nused, nblk, wait_one, 0)

    @pl.when(i >= 2)
    def _():
        wait_tile(slot)

    @pl.when(i < n_prompt_tiles)
    def _():
        sbuf[slot] = hp_ref[...]

    @pl.when(i >= n_prompt_tiles)
    def _():
        sbuf[slot] = hs_ref[...]

    base = EXP_TOPK * i * tm
    for r in range(tm):
        for k in range(EXP_TOPK):
            pltpu.make_async_copy(sbuf.at[slot, pl.ds(r, 1)],
                                  xs_ref.at[pl.ds(dest_ref[base + EXP_TOPK * r + k], 1)],
                                  sem.at[slot]).start(priority=k % 2)

    @pl.when(i == n_tiles - 1)
    def _():
        wait_tile(slot)
        wait_tile(1 - slot)


def _scatter_rows(dest, last_blk, nused, h_p, h_s, nblk, n_spare, tm=ROW_TILE):
    d = h_p.shape[1]
    npt, nst = h_p.shape[0] // tm, h_s.shape[0] // tm
    assert npt + nst >= 2 and MOE_TILE == tm
    grid_spec = pltpu.PrefetchScalarGridSpec(
        num_scalar_prefetch=3,
        grid=(npt + nst,),
        in_specs=[pl.BlockSpec((tm, d), lambda i, de, lb, nu: (jnp.minimum(i, npt - 1), 0)),
                  pl.BlockSpec((tm, d), lambda i, de, lb, nu: (jnp.maximum(i - npt, 0), 0))],
        out_specs=pl.BlockSpec(memory_space=pl.ANY),
        scratch_shapes=[pltpu.VMEM((2, tm, d), F32), pltpu.SemaphoreType.DMA((2,)), pltpu.SemaphoreType.DMA(())],
    )
    return pl.pallas_call(
        functools.partial(_scatter_rows_kernel, tm=tm, n_prompt_tiles=npt, n_tiles=npt + nst, nblk=nblk),
        grid_spec=grid_spec,
        out_shape=jax.ShapeDtypeStruct((nblk * tm + n_spare, d), F32),
        compiler_params=_cparams(("arbitrary",), 32),
        name="scatter_rows",
    )(dest, last_blk, nused, h_p, h_s)


def _moe_kernel(blk_e_ref, nused_ref, x_ref, w1_hbm, w3_hbm, w2_hbm, o_ref, wf_ref, wb_ref, sem):
    i = pl.program_id(0)
    nused = nused_ref[0]
    e = blk_e_ref[i]
    w_hbm = (w1_hbm, w3_hbm, w2_hbm)

    def start_fetch(expert):
        for k in range(3):
            pltpu.make_async_copy(w_hbm[k].at[expert], wf_ref.at[k], sem.at[k]).start()

    @pl.when(i == 0)
    def _():
        start_fetch(e)

    @pl.when((i < nused) & ((i == 0) | (e != blk_e_ref[jnp.maximum(i - 1, 0)])))
    def _():
        for k in range(3):
            pltpu.make_async_copy(w_hbm[k].at[e], wf_ref.at[k], sem.at[k]).wait()
            wb_ref[k] = wf_ref[k].astype(BF16)
        nxt = lax.while_loop(lambda j: (j < nused) & (blk_e_ref[jnp.minimum(j, nused - 1)] == e),
                             lambda j: j + 1, i + 1)

        @pl.when(nxt < nused)
        def _():
            start_fetch(blk_e_ref[nxt])

    @pl.when(i < nused)
    def _():
        x = x_ref[...].astype(BF16)
        a = jnp.dot(x, wb_ref[0], preferred_element_type=F32)
        b = jnp.dot(x, wb_ref[1], preferred_element_type=F32)
        hmid = (_silu(a) * b).astype(BF16)
        o_ref[...] = jnp.dot(hmid, wb_ref[2], preferred_element_type=F32)

    @pl.when(i >= nused)
    def _():
        o_ref[...] = jnp.zeros(o_ref.shape, o_ref.dtype)


def _moe_experts(xs, blk_e, nused, w1, w3, w2, tm=MOE_TILE):
    nblk = blk_e.shape[0]
    d = xs.shape[1]
    wspec = pl.BlockSpec(memory_space=pl.ANY)
    grid_spec = pltpu.PrefetchScalarGridSpec(
        num_scalar_prefetch=2,
        grid=(nblk,),
        in_specs=[pl.BlockSpec((tm, d), lambda i, be, nu: (jnp.minimum(i, nu[0] - 1), 0)), wspec, wspec, wspec],
        out_specs=pl.BlockSpec((tm, d), lambda i, be, nu: (i, 0)),
        scratch_shapes=[pltpu.VMEM((3, d, d), F32), pltpu.VMEM((3, d, d), BF16), pltpu.SemaphoreType.DMA((3,))],
    )
    return pl.pallas_call(
        _moe_kernel,
        grid_spec=grid_spec,
        out_shape=jax.ShapeDtypeStruct((nblk * tm, d), F32),
        compiler_params=_cparams(("arbitrary",), 40),
        name="moe_experts",
    )(blk_e, nused, xs, w1, w3, w2)


def _combine_kernel(dest_ref, x2_ref, r_ref, g_ref, outs_hbm, y_ref, buf, sem, *, tm):
    i = pl.program_id(0)
    n = pl.num_programs(0)
    n_rows = outs_hbm.shape[0]

    def start_tile(tile, slot):
        base = EXP_TOPK * tile * tm
        for r in range(tm):
            for k in range(EXP_TOPK):
                src = dest_ref[base + EXP_TOPK * r + k]
                src = jnp.where(src >= n_rows, src - n_rows, src)
                pltpu.make_async_copy(outs_hbm.at[pl.ds(src, 1)], buf.at[slot, k, pl.ds(r, 1)],
                                      sem.at[slot]).start(priority=k % 2)

    @pl.when(i == 0)
    def _():
        start_tile(0, 0)

    @pl.when(i + 1 < n)
    def _():
        start_tile(i + 1, (i + 1) % 2)

    slot = i % 2
    for k in range(EXP_TOPK):
        pltpu.make_async_copy(outs_hbm.at[pl.ds(0, tm)], buf.at[slot, k], sem.at[slot]).wait()
    route = r_ref[...]
    y = buf[slot, 0] * route[:, 2:3] + buf[slot, 1] * route[:, 3:4]
    y_ref[...] = _rms(x2_ref[...] + y, g_ref[...])


def _combine(x2, route, g_final, outs, dest, tm=ROW_TILE):
    rows, d = x2.shape
    grid_spec = pltpu.PrefetchScalarGridSpec(
        num_scalar_prefetch=1,
        grid=(rows // tm,),
        in_specs=[pl.BlockSpec((tm, d), lambda i, de: (i, 0)),
                  pl.BlockSpec((tm, LANES), lambda i, de: (i, 0)),
                  pl.BlockSpec((1, d), lambda i, de: (0, 0)),
                  pl.BlockSpec(memory_space=pl.ANY)],
        out_specs=pl.BlockSpec((tm, d), lambda i, de: (i, 0)),
        scratch_shapes=[pltpu.VMEM((2, EXP_TOPK, tm, d), F32), pltpu.SemaphoreType.DMA((2,))],
    )
    return pl.pallas_call(
        functools.partial(_combine_kernel, tm=tm),
        grid_spec=grid_spec,
        out_shape=jax.ShapeDtypeStruct((rows, d), F32),
        compiler_params=_cparams(("arbitrary",), 32),
        name="moe_combine",
    )(dest, x2, route, g_final.reshape(1, d), outs)


def _layer_tail(x, o, att, att_transposed, mem_k, mem_v, batch, t, xq_tile, wts):
    q_dtype = BF16 if xq_tile % (2 * SUBLANES) == 0 else F32
    x1, q = _outproj(x, o, att, wts["w_out"], wts["g_cross"], wts["w_xq"], att_transposed, q_dtype)
    xo = _xattn(q, mem_k, mem_v, batch, t, xq_tile)
    x2, h, route = _xo_router(x1, xo, wts["w_xo"], wts["g_ffn"], wts["w_router"], wts["b_router"])
    return x2, h, route


def _moe_and_final(x2_p, h_p, route_p, x2_s, h_s, route_s, n_sample_tokens, t_pad, t_valid, wts):
    rp, d = h_p.shape
    rs = h_s.shape[0]
    tm = MOE_TILE
    n_assign = (rp + n_sample_tokens) * EXP_TOPK
    nblk = -(-(n_assign + N_EXPERTS * (tm - 1)) // tm)
    dest, blk_e, nused, last_blk = _route_dest(route_p, route_s, t_pad, t_valid, tm, nblk)
    n_spare = EXP_TOPK * (rs // t_pad) * (t_pad - t_valid)
    xs = _scatter_rows(dest, last_blk, nused, h_p, h_s, nblk, n_spare)
    outs = _moe_experts(xs, blk_e, nused, wts["w1"], wts["w3"], wts["w2"])
    y_p = _combine(x2_p, route_p, wts["g_final"], outs, dest[:EXP_TOPK * rp])
    y_s = _combine(x2_s, route_s, wts["g_final"], outs, dest[EXP_TOPK * rp:])
    return y_p, y_s


def kernel(x_prompt, x_sample, state_hgrn, cache_moba_k, cache_moba_v, cache_mem_k, cache_mem_v, page_table,
           mem_prompt, g_mix, w_in, hg_lb, hg_norm, w_out, g_cross, g_mem, w_xq, w_xk, w_xv, w_xo, g_ffn,
           w_grp, b_grp, w_exp, b_exp, w1, w3, w2, g_final):
    depth = g_mix.shape[0]
    assert depth == 1
    bp, tp, d = x_prompt.shape
    bs, ts, _ = x_sample.shape
    mem_len = mem_prompt.shape[1]
    l = 0

    lb = jnp.cumsum(jax.nn.softmax(hg_lb.astype(F32), axis=0), axis=0)[l]
    slopes = jnp.asarray(np.power(2.0, -8.0 * np.arange(1, MB_HEADS + 1) / MB_HEADS).astype(np.float32))
    n_router = N_GROUPS + N_EXPERTS
    w_router = jnp.pad(jnp.concatenate([w_grp[l], w_exp[l]], axis=1), ((0, 0), (0, LANES - n_router)))
    b_router = jnp.pad(jnp.concatenate([b_grp[l], b_exp[l]]), (0, LANES - n_router)).reshape(1, LANES)
    wts = {
        "w_out": w_out[l].astype(BF16), "g_cross": g_cross[l], "w_xq": w_xq[l].astype(BF16),
        "w_xo": w_xo[l].astype(BF16), "g_ffn": g_ffn[l], "w_router": w_router.astype(BF16),
        "b_router": b_router.astype(F32), "w1": w1[l], "w3": w3[l], "w2": w2[l], "g_final": g_final,
    }
    w_in_bf = w_in[l].astype(BF16)
    o_hq, o_mq, o_mk, o_mv = 0, 4 * HG_WIDTH, 4 * HG_WIDTH + MB_WIDTH, 4 * HG_WIDTH + 2 * MB_WIDTH
    o_end = o_mv + MB_WIDTH

    xp = x_prompt.reshape(bp * tp, d)
    w_kv = jnp.concatenate([w_xk[l], w_xv[l]], axis=1).astype(BF16)
    memk_p, memv_p = _rms_proj(mem_prompt.reshape(bp * mem_len, d), g_mem[l], w_kv,
                               [(0, d, [(1.0, False)]), (d, 2 * d, [(1.0, False)])], [F32, F32])
    hp_p, qt_p, kt_p, vt_p = _rms_proj(
        xp, g_mix[l], w_in_bf,
        [(o_hq, o_mq, [(1.0, False)]), (o_mq, o_mk, [(MB_DIM ** -0.5, True)]), (o_mk, o_mv, [(1.0, "batched")]),
         (o_mv, o_end, [(1.0, "batched")])],
        [F32, BF16, F32, F32], seq_len=tp)
    o_p, s_p = _hgrn(hp_p, lb, hg_norm[l], None, bp, tp, tp, 128)
    att_p = _moba_prompt(qt_p, kt_p, vt_p, slopes, bp, tp)
    x2_p, h_p, route_p = _tail_fused(xp, o_p, att_p, memk_p, memv_p, bp, tp, wts)

    tpad = SAMPLE_PAD_T
    xs = jnp.pad(x_sample, ((0, 0), (0, tpad - ts), (0, 0))).reshape(bs * tpad, d)
    hp_s, mq_s, mk_s, mv_s = _rms_proj(
        xs, g_mix[l], w_in_bf,
        [(o_hq, o_mq, [(1.0, False)]), (o_mq, o_mk, [(1.0, False)]), (o_mk, o_mv, [(1.0, False)]),
         (o_mv, o_end, [(1.0, False)])],
        [F32, F32, F32, F32])
    o_s, s_s = _hgrn(hp_s, lb, hg_norm[l], state_hgrn[l], bs, tpad, ts, tpad)
    mk4 = mk_s.reshape(bs, tpad, MB_HEADS, MB_DIM)
    mv4 = mv_s.reshape(bs, tpad, MB_HEADS, MB_DIM)
    n_phys, page = cache_moba_k.shape[1], cache_moba_k.shape[2]
    pool_kt = jnp.transpose(cache_moba_k[l], (0, 2, 3, 1)).reshape(n_phys, MB_WIDTH, page)
    pool_vt = jnp.transpose(cache_moba_v[l], (0, 2, 3, 1)).reshape(n_phys, MB_WIDTH, page)
    att_s = _moba_sample(mq_s, mk_s, mv_s, pool_kt, pool_vt, page_table, ts)

    def mem_rows(c):
        c = c[l].reshape(bs, mem_len, XA_HEADS, XA_DIM // LANES, LANES)
        return jnp.transpose(c, (0, 1, 3, 2, 4)).reshape(bs, mem_len * d // LANES, LANES)

    memk_s, memv_s = mem_rows(cache_mem_k), mem_rows(cache_mem_v)
    x2_s, h_s, route_s = _layer_tail(xs, o_s, att_s, False, memk_s, memv_s, bs, tpad, tpad, wts)
    y_p, y_s = _moe_and_final(x2_p, h_p, route_p, x2_s, h_s, route_s, bs * ts, tpad, ts, wts)

    return (y_p.reshape(bp, tp, d),
            y_s.reshape(bs, tpad, d)[:, :ts],
            s_p.reshape(1, bp, HG_HEADS, HG_DIM, HG_DIM),
            jnp.transpose(kt_p.reshape(1, bp, MB_HEADS, MB_DIM, tp), (0, 1, 4, 2, 3)),
            jnp.transpose(vt_p.reshape(1, bp, MB_HEADS, MB_DIM, tp), (0, 1, 4, 2, 3)),
            memk_p.reshape(1, bp, mem_len, XA_HEADS, XA_DIM),
            memv_p.reshape(1, bp, mem_len, XA_HEADS, XA_DIM),
            s_s.reshape(1, bs, HG_HEADS, HG_DIM, HG_DIM),
            mk4[:, :ts].reshape(1, bs, ts, MB_HEADS, MB_DIM),
            mv4[:, :ts].reshape(1, bs, ts, MB_HEADS, MB_DIM))
```

```python
import functools

import numpy as np
import jax
import jax.numpy as jnp
from jax import lax
from jax.experimental import pallas as pl
from jax.experimental.pallas import tpu as pltpu

F32 = jnp.float32
BF16 = jnp.bfloat16
I32 = jnp.int32

D_MODEL = 1024
HG_HEADS = 4
HG_DIM = 128
HG_WIDTH = HG_HEADS * HG_DIM
MB_HEADS = 8
MB_DIM = 64
MB_WIDTH = MB_HEADS * MB_DIM
MB_BLOCK = 256
MB_TOPK = 3
XA_HEADS = 4
XA_DIM = 256
N_GROUPS = 4
EXP_PER_GROUP = 8
N_EXPERTS = N_GROUPS * EXP_PER_GROUP
EXP_TOPK = 2
RMS_EPS = 1e-6
NEG_INF = float("-inf")

LANES = 128
SUBLANES = 8
MIB = 1024 * 1024

ROW_TILE = 256
MOE_TILE = 256
SAMPLE_PAD_T = 8
MOBA_PAIRS = 4
XATTN_SEQS_PER_STEP = 4

_NT = (((1,), (1,)), ((), ()))


def _cparams(semantics, vmem_mib):
    return pltpu.CompilerParams(dimension_semantics=semantics, vmem_limit_bytes=vmem_mib * MIB)


def _rms(x, g):
    return x * lax.rsqrt(jnp.mean(x * x, axis=-1, keepdims=True) + RMS_EPS) * g


def _sigmoid(x):
    return 0.5 * jnp.tanh(0.5 * x) + 0.5


def _silu(x):
    return x * _sigmoid(x)


def _rms_proj_kernel(x_ref, g_ref, w_ref, *o_refs, segs):
    h = _rms(x_ref[...], g_ref[...]).astype(BF16)
    k = 0
    for lo, hi, outs in segs:
        r = jnp.dot(h, w_ref[:, lo:hi], preferred_element_type=F32)
        for scale, transposed in outs:
            y = r if scale == 1.0 else r * scale
            if transposed:
                y = y.T
            o_refs[k][...] = y.astype(o_refs[k].dtype)
            k += 1


def _rms_proj(x, g, w_bf, segs, dtypes, tm=ROW_TILE, vmem_mib=48, seq_len=None):
    rows, d = x.shape
    assert rows % tm == 0
    out_shape, out_specs = [], []
    k = 0
    for lo, hi, outs in segs:
        n = hi - lo
        for _, transposed in outs:
            if transposed == "batched":
                per_seq = seq_len // tm
                out_shape.append(jax.ShapeDtypeStruct((rows // seq_len, n, seq_len), dtypes[k]))
                out_specs.append(pl.BlockSpec((None, n, tm), lambda i: (i // per_seq, 0, i % per_seq)))
            elif transposed:
                out_shape.append(jax.ShapeDtypeStruct((n, rows), dtypes[k]))
                out_specs.append(pl.BlockSpec((n, tm), lambda i: (0, i)))
            else:
                out_shape.append(jax.ShapeDtypeStruct((rows, n), dtypes[k]))
                out_specs.append(pl.BlockSpec((tm, n), lambda i: (i, 0)))
            k += 1
    return pl.pallas_call(
        functools.partial(_rms_proj_kernel, segs=segs),
        grid=(rows // tm,),
        in_specs=[pl.BlockSpec((tm, d), lambda i: (i, 0)),
                  pl.BlockSpec((1, d), lambda i: (0, 0)),
                  pl.BlockSpec(w_bf.shape, lambda i: (0, 0))],
        out_specs=out_specs,
        out_shape=out_shape,
        compiler_params=_cparams(("parallel",), vmem_mib),
        name="rms_proj",
    )(x, g.reshape(1, d), w_bf)


def _cumsum_rows(x):
    n = x.shape[0]
    row = lax.broadcasted_iota(I32, x.shape, 0)
    s = 1
    while s < n:
        x = x + jnp.where(row >= s, pltpu.roll(x, s, 0), 0.0)
        s *= 2
    return x


def _hgrn_intra(q, kk, b, c, w):
    row = lax.broadcasted_iota(I32, (c, HG_DIM), 0)
    acc = None
    half = c // 2
    while half >= SUBLANES:
        two = 2 * half
        nblk = c // two
        pieces = [jnp.broadcast_to(b[i * two + half - 1:i * two + half, :], (two, HG_DIM)) for i in range(nblk)]
        bm = pieces[0] if nblk == 1 else jnp.concatenate(pieces, axis=0)
        second = (row & (two - 1)) >= half
        qt = jnp.where(second, q * jnp.exp(jnp.where(second, b - bm, 0.0)), 0.0)
        kt = jnp.where(second, 0.0, kk * jnp.exp(jnp.where(second, 0.0, bm - b)))
        al = lax.dot_general(qt.astype(BF16), kt.astype(BF16), _NT, preferred_element_type=F32)
        if nblk > 1:
            shift = two.bit_length() - 1
            rt = lax.broadcasted_iota(I32, (c, c), 0) >> shift
            cs = lax.broadcasted_iota(I32, (c, c), 1) >> shift
            al = jnp.where(rt == cs, al, 0.0)
        acc = al if acc is None else acc + al
        half //= 2
    r8 = lax.broadcasted_iota(I32, (SUBLANES, HG_DIM), 0)
    lane = lax.broadcasted_iota(I32, (SUBLANES, w), 1)
    blocks = []
    for g in range(c // SUBLANES):
        lo = g * SUBLANES
        qg, kg, bg = q[lo:lo + SUBLANES], kk[lo:lo + SUBLANES], b[lo:lo + SUBLANES]
        ag = jnp.zeros((SUBLANES, w), F32)
        for s in range(SUBLANES):
            e = jnp.exp(jnp.where(r8 >= s, bg - bg[s:s + 1, :], NEG_INF))
            p = jnp.sum(qg * kg[s:s + 1, :] * e, axis=1, keepdims=True)
            ag = jnp.where(lane == lo + s, p, ag)
        blocks.append(ag)
    diag = blocks[0] if len(blocks) == 1 else jnp.concatenate(blocks, axis=0)
    return diag if acc is None else acc + diag


def _hgrn_kernel(*refs, c, t_valid, has_s0):
    if has_s0:
        hp_ref, lb_ref, gn_ref, s0_ref, o_ref, sout_ref, st_ref = refs
    else:
        hp_ref, lb_ref, gn_ref, o_ref, sout_ref, st_ref = refs
    ci = pl.program_id(1)
    last = pl.num_programs(1) - 1
    w = max(c, LANES)

    @pl.when(ci == 0)
    def _():
        for h in range(HG_HEADS):
            st_ref[h] = s0_ref[0, h].T if has_s0 else jnp.zeros((HG_DIM, HG_DIM), F32)

    row = lax.broadcasted_iota(I32, (c, HG_DIM), 0)
    for h in range(HG_HEADS):
        lo = h * HG_DIM
        hq = hp_ref[:, lo:lo + HG_DIM]
        hf = hp_ref[:, HG_WIDTH + lo:HG_WIDTH + lo + HG_DIM]
        v = hp_ref[:, 2 * HG_WIDTH + lo:2 * HG_WIDTH + lo + HG_DIM]
        hg = hp_ref[:, 3 * HG_WIDTH + lo:3 * HG_WIDTH + lo + HG_DIM]
        lb = lb_ref[:, lo:lo + HG_DIM]
        q = _silu(hq)
        f = lb + (1.0 - lb) * _sigmoid(hf)
        logf = jnp.log(f)
        kk = 1.0 - f
        if t_valid < c:
            valid = row < t_valid
            logf = jnp.where(valid, logf, 0.0)
            kk = jnp.where(valid, kk, 0.0)
            v = jnp.where(valid, v, 0.0)
        b = _cumsum_rows(logf)
        a = _hgrn_intra(q, kk, b, c, w)
        st = st_ref[h]
        bl = b[c - 1:c, :]
        k2 = kk * jnp.exp(bl - b)
        if c < w:
            zpad = jnp.zeros((w - c, HG_DIM), F32)
            vp = jnp.concatenate([v, zpad], axis=0)
            k2 = jnp.concatenate([k2, zpad], axis=0)
        else:
            vp = v
        vb = vp.astype(BF16)
        o = lax.dot_general((q * jnp.exp(b)).astype(BF16), st.astype(BF16), _NT, preferred_element_type=F32)
        o = o + jnp.dot(a.astype(BF16), vb, preferred_element_type=F32)
        o_ref[:, lo:lo + HG_DIM] = (_rms(o, gn_ref[...]) * _silu(hg)).astype(o_ref.dtype)
        st_new = st * jnp.exp(bl) + jnp.dot(vp.T.astype(BF16), k2.astype(BF16), preferred_element_type=F32)
        st_ref[h] = st_new

        @pl.when(ci == last)
        def _():
            sout_ref[0, h] = st_new.T


def _hgrn(hp, lb, gn, s0, batch, t_pad, t_valid, c):
    rows = hp.shape[0]
    nc = t_pad // c
    has_s0 = s0 is not None
    in_specs = [pl.BlockSpec((c, 4 * HG_WIDTH), lambda b, i: (b * nc + i, 0)),
                pl.BlockSpec((1, HG_WIDTH), lambda b, i: (0, 0)),
                pl.BlockSpec((1, HG_DIM), lambda b, i: (0, 0))]
    args = [hp, lb.reshape(1, HG_WIDTH), gn.reshape(1, HG_DIM)]
    if has_s0:
        in_specs.append(pl.BlockSpec((1, HG_HEADS, HG_DIM, HG_DIM), lambda b, i: (b, 0, 0, 0)))
        args.append(s0)
    return pl.pallas_call(
        functools.partial(_hgrn_kernel, c=c, t_valid=min(t_valid, c), has_s0=has_s0),
        grid=(batch, nc),
        in_specs=in_specs,
        out_specs=[pl.BlockSpec((c, HG_WIDTH), lambda b, i: (b * nc + i, 0)),
                   pl.BlockSpec((1, HG_HEADS, HG_DIM, HG_DIM), lambda b, i: (b, 0, 0, 0))],
        out_shape=[jax.ShapeDtypeStruct((rows, HG_WIDTH), BF16),
                   jax.ShapeDtypeStruct((batch, HG_HEADS, HG_DIM, HG_DIM), F32)],
        scratch_shapes=[pltpu.VMEM((HG_HEADS, HG_DIM, HG_DIM), F32)],
        compiler_params=_cparams(("parallel", "arbitrary"), 32),
        name="hgrn2",
    )(*args)


def _topk_block_mask(gate, n_valid, nrow):
    nb = gate.shape[0]
    cnt = jnp.zeros(gate.shape, I32)
    for n2 in range(nb):
        g2 = gate[n2:n2 + 1, :]
        beats = (g2 > gate) | ((g2 == gate) & (n2 < nrow))
        cnt = cnt + jnp.where(beats, (n_valid > n2).astype(I32), 0)
    return (cnt < MB_TOPK) & (nrow < n_valid)


def _moba_prompt_kernel(slope_ref, qt_ref, kt_ref, vt_ref, o_ref, kb_ref, vb_ref, km_ref, sel_ref, *, nb, npairs):
    hg = pl.program_id(1)
    qi = pl.program_id(2)
    tq = MB_BLOCK
    pw = 2 * tq
    width = npairs * pw
    pair_rows = 2 * MB_DIM

    @pl.when(qi == 0)
    def _():
        pos = jnp.where(lax.broadcasted_iota(I32, (MB_BLOCK, pair_rows), 1) == 0,
                        lax.broadcasted_iota(I32, (MB_BLOCK, pair_rows), 0), 0).astype(BF16)
        for pp in range(npairs):
            means = []
            for j in range(nb):
                kj = kt_ref[pp * pair_rows:(pp + 1) * pair_rows, j * MB_BLOCK:(j + 1) * MB_BLOCK].T
                kb_ref[pp, j * MB_BLOCK:(j + 1) * MB_BLOCK, :pair_rows] = kj.astype(BF16)
                kb_ref[pp, j * MB_BLOCK:(j + 1) * MB_BLOCK, pair_rows:] = pos
                means.append(jnp.sum(kj, axis=0, keepdims=True) * (1.0 / MB_BLOCK))
                vb_ref[pp, j] = vt_ref[pp * pair_rows:(pp + 1) * pair_rows,
                                       j * MB_BLOCK:(j + 1) * MB_BLOCK].astype(BF16)
            km_ref[pp] = jnp.concatenate(means, axis=0)

    second_head = lax.broadcasted_iota(I32, (pair_rows, tq), 0) >= MB_DIM
    first_row = lax.broadcasted_iota(I32, (pair_rows, pw), 0) == 0
    qaugs, gates, slope_parts = [], [], []
    for pp in range(npairs):
        qt2 = qt_ref[pp * pair_rows:(pp + 1) * pair_rows, :]
        zero = jnp.zeros_like(qt2)
        qcat = jnp.concatenate([jnp.where(second_head, zero, qt2), jnp.where(second_head, qt2, zero)], axis=1)
        gates.append(jnp.dot(km_ref[pp].astype(BF16), qcat, preferred_element_type=F32))
        pair_slopes = jnp.concatenate(
            [jnp.full((1, tq), slope_ref[(hg * npairs + pp) * 2 + i], F32) for i in range(2)], axis=1)
        slope_parts.append(pair_slopes)
        srows = jnp.where(first_row, pair_slopes, 0.0).astype(BF16)
        qaugs.append(jnp.concatenate([qcat, srows], axis=0))
    slope_row = slope_parts[0] if npairs == 1 else jnp.concatenate(slope_parts, axis=1)
    gate = gates[0] if npairs == 1 else jnp.concatenate(gates, axis=1)
    nrow = lax.broadcasted_iota(I32, (nb, width), 0)
    sel = _topk_block_mask(gate, qi, nrow).astype(F32)
    for n in range(nb):
        sel_ref[n] = jnp.broadcast_to(sel[n:n + 1, :], (SUBLANES, width))

    def scores(j):
        start = pl.multiple_of(j * MB_BLOCK, MB_BLOCK)
        parts = [jnp.dot(kb_ref[pp, pl.ds(start, MB_BLOCK), :], qaugs[pp], preferred_element_type=F32)
                 for pp in range(npairs)]
        return parts[0] if npairs == 1 else jnp.concatenate(parts, axis=1)

    def values(j, p):
        pb = p.astype(BF16)
        return [jnp.dot(vb_ref[pp, j], pb[:, pp * pw:(pp + 1) * pw], preferred_element_type=F32)
                for pp in range(npairs)]

    lane = lax.broadcasted_iota(I32, (MB_BLOCK, width), 1) & (tq - 1)
    rowk = lax.broadcasted_iota(I32, (MB_BLOCK, width), 0)
    s = jnp.where(rowk <= lane, scores(qi), NEG_INF)
    m = jnp.max(s, axis=0, keepdims=True)
    p = jnp.exp(s - m)
    l = jnp.sum(p, axis=0, keepdims=True)
    accs = values(qi, p)

    def body(j, carry):
        m, l, accs = carry
        off = jnp.full((1, width), (qi - j) * MB_BLOCK, I32).astype(F32) * slope_row
        picked = sel_ref[j][0:1, :] > 0.5
        s = scores(j)
        m_new = jnp.maximum(m, jnp.where(picked, jnp.max(s, axis=0, keepdims=True) - off, NEG_INF))
        alpha = jnp.exp(m - m_new)
        p = jnp.exp(s - jnp.where(picked, m_new + off, float("inf")))
        l = alpha * l + jnp.sum(p, axis=0, keepdims=True)
        pv = values(j, p)
        accs = [alpha[:, pp * pw:(pp + 1) * pw] * accs[pp] + pv[pp] for pp in range(npairs)]
        return m_new, l, accs

    m, l, accs = lax.fori_loop(0, qi, body, (m, l, accs))
    for pp in range(npairs):
        a = accs[pp] / l[:, pp * pw:(pp + 1) * pw]
        o_ref[pp * pair_rows:(pp + 1) * pair_rows, :] = jnp.where(second_head, a[:, tq:], a[:, :tq])


def _moba_prompt(qt, kt, vt, slopes, batch, t, npairs=MOBA_PAIRS):
    nb = t // MB_BLOCK
    rows = batch * t
    gr = 2 * MB_DIM * npairs
    width = npairs * 2 * MB_BLOCK
    return pl.pallas_call(
        functools.partial(_moba_prompt_kernel, nb=nb, npairs=npairs),
        grid=(batch, MB_WIDTH // gr, nb),
        in_specs=[pl.BlockSpec(memory_space=pltpu.SMEM),
                  pl.BlockSpec((gr, MB_BLOCK), lambda b, h, i: (h, b * nb + i)),
                  pl.BlockSpec((None, gr, t), lambda b, h, i: (b, h, 0)),
                  pl.BlockSpec((None, gr, t), lambda b, h, i: (b, h, 0))],
        out_specs=pl.BlockSpec((gr, MB_BLOCK), lambda b, h, i: (h, b * nb + i)),
        out_shape=jax.ShapeDtypeStruct((MB_WIDTH, rows), F32),
        scratch_shapes=[pltpu.VMEM((npairs, t, 4 * MB_DIM), BF16),
                        pltpu.VMEM((npairs, nb, 2 * MB_DIM, MB_BLOCK), BF16),
                        pltpu.VMEM((npairs, nb, 2 * MB_DIM), F32),
                        pltpu.VMEM((nb, SUBLANES, width), F32)],
        compiler_params=_cparams(("parallel", "parallel", "arbitrary"), 48),
        name="moba_prompt",
    )(slopes, qt, kt, vt)


def _moba_sample_kernel(pt_ref, slope_ref, hm_ref, q_ref, kn_ref, vn_ref, *rest, npages, page, t_new, past_len):
    del pt_ref
    k_pages = rest[:npages]
    v_pages = rest[npages:2 * npages]
    o_ref = rest[2 * npages]
    nrows = t_new * MB_HEADS
    pages_per_block = MB_BLOCK // page
    nb = npages // pages_per_block
    hm = hm_ref[...]
    slope = slope_ref[...][:, 0:1]
    row_q = lax.broadcasted_iota(I32, (nrows, 1), 0) >> 3
    pos_q = past_len + row_q

    q = q_ref[...] * (MB_DIM ** -0.5)
    qbd = jnp.concatenate([jnp.broadcast_to(q[t:t + 1, :], (MB_HEADS, MB_WIDTH)) * hm for t in range(t_new)],
                          axis=0).astype(BF16)

    lane = lax.broadcasted_iota(I32, (nrows, page), 1)
    scores = []
    gates = [None] * nb
    for p in range(npages):
        s = jnp.dot(qbd, k_pages[p][0].astype(BF16), preferred_element_type=F32)
        rs = jnp.sum(s, axis=1, keepdims=True)
        n = p // pages_per_block
        gates[n] = rs if gates[n] is None else gates[n] + rs
        scores.append(s - slope * (pos_q - (p * page + lane)).astype(F32))
    sels = []
    for n in range(nb):
        cnt = jnp.zeros((nrows, 1), I32)
        for n2 in range(nb):
            beats = (gates[n2] > gates[n]) | ((gates[n2] == gates[n]) & (n2 < n))
            cnt = cnt + jnp.where(beats, 1, 0)
        sels.append(jnp.where(cnt < MB_TOPK, 1.0, 0.0))

    zrows = jnp.zeros((LANES - SAMPLE_PAD_T, MB_WIDTH), F32)
    knp = jnp.concatenate([kn_ref[...], zrows], axis=0).astype(BF16)
    vnp = jnp.concatenate([vn_ref[...], zrows], axis=0).astype(BF16)
    lane_n = lax.broadcasted_iota(I32, (nrows, LANES), 1)
    s_new = lax.dot_general(qbd, knp, _NT, preferred_element_type=F32)
    s_new = jnp.where(lane_n <= row_q, s_new - slope * (row_q - lane_n).astype(F32), NEG_INF)
    m = jnp.max(s_new, axis=1, keepdims=True)
    for p in range(npages):
        sel_p = jnp.broadcast_to(sels[p // pages_per_block], (nrows, page)) > 0.5
        scores[p] = jnp.where(sel_p, scores[p], NEG_INF)
        m = jnp.maximum(m, jnp.max(scores[p], axis=1, keepdims=True))

    p_new = jnp.exp(s_new - m)
    l = jnp.sum(p_new, axis=1, keepdims=True)
    acc = jnp.dot(p_new.astype(BF16), vnp, preferred_element_type=F32)
    for p in range(npages):
        pr = jnp.exp(scores[p] - m)
        l = l + jnp.sum(pr, axis=1, keepdims=True)
        acc = acc + lax.dot_general(pr.astype(BF16), v_pages[p][0].astype(BF16), _NT,
                                    preferred_element_type=F32)
    acc = acc / l
    rows = [jnp.sum(acc[t * MB_HEADS:(t + 1) * MB_HEADS, :] * hm, axis=0, keepdims=True) for t in range(t_new)]
    rows.append(jnp.zeros((SAMPLE_PAD_T - t_new, MB_WIDTH), F32))
    o_ref[...] = jnp.concatenate(rows, axis=0)


def _moba_sample(mq, mk, mv, pool_kt, pool_vt, page_table, t_new):
    batch, npages = page_table.shape
    page = pool_kt.shape[2]
    past_len = npages * page
    assert past_len % MB_BLOCK == 0 and MB_BLOCK % page == 0 and t_new <= SAMPLE_PAD_T and page == LANES
    nrows = t_new * MB_HEADS
    slopes = np.power(2.0, -8.0 * np.arange(1, MB_HEADS + 1) / MB_HEADS).astype(np.float32)
    slope_rows = jnp.asarray(np.tile(np.tile(slopes, t_new)[:, None], (1, LANES)))
    head_mask = jnp.asarray((np.arange(MB_WIDTH)[None, :] // MB_DIM == np.arange(MB_HEADS)[:, None])
                            .astype(np.float32))
    new_spec = pl.BlockSpec((SAMPLE_PAD_T, MB_WIDTH), lambda b, pt: (b, 0))
    page_specs = [pl.BlockSpec((1, MB_WIDTH, page),
                               functools.partial(lambda b, pt, i: (pt[b * npages + i], 0, 0), i=i))
                  for i in range(npages)]
    grid_spec = pltpu.PrefetchScalarGridSpec(
        num_scalar_prefetch=1,
        grid=(batch,),
        in_specs=[pl.BlockSpec((nrows, LANES), lambda b, pt: (0, 0)),
                  pl.BlockSpec((MB_HEADS, MB_WIDTH), lambda b, pt: (0, 0)),
                  new_spec, new_spec, new_spec] + page_specs + page_specs,
        out_specs=new_spec,
    )
    return pl.pallas_call(
        functools.partial(_moba_sample_kernel, npages=npages, page=page, t_new=t_new, past_len=past_len),
        grid_spec=grid_spec,
        out_shape=jax.ShapeDtypeStruct((batch * SAMPLE_PAD_T, MB_WIDTH), F32),
        compiler_params=_cparams(("parallel",), 40),
        name="moba_sample",
    )(page_table.reshape(-1), slope_rows, head_mask, mq, mk, mv, *([pool_kt] * npages), *([pool_vt] * npages))


def _outproj_math(x, o, att, wo_ref, g, wq_ref):
    mix = jnp.dot(o, wo_ref[:HG_WIDTH, :], preferred_element_type=F32)
    mix = mix + jnp.dot(att.astype(BF16), wo_ref[HG_WIDTH:, :], preferred_element_type=F32)
    x1 = x + mix
    h = _rms(x1, g).astype(BF16)
    return x1, jnp.dot(h, wq_ref[...], preferred_element_type=F32) * (XA_DIM ** -0.5)


def _outproj_kernel(x_ref, o_ref, att_ref, wo_ref, g_ref, wq_ref, x1_ref, q_ref, *, att_transposed):
    att = att_ref[...]
    if att_transposed:
        att = att.T
    x1, q = _outproj_math(x_ref[...], o_ref[...], att, wo_ref, g_ref[...], wq_ref)
    x1_ref[...] = x1
    q_ref[...] = q.astype(q_ref.dtype)


def _outproj(x, o, att, wo_bf, g, wq_bf, att_transposed, q_dtype, tm=ROW_TILE):
    rows, d = x.shape
    att_spec = (pl.BlockSpec((MB_WIDTH, tm), lambda i: (0, i)) if att_transposed
                else pl.BlockSpec((tm, MB_WIDTH), lambda i: (i, 0)))
    return pl.pallas_call(
        functools.partial(_outproj_kernel, att_transposed=att_transposed),
        grid=(rows // tm,),
        in_specs=[pl.BlockSpec((tm, d), lambda i: (i, 0)),
                  pl.BlockSpec((tm, HG_WIDTH), lambda i: (i, 0)),
                  att_spec,
                  pl.BlockSpec((d, d), lambda i: (0, 0)),
                  pl.BlockSpec((1, d), lambda i: (0, 0)),
                  pl.BlockSpec((d, d), lambda i: (0, 0))],
        out_specs=[pl.BlockSpec((tm, d), lambda i: (i, 0)), pl.BlockSpec((tm, d), lambda i: (i, 0))],
        out_shape=[jax.ShapeDtypeStruct((rows, d), F32), jax.ShapeDtypeStruct((rows, d), q_dtype)],
        compiler_params=_cparams(("parallel",), 40),
        name="outproj_q",
    )(x, o, att, wo_bf, g.reshape(1, d), wq_bf)


def _xattn_tiled_kernel(q_ref, k_ref, v_ref, o_ref, *, nt):
    for g in range(k_ref.shape[0]):
        o_ref[g * nt:(g + 1) * nt, :] = _xattn_tiled_one(
            q_ref[g * nt:(g + 1) * nt, :], k_ref[g], v_ref[g], nt).astype(o_ref.dtype)


def _xattn_tiled_one(q, k, v, nt):
    ndt = XA_DIM // LANES
    nc = XA_HEADS * ndt
    half = XA_HEADS * nt
    q = q.astype(F32)
    a = jnp.concatenate([q[:, (h * ndt + dt) * LANES:(h * ndt + dt + 1) * LANES]
                         for dt in range(ndt) for h in range(XA_HEADS)], axis=0).astype(BF16)
    s = lax.dot_general(a, k.astype(BF16), _NT, preferred_element_type=F32)
    width = s.shape[1]
    lane_c = lax.broadcasted_iota(I32, s.shape, 1) & (nc - 1)
    assert nt & (nt - 1) == 0 and nc & (nc - 1) == 0
    row_c = lax.broadcasted_iota(I32, s.shape, 0) >> (nt.bit_length() - 1)
    s = jnp.where(lane_c == row_c, s, 0.0)
    tot = s[:half]
    for dt in range(1, ndt):
        tot = tot + pltpu.roll(s[dt * half:(dt + 1) * half], width - dt * XA_HEADS, 1)
    ok = ((lax.broadcasted_iota(I32, tot.shape, 1) & (nc - 1))
          == (lax.broadcasted_iota(I32, tot.shape, 0) >> (nt.bit_length() - 1)))
    tot = jnp.where(ok, tot, NEG_INF)
    p = jnp.exp(tot - jnp.max(tot, axis=-1, keepdims=True))
    p = p / jnp.sum(p, axis=-1, keepdims=True)
    pe = jnp.concatenate([p] + [pltpu.roll(p, dt * XA_HEADS, 1) for dt in range(1, ndt)], axis=0)
    o = jnp.dot(pe.astype(BF16), v.astype(BF16), preferred_element_type=F32)
    return jnp.concatenate([o[(dt * XA_HEADS + h) * nt:(dt * XA_HEADS + h + 1) * nt, :]
                            for h in range(XA_HEADS) for dt in range(ndt)], axis=1)


def _xattn_math(q, k_ref, v_ref):
    outs = []
    for h in range(XA_HEADS):
        lo = h * XA_DIM
        kh = k_ref[:, lo:lo + XA_DIM].astype(BF16)
        vh = v_ref[:, lo:lo + XA_DIM].astype(BF16)
        s = lax.dot_general(q[:, lo:lo + XA_DIM], kh, _NT, preferred_element_type=F32)
        p = jnp.exp(s - jnp.max(s, axis=-1, keepdims=True))
        p = p / jnp.sum(p, axis=-1, keepdims=True)
        outs.append(jnp.dot(p.astype(BF16), vh, preferred_element_type=F32))
    return outs


def _xattn_kernel(q_ref, k_ref, v_ref, o_ref):
    outs = _xattn_math(q_ref[...].astype(BF16), k_ref, v_ref)
    for h in range(XA_HEADS):
        o_ref[:, h * XA_DIM:(h + 1) * XA_DIM] = outs[h].astype(o_ref.dtype)


def _xattn(q, mem_k, mem_v, batch, t, tq):
    rows, d = q.shape
    nq = t // tq
    if mem_k.ndim == 3:
        assert tq == t and batch % XATTN_SEQS_PER_STEP == 0
        g = XATTN_SEQS_PER_STEP
        kv_spec = pl.BlockSpec((g,) + mem_k.shape[1:], lambda b: (b, 0, 0))
        return pl.pallas_call(
            functools.partial(_xattn_tiled_kernel, nt=t),
            grid=(batch // g,),
            in_specs=[pl.BlockSpec((g * t, d), lambda b: (b, 0)), kv_spec, kv_spec],
            out_specs=pl.BlockSpec((g * t, d), lambda b: (b, 0)),
            out_shape=jax.ShapeDtypeStruct((rows, d), q.dtype),
            compiler_params=_cparams(("parallel",), 40),
            name="xattn_tiled",
        )(q, mem_k, mem_v)
    m = mem_k.shape[0] // batch
    kv_spec = pl.BlockSpec((m, d), lambda b, i: (b, 0))
    return pl.pallas_call(
        _xattn_kernel,
        grid=(batch, nq),
        in_specs=[pl.BlockSpec((tq, d), lambda b, i: (b * nq + i, 0)), kv_spec, kv_spec],
        out_specs=pl.BlockSpec((tq, d), lambda b, i: (b * nq + i, 0)),
        out_shape=jax.ShapeDtypeStruct((rows, d), q.dtype),
        compiler_params=_cparams(("parallel", "arbitrary"), 32),
        name="xattn",
    )(q, mem_k, mem_v)


def _xo_router_kernel(x1_ref, o_ref, wo_ref, g_ref, wr_ref, br_ref, x2_ref, h_ref, r_ref):
    x2 = x1_ref[...] + jnp.dot(o_ref[...].astype(BF16), wo_ref[...], preferred_element_type=F32)
    x2_ref[...] = x2
    h = _rms(x2, g_ref[...])
    h_ref[...] = h
    r_ref[...] = _router_math(h, wr_ref, br_ref)


def _router_math(h, wr_ref, br_ref):
    logits = jnp.dot(h.astype(BF16), wr_ref[...], preferred_element_type=F32) + br_ref[...]
    lane = lax.broadcasted_iota(I32, logits.shape, 1)
    big = jnp.int32(LANES)

    def top1(mask):
        mx = jnp.max(jnp.where(mask, logits, NEG_INF), axis=-1, keepdims=True)
        idx = jnp.min(jnp.where(mask & (logits == mx), lane, big), axis=-1, keepdims=True)
        return mx, idx

    gmask = lane < N_GROUPS
    gmx, gsel = top1(gmask)
    gw = 1.0 / jnp.sum(jnp.where(gmask, jnp.exp(logits - gmx), 0.0), axis=-1, keepdims=True)
    elo = N_GROUPS + gsel * EXP_PER_GROUP
    emask = (lane >= elo) & (lane < elo + EXP_PER_GROUP)
    m1, i1 = top1(emask)
    m2, i2 = top1(emask & (lane != i1))
    e2 = jnp.exp(m2 - m1)
    g1 = gw / (1.0 + e2)
    g2 = gw * e2 / (1.0 + e2)
    out = jnp.where(lane == 0, (i1 - N_GROUPS).astype(F32), 0.0)
    out = jnp.where(lane == 1, (i2 - N_GROUPS).astype(F32), out)
    out = jnp.where(lane == 2, g1, out)
    return jnp.where(lane == 3, g2, out)


def _tail_fused_kernel(x_ref, o_ref, att_ref, wo_ref, gc_ref, wq_ref, k_ref, v_ref, wxo_ref, gf_ref, wr_ref,
                       br_ref, x2_ref, h_ref, r_ref, *, n_tiles):
    i = pl.program_id(0)

    @pl.when(i < n_tiles)
    def _():
        x1, q = _outproj_math(x_ref[...], o_ref[...], att_ref[...].T, wo_ref, gc_ref[...], wq_ref)
        outs = _xattn_math(q.astype(BF16), k_ref, v_ref)
        x2 = x1
        for h in range(XA_HEADS):
            x2 = x2 + jnp.dot(outs[h].astype(BF16), wxo_ref[h * XA_DIM:(h + 1) * XA_DIM, :],
                              preferred_element_type=F32)
        x2_ref[...] = x2
        hn = _rms(x2, gf_ref[...])
        h_ref[...] = hn
        r_ref[...] = _router_math(hn, wr_ref, br_ref)

    @pl.when(i >= n_tiles)
    def _():
        h_ref[...] = jnp.zeros(h_ref.shape, h_ref.dtype)


def _tail_fused(x, o, att_t, mem_k, mem_v, batch, t, wts, h_rows_total, tm=512):
    rows, d = x.shape
    nq = t // tm
    n_tiles = rows // tm
    n_extra = (h_rows_total - rows) // tm
    assert (h_rows_total - rows) % tm == 0
    m = mem_k.shape[0] // batch

    def tile(i):
        return jnp.minimum(i, n_tiles - 1)

    row_spec = pl.BlockSpec((tm, d), lambda i: (tile(i), 0))
    w_spec = pl.BlockSpec((d, d), lambda i: (0, 0))
    g_spec = pl.BlockSpec((1, d), lambda i: (0, 0))
    kv_spec = pl.BlockSpec((m, d), lambda i: (tile(i) // nq, 0))
    return pl.pallas_call(
        functools.partial(_tail_fused_kernel, n_tiles=n_tiles),
        grid=(n_tiles + n_extra,),
        in_specs=[row_spec,
                  pl.BlockSpec((tm, HG_WIDTH), lambda i: (tile(i), 0)),
                  pl.BlockSpec((MB_WIDTH, tm), lambda i: (0, tile(i))),
                  w_spec, g_spec, w_spec, kv_spec, kv_spec, w_spec, g_spec,
                  pl.BlockSpec((d, LANES), lambda i: (0, 0)),
                  pl.BlockSpec((1, LANES), lambda i: (0, 0))],
        out_specs=[row_spec, pl.BlockSpec((tm, d), lambda i: (i, 0)),
                   pl.BlockSpec((tm, LANES), lambda i: (tile(i), 0))],
        out_shape=[jax.ShapeDtypeStruct((rows, d), F32),
                   jax.ShapeDtypeStruct((h_rows_total, d), F32),
                   jax.ShapeDtypeStruct((rows, LANES), F32)],
        compiler_params=_cparams(("arbitrary",), 56),
        name="tail_fused",
    )(x, o, att_t, wts["w_out"], wts["g_cross"].reshape(1, d), wts["w_xq"], mem_k, mem_v, wts["w_xo"],
      wts["g_ffn"].reshape(1, d), wts["w_router"], wts["b_router"])


def _xo_router_into_kernel(x1_ref, o_ref, wo_ref, g_ref, wr_ref, br_ref, h_all_ref, x2_ref, h_ref, r_ref):
    del h_all_ref
    _xo_router_kernel(x1_ref, o_ref, wo_ref, g_ref, wr_ref, br_ref, x2_ref, h_ref, r_ref)


def _xo_router(x1, o, wxo_bf, g, wr_bf, br, h_all, h_first_row, tm=ROW_TILE):
    rows, d = x1.shape
    first_tile = h_first_row // tm
    assert h_first_row % tm == 0
    return pl.pallas_call(
        _xo_router_into_kernel,
        grid=(rows // tm,),
        in_specs=[pl.BlockSpec((tm, d), lambda i: (i, 0)),
                  pl.BlockSpec((tm, d), lambda i: (i, 0)),
                  pl.BlockSpec((d, d), lambda i: (0, 0)),
                  pl.BlockSpec((1, d), lambda i: (0, 0)),
                  pl.BlockSpec((d, LANES), lambda i: (0, 0)),
                  pl.BlockSpec((1, LANES), lambda i: (0, 0)),
                  pl.BlockSpec(memory_space=pl.ANY)],
        out_specs=[pl.BlockSpec((tm, d), lambda i: (i, 0)),
                   pl.BlockSpec((tm, d), lambda i: (i + first_tile, 0)),
                   pl.BlockSpec((tm, LANES), lambda i: (i, 0))],
        out_shape=[jax.ShapeDtypeStruct((rows, d), F32),
                   jax.ShapeDtypeStruct(h_all.shape, F32),
                   jax.ShapeDtypeStruct((rows, LANES), F32)],
        input_output_aliases={6: 1},
        compiler_params=_cparams(("parallel",), 40),
        name="xo_router",
    )(x1, o, wxo_bf, g.reshape(1, d), wr_bf, br, h_all)


def _lane_cumsum(x):
    lane = lax.broadcasted_iota(I32, x.shape, 1)
    s = 1
    while s < LANES:
        x = x + jnp.where(lane >= s, pltpu.roll(x, s, 1), 0.0)
        s *= 2
    return x


def _route_dest_kernel(rp_ref, rs_ref, dest_ref, tab_ref, last_ref, r_buf, cnt_ref, carry_ref, pstart_ref, *,
                       n_prompt_chunks, t_pad, t_valid, tm, rows_cap, nblk_lanes):
    ph = pl.program_id(0)
    c = pl.program_id(1)
    chunk = r_buf.shape[0]
    is_prompt = c < n_prompt_chunks

    @pl.when(is_prompt)
    def _():
        r_buf[...] = rp_ref[...]

    @pl.when(jnp.logical_not(is_prompt))
    def _():
        r_buf[...] = rs_ref[...]

    @pl.when((ph == 0) & (c == 0))
    def _():
        cnt_ref[...] = jnp.zeros(cnt_ref.shape, F32)

    route = r_buf[...]
    lane = lax.broadcasted_iota(I32, (chunk, LANES), 1)
    row = lax.broadcasted_iota(I32, (chunk, LANES), 0)
    lanef = lane.astype(F32)
    valid = (((row & (t_pad - 1)) < t_valid).astype(I32) | is_prompt.astype(I32)) > 0
    oh0 = jnp.where((lanef == route[:, 0:1]) & valid, 1.0, 0.0)
    oh1 = jnp.where((lanef == route[:, 1:2]) & valid, 1.0, 0.0)
    cmat = oh0 + oh1
    csum = jnp.sum(cmat, axis=0, keepdims=True)

    @pl.when(ph == 0)
    def _():
        cnt_ref[...] = cnt_ref[...] + csum

    @pl.when((ph == 1) & (c == 0))
    def _():
        cnt = jnp.broadcast_to(cnt_ref[...], (SUBLANES, LANES))
        padc = jnp.floor((cnt + (tm - 1)) * (1.0 / tm)) * tm
        pend = _lane_cumsum(padc)
        pstart_ref[...] = (pend - padc)[0:1, :]
        carry_ref[...] = jnp.zeros(carry_ref.shape, F32)
        starts = lax.broadcasted_iota(I32, (SUBLANES, nblk_lanes), 1).astype(F32) * tm
        blk = jnp.zeros((SUBLANES, nblk_lanes), F32)
        for e in range(N_EXPERTS):
            blk = blk + jnp.where(pend[:, e:e + 1] <= starts, 1.0, 0.0)
        blk = jnp.minimum(blk, N_EXPERTS - 1.0)
        nused = jnp.broadcast_to(pend[:, N_EXPERTS - 1:N_EXPERTS] * (1.0 / tm), (SUBLANES, nblk_lanes))
        sub = lax.broadcasted_iota(I32, (SUBLANES, nblk_lanes), 0)
        tab_ref[...] = jnp.where(sub == 0, blk, nused)
        last_blk = jnp.where(cnt > 0.0, pend * (1.0 / tm) - 1.0, -1.0)
        sub_e = lax.broadcasted_iota(I32, (SUBLANES, LANES), 0)
        last_ref[...] = jnp.where(sub_e == 0, last_blk, jnp.where(sub_e == 1, pend - padc + cnt, pend))

    @pl.when(ph == 1)
    def _():
        rt = lax.broadcasted_iota(I32, (chunk, chunk), 0)
        cs = lax.broadcasted_iota(I32, (chunk, chunk), 1)
        tri = jnp.where(rt > cs, 1.0, 0.0).astype(BF16)
        before = jnp.dot(tri, cmat.astype(BF16), preferred_element_type=F32) + carry_ref[...]
        base = before + pstart_ref[...]
        d0 = jnp.sum(base * oh0, axis=1, keepdims=True)
        d1 = jnp.sum(base * oh1, axis=1, keepdims=True)
        srow = (c - n_prompt_chunks) * chunk + row
        tshift = t_pad.bit_length() - 1
        padded_idx = (srow >> tshift) * (t_pad - t_valid) + (srow & (t_pad - 1)) - t_valid
        spare = (rows_cap + EXP_TOPK * padded_idx).astype(F32)
        out = jnp.where(lane == 0, d0, d1)
        out = jnp.where(valid, out, spare + lanef)
        dest_ref[...] = jnp.where(lane < EXP_TOPK, out, 0.0)
        carry_ref[...] = carry_ref[...] + csum


def _route_dest(route_p, route_s, t_pad, t_valid, tm, nblk, chunk=ROW_TILE):
    rp, rs = route_p.shape[0], route_s.shape[0]
    assert rp % chunk == 0 and rs % chunk == 0 and chunk % t_pad == 0
    npc, nsc = rp // chunk, rs // chunk
    rows_cap = nblk * tm
    nblk_lanes = -(-nblk // LANES) * LANES
    dest_f, tab, last = pl.pallas_call(
        functools.partial(_route_dest_kernel, n_prompt_chunks=npc, t_pad=t_pad, t_valid=t_valid, tm=tm,
                          rows_cap=rows_cap, nblk_lanes=nblk_lanes),
        grid=(2, npc + nsc),
        in_specs=[pl.BlockSpec((chunk, LANES), lambda ph, c: (jnp.minimum(c, npc - 1), 0)),
                  pl.BlockSpec((chunk, LANES), lambda ph, c: (jnp.maximum(c - npc, 0), 0))],
        out_specs=[pl.BlockSpec((chunk, LANES), lambda ph, c: (c * ph, 0)),
                   pl.BlockSpec((SUBLANES, nblk_lanes), lambda ph, c: (0, 0)),
                   pl.BlockSpec((SUBLANES, LANES), lambda ph, c: (0, 0))],
        out_shape=[jax.ShapeDtypeStruct((rp + rs, LANES), F32),
                   jax.ShapeDtypeStruct((SUBLANES, nblk_lanes), F32),
                   jax.ShapeDtypeStruct((SUBLANES, LANES), F32)],
        scratch_shapes=[pltpu.VMEM((chunk, LANES), F32), pltpu.VMEM((1, LANES), F32),
                        pltpu.VMEM((1, LANES), F32), pltpu.VMEM((1, LANES), F32)],
        compiler_params=_cparams(("arbitrary", "arbitrary"), 32),
        name="route_dest",
    )(route_p, route_s)
    dest = dest_f[:, :EXP_TOPK].astype(I32).reshape(-1)
    blk_e = tab[0, :nblk].astype(I32)
    nused = tab[1, :1].astype(I32)
    last_blk = last[0, :N_EXPERTS].astype(I32)
    pad_rows = last[1:3, :N_EXPERTS].astype(I32).reshape(-1)
    return dest, blk_e, nused, last_blk, pad_rows


def _scatter_rows_kernel(dest_ref, last_ref, nused_ref, hp_ref, hs_ref, xs_ref, sbuf, sem, zsem, *,
                         tm, n_prompt_tiles, n_tiles, nblk):
    i = pl.program_id(0)
    slot = i % 2

    def wait_tile(s):
        for _ in range(EXP_TOPK):
            pltpu.make_async_copy(sbuf.at[s], xs_ref.at[pl.ds(0, tm)], sem.at[s]).wait()

    def clear_block(b):
        return pltpu.make_async_copy(sbuf.at[1], xs_ref.at[pl.ds(b * tm, tm)], zsem)

    @pl.when(i == 0)
    def _():
        sbuf[1] = jnp.zeros(sbuf.shape[1:], sbuf.dtype)
        nused = nused_ref[0]
        for e in range(N_EXPERTS):
            @pl.when(last_ref[e] >= 0)
            def _():
                clear_block(last_ref[e]).start()

        def start_unused(b, carry):
            clear_block(b).start()
            return carry

        def wait_one(b, carry):
            clear_block(0).wait()
            return carry

        lax.fori_loop(nused, nblk, start_unused, 0)
        for e in range(N_EXPERTS):
            @pl.when(last_ref[e] >= 0)
            def _():
                clear_block(0).wait()
        lax.fori_loop(nused, nblk, wait_one, 0)

    @pl.when(i >= 2)
    def _():
        wait_tile(slot)

    @pl.when(i < n_prompt_tiles)
    def _():
        sbuf[slot] = hp_ref[...]

    @pl.when(i >= n_prompt_tiles)
    def _():
        sbuf[slot] = hs_ref[...]

    base = EXP_TOPK * i * tm
    for r in range(tm):
        for k in range(EXP_TOPK):
            pltpu.make_async_copy(sbuf.at[slot, pl.ds(r, 1)],
                                  xs_ref.at[pl.ds(dest_ref[base + EXP_TOPK * r + k], 1)],
                                  sem.at[slot]).start(priority=k % 2)

    @pl.when(i == n_tiles - 1)
    def _():
        wait_tile(slot)
        wait_tile(1 - slot)


def _scatter_rows(dest, last_blk, nused, h_p, h_s, nblk, n_spare, tm=ROW_TILE):
    d = h_p.shape[1]
    npt, nst = h_p.shape[0] // tm, h_s.shape[0] // tm
    assert npt + nst >= 2 and MOE_TILE == tm
    grid_spec = pltpu.PrefetchScalarGridSpec(
        num_scalar_prefetch=3,
        grid=(npt + nst,),
        in_specs=[pl.BlockSpec((tm, d), lambda i, de, lb, nu: (jnp.minimum(i, npt - 1), 0)),
                  pl.BlockSpec((tm, d), lambda i, de, lb, nu: (jnp.maximum(i - npt, 0), 0))],
        out_specs=pl.BlockSpec(memory_space=pl.ANY),
        scratch_shapes=[pltpu.VMEM((2, tm, d), F32), pltpu.SemaphoreType.DMA((2,)), pltpu.SemaphoreType.DMA(())],
    )
    return pl.pallas_call(
        functools.partial(_scatter_rows_kernel, tm=tm, n_prompt_tiles=npt, n_tiles=npt + nst, nblk=nblk),
        grid_spec=grid_spec,
        out_shape=jax.ShapeDtypeStruct((nblk * tm + n_spare, d), F32),
        compiler_params=_cparams(("arbitrary",), 32),
        name="scatter_rows",
    )(dest, last_blk, nused, h_p, h_s)


def _moe_kernel(blk_e_ref, nused_ref, x_ref, w1_hbm, w3_hbm, w2_hbm, o_ref, wf_ref, wb_ref, sem):
    i = pl.program_id(0)
    nused = nused_ref[0]
    e = blk_e_ref[i]
    w_hbm = (w1_hbm, w3_hbm, w2_hbm)

    def start_fetch(expert):
        for k in range(3):
            pltpu.make_async_copy(w_hbm[k].at[expert], wf_ref.at[k], sem.at[k]).start()

    @pl.when(i == 0)
    def _():
        start_fetch(e)

    @pl.when((i < nused) & ((i == 0) | (e != blk_e_ref[jnp.maximum(i - 1, 0)])))
    def _():
        for k in range(3):
            pltpu.make_async_copy(w_hbm[k].at[e], wf_ref.at[k], sem.at[k]).wait()
            wb_ref[k] = wf_ref[k].astype(BF16)
        nxt = lax.while_loop(lambda j: (j < nused) & (blk_e_ref[jnp.minimum(j, nused - 1)] == e),
                             lambda j: j + 1, i + 1)

        @pl.when(nxt < nused)
        def _():
            start_fetch(blk_e_ref[nxt])

    @pl.when(i < nused)
    def _():
        x = x_ref[...].astype(BF16)
        a = jnp.dot(x, wb_ref[0], preferred_element_type=F32)
        b = jnp.dot(x, wb_ref[1], preferred_element_type=F32)
        hmid = (_silu(a) * b).astype(BF16)
        o_ref[...] = jnp.dot(hmid, wb_ref[2], preferred_element_type=F32)

    @pl.when(i >= nused)
    def _():
        o_ref[...] = jnp.zeros(o_ref.shape, o_ref.dtype)


def _moe_experts(xs, blk_e, nused, w1, w3, w2, tm=MOE_TILE):
    nblk = blk_e.shape[0]
    d = xs.shape[1]
    wspec = pl.BlockSpec(memory_space=pl.ANY)
    grid_spec = pltpu.PrefetchScalarGridSpec(
        num_scalar_prefetch=2,
        grid=(nblk,),
        in_specs=[pl.BlockSpec((tm, d), lambda i, be, nu: (jnp.minimum(i, nu[0] - 1), 0)), wspec, wspec, wspec],
        out_specs=pl.BlockSpec((tm, d), lambda i, be, nu: (i, 0)),
        scratch_shapes=[pltpu.VMEM((3, d, d), F32), pltpu.VMEM((3, d, d), BF16), pltpu.SemaphoreType.DMA((3,))],
    )
    return pl.pallas_call(
        _moe_kernel,
        grid_spec=grid_spec,
        out_shape=jax.ShapeDtypeStruct((nblk * tm, d), F32),
        compiler_params=_cparams(("arbitrary",), 40),
        name="moe_experts",
    )(blk_e, nused, xs, w1, w3, w2)


def _combine_kernel(dest_ref, x2_ref, r_ref, g_ref, outs_hbm, y_ref, buf, sem, *, tm):
    i = pl.program_id(0)
    n = pl.num_programs(0)
    n_rows = outs_hbm.shape[0]

    def start_tile(tile, slot):
        base = EXP_TOPK * tile * tm
        for r in range(tm):
            for k in range(EXP_TOPK):
                src = dest_ref[base + EXP_TOPK * r + k]
                src = jnp.where(src >= n_rows, src - n_rows, src)
                pltpu.make_async_copy(outs_hbm.at[pl.ds(src, 1)], buf.at[slot, k, pl.ds(r, 1)],
                                      sem.at[slot]).start(priority=k % 2)

    @pl.when(i == 0)
    def _():
        start_tile(0, 0)

    @pl.when(i + 1 < n)
    def _():
        start_tile(i + 1, (i + 1) % 2)

    slot = i % 2
    for k in range(EXP_TOPK):
        pltpu.make_async_copy(outs_hbm.at[pl.ds(0, tm)], buf.at[slot, k], sem.at[slot]).wait()
    route = r_ref[...]
    y = buf[slot, 0] * route[:, 2:3] + buf[slot, 1] * route[:, 3:4]
    y_ref[...] = _rms(x2_ref[...] + y, g_ref[...])


def _combine(x2, route, g_final, outs, dest, tm=ROW_TILE):
    rows, d = x2.shape
    grid_spec = pltpu.PrefetchScalarGridSpec(
        num_scalar_prefetch=1,
        grid=(rows // tm,),
        in_specs=[pl.BlockSpec((tm, d), lambda i, de: (i, 0)),
                  pl.BlockSpec((tm, LANES), lambda i, de: (i, 0)),
                  pl.BlockSpec((1, d), lambda i, de: (0, 0)),
                  pl.BlockSpec(memory_space=pl.ANY)],
        out_specs=pl.BlockSpec((tm, d), lambda i, de: (i, 0)),
        scratch_shapes=[pltpu.VMEM((2, EXP_TOPK, tm, d), F32), pltpu.SemaphoreType.DMA((2,))],
    )
    return pl.pallas_call(
        functools.partial(_combine_kernel, tm=tm),
        grid_spec=grid_spec,
        out_shape=jax.ShapeDtypeStruct((rows, d), F32),
        compiler_params=_cparams(("arbitrary",), 32),
        name="moe_combine",
    )(dest, x2, route, g_final.reshape(1, d), outs)


def _layer_tail(x, o, att, att_transposed, mem_k, mem_v, batch, t, xq_tile, wts, h_all, h_first_row):
    q_dtype = BF16 if xq_tile % (2 * SUBLANES) == 0 else F32
    x1, q = _outproj(x, o, att, wts["w_out"], wts["g_cross"], wts["w_xq"], att_transposed, q_dtype)
    xo = _xattn(q, mem_k, mem_v, batch, t, xq_tile)
    x2, h_all, route = _xo_router(x1, xo, wts["w_xo"], wts["g_ffn"], wts["w_router"], wts["b_router"],
                                  h_all, h_first_row)
    return x2, h_all, route


def _invert_rows_kernel(dest_ref, pad_ref, nused_ref, inv_ref, *, n_assign, rows_cap, tm, trash_base, n_trash):
    def put(a, carry):
        inv_ref[dest_ref[a]] = a
        return carry

    lax.fori_loop(0, n_assign, put, 0, unroll=8)

    def fill(p, carry):
        inv_ref[p] = trash_base + (p & (n_trash - 1))
        return carry

    for e in range(N_EXPERTS):
        lax.fori_loop(pad_ref[e], pad_ref[N_EXPERTS + e], fill, 0)
    lax.fori_loop(nused_ref[0] * tm, rows_cap, fill, 0)


def _invert_rows(dest, pad_rows, nused, rows_cap, n_spare, trash_base, n_trash, tm=MOE_TILE):
    n_assign = dest.shape[0]
    smem = pl.BlockSpec(memory_space=pltpu.SMEM)
    return pl.pallas_call(
        functools.partial(_invert_rows_kernel, n_assign=n_assign, rows_cap=rows_cap, tm=tm,
                          trash_base=trash_base, n_trash=n_trash),
        in_specs=[smem, smem, smem],
        out_specs=smem,
        out_shape=jax.ShapeDtypeStruct((rows_cap + n_spare,), I32),
        name="invert_rows",
    )(dest, pad_rows, nused)


def _moe_direct_kernel(blk_e_ref, nused_ref, inv_ref, h_hbm, w1_hbm, w3_hbm, w2_hbm, y2_hbm,
                       xbuf, obuf, wf_ref, wb_ref, gsem, ssem, wsem, zsem, *, tm, n_tokens, zero_runs):
    i = pl.program_id(0)
    nused = nused_ref[0]
    slot = i % 2
    e = blk_e_ref[i]
    w_hbm = (w1_hbm, w3_hbm, w2_hbm)
    trash_base = EXP_TOPK * n_tokens
    thirds = [(0, tm // 3), (tm // 3, 2 * (tm // 3)), (2 * (tm // 3), tm)]

    def start_fetch(expert):
        for k in range(3):
            pltpu.make_async_copy(w_hbm[k].at[expert], wf_ref.at[k], wsem.at[k]).start()

    def gather_rows(blk, s, lo, hi):
        for r in range(lo, hi):
            a = inv_ref[blk * tm + r]
            tok = jnp.where(a >= trash_base, r, lax.shift_right_logical(a, 1))
            pltpu.make_async_copy(h_hbm.at[pl.ds(tok, 1)], xbuf.at[s, pl.ds(r, 1)], gsem.at[s]).start()

    def scatter_rows(blk, s, lo, hi):
        for r in range(lo, hi):
            pltpu.make_async_copy(obuf.at[s, pl.ds(r, 1)], y2_hbm.at[pl.ds(inv_ref[blk * tm + r], 1)],
                                  ssem.at[s]).start()

    def wait_rows(buf, sem, s):
        if buf is xbuf:
            pltpu.make_async_copy(h_hbm.at[pl.ds(0, tm)], buf.at[s], sem.at[s]).wait()
        else:
            pltpu.make_async_copy(buf.at[s], y2_hbm.at[pl.ds(0, tm)], sem.at[s]).wait()

    @pl.when(i == 0)
    def _():
        start_fetch(e)
        gather_rows(0, 0, 0, tm)
        obuf[1] = jnp.zeros(obuf.shape[1:], obuf.dtype)
        for start, size in zero_runs:
            pltpu.make_async_copy(obuf.at[1, pl.ds(0, size)], y2_hbm.at[pl.ds(start, size)], zsem).start()
        for start, size in zero_runs:
            pltpu.make_async_copy(obuf.at[1, pl.ds(0, size)], y2_hbm.at[pl.ds(start, size)], zsem).wait()

    @pl.when((i < nused) & ((i == 0) | (e != blk_e_ref[jnp.maximum(i - 1, 0)])))
    def _():
        for k in range(3):
            pltpu.make_async_copy(w_hbm[k].at[e], wf_ref.at[k], wsem.at[k]).wait()
            wb_ref[k] = wf_ref[k].astype(BF16)
        nxt = lax.while_loop(lambda j: (j < nused) & (blk_e_ref[jnp.minimum(j, nused - 1)] == e),
                             lambda j: j + 1, i + 1)

        @pl.when(nxt < nused)
        def _():
            start_fetch(blk_e_ref[nxt])

    @pl.when((i >= 1) & (i < nused))
    def _():
        wait_rows(obuf, ssem, slot)

    @pl.when(i < nused)
    def _():
        wait_rows(xbuf, gsem, slot)
        nb = jnp.minimum(i + 1, nused - 1)
        pb = jnp.maximum(i - 1, 0)
        x = xbuf[slot].astype(BF16)
        gather_rows(nb, 1 - slot, *thirds[0])
        scatter_rows(pb, 1 - slot, *thirds[0])
        a = jnp.dot(x, wb_ref[0], preferred_element_type=F32)
        gather_rows(nb, 1 - slot, *thirds[1])
        scatter_rows(pb, 1 - slot, *thirds[1])
        b = jnp.dot(x, wb_ref[1], preferred_element_type=F32)
        gather_rows(nb, 1 - slot, *thirds[2])
        scatter_rows(pb, 1 - slot, *thirds[2])
        hmid = (_silu(a) * b).astype(BF16)
        obuf[slot] = jnp.dot(hmid, wb_ref[2], preferred_element_type=F32)

    @pl.when(i == nused - 1)
    def _():
        wait_rows(xbuf, gsem, 1 - slot)
        wait_rows(obuf, ssem, 1 - slot)
        scatter_rows(i, slot, 0, tm)
        wait_rows(obuf, ssem, slot)


def _moe_direct(h_all, blk_e, nused, inv, w1, w3, w2, n_tokens, zero_runs, n_trash, tm=MOE_TILE):
    nblk = blk_e.shape[0]
    d = h_all.shape[1]
    any_spec = pl.BlockSpec(memory_space=pl.ANY)
    grid_spec = pltpu.PrefetchScalarGridSpec(
        num_scalar_prefetch=3,
        grid=(nblk,),
        in_specs=[any_spec, any_spec, any_spec, any_spec],
        out_specs=any_spec,
        scratch_shapes=[pltpu.VMEM((2, tm, d), F32), pltpu.VMEM((2, tm, d), F32),
                        pltpu.VMEM((3, d, d), F32), pltpu.VMEM((3, d, d), BF16),
                        pltpu.SemaphoreType.DMA((2,)), pltpu.SemaphoreType.DMA((2,)),
                        pltpu.SemaphoreType.DMA((3,)), pltpu.SemaphoreType.DMA(())],
    )
    return pl.pallas_call(
        functools.partial(_moe_direct_kernel, tm=tm, n_tokens=n_tokens, zero_runs=zero_runs),
        grid_spec=grid_spec,
        out_shape=jax.ShapeDtypeStruct((EXP_TOPK * n_tokens + n_trash, d), F32),
        compiler_params=_cparams(("arbitrary",), 40),
        name="moe_direct",
    )(blk_e, nused, inv, h_all, w1, w3, w2)


def _combine_stream_kernel(x2_ref, r_ref, g_ref, y2_ref, y_ref):
    d = x2_ref.shape[1]
    route = r_ref[...]
    y = y2_ref[:, :d] * route[:, 2:3] + y2_ref[:, d:] * route[:, 3:4]
    y_ref[...] = _rms(x2_ref[...] + y, g_ref[...])


def _combine_stream(x2, route, g_final, y2_pairs, first_tile, tm=ROW_TILE):
    rows, d = x2.shape
    return pl.pallas_call(
        _combine_stream_kernel,
        grid=(rows // tm,),
        in_specs=[pl.BlockSpec((tm, d), lambda i: (i, 0)),
                  pl.BlockSpec((tm, LANES), lambda i: (i, 0)),
                  pl.BlockSpec((1, d), lambda i: (0, 0)),
                  pl.BlockSpec((tm, EXP_TOPK * d), lambda i: (i + first_tile, 0))],
        out_specs=pl.BlockSpec((tm, d), lambda i: (i, 0)),
        out_shape=jax.ShapeDtypeStruct((rows, d), F32),
        compiler_params=_cparams(("parallel",), 40),
        name="combine_stream",
    )(x2, route, g_final.reshape(1, d), y2_pairs)


def _moe_and_final(x2_p, route_p, x2_s, route_s, h_all, n_sample_tokens, t_pad, t_valid, wts):
    rp, d = x2_p.shape
    rs = x2_s.shape[0]
    n_tokens = rp + rs
    tm = MOE_TILE
    n_assign = (rp + n_sample_tokens) * EXP_TOPK
    nblk = -(-(n_assign + N_EXPERTS * (tm - 1)) // tm)
    dest, blk_e, nused, _, pad_rows = _route_dest(route_p, route_s, t_pad, t_valid, tm, nblk)
    n_spare = EXP_TOPK * (rs // t_pad) * (t_pad - t_valid)
    n_trash = 2 * tm
    inv = _invert_rows(dest, pad_rows, nused, nblk * tm, n_spare, EXP_TOPK * n_tokens, n_trash)
    zero_runs = [(EXP_TOPK * (rp + b * t_pad + t_valid), EXP_TOPK * (t_pad - t_valid)) for b in range(rs // t_pad)]
    zero_runs += [(EXP_TOPK * n_tokens + k * tm, tm) for k in range(n_trash // tm)]
    y2 = _moe_direct(h_all, blk_e, nused, inv, wts["w1"], wts["w3"], wts["w2"], n_tokens, zero_runs, n_trash)
    y2_pairs = y2.reshape((EXP_TOPK * n_tokens + n_trash) // EXP_TOPK, EXP_TOPK * d)
    y_p = _combine_stream(x2_p, route_p, wts["g_final"], y2_pairs, 0)
    y_s = _combine_stream(x2_s, route_s, wts["g_final"], y2_pairs, rp // ROW_TILE)
    return y_p, y_s


def kernel(x_prompt, x_sample, state_hgrn, cache_moba_k, cache_moba_v, cache_mem_k, cache_mem_v, page_table,
           mem_prompt, g_mix, w_in, hg_lb, hg_norm, w_out, g_cross, g_mem, w_xq, w_xk, w_xv, w_xo, g_ffn,
           w_grp, b_grp, w_exp, b_exp, w1, w3, w2, g_final):
    depth = g_mix.shape[0]
    assert depth == 1
    bp, tp, d = x_prompt.shape
    bs, ts, _ = x_sample.shape
    mem_len = mem_prompt.shape[1]
    l = 0

    lb = jnp.cumsum(jax.nn.softmax(hg_lb.astype(F32), axis=0), axis=0)[l]
    slopes = jnp.asarray(np.power(2.0, -8.0 * np.arange(1, MB_HEADS + 1) / MB_HEADS).astype(np.float32))
    n_router = N_GROUPS + N_EXPERTS
    w_router = jnp.pad(jnp.concatenate([w_grp[l], w_exp[l]], axis=1), ((0, 0), (0, LANES - n_router)))
    b_router = jnp.pad(jnp.concatenate([b_grp[l], b_exp[l]]), (0, LANES - n_router)).reshape(1, LANES)
    wts = {
        "w_out": w_out[l].astype(BF16), "g_cross": g_cross[l], "w_xq": w_xq[l].astype(BF16),
        "w_xo": w_xo[l].astype(BF16), "g_ffn": g_ffn[l], "w_router": w_router.astype(BF16),
        "b_router": b_router.astype(F32), "w1": w1[l], "w3": w3[l], "w2": w2[l], "g_final": g_final,
    }
    w_in_bf = w_in[l].astype(BF16)
    o_hq, o_mq, o_mk, o_mv = 0, 4 * HG_WIDTH, 4 * HG_WIDTH + MB_WIDTH, 4 * HG_WIDTH + 2 * MB_WIDTH
    o_end = o_mv + MB_WIDTH

    xp = x_prompt.reshape(bp * tp, d)
    w_kv = jnp.concatenate([w_xk[l], w_xv[l]], axis=1).astype(BF16)
    memk_p, memv_p = _rms_proj(mem_prompt.reshape(bp * mem_len, d), g_mem[l], w_kv,
                               [(0, d, [(1.0, False)]), (d, 2 * d, [(1.0, False)])], [F32, F32])
    hp_p, qt_p, kt_p, vt_p = _rms_proj(
        xp, g_mix[l], w_in_bf,
        [(o_hq, o_mq, [(1.0, False)]), (o_mq, o_mk, [(MB_DIM ** -0.5, True)]), (o_mk, o_mv, [(1.0, "batched")]),
         (o_mv, o_end, [(1.0, "batched")])],
        [F32, BF16, F32, F32], seq_len=tp)
    o_p, s_p = _hgrn(hp_p, lb, hg_norm[l], None, bp, tp, tp, 128)
    att_p = _moba_prompt(qt_p, kt_p, vt_p, slopes, bp, tp)
    n_rows_all = bp * tp + bs * SAMPLE_PAD_T
    x2_p, h_all, route_p = _tail_fused(xp, o_p, att_p, memk_p, memv_p, bp, tp, wts, n_rows_all)

    tpad = SAMPLE_PAD_T
    xs = jnp.pad(x_sample, ((0, 0), (0, tpad - ts), (0, 0))).reshape(bs * tpad, d)
    hp_s, mq_s, mk_s, mv_s = _rms_proj(
        xs, g_mix[l], w_in_bf,
        [(o_hq, o_mq, [(1.0, False)]), (o_mq, o_mk, [(1.0, False)]), (o_mk, o_mv, [(1.0, False)]),
         (o_mv, o_end, [(1.0, False)])],
        [F32, F32, F32, F32])
    o_s, s_s = _hgrn(hp_s, lb, hg_norm[l], state_hgrn[l], bs, tpad, ts, tpad)
    mk4 = mk_s.reshape(bs, tpad, MB_HEADS, MB_DIM)
    mv4 = mv_s.reshape(bs, tpad, MB_HEADS, MB_DIM)
    n_phys, page = cache_moba_k.shape[1], cache_moba_k.shape[2]
    pool_kt = jnp.transpose(cache_moba_k[l], (0, 2, 3, 1)).reshape(n_phys, MB_WIDTH, page)
    pool_vt = jnp.transpose(cache_moba_v[l], (0, 2, 3, 1)).reshape(n_phys, MB_WIDTH, page)
    att_s = _moba_sample(mq_s, mk_s, mv_s, pool_kt, pool_vt, page_table, ts)

    def mem_rows(c):
        c = c[l].reshape(bs, mem_len, XA_HEADS, XA_DIM // LANES, LANES)
        return jnp.transpose(c, (0, 1, 3, 2, 4)).reshape(bs, mem_len * d // LANES, LANES)

    memk_s, memv_s = mem_rows(cache_mem_k), mem_rows(cache_mem_v)
    x2_s, h_all, route_s = _layer_tail(xs, o_s, att_s, False, memk_s, memv_s, bs, tpad, tpad, wts, h_all, bp * tp)
    y_p, y_s = _moe_and_final(x2_p, route_p, x2_s, route_s, h_all, bs * ts, tpad, ts, wts)

    return (y_p.reshape(bp, tp, d),
            y_s.reshape(bs, tpad, d)[:, :ts],
            s_p.reshape(1, bp, HG_HEADS, HG_DIM, HG_DIM),
            jnp.transpose(kt_p.reshape(1, bp, MB_HEADS, MB_DIM, tp), (0, 1, 4, 2, 3)),
            jnp.transpose(vt_p.reshape(1, bp, MB_HEADS, MB_DIM, tp), (0, 1, 4, 2, 3)),
            memk_p.reshape(1, bp, mem_len, XA_HEADS, XA_DIM),
            memv_p.reshape(1, bp, mem_len, XA_HEADS, XA_DIM),
            s_s.reshape(1, bs, HG_HEADS, HG_DIM, HG_DIM),
            mk4[:, :ts].reshape(1, bs, ts, MB_HEADS, MB_DIM),
            mv4[:, :ts].reshape(1, bs, ts, MB_HEADS, MB_DIM))
```

```python
import functools

import numpy as np
import jax
import jax.numpy as jnp
from jax import lax
from jax.experimental import pallas as pl
from jax.experimental.pallas import tpu as pltpu

F32 = jnp.float32
BF16 = jnp.bfloat16
I32 = jnp.int32

D_MODEL = 1024
HG_HEADS = 4
HG_DIM = 128
HG_WIDTH = HG_HEADS * HG_DIM
MB_HEADS = 8
MB_DIM = 64
MB_WIDTH = MB_HEADS * MB_DIM
MB_BLOCK = 256
MB_TOPK = 3
XA_HEADS = 4
XA_DIM = 256
N_GROUPS = 4
EXP_PER_GROUP = 8
N_EXPERTS = N_GROUPS * EXP_PER_GROUP
EXP_TOPK = 2
RMS_EPS = 1e-6
NEG_INF = float("-inf")

LANES = 128
SUBLANES = 8
MIB = 1024 * 1024

ROW_TILE = 256
MOE_TILE = 256
SAMPLE_PAD_T = 8
MOBA_PAIRS = 4
XATTN_SEQS_PER_STEP = 4

_NT = (((1,), (1,)), ((), ()))


def _cparams(semantics, vmem_mib):
    return pltpu.CompilerParams(dimension_semantics=semantics, vmem_limit_bytes=vmem_mib * MIB)


def _rms(x, g):
    return x * lax.rsqrt(jnp.mean(x * x, axis=-1, keepdims=True) + RMS_EPS) * g


def _sigmoid(x):
    return 0.5 * jnp.tanh(0.5 * x) + 0.5


def _silu(x):
    return x * _sigmoid(x)


def _rms_proj_kernel(x_ref, g_ref, w_ref, *o_refs, segs):
    h = _rms(x_ref[...], g_ref[...]).astype(BF16)
    k = 0
    for lo, hi, outs in segs:
        r = jnp.dot(h, w_ref[:, lo:hi], preferred_element_type=F32)
        for scale, transposed in outs:
            y = r if scale == 1.0 else r * scale
            if transposed:
                y = y.T
            o_refs[k][...] = y.astype(o_refs[k].dtype)
            k += 1


def _rms_proj(x, g, w_bf, segs, dtypes, tm=ROW_TILE, vmem_mib=48, seq_len=None):
    rows, d = x.shape
    assert rows % tm == 0
    out_shape, out_specs = [], []
    k = 0
    for lo, hi, outs in segs:
        n = hi - lo
        for _, transposed in outs:
            if transposed == "batched":
                per_seq = seq_len // tm
                out_shape.append(jax.ShapeDtypeStruct((rows // seq_len, n, seq_len), dtypes[k]))
                out_specs.append(pl.BlockSpec((None, n, tm), lambda i: (i // per_seq, 0, i % per_seq)))
            elif transposed:
                out_shape.append(jax.ShapeDtypeStruct((n, rows), dtypes[k]))
                out_specs.append(pl.BlockSpec((n, tm), lambda i: (0, i)))
            else:
                out_shape.append(jax.ShapeDtypeStruct((rows, n), dtypes[k]))
                out_specs.append(pl.BlockSpec((tm, n), lambda i: (i, 0)))
            k += 1
    return pl.pallas_call(
        functools.partial(_rms_proj_kernel, segs=segs),
        grid=(rows // tm,),
        in_specs=[pl.BlockSpec((tm, d), lambda i: (i, 0)),
                  pl.BlockSpec((1, d), lambda i: (0, 0)),
                  pl.BlockSpec(w_bf.shape, lambda i: (0, 0))],
        out_specs=out_specs,
        out_shape=out_shape,
        compiler_params=_cparams(("parallel",), vmem_mib),
        name="rms_proj",
    )(x, g.reshape(1, d), w_bf)


def _cumsum_rows(x):
    n = x.shape[0]
    row = lax.broadcasted_iota(I32, x.shape, 0)
    s = 1
    while s < n:
        x = x + jnp.where(row >= s, pltpu.roll(x, s, 0), 0.0)
        s *= 2
    return x


def _hgrn_intra(q, kk, b, c, w):
    row = lax.broadcasted_iota(I32, (c, HG_DIM), 0)
    acc = None
    half = c // 2
    while half >= SUBLANES:
        two = 2 * half
        nblk = c // two
        pieces = [jnp.broadcast_to(b[i * two + half - 1:i * two + half, :], (two, HG_DIM)) for i in range(nblk)]
        bm = pieces[0] if nblk == 1 else jnp.concatenate(pieces, axis=0)
        second = (row & (two - 1)) >= half
        qt = jnp.where(second, q * jnp.exp(jnp.where(second, b - bm, 0.0)), 0.0)
        kt = jnp.where(second, 0.0, kk * jnp.exp(jnp.where(second, 0.0, bm - b)))
        al = lax.dot_general(qt.astype(BF16), kt.astype(BF16), _NT, preferred_element_type=F32)
        if nblk > 1:
            shift = two.bit_length() - 1
            rt = lax.broadcasted_iota(I32, (c, c), 0) >> shift
            cs = lax.broadcasted_iota(I32, (c, c), 1) >> shift
            al = jnp.where(rt == cs, al, 0.0)
        acc = al if acc is None else acc + al
        half //= 2
    r8 = lax.broadcasted_iota(I32, (SUBLANES, HG_DIM), 0)
    lane = lax.broadcasted_iota(I32, (SUBLANES, w), 1)
    blocks = []
    for g in range(c // SUBLANES):
        lo = g * SUBLANES
        qg, kg, bg = q[lo:lo + SUBLANES], kk[lo:lo + SUBLANES], b[lo:lo + SUBLANES]
        ag = jnp.zeros((SUBLANES, w), F32)
        for s in range(SUBLANES):
            e = jnp.exp(jnp.where(r8 >= s, bg - bg[s:s + 1, :], NEG_INF))
            p = jnp.sum(qg * kg[s:s + 1, :] * e, axis=1, keepdims=True)
            ag = jnp.where(lane == lo + s, p, ag)
        blocks.append(ag)
    diag = blocks[0] if len(blocks) == 1 else jnp.concatenate(blocks, axis=0)
    return diag if acc is None else acc + diag


def _hgrn_kernel(*refs, c, t_valid, has_s0):
    if has_s0:
        hp_ref, lb_ref, gn_ref, s0_ref, o_ref, sout_ref, st_ref = refs
    else:
        hp_ref, lb_ref, gn_ref, o_ref, sout_ref, st_ref = refs
    ci = pl.program_id(1)
    last = pl.num_programs(1) - 1
    w = max(c, LANES)

    @pl.when(ci == 0)
    def _():
        for h in range(HG_HEADS):
            st_ref[h] = s0_ref[0, h].T if has_s0 else jnp.zeros((HG_DIM, HG_DIM), F32)

    row = lax.broadcasted_iota(I32, (c, HG_DIM), 0)
    for h in range(HG_HEADS):
        lo = h * HG_DIM
        hq = hp_ref[:, lo:lo + HG_DIM]
        hf = hp_ref[:, HG_WIDTH + lo:HG_WIDTH + lo + HG_DIM]
        v = hp_ref[:, 2 * HG_WIDTH + lo:2 * HG_WIDTH + lo + HG_DIM]
        hg = hp_ref[:, 3 * HG_WIDTH + lo:3 * HG_WIDTH + lo + HG_DIM]
        lb = lb_ref[:, lo:lo + HG_DIM]
        q = _silu(hq)
        f = lb + (1.0 - lb) * _sigmoid(hf)
        logf = jnp.log(f)
        kk = 1.0 - f
        if t_valid < c:
            valid = row < t_valid
            logf = jnp.where(valid, logf, 0.0)
            kk = jnp.where(valid, kk, 0.0)
            v = jnp.where(valid, v, 0.0)
        b = _cumsum_rows(logf)
        a = _hgrn_intra(q, kk, b, c, w)
        st = st_ref[h]
        bl = b[c - 1:c, :]
        k2 = kk * jnp.exp(bl - b)
        if c < w:
            zpad = jnp.zeros((w - c, HG_DIM), F32)
            vp = jnp.concatenate([v, zpad], axis=0)
            k2 = jnp.concatenate([k2, zpad], axis=0)
        else:
            vp = v
        vb = vp.astype(BF16)
        o = lax.dot_general((q * jnp.exp(b)).astype(BF16), st.astype(BF16), _NT, preferred_element_type=F32)
        o = o + jnp.dot(a.astype(BF16), vb, preferred_element_type=F32)
        o_ref[:, lo:lo + HG_DIM] = (_rms(o, gn_ref[...]) * _silu(hg)).astype(o_ref.dtype)
        st_new = st * jnp.exp(bl) + jnp.dot(vp.T.astype(BF16), k2.astype(BF16), preferred_element_type=F32)
        st_ref[h] = st_new

        @pl.when(ci == last)
        def _():
            sout_ref[0, h] = st_new.T


def _hgrn(hp, lb, gn, s0, batch, t_pad, t_valid, c):
    rows = hp.shape[0]
    nc = t_pad // c
    has_s0 = s0 is not None
    in_specs = [pl.BlockSpec((c, 4 * HG_WIDTH), lambda b, i: (b * nc + i, 0)),
                pl.BlockSpec((1, HG_WIDTH), lambda b, i: (0, 0)),
                pl.BlockSpec((1, HG_DIM), lambda b, i: (0, 0))]
    args = [hp, lb.reshape(1, HG_WIDTH), gn.reshape(1, HG_DIM)]
    if has_s0:
        in_specs.append(pl.BlockSpec((1, HG_HEADS, HG_DIM, HG_DIM), lambda b, i: (b, 0, 0, 0)))
        args.append(s0)
    return pl.pallas_call(
        functools.partial(_hgrn_kernel, c=c, t_valid=min(t_valid, c), has_s0=has_s0),
        grid=(batch, nc),
        in_specs=in_specs,
        out_specs=[pl.BlockSpec((c, HG_WIDTH), lambda b, i: (b * nc + i, 0)),
                   pl.BlockSpec((1, HG_HEADS, HG_DIM, HG_DIM), lambda b, i: (b, 0, 0, 0))],
        out_shape=[jax.ShapeDtypeStruct((rows, HG_WIDTH), BF16),
                   jax.ShapeDtypeStruct((batch, HG_HEADS, HG_DIM, HG_DIM), F32)],
        scratch_shapes=[pltpu.VMEM((HG_HEADS, HG_DIM, HG_DIM), F32)],
        compiler_params=_cparams(("parallel", "arbitrary"), 32),
        name="hgrn2",
    )(*args)


def _topk_block_mask(gate, n_valid, nrow):
    nb = gate.shape[0]
    cnt = jnp.zeros(gate.shape, I32)
    for n2 in range(nb):
        g2 = gate[n2:n2 + 1, :]
        beats = (g2 > gate) | ((g2 == gate) & (n2 < nrow))
        cnt = cnt + jnp.where(beats, (n_valid > n2).astype(I32), 0)
    return (cnt < MB_TOPK) & (nrow < n_valid)


def _moba_prompt_kernel(slope_ref, qt_ref, kt_ref, vt_ref, o_ref, kb_ref, vb_ref, km_ref, sel_ref, *, nb, npairs):
    hg = pl.program_id(1)
    qi = pl.program_id(2)
    tq = MB_BLOCK
    pw = 2 * tq
    width = npairs * pw
    pair_rows = 2 * MB_DIM

    @pl.when(qi == 0)
    def _():
        pos = jnp.where(lax.broadcasted_iota(I32, (MB_BLOCK, pair_rows), 1) == 0,
                        lax.broadcasted_iota(I32, (MB_BLOCK, pair_rows), 0), 0).astype(BF16)
        for pp in range(npairs):
            means = []
            for j in range(nb):
                kj = kt_ref[pp * pair_rows:(pp + 1) * pair_rows, j * MB_BLOCK:(j + 1) * MB_BLOCK].T
                kb_ref[pp, j * MB_BLOCK:(j + 1) * MB_BLOCK, :pair_rows] = kj.astype(BF16)
                kb_ref[pp, j * MB_BLOCK:(j + 1) * MB_BLOCK, pair_rows:] = pos
                means.append(jnp.sum(kj, axis=0, keepdims=True) * (1.0 / MB_BLOCK))
                vb_ref[pp, j] = vt_ref[pp * pair_rows:(pp + 1) * pair_rows,
                                       j * MB_BLOCK:(j + 1) * MB_BLOCK].astype(BF16)
            km_ref[pp] = jnp.concatenate(means, axis=0)

    second_head = lax.broadcasted_iota(I32, (pair_rows, tq), 0) >= MB_DIM
    first_row = lax.broadcasted_iota(I32, (pair_rows, pw), 0) == 0
    qaugs, gates, slope_parts = [], [], []
    for pp in range(npairs):
        qt2 = qt_ref[pp * pair_rows:(pp + 1) * pair_rows, :]
        zero = jnp.zeros_like(qt2)
        qcat = jnp.concatenate([jnp.where(second_head, zero, qt2), jnp.where(second_head, qt2, zero)], axis=1)
        gates.append(jnp.dot(km_ref[pp].astype(BF16), qcat, preferred_element_type=F32))
        pair_slopes = jnp.concatenate(
            [jnp.full((1, tq), slope_ref[(hg * npairs + pp) * 2 + i], F32) for i in range(2)], axis=1)
        slope_parts.append(pair_slopes)
        srows = jnp.where(first_row, pair_slopes, 0.0).astype(BF16)
        qaugs.append(jnp.concatenate([qcat, srows], axis=0))
    slope_row = slope_parts[0] if npairs == 1 else jnp.concatenate(slope_parts, axis=1)
    gate = gates[0] if npairs == 1 else jnp.concatenate(gates, axis=1)
    nrow = lax.broadcasted_iota(I32, (nb, width), 0)
    sel = _topk_block_mask(gate, qi, nrow).astype(F32)
    for n in range(nb):
        sel_ref[n] = jnp.broadcast_to(sel[n:n + 1, :], (SUBLANES, width))

    def scores(j):
        start = pl.multiple_of(j * MB_BLOCK, MB_BLOCK)
        parts = [jnp.dot(kb_ref[pp, pl.ds(start, MB_BLOCK), :], qaugs[pp], preferred_element_type=F32)
                 for pp in range(npairs)]
        return parts[0] if npairs == 1 else jnp.concatenate(parts, axis=1)

    def values(j, p):
        pb = p.astype(BF16)
        return [jnp.dot(vb_ref[pp, j], pb[:, pp * pw:(pp + 1) * pw], preferred_element_type=F32)
                for pp in range(npairs)]

    lane = lax.broadcasted_iota(I32, (MB_BLOCK, width), 1) & (tq - 1)
    rowk = lax.broadcasted_iota(I32, (MB_BLOCK, width), 0)
    s = jnp.where(rowk <= lane, scores(qi), NEG_INF)
    m = jnp.max(s, axis=0, keepdims=True)
    p = jnp.exp(s - m)
    l = jnp.sum(p, axis=0, keepdims=True)
    accs = values(qi, p)

    def body(j, carry):
        m, l, accs = carry
        off = jnp.full((1, width), (qi - j) * MB_BLOCK, I32).astype(F32) * slope_row
        picked = sel_ref[j][0:1, :] > 0.5
        s = scores(j)
        m_new = jnp.maximum(m, jnp.where(picked, jnp.max(s, axis=0, keepdims=True) - off, NEG_INF))
        alpha = jnp.exp(m - m_new)
        p = jnp.exp(s - jnp.where(picked, m_new + off, float("inf")))
        l = alpha * l + jnp.sum(p, axis=0, keepdims=True)
        pv = values(j, p)
        accs = [alpha[:, pp * pw:(pp + 1) * pw] * accs[pp] + pv[pp] for pp in range(npairs)]
        return m_new, l, accs

    m, l, accs = lax.fori_loop(0, qi, body, (m, l, accs))
    for pp in range(npairs):
        a = accs[pp] / l[:, pp * pw:(pp + 1) * pw]
        o_ref[pp * pair_rows:(pp + 1) * pair_rows, :] = jnp.where(second_head, a[:, tq:], a[:, :tq])


def _moba_prompt(qt, kt, vt, slopes, batch, t, npairs=MOBA_PAIRS):
    nb = t // MB_BLOCK
    rows = batch * t
    gr = 2 * MB_DIM * npairs
    width = npairs * 2 * MB_BLOCK
    return pl.pallas_call(
        functools.partial(_moba_prompt_kernel, nb=nb, npairs=npairs),
        grid=(batch, MB_WIDTH // gr, nb),
        in_specs=[pl.BlockSpec(memory_space=pltpu.SMEM),
                  pl.BlockSpec((gr, MB_BLOCK), lambda b, h, i: (h, b * nb + i)),
                  pl.BlockSpec((None, gr, t), lambda b, h, i: (b, h, 0)),
                  pl.BlockSpec((None, gr, t), lambda b, h, i: (b, h, 0))],
        out_specs=pl.BlockSpec((gr, MB_BLOCK), lambda b, h, i: (h, b * nb + i)),
        out_shape=jax.ShapeDtypeStruct((MB_WIDTH, rows), F32),
        scratch_shapes=[pltpu.VMEM((npairs, t, 4 * MB_DIM), BF16),
                        pltpu.VMEM((npairs, nb, 2 * MB_DIM, MB_BLOCK), BF16),
                        pltpu.VMEM((npairs, nb, 2 * MB_DIM), F32),
                        pltpu.VMEM((nb, SUBLANES, width), F32)],
        compiler_params=_cparams(("parallel", "parallel", "arbitrary"), 48),
        name="moba_prompt",
    )(slopes, qt, kt, vt)


def _moba_sample_kernel(pt_ref, slope_ref, hm_ref, q_ref, kn_ref, vn_ref, *rest, npages, page, t_new, past_len):
    del pt_ref
    k_pages = rest[:npages]
    v_pages = rest[npages:2 * npages]
    o_ref = rest[2 * npages]
    nrows = t_new * MB_HEADS
    pages_per_block = MB_BLOCK // page
    nb = npages // pages_per_block
    hm = hm_ref[...]
    slope = slope_ref[...][:, 0:1]
    row_q = lax.broadcasted_iota(I32, (nrows, 1), 0) >> 3
    pos_q = past_len + row_q

    q = q_ref[...] * (MB_DIM ** -0.5)
    qbd = jnp.concatenate([jnp.broadcast_to(q[t:t + 1, :], (MB_HEADS, MB_WIDTH)) * hm for t in range(t_new)],
                          axis=0).astype(BF16)

    lane = lax.broadcasted_iota(I32, (nrows, page), 1)
    scores = []
    gates = [None] * nb
    for p in range(npages):
        s = jnp.dot(qbd, k_pages[p][0].astype(BF16), preferred_element_type=F32)
        rs = jnp.sum(s, axis=1, keepdims=True)
        n = p // pages_per_block
        gates[n] = rs if gates[n] is None else gates[n] + rs
        scores.append(s - slope * (pos_q - (p * page + lane)).astype(F32))
    sels = []
    for n in range(nb):
        cnt = jnp.zeros((nrows, 1), I32)
        for n2 in range(nb):
            beats = (gates[n2] > gates[n]) | ((gates[n2] == gates[n]) & (n2 < n))
            cnt = cnt + jnp.where(beats, 1, 0)
        sels.append(jnp.where(cnt < MB_TOPK, 1.0, 0.0))

    zrows = jnp.zeros((LANES - SAMPLE_PAD_T, MB_WIDTH), F32)
    knp = jnp.concatenate([kn_ref[...], zrows], axis=0).astype(BF16)
    vnp = jnp.concatenate([vn_ref[...], zrows], axis=0).astype(BF16)
    lane_n = lax.broadcasted_iota(I32, (nrows, LANES), 1)
    s_new = lax.dot_general(qbd, knp, _NT, preferred_element_type=F32)
    s_new = jnp.where(lane_n <= row_q, s_new - slope * (row_q - lane_n).astype(F32), NEG_INF)
    m = jnp.max(s_new, axis=1, keepdims=True)
    for p in range(npages):
        sel_p = jnp.broadcast_to(sels[p // pages_per_block], (nrows, page)) > 0.5
        scores[p] = jnp.where(sel_p, scores[p], NEG_INF)
        m = jnp.maximum(m, jnp.max(scores[p], axis=1, keepdims=True))

    p_new = jnp.exp(s_new - m)
    l = jnp.sum(p_new, axis=1, keepdims=True)
    acc = jnp.dot(p_new.astype(BF16), vnp, preferred_element_type=F32)
    for p in range(npages):
        pr = jnp.exp(scores[p] - m)
        l = l + jnp.sum(pr, axis=1, keepdims=True)
        acc = acc + lax.dot_general(pr.astype(BF16), v_pages[p][0].astype(BF16), _NT,
                                    preferred_element_type=F32)
    acc = acc / l
    rows = [jnp.sum(acc[t * MB_HEADS:(t + 1) * MB_HEADS, :] * hm, axis=0, keepdims=True) for t in range(t_new)]
    rows.append(jnp.zeros((SAMPLE_PAD_T - t_new, MB_WIDTH), F32))
    o_ref[...] = jnp.concatenate(rows, axis=0)


def _moba_sample(mq, mk, mv, pool_kt, pool_vt, page_table, t_new):
    batch, npages = page_table.shape
    page = pool_kt.shape[2]
    past_len = npages * page
    assert past_len % MB_BLOCK == 0 and MB_BLOCK % page == 0 and t_new <= SAMPLE_PAD_T and page == LANES
    nrows = t_new * MB_HEADS
    slopes = np.power(2.0, -8.0 * np.arange(1, MB_HEADS + 1) / MB_HEADS).astype(np.float32)
    slope_rows = jnp.asarray(np.tile(np.tile(slopes, t_new)[:, None], (1, LANES)))
    head_mask = jnp.asarray((np.arange(MB_WIDTH)[None, :] // MB_DIM == np.arange(MB_HEADS)[:, None])
                            .astype(np.float32))
    new_spec = pl.BlockSpec((SAMPLE_PAD_T, MB_WIDTH), lambda b, pt: (b, 0))
    page_specs = [pl.BlockSpec((1, MB_WIDTH, page),
                               functools.partial(lambda b, pt, i: (pt[b * npages + i], 0, 0), i=i))
                  for i in range(npages)]
    grid_spec = pltpu.PrefetchScalarGridSpec(
        num_scalar_prefetch=1,
        grid=(batch,),
        in_specs=[pl.BlockSpec((nrows, LANES), lambda b, pt: (0, 0)),
                  pl.BlockSpec((MB_HEADS, MB_WIDTH), lambda b, pt: (0, 0)),
                  new_spec, new_spec, new_spec] + page_specs + page_specs,
        out_specs=new_spec,
    )
    return pl.pallas_call(
        functools.partial(_moba_sample_kernel, npages=npages, page=page, t_new=t_new, past_len=past_len),
        grid_spec=grid_spec,
        out_shape=jax.ShapeDtypeStruct((batch * SAMPLE_PAD_T, MB_WIDTH), F32),
        compiler_params=_cparams(("parallel",), 40),
        name="moba_sample",
    )(page_table.reshape(-1), slope_rows, head_mask, mq, mk, mv, *([pool_kt] * npages), *([pool_vt] * npages))


def _outproj_math(x, o, att, wo_ref, g, wq_ref):
    mix = jnp.dot(o, wo_ref[:HG_WIDTH, :], preferred_element_type=F32)
    mix = mix + jnp.dot(att.astype(BF16), wo_ref[HG_WIDTH:, :], preferred_element_type=F32)
    x1 = x + mix
    h = _rms(x1, g).astype(BF16)
    return x1, jnp.dot(h, wq_ref[...], preferred_element_type=F32) * (XA_DIM ** -0.5)


def _outproj_kernel(x_ref, o_ref, att_ref, wo_ref, g_ref, wq_ref, x1_ref, q_ref, *, att_transposed):
    att = att_ref[...]
    if att_transposed:
        att = att.T
    x1, q = _outproj_math(x_ref[...], o_ref[...], att, wo_ref, g_ref[...], wq_ref)
    x1_ref[...] = x1
    q_ref[...] = q.astype(q_ref.dtype)


def _outproj(x, o, att, wo_bf, g, wq_bf, att_transposed, q_dtype, tm=ROW_TILE):
    rows, d = x.shape
    att_spec = (pl.BlockSpec((MB_WIDTH, tm), lambda i: (0, i)) if att_transposed
                else pl.BlockSpec((tm, MB_WIDTH), lambda i: (i, 0)))
    return pl.pallas_call(
        functools.partial(_outproj_kernel, att_transposed=att_transposed),
        grid=(rows // tm,),
        in_specs=[pl.BlockSpec((tm, d), lambda i: (i, 0)),
                  pl.BlockSpec((tm, HG_WIDTH), lambda i: (i, 0)),
                  att_spec,
                  pl.BlockSpec((d, d), lambda i: (0, 0)),
                  pl.BlockSpec((1, d), lambda i: (0, 0)),
                  pl.BlockSpec((d, d), lambda i: (0, 0))],
        out_specs=[pl.BlockSpec((tm, d), lambda i: (i, 0)), pl.BlockSpec((tm, d), lambda i: (i, 0))],
        out_shape=[jax.ShapeDtypeStruct((rows, d), F32), jax.ShapeDtypeStruct((rows, d), q_dtype)],
        compiler_params=_cparams(("parallel",), 40),
        name="outproj_q",
    )(x, o, att, wo_bf, g.reshape(1, d), wq_bf)


def _xattn_tiled_kernel(q_ref, k_ref, v_ref, o_ref, *, nt):
    for g in range(k_ref.shape[0]):
        o_ref[g * nt:(g + 1) * nt, :] = _xattn_tiled_one(
            q_ref[g * nt:(g + 1) * nt, :], k_ref[g], v_ref[g], nt).astype(o_ref.dtype)


def _xattn_tiled_one(q, k, v, nt):
    ndt = XA_DIM // LANES
    nc = XA_HEADS * ndt
    half = XA_HEADS * nt
    q = q.astype(F32)
    a = jnp.concatenate([q[:, (h * ndt + dt) * LANES:(h * ndt + dt + 1) * LANES]
                         for dt in range(ndt) for h in range(XA_HEADS)], axis=0).astype(BF16)
    s = lax.dot_general(a, k.astype(BF16), _NT, preferred_element_type=F32)
    width = s.shape[1]
    lane_c = lax.broadcasted_iota(I32, s.shape, 1) & (nc - 1)
    assert nt & (nt - 1) == 0 and nc & (nc - 1) == 0
    row_c = lax.broadcasted_iota(I32, s.shape, 0) >> (nt.bit_length() - 1)
    s = jnp.where(lane_c == row_c, s, 0.0)
    tot = s[:half]
    for dt in range(1, ndt):
        tot = tot + pltpu.roll(s[dt * half:(dt + 1) * half], width - dt * XA_HEADS, 1)
    ok = ((lax.broadcasted_iota(I32, tot.shape, 1) & (nc - 1))
          == (lax.broadcasted_iota(I32, tot.shape, 0) >> (nt.bit_length() - 1)))
    tot = jnp.where(ok, tot, NEG_INF)
    p = jnp.exp(tot - jnp.max(tot, axis=-1, keepdims=True))
    p = p / jnp.sum(p, axis=-1, keepdims=True)
    pe = jnp.concatenate([p] + [pltpu.roll(p, dt * XA_HEADS, 1) for dt in range(1, ndt)], axis=0)
    o = jnp.dot(pe.astype(BF16), v.astype(BF16), preferred_element_type=F32)
    return jnp.concatenate([o[(dt * XA_HEADS + h) * nt:(dt * XA_HEADS + h + 1) * nt, :]
                            for h in range(XA_HEADS) for dt in range(ndt)], axis=1)


def _xattn_math(q, k_ref, v_ref):
    outs = []
    for h in range(XA_HEADS):
        lo = h * XA_DIM
        kh = k_ref[:, lo:lo + XA_DIM].astype(BF16)
        vh = v_ref[:, lo:lo + XA_DIM].astype(BF16)
        s = lax.dot_general(q[:, lo:lo + XA_DIM], kh, _NT, preferred_element_type=F32)
        p = jnp.exp(s - jnp.max(s, axis=-1, keepdims=True))
        p = p / jnp.sum(p, axis=-1, keepdims=True)
        outs.append(jnp.dot(p.astype(BF16), vh, preferred_element_type=F32))
    return outs


def _xattn_kernel(q_ref, k_ref, v_ref, o_ref):
    outs = _xattn_math(q_ref[...].astype(BF16), k_ref, v_ref)
    for h in range(XA_HEADS):
        o_ref[:, h * XA_DIM:(h + 1) * XA_DIM] = outs[h].astype(o_ref.dtype)


def _xattn(q, mem_k, mem_v, batch, t, tq):
    rows, d = q.shape
    nq = t // tq
    if mem_k.ndim == 3:
        assert tq == t and batch % XATTN_SEQS_PER_STEP == 0
        g = XATTN_SEQS_PER_STEP
        kv_spec = pl.BlockSpec((g,) + mem_k.shape[1:], lambda b: (b, 0, 0))
        return pl.pallas_call(
            functools.partial(_xattn_tiled_kernel, nt=t),
            grid=(batch // g,),
            in_specs=[pl.BlockSpec((g * t, d), lambda b: (b, 0)), kv_spec, kv_spec],
            out_specs=pl.BlockSpec((g * t, d), lambda b: (b, 0)),
            out_shape=jax.ShapeDtypeStruct((rows, d), q.dtype),
            compiler_params=_cparams(("parallel",), 40),
            name="xattn_tiled",
        )(q, mem_k, mem_v)
    m = mem_k.shape[0] // batch
    kv_spec = pl.BlockSpec((m, d), lambda b, i: (b, 0))
    return pl.pallas_call(
        _xattn_kernel,
        grid=(batch, nq),
        in_specs=[pl.BlockSpec((tq, d), lambda b, i: (b * nq + i, 0)), kv_spec, kv_spec],
        out_specs=pl.BlockSpec((tq, d), lambda b, i: (b * nq + i, 0)),
        out_shape=jax.ShapeDtypeStruct((rows, d), q.dtype),
        compiler_params=_cparams(("parallel", "arbitrary"), 32),
        name="xattn",
    )(q, mem_k, mem_v)


def _xo_router_kernel(x1_ref, o_ref, wo_ref, g_ref, wr_ref, br_ref, x2_ref, h_ref, r_ref):
    x2 = x1_ref[...] + jnp.dot(o_ref[...].astype(BF16), wo_ref[...], preferred_element_type=F32)
    x2_ref[...] = x2
    h = _rms(x2, g_ref[...])
    h_ref[...] = h
    r_ref[...] = _router_math(h, wr_ref, br_ref)


def _router_math(h, wr_ref, br_ref):
    logits = jnp.dot(h.astype(BF16), wr_ref[...], preferred_element_type=F32) + br_ref[...]
    lane = lax.broadcasted_iota(I32, logits.shape, 1)
    big = jnp.int32(LANES)

    def top1(mask):
        mx = jnp.max(jnp.where(mask, logits, NEG_INF), axis=-1, keepdims=True)
        idx = jnp.min(jnp.where(mask & (logits == mx), lane, big), axis=-1, keepdims=True)
        return mx, idx

    gmask = lane < N_GROUPS
    gmx, gsel = top1(gmask)
    gw = 1.0 / jnp.sum(jnp.where(gmask, jnp.exp(logits - gmx), 0.0), axis=-1, keepdims=True)
    elo = N_GROUPS + gsel * EXP_PER_GROUP
    emask = (lane >= elo) & (lane < elo + EXP_PER_GROUP)
    m1, i1 = top1(emask)
    m2, i2 = top1(emask & (lane != i1))
    e2 = jnp.exp(m2 - m1)
    g1 = gw / (1.0 + e2)
    g2 = gw * e2 / (1.0 + e2)
    out = jnp.where(lane == 0, (i1 - N_GROUPS).astype(F32), 0.0)
    out = jnp.where(lane == 1, (i2 - N_GROUPS).astype(F32), out)
    out = jnp.where(lane == 2, g1, out)
    return jnp.where(lane == 3, g2, out)


def _tail_fused_kernel(x_ref, o_ref, att_ref, wo_ref, gc_ref, wq_ref, k_ref, v_ref, wxo_ref, gf_ref, wr_ref,
                       br_ref, x2_ref, h_ref, r_ref, *, n_tiles):
    i = pl.program_id(0)

    @pl.when(i < n_tiles)
    def _():
        x1, q = _outproj_math(x_ref[...], o_ref[...], att_ref[...].T, wo_ref, gc_ref[...], wq_ref)
        outs = _xattn_math(q.astype(BF16), k_ref, v_ref)
        x2 = x1
        for h in range(XA_HEADS):
            x2 = x2 + jnp.dot(outs[h].astype(BF16), wxo_ref[h * XA_DIM:(h + 1) * XA_DIM, :],
                              preferred_element_type=F32)
        x2_ref[...] = x2
        hn = _rms(x2, gf_ref[...])
        h_ref[...] = hn
        r_ref[...] = _router_math(hn, wr_ref, br_ref)

    @pl.when(i >= n_tiles)
    def _():
        h_ref[...] = jnp.zeros(h_ref.shape, h_ref.dtype)


def _tail_fused(x, o, att_t, mem_k, mem_v, batch, t, wts, h_rows_total, tm=512):
    rows, d = x.shape
    nq = t // tm
    n_tiles = rows // tm
    n_extra = (h_rows_total - rows) // tm
    assert (h_rows_total - rows) % tm == 0
    m = mem_k.shape[0] // batch

    def tile(i):
        return jnp.minimum(i, n_tiles - 1)

    row_spec = pl.BlockSpec((tm, d), lambda i: (tile(i), 0))
    w_spec = pl.BlockSpec((d, d), lambda i: (0, 0))
    g_spec = pl.BlockSpec((1, d), lambda i: (0, 0))
    kv_spec = pl.BlockSpec((m, d), lambda i: (tile(i) // nq, 0))
    return pl.pallas_call(
        functools.partial(_tail_fused_kernel, n_tiles=n_tiles),
        grid=(n_tiles + n_extra,),
        in_specs=[row_spec,
                  pl.BlockSpec((tm, HG_WIDTH), lambda i: (tile(i), 0)),
                  pl.BlockSpec((MB_WIDTH, tm), lambda i: (0, tile(i))),
                  w_spec, g_spec, w_spec, kv_spec, kv_spec, w_spec, g_spec,
                  pl.BlockSpec((d, LANES), lambda i: (0, 0)),
                  pl.BlockSpec((1, LANES), lambda i: (0, 0))],
        out_specs=[row_spec, pl.BlockSpec((tm, d), lambda i: (i, 0)),
                   pl.BlockSpec((tm, LANES), lambda i: (tile(i), 0))],
        out_shape=[jax.ShapeDtypeStruct((rows, d), F32),
                   jax.ShapeDtypeStruct((h_rows_total, d), F32),
                   jax.ShapeDtypeStruct((rows, LANES), F32)],
        compiler_params=_cparams(("arbitrary",), 56),
        name="tail_fused",
    )(x, o, att_t, wts["w_out"], wts["g_cross"].reshape(1, d), wts["w_xq"], mem_k, mem_v, wts["w_xo"],
      wts["g_ffn"].reshape(1, d), wts["w_router"], wts["b_router"])


def _xo_router_into_kernel(x1_ref, o_ref, wo_ref, g_ref, wr_ref, br_ref, h_all_ref, x2_ref, h_ref, r_ref):
    del h_all_ref
    _xo_router_kernel(x1_ref, o_ref, wo_ref, g_ref, wr_ref, br_ref, x2_ref, h_ref, r_ref)


def _xo_router(x1, o, wxo_bf, g, wr_bf, br, h_all, h_first_row, tm=ROW_TILE):
    rows, d = x1.shape
    first_tile = h_first_row // tm
    assert h_first_row % tm == 0
    return pl.pallas_call(
        _xo_router_into_kernel,
        grid=(rows // tm,),
        in_specs=[pl.BlockSpec((tm, d), lambda i: (i, 0)),
                  pl.BlockSpec((tm, d), lambda i: (i, 0)),
                  pl.BlockSpec((d, d), lambda i: (0, 0)),
                  pl.BlockSpec((1, d), lambda i: (0, 0)),
                  pl.BlockSpec((d, LANES), lambda i: (0, 0)),
                  pl.BlockSpec((1, LANES), lambda i: (0, 0)),
                  pl.BlockSpec(memory_space=pl.ANY)],
        out_specs=[pl.BlockSpec((tm, d), lambda i: (i, 0)),
                   pl.BlockSpec((tm, d), lambda i: (i + first_tile, 0)),
                   pl.BlockSpec((tm, LANES), lambda i: (i, 0))],
        out_shape=[jax.ShapeDtypeStruct((rows, d), F32),
                   jax.ShapeDtypeStruct(h_all.shape, F32),
                   jax.ShapeDtypeStruct((rows, LANES), F32)],
        input_output_aliases={6: 1},
        compiler_params=_cparams(("parallel",), 40),
        name="xo_router",
    )(x1, o, wxo_bf, g.reshape(1, d), wr_bf, br, h_all)


def _lane_cumsum(x):
    lane = lax.broadcasted_iota(I32, x.shape, 1)
    s = 1
    while s < LANES:
        x = x + jnp.where(lane >= s, pltpu.roll(x, s, 1), 0.0)
        s *= 2
    return x


def _route_dest_kernel(rp_ref, rs_ref, dest_ref, tab_ref, last_ref, r_buf, cnt_ref, carry_ref, pstart_ref, *,
                       n_prompt_chunks, t_pad, t_valid, tm, rows_cap, nblk_lanes):
    ph = pl.program_id(0)
    c = pl.program_id(1)
    chunk = r_buf.shape[0]
    is_prompt = c < n_prompt_chunks

    @pl.when(is_prompt)
    def _():
        r_buf[...] = rp_ref[...]

    @pl.when(jnp.logical_not(is_prompt))
    def _():
        r_buf[...] = rs_ref[...]

    @pl.when((ph == 0) & (c == 0))
    def _():
        cnt_ref[...] = jnp.zeros(cnt_ref.shape, F32)

    route = r_buf[...]
    lane = lax.broadcasted_iota(I32, (chunk, LANES), 1)
    row = lax.broadcasted_iota(I32, (chunk, LANES), 0)
    lanef = lane.astype(F32)
    valid = (((row & (t_pad - 1)) < t_valid).astype(I32) | is_prompt.astype(I32)) > 0
    oh0 = jnp.where((lanef == route[:, 0:1]) & valid, 1.0, 0.0)
    oh1 = jnp.where((lanef == route[:, 1:2]) & valid, 1.0, 0.0)
    cmat = oh0 + oh1
    csum = jnp.sum(cmat, axis=0, keepdims=True)

    @pl.when(ph == 0)
    def _():
        cnt_ref[...] = cnt_ref[...] + csum

    @pl.when((ph == 1) & (c == 0))
    def _():
        cnt = jnp.broadcast_to(cnt_ref[...], (SUBLANES, LANES))
        padc = jnp.floor((cnt + (tm - 1)) * (1.0 / tm)) * tm
        pend = _lane_cumsum(padc)
        pstart_ref[...] = (pend - padc)[0:1, :]
        carry_ref[...] = jnp.zeros(carry_ref.shape, F32)
        starts = lax.broadcasted_iota(I32, (SUBLANES, nblk_lanes), 1).astype(F32) * tm
        blk = jnp.zeros((SUBLANES, nblk_lanes), F32)
        for e in range(N_EXPERTS):
            blk = blk + jnp.where(pend[:, e:e + 1] <= starts, 1.0, 0.0)
        blk = jnp.minimum(blk, N_EXPERTS - 1.0)
        nused = jnp.broadcast_to(pend[:, N_EXPERTS - 1:N_EXPERTS] * (1.0 / tm), (SUBLANES, nblk_lanes))
        sub = lax.broadcasted_iota(I32, (SUBLANES, nblk_lanes), 0)
        tab_ref[...] = jnp.where(sub == 0, blk, nused)
        last_blk = jnp.where(cnt > 0.0, pend * (1.0 / tm) - 1.0, -1.0)
        sub_e = lax.broadcasted_iota(I32, (SUBLANES, LANES), 0)
        last_ref[...] = jnp.where(sub_e == 0, last_blk, jnp.where(sub_e == 1, pend - padc + cnt, pend))

    @pl.when(ph == 1)
    def _():
        rt = lax.broadcasted_iota(I32, (chunk, chunk), 0)
        cs = lax.broadcasted_iota(I32, (chunk, chunk), 1)
        tri = jnp.where(rt > cs, 1.0, 0.0).astype(BF16)
        before = jnp.dot(tri, cmat.astype(BF16), preferred_element_type=F32) + carry_ref[...]
        base = before + pstart_ref[...]
        d0 = jnp.sum(base * oh0, axis=1, keepdims=True)
        d1 = jnp.sum(base * oh1, axis=1, keepdims=True)
        srow = (c - n_prompt_chunks) * chunk + row
        tshift = t_pad.bit_length() - 1
        padded_idx = (srow >> tshift) * (t_pad - t_valid) + (srow & (t_pad - 1)) - t_valid
        spare = (rows_cap + EXP_TOPK * padded_idx).astype(F32)
        out = jnp.where(lane == 0, d0, d1)
        out = jnp.where(valid, out, spare + lanef)
        dest_ref[...] = jnp.where(lane < EXP_TOPK, out, 0.0)
        carry_ref[...] = carry_ref[...] + csum


def _route_dest(route_p, route_s, t_pad, t_valid, tm, nblk, chunk=ROW_TILE):
    rp, rs = route_p.shape[0], route_s.shape[0]
    assert rp % chunk == 0 and rs % chunk == 0 and chunk % t_pad == 0
    npc, nsc = rp // chunk, rs // chunk
    rows_cap = nblk * tm
    nblk_lanes = -(-nblk // LANES) * LANES
    dest_f, tab, last = pl.pallas_call(
        functools.partial(_route_dest_kernel, n_prompt_chunks=npc, t_pad=t_pad, t_valid=t_valid, tm=tm,
                          rows_cap=rows_cap, nblk_lanes=nblk_lanes),
        grid=(2, npc + nsc),
        in_specs=[pl.BlockSpec((chunk, LANES), lambda ph, c: (jnp.minimum(c, npc - 1), 0)),
                  pl.BlockSpec((chunk, LANES), lambda ph, c: (jnp.maximum(c - npc, 0), 0))],
        out_specs=[pl.BlockSpec((chunk, LANES), lambda ph, c: (c * ph, 0)),
                   pl.BlockSpec((SUBLANES, nblk_lanes), lambda ph, c: (0, 0)),
                   pl.BlockSpec((SUBLANES, LANES), lambda ph, c: (0, 0))],
        out_shape=[jax.ShapeDtypeStruct((rp + rs, LANES), F32),
                   jax.ShapeDtypeStruct((SUBLANES, nblk_lanes), F32),
                   jax.ShapeDtypeStruct((SUBLANES, LANES), F32)],
        scratch_shapes=[pltpu.VMEM((chunk, LANES), F32), pltpu.VMEM((1, LANES), F32),
                        pltpu.VMEM((1, LANES), F32), pltpu.VMEM((1, LANES), F32)],
        compiler_params=_cparams(("arbitrary", "arbitrary"), 32),
        name="route_dest",
    )(route_p, route_s)
    dest = dest_f[:, :EXP_TOPK].astype(I32).T.reshape(-1)
    blk_e = tab[0, :nblk].astype(I32)
    nused = tab[1, :1].astype(I32)
    last_blk = last[0, :N_EXPERTS].astype(I32)
    pad_rows = last[1:3, :N_EXPERTS].astype(I32).reshape(-1)
    return dest, blk_e, nused, last_blk, pad_rows


def _scatter_rows_kernel(dest_ref, last_ref, nused_ref, hp_ref, hs_ref, xs_ref, sbuf, sem, zsem, *,
                         tm, n_prompt_tiles, n_tiles, nblk):
    i = pl.program_id(0)
    slot = i % 2

    def wait_tile(s):
        for _ in range(EXP_TOPK):
            pltpu.make_async_copy(sbuf.at[s], xs_ref.at[pl.ds(0, tm)], sem.at[s]).wait()

    def clear_block(b):
        return pltpu.make_async_copy(sbuf.at[1], xs_ref.at[pl.ds(b * tm, tm)], zsem)

    @pl.when(i == 0)
    def _():
        sbuf[1] = jnp.zeros(sbuf.shape[1:], sbuf.dtype)
        nused = nused_ref[0]
        for e in range(N_EXPERTS):
            @pl.when(last_ref[e] >= 0)
            def _():
                clear_block(last_ref[e]).start()

        def start_unused(b, carry):
            clear_block(b).start()
            return carry

        def wait_one(b, carry):
            clear_block(0).wait()
            return carry

        lax.fori_loop(nused, nblk, start_unused, 0)
        for e in range(N_EXPERTS):
            @pl.when(last_ref[e] >= 0)
            def _():
                clear_block(0).wait()
        lax.fori_loop(nused, nblk, wait_one, 0)

    @pl.when(i >= 2)
    def _():
        wait_tile(slot)

    @pl.when(i < n_prompt_tiles)
    def _():
        sbuf[slot] = hp_ref[...]

    @pl.when(i >= n_prompt_tiles)
    def _():
        sbuf[slot] = hs_ref[...]

    base = EXP_TOPK * i * tm
    for r in range(tm):
        for k in range(EXP_TOPK):
            pltpu.make_async_copy(sbuf.at[slot, pl.ds(r, 1)],
                                  xs_ref.at[pl.ds(dest_ref[base + EXP_TOPK * r + k], 1)],
                                  sem.at[slot]).start(priority=k % 2)

    @pl.when(i == n_tiles - 1)
    def _():
        wait_tile(slot)
        wait_tile(1 - slot)


def _scatter_rows(dest, last_blk, nused, h_p, h_s, nblk, n_spare, tm=ROW_TILE):
    d = h_p.shape[1]
    npt, nst = h_p.shape[0] // tm, h_s.shape[0] // tm
    assert npt + nst >= 2 and MOE_TILE == tm
    grid_spec = pltpu.PrefetchScalarGridSpec(
        num_scalar_prefetch=3,
        grid=(npt + nst,),
        in_specs=[pl.BlockSpec((tm, d), lambda i, de, lb, nu: (jnp.minimum(i, npt - 1), 0)),
                  pl.BlockSpec((tm, d), lambda i, de, lb, nu: (jnp.maximum(i - npt, 0), 0))],
        out_specs=pl.BlockSpec(memory_space=pl.ANY),
        scratch_shapes=[pltpu.VMEM((2, tm, d), F32), pltpu.SemaphoreType.DMA((2,)), pltpu.SemaphoreType.DMA(())],
    )
    return pl.pallas_call(
        functools.partial(_scatter_rows_kernel, tm=tm, n_prompt_tiles=npt, n_tiles=npt + nst, nblk=nblk),
        grid_spec=grid_spec,
        out_shape=jax.ShapeDtypeStruct((nblk * tm + n_spare, d), F32),
        compiler_params=_cparams(("arbitrary",), 32),
        name="scatter_rows",
    )(dest, last_blk, nused, h_p, h_s)


def _moe_kernel(blk_e_ref, nused_ref, x_ref, w1_hbm, w3_hbm, w2_hbm, o_ref, wf_ref, wb_ref, sem):
    i = pl.program_id(0)
    nused = nused_ref[0]
    e = blk_e_ref[i]
    w_hbm = (w1_hbm, w3_hbm, w2_hbm)

    def start_fetch(expert):
        for k in range(3):
            pltpu.make_async_copy(w_hbm[k].at[expert], wf_ref.at[k], sem.at[k]).start()

    @pl.when(i == 0)
    def _():
        start_fetch(e)

    @pl.when((i < nused) & ((i == 0) | (e != blk_e_ref[jnp.maximum(i - 1, 0)])))
    def _():
        for k in range(3):
            pltpu.make_async_copy(w_hbm[k].at[e], wf_ref.at[k], sem.at[k]).wait()
            wb_ref[k] = wf_ref[k].astype(BF16)
        nxt = lax.while_loop(lambda j: (j < nused) & (blk_e_ref[jnp.minimum(j, nused - 1)] == e),
                             lambda j: j + 1, i + 1)

        @pl.when(nxt < nused)
        def _():
            start_fetch(blk_e_ref[nxt])

    @pl.when(i < nused)
    def _():
        x = x_ref[...].astype(BF16)
        a = jnp.dot(x, wb_ref[0], preferred_element_type=F32)
        b = jnp.dot(x, wb_ref[1], preferred_element_type=F32)
        hmid = (_silu(a) * b).astype(BF16)
        o_ref[...] = jnp.dot(hmid, wb_ref[2], preferred_element_type=F32)

    @pl.when(i >= nused)
    def _():
        o_ref[...] = jnp.zeros(o_ref.shape, o_ref.dtype)


def _moe_experts(xs, blk_e, nused, w1, w3, w2, tm=MOE_TILE):
    nblk = blk_e.shape[0]
    d = xs.shape[1]
    wspec = pl.BlockSpec(memory_space=pl.ANY)
    grid_spec = pltpu.PrefetchScalarGridSpec(
        num_scalar_prefetch=2,
        grid=(nblk,),
        in_specs=[pl.BlockSpec((tm, d), lambda i, be, nu: (jnp.minimum(i, nu[0] - 1), 0)), wspec, wspec, wspec],
        out_specs=pl.BlockSpec((tm, d), lambda i, be, nu: (i, 0)),
        scratch_shapes=[pltpu.VMEM((3, d, d), F32), pltpu.VMEM((3, d, d), BF16), pltpu.SemaphoreType.DMA((3,))],
    )
    return pl.pallas_call(
        _moe_kernel,
        grid_spec=grid_spec,
        out_shape=jax.ShapeDtypeStruct((nblk * tm, d), F32),
        compiler_params=_cparams(("arbitrary",), 40),
        name="moe_experts",
    )(blk_e, nused, xs, w1, w3, w2)


def _combine_kernel(dest_ref, x2_ref, r_ref, g_ref, outs_hbm, y_ref, buf, sem, *, tm):
    i = pl.program_id(0)
    n = pl.num_programs(0)
    n_rows = outs_hbm.shape[0]

    def start_tile(tile, slot):
        base = EXP_TOPK * tile * tm
        for r in range(tm):
            for k in range(EXP_TOPK):
                src = dest_ref[base + EXP_TOPK * r + k]
                src = jnp.where(src >= n_rows, src - n_rows, src)
                pltpu.make_async_copy(outs_hbm.at[pl.ds(src, 1)], buf.at[slot, k, pl.ds(r, 1)],
                                      sem.at[slot]).start(priority=k % 2)

    @pl.when(i == 0)
    def _():
        start_tile(0, 0)

    @pl.when(i + 1 < n)
    def _():
        start_tile(i + 1, (i + 1) % 2)

    slot = i % 2
    for k in range(EXP_TOPK):
        pltpu.make_async_copy(outs_hbm.at[pl.ds(0, tm)], buf.at[slot, k], sem.at[slot]).wait()
    route = r_ref[...]
    y = buf[slot, 0] * route[:, 2:3] + buf[slot, 1] * route[:, 3:4]
    y_ref[...] = _rms(x2_ref[...] + y, g_ref[...])


def _combine(x2, route, g_final, outs, dest, tm=ROW_TILE):
    rows, d = x2.shape
    grid_spec = pltpu.PrefetchScalarGridSpec(
        num_scalar_prefetch=1,
        grid=(rows // tm,),
        in_specs=[pl.BlockSpec((tm, d), lambda i, de: (i, 0)),
                  pl.BlockSpec((tm, LANES), lambda i, de: (i, 0)),
                  pl.BlockSpec((1, d), lambda i, de: (0, 0)),
                  pl.BlockSpec(memory_space=pl.ANY)],
        out_specs=pl.BlockSpec((tm, d), lambda i, de: (i, 0)),
        scratch_shapes=[pltpu.VMEM((2, EXP_TOPK, tm, d), F32), pltpu.SemaphoreType.DMA((2,))],
    )
    return pl.pallas_call(
        functools.partial(_combine_kernel, tm=tm),
        grid_spec=grid_spec,
        out_shape=jax.ShapeDtypeStruct((rows, d), F32),
        compiler_params=_cparams(("arbitrary",), 32),
        name="moe_combine",
    )(dest, x2, route, g_final.reshape(1, d), outs)


def _layer_tail(x, o, att, att_transposed, mem_k, mem_v, batch, t, xq_tile, wts, h_all, h_first_row):
    q_dtype = BF16 if xq_tile % (2 * SUBLANES) == 0 else F32
    x1, q = _outproj(x, o, att, wts["w_out"], wts["g_cross"], wts["w_xq"], att_transposed, q_dtype)
    xo = _xattn(q, mem_k, mem_v, batch, t, xq_tile)
    x2, h_all, route = _xo_router(x1, xo, wts["w_xo"], wts["g_ffn"], wts["w_router"], wts["b_router"],
                                  h_all, h_first_row)
    return x2, h_all, route


def _invert_rows_kernel(dest_ref, pad_ref, nused_ref, inv_ref, *, n_assign, rows_cap, tm, trash_base, n_trash):
    def put(a, carry):
        inv_ref[dest_ref[a]] = a
        return carry

    lax.fori_loop(0, n_assign, put, 0, unroll=8)

    def fill(p, carry):
        inv_ref[p] = trash_base + (p & (n_trash - 1))
        return carry

    for e in range(N_EXPERTS):
        lax.fori_loop(pad_ref[e], pad_ref[N_EXPERTS + e], fill, 0)
    lax.fori_loop(nused_ref[0] * tm, rows_cap, fill, 0)


def _invert_rows(dest, pad_rows, nused, rows_cap, n_spare, trash_base, n_trash, tm=MOE_TILE):
    n_assign = dest.shape[0]
    smem = pl.BlockSpec(memory_space=pltpu.SMEM)
    return pl.pallas_call(
        functools.partial(_invert_rows_kernel, n_assign=n_assign, rows_cap=rows_cap, tm=tm,
                          trash_base=trash_base, n_trash=n_trash),
        in_specs=[smem, smem, smem],
        out_specs=smem,
        out_shape=jax.ShapeDtypeStruct((rows_cap + n_spare,), I32),
        name="invert_rows",
    )(dest, pad_rows, nused)


def _moe_direct_kernel(blk_e_ref, nused_ref, inv_ref, h_hbm, w1_hbm, w3_hbm, w2_hbm, y2_hbm,
                       xbuf, obuf, wf_ref, wb_ref, gsem, ssem, wsem, zsem, *, tm, n_tokens, zero_runs):
    i = pl.program_id(0)
    nused = nused_ref[0]
    slot = i % 2
    e = blk_e_ref[i]
    w_hbm = (w1_hbm, w3_hbm, w2_hbm)
    trash_base = EXP_TOPK * n_tokens
    thirds = [(0, tm // 3), (tm // 3, 2 * (tm // 3)), (2 * (tm // 3), tm)]

    def start_fetch(expert):
        for k in range(3):
            pltpu.make_async_copy(w_hbm[k].at[expert], wf_ref.at[k], wsem.at[k]).start()

    def gather_rows(blk, s, lo, hi):
        for r in range(lo, hi):
            a = inv_ref[blk * tm + r]
            tok = jnp.where(a >= n_tokens, a - n_tokens, a)
            tok = jnp.where(a >= trash_base, r, tok)
            pltpu.make_async_copy(h_hbm.at[pl.ds(tok, 1)], xbuf.at[s, pl.ds(r, 1)], gsem.at[s]).start(priority=0)

    def scatter_rows(blk, s, lo, hi):
        for r in range(lo, hi):
            pltpu.make_async_copy(obuf.at[s, pl.ds(r, 1)], y2_hbm.at[pl.ds(inv_ref[blk * tm + r], 1)],
                                  ssem.at[s]).start(priority=1)

    def wait_rows(buf, sem, s):
        if buf is xbuf:
            pltpu.make_async_copy(h_hbm.at[pl.ds(0, tm)], buf.at[s], sem.at[s]).wait()
        else:
            pltpu.make_async_copy(buf.at[s], y2_hbm.at[pl.ds(0, tm)], sem.at[s]).wait()

    @pl.when(i == 0)
    def _():
        start_fetch(e)
        gather_rows(0, 0, 0, tm)
        obuf[1] = jnp.zeros(obuf.shape[1:], obuf.dtype)
        for start, size in zero_runs:
            pltpu.make_async_copy(obuf.at[1, pl.ds(0, size)], y2_hbm.at[pl.ds(start, size)], zsem).start()
        for start, size in zero_runs:
            pltpu.make_async_copy(obuf.at[1, pl.ds(0, size)], y2_hbm.at[pl.ds(start, size)], zsem).wait()

    @pl.when((i < nused) & ((i == 0) | (e != blk_e_ref[jnp.maximum(i - 1, 0)])))
    def _():
        for k in range(3):
            pltpu.make_async_copy(w_hbm[k].at[e], wf_ref.at[k], wsem.at[k]).wait()
            wb_ref[k] = wf_ref[k].astype(BF16)
        nxt = lax.while_loop(lambda j: (j < nused) & (blk_e_ref[jnp.minimum(j, nused - 1)] == e),
                             lambda j: j + 1, i + 1)

        @pl.when(nxt < nused)
        def _():
            start_fetch(blk_e_ref[nxt])

    @pl.when((i >= 1) & (i < nused))
    def _():
        wait_rows(obuf, ssem, slot)

    @pl.when(i < nused)
    def _():
        wait_rows(xbuf, gsem, slot)
        nb = jnp.minimum(i + 1, nused - 1)
        pb = jnp.maximum(i - 1, 0)
        x = xbuf[slot].astype(BF16)
        gather_rows(nb, 1 - slot, *thirds[0])
        scatter_rows(pb, 1 - slot, *thirds[0])
        a = jnp.dot(x, wb_ref[0], preferred_element_type=F32)
        gather_rows(nb, 1 - slot, *thirds[1])
        scatter_rows(pb, 1 - slot, *thirds[1])
        b = jnp.dot(x, wb_ref[1], preferred_element_type=F32)
        gather_rows(nb, 1 - slot, *thirds[2])
        scatter_rows(pb, 1 - slot, *thirds[2])
        hmid = (_silu(a) * b).astype(BF16)
        obuf[slot] = jnp.dot(hmid, wb_ref[2], preferred_element_type=F32)

    @pl.when(i == nused - 1)
    def _():
        wait_rows(xbuf, gsem, 1 - slot)
        wait_rows(obuf, ssem, 1 - slot)
        scatter_rows(i, slot, 0, tm)
        wait_rows(obuf, ssem, slot)


def _moe_direct(h_all, blk_e, nused, inv, w1, w3, w2, n_tokens, zero_runs, n_trash, tm=MOE_TILE):
    nblk = blk_e.shape[0]
    d = h_all.shape[1]
    any_spec = pl.BlockSpec(memory_space=pl.ANY)
    grid_spec = pltpu.PrefetchScalarGridSpec(
        num_scalar_prefetch=3,
        grid=(nblk,),
        in_specs=[any_spec, any_spec, any_spec, any_spec],
        out_specs=any_spec,
        scratch_shapes=[pltpu.VMEM((2, tm, d), F32), pltpu.VMEM((2, tm, d), F32),
                        pltpu.VMEM((3, d, d), F32), pltpu.VMEM((3, d, d), BF16),
                        pltpu.SemaphoreType.DMA((2,)), pltpu.SemaphoreType.DMA((2,)),
                        pltpu.SemaphoreType.DMA((3,)), pltpu.SemaphoreType.DMA(())],
    )
    return pl.pallas_call(
        functools.partial(_moe_direct_kernel, tm=tm, n_tokens=n_tokens, zero_runs=zero_runs),
        grid_spec=grid_spec,
        out_shape=jax.ShapeDtypeStruct((EXP_TOPK * n_tokens + n_trash, d), F32),
        compiler_params=_cparams(("arbitrary",), 40),
        name="moe_direct",
    )(blk_e, nused, inv, h_all, w1, w3, w2)


def _combine_stream_kernel(x2_ref, r_ref, g_ref, ya_ref, yb_ref, y_ref):
    route = r_ref[...]
    y = ya_ref[...] * route[:, 2:3] + yb_ref[...] * route[:, 3:4]
    y_ref[...] = _rms(x2_ref[...] + y, g_ref[...])


def _combine_stream(x2, route, g_final, y2, first_tile, n_tokens, tm=ROW_TILE):
    rows, d = x2.shape
    slot_tiles = n_tokens // tm
    return pl.pallas_call(
        _combine_stream_kernel,
        grid=(rows // tm,),
        in_specs=[pl.BlockSpec((tm, d), lambda i: (i, 0)),
                  pl.BlockSpec((tm, LANES), lambda i: (i, 0)),
                  pl.BlockSpec((1, d), lambda i: (0, 0)),
                  pl.BlockSpec((tm, d), lambda i: (i + first_tile, 0)),
                  pl.BlockSpec((tm, d), lambda i: (i + first_tile + slot_tiles, 0))],
        out_specs=pl.BlockSpec((tm, d), lambda i: (i, 0)),
        out_shape=jax.ShapeDtypeStruct((rows, d), F32),
        compiler_params=_cparams(("parallel",), 40),
        name="combine_stream",
    )(x2, route, g_final.reshape(1, d), y2, y2)


def _moe_and_final(x2_p, route_p, x2_s, route_s, h_all, n_sample_tokens, t_pad, t_valid, wts):
    rp, d = x2_p.shape
    rs = x2_s.shape[0]
    n_tokens = rp + rs
    tm = MOE_TILE
    n_assign = (rp + n_sample_tokens) * EXP_TOPK
    nblk = -(-(n_assign + N_EXPERTS * (tm - 1)) // tm)
    dest, blk_e, nused, _, pad_rows = _route_dest(route_p, route_s, t_pad, t_valid, tm, nblk)
    n_spare = EXP_TOPK * (rs // t_pad) * (t_pad - t_valid)
    n_trash = 2 * tm
    inv = _invert_rows(dest, pad_rows, nused, nblk * tm, n_spare, EXP_TOPK * n_tokens, n_trash)
    zero_runs = [(k * n_tokens + rp + b * t_pad + t_valid, t_pad - t_valid)
                 for k in range(EXP_TOPK) for b in range(rs // t_pad)]
    zero_runs += [(EXP_TOPK * n_tokens + k * tm, tm) for k in range(n_trash // tm)]
    y2 = _moe_direct(h_all, blk_e, nused, inv, wts["w1"], wts["w3"], wts["w2"], n_tokens, zero_runs, n_trash)
    y_p = _combine_stream(x2_p, route_p, wts["g_final"], y2, 0, n_tokens)
    y_s = _combine_stream(x2_s, route_s, wts["g_final"], y2, rp // ROW_TILE, n_tokens)
    return y_p, y_s


def kernel(x_prompt, x_sample, state_hgrn, cache_moba_k, cache_moba_v, cache_mem_k, cache_mem_v, page_table,
           mem_prompt, g_mix, w_in, hg_lb, hg_norm, w_out, g_cross, g_mem, w_xq, w_xk, w_xv, w_xo, g_ffn,
           w_grp, b_grp, w_exp, b_exp, w1, w3, w2, g_final):
    depth = g_mix.shape[0]
    assert depth == 1
    bp, tp, d = x_prompt.shape
    bs, ts, _ = x_sample.shape
    mem_len = mem_prompt.shape[1]
    l = 0

    lb = jnp.cumsum(jax.nn.softmax(hg_lb.astype(F32), axis=0), axis=0)[l]
    slopes = jnp.asarray(np.power(2.0, -8.0 * np.arange(1, MB_HEADS + 1) / MB_HEADS).astype(np.float32))
    n_router = N_GROUPS + N_EXPERTS
    w_router = jnp.pad(jnp.concatenate([w_grp[l], w_exp[l]], axis=1), ((0, 0), (0, LANES - n_router)))
    b_router = jnp.pad(jnp.concatenate([b_grp[l], b_exp[l]]), (0, LANES - n_router)).reshape(1, LANES)
    wts = {
        "w_out": w_out[l].astype(BF16), "g_cross": g_cross[l], "w_xq": w_xq[l].astype(BF16),
        "w_xo": w_xo[l].astype(BF16), "g_ffn": g_ffn[l], "w_router": w_router.astype(BF16),
        "b_router": b_router.astype(F32), "w1": w1[l], "w3": w3[l], "w2": w2[l], "g_final": g_final,
    }
    w_in_bf = w_in[l].astype(BF16)
    o_hq, o_mq, o_mk, o_mv = 0, 4 * HG_WIDTH, 4 * HG_WIDTH + MB_WIDTH, 4 * HG_WIDTH + 2 * MB_WIDTH
    o_end = o_mv + MB_WIDTH

    xp = x_prompt.reshape(bp * tp, d)
    w_kv = jnp.concatenate([w_xk[l], w_xv[l]], axis=1).astype(BF16)
    memk_p, memv_p = _rms_proj(mem_prompt.reshape(bp * mem_len, d), g_mem[l], w_kv,
                               [(0, d, [(1.0, False)]), (d, 2 * d, [(1.0, False)])], [F32, F32])
    hp_p, qt_p, kt_p, vt_p = _rms_proj(
        xp, g_mix[l], w_in_bf,
        [(o_hq, o_mq, [(1.0, False)]), (o_mq, o_mk, [(MB_DIM ** -0.5, True)]), (o_mk, o_mv, [(1.0, "batched")]),
         (o_mv, o_end, [(1.0, "batched")])],
        [F32, BF16, F32, F32], seq_len=tp)
    o_p, s_p = _hgrn(hp_p, lb, hg_norm[l], None, bp, tp, tp, 128)
    att_p = _moba_prompt(qt_p, kt_p, vt_p, slopes, bp, tp)
    n_rows_all = bp * tp + bs * SAMPLE_PAD_T
    x2_p, h_all, route_p = _tail_fused(xp, o_p, att_p, memk_p, memv_p, bp, tp, wts, n_rows_all)

    tpad = SAMPLE_PAD_T
    xs = jnp.pad(x_sample, ((0, 0), (0, tpad - ts), (0, 0))).reshape(bs * tpad, d)
    hp_s, mq_s, mk_s, mv_s = _rms_proj(
        xs, g_mix[l], w_in_bf,
        [(o_hq, o_mq, [(1.0, False)]), (o_mq, o_mk, [(1.0, False)]), (o_mk, o_mv, [(1.0, False)]),
         (o_mv, o_end, [(1.0, False)])],
        [F32, F32, F32, F32])
    o_s, s_s = _hgrn(hp_s, lb, hg_norm[l], state_hgrn[l], bs, tpad, ts, tpad)
    mk4 = mk_s.reshape(bs, tpad, MB_HEADS, MB_DIM)
    mv4 = mv_s.reshape(bs, tpad, MB_HEADS, MB_DIM)
    n_phys, page = cache_moba_k.shape[1], cache_moba_k.shape[2]
    pool_kt = jnp.transpose(cache_moba_k[l], (0, 2, 3, 1)).reshape(n_phys, MB_WIDTH, page)
    pool_vt = jnp.transpose(cache_moba_v[l], (0, 2, 3, 1)).reshape(n_phys, MB_WIDTH, page)
    att_s = _moba_sample(mq_s, mk_s, mv_s, pool_kt, pool_vt, page_table, ts)

    def mem_rows(c):
        c = c[l].reshape(bs, mem_len, XA_HEADS, XA_DIM // LANES, LANES)
        return jnp.transpose(c, (0, 1, 3, 2, 4)).reshape(bs, mem_len * d // LANES, LANES)

    memk_s, memv_s = mem_rows(cache_mem_k), mem_rows(cache_mem_v)
    x2_s, h_all, route_s = _layer_tail(xs, o_s, att_s, False, memk_s, memv_s, bs, tpad, tpad, wts, h_all, bp * tp)
    y_p, y_s = _moe_and_final(x2_p, route_p, x2_s, route_s, h_all, bs * ts, tpad, ts, wts)

    return (y_p.reshape(bp, tp, d),
            y_s.reshape(bs, tpad, d)[:, :ts],
            s_p.reshape(1, bp, HG_HEADS, HG_DIM, HG_DIM),
            jnp.transpose(kt_p.reshape(1, bp, MB_HEADS, MB_DIM, tp), (0, 1, 4, 2, 3)),
            jnp.transpose(vt_p.reshape(1, bp, MB_HEADS, MB_DIM, tp), (0, 1, 4, 2, 3)),
            memk_p.reshape(1, bp, mem_len, XA_HEADS, XA_DIM),
            memv_p.reshape(1, bp, mem_len, XA_HEADS, XA_DIM),
            s_s.reshape(1, bs, HG_HEADS, HG_DIM, HG_DIM),
            mk4[:, :ts].reshape(1, bs, ts, MB_HEADS, MB_DIM),
            mv4[:, :ts].reshape(1, bs, ts, MB_HEADS, MB_DIM))
```

```python
import functools

import numpy as np
import jax
import jax.numpy as jnp
from jax import lax
from jax.experimental import pallas as pl
from jax.experimental.pallas import tpu as pltpu

F32 = jnp.float32
BF16 = jnp.bfloat16
I32 = jnp.int32

D_MODEL = 1024
HG_HEADS = 4
HG_DIM = 128
HG_WIDTH = HG_HEADS * HG_DIM
MB_HEADS = 8
MB_DIM = 64
MB_WIDTH = MB_HEADS * MB_DIM
MB_BLOCK = 256
MB_TOPK = 3
XA_HEADS = 4
XA_DIM = 256
N_GROUPS = 4
EXP_PER_GROUP = 8
N_EXPERTS = N_GROUPS * EXP_PER_GROUP
EXP_TOPK = 2
RMS_EPS = 1e-6
NEG_INF = float("-inf")

LANES = 128
SUBLANES = 8
MIB = 1024 * 1024

ROW_TILE = 256
MOE_TILE = 256
SAMPLE_PAD_T = 8
MOBA_PAIRS = 4
XATTN_SEQS_PER_STEP = 4
HGRN_SEQS_PER_STEP = 2

_NT = (((1,), (1,)), ((), ()))


def _cparams(semantics, vmem_mib):
    return pltpu.CompilerParams(dimension_semantics=semantics, vmem_limit_bytes=vmem_mib * MIB)


def _rms(x, g):
    return x * lax.rsqrt(jnp.mean(x * x, axis=-1, keepdims=True) + RMS_EPS) * g


def _sigmoid(x):
    return 0.5 * jnp.tanh(0.5 * x) + 0.5


def _silu(x):
    return x * _sigmoid(x)


def _rms_proj_kernel(x_ref, g_ref, w_ref, *o_refs, segs):
    h = _rms(x_ref[...], g_ref[...]).astype(BF16)
    k = 0
    for lo, hi, outs in segs:
        r = jnp.dot(h, w_ref[:, lo:hi], preferred_element_type=F32)
        for scale, transposed in outs:
            y = r if scale == 1.0 else r * scale
            if transposed:
                y = y.T
            o_refs[k][...] = y.astype(o_refs[k].dtype)
            k += 1


def _rms_proj(x, g, w_bf, segs, dtypes, tm=ROW_TILE, vmem_mib=48, seq_len=None):
    rows, d = x.shape
    assert rows % tm == 0
    out_shape, out_specs = [], []
    k = 0
    for lo, hi, outs in segs:
        n = hi - lo
        for _, transposed in outs:
            if transposed == "batched":
                per_seq = seq_len // tm
                out_shape.append(jax.ShapeDtypeStruct((rows // seq_len, n, seq_len), dtypes[k]))
                out_specs.append(pl.BlockSpec((None, n, tm), lambda i: (i // per_seq, 0, i % per_seq)))
            elif transposed:
                out_shape.append(jax.ShapeDtypeStruct((n, rows), dtypes[k]))
                out_specs.append(pl.BlockSpec((n, tm), lambda i: (0, i)))
            else:
                out_shape.append(jax.ShapeDtypeStruct((rows, n), dtypes[k]))
                out_specs.append(pl.BlockSpec((tm, n), lambda i: (i, 0)))
            k += 1
    return pl.pallas_call(
        functools.partial(_rms_proj_kernel, segs=segs),
        grid=(rows // tm,),
        in_specs=[pl.BlockSpec((tm, d), lambda i: (i, 0)),
                  pl.BlockSpec((1, d), lambda i: (0, 0)),
                  pl.BlockSpec(w_bf.shape, lambda i: (0, 0))],
        out_specs=out_specs,
        out_shape=out_shape,
        compiler_params=_cparams(("parallel",), vmem_mib),
        name="rms_proj",
    )(x, g.reshape(1, d), w_bf)


def _cumsum_rows(x):
    n = x.shape[0]
    row = lax.broadcasted_iota(I32, x.shape, 0)
    s = 1
    while s < n:
        x = x + jnp.where(row >= s, pltpu.roll(x, s, 0), 0.0)
        s *= 2
    return x


def _hgrn_intra(q, kk, b, c, w):
    row = lax.broadcasted_iota(I32, (c, HG_DIM), 0)
    acc = None
    half = c // 2
    while half >= SUBLANES:
        two = 2 * half
        nblk = c // two
        pieces = [jnp.broadcast_to(b[i * two + half - 1:i * two + half, :], (two, HG_DIM)) for i in range(nblk)]
        bm = pieces[0] if nblk == 1 else jnp.concatenate(pieces, axis=0)
        second = (row & (two - 1)) >= half
        qt = jnp.where(second, q * jnp.exp(jnp.where(second, b - bm, 0.0)), 0.0)
        kt = jnp.where(second, 0.0, kk * jnp.exp(jnp.where(second, 0.0, bm - b)))
        al = lax.dot_general(qt.astype(BF16), kt.astype(BF16), _NT, preferred_element_type=F32)
        if nblk > 1:
            shift = two.bit_length() - 1
            rt = lax.broadcasted_iota(I32, (c, c), 0) >> shift
            cs = lax.broadcasted_iota(I32, (c, c), 1) >> shift
            al = jnp.where(rt == cs, al, 0.0)
        acc = al if acc is None else acc + al
        half //= 2
    r8 = lax.broadcasted_iota(I32, (SUBLANES, HG_DIM), 0)
    lane = lax.broadcasted_iota(I32, (SUBLANES, w), 1)
    blocks = []
    for g in range(c // SUBLANES):
        lo = g * SUBLANES
        qg, kg, bg = q[lo:lo + SUBLANES], kk[lo:lo + SUBLANES], b[lo:lo + SUBLANES]
        ag = jnp.zeros((SUBLANES, w), F32)
        for s in range(SUBLANES):
            e = jnp.exp(jnp.where(r8 >= s, bg - bg[s:s + 1, :], NEG_INF))
            p = jnp.sum(qg * kg[s:s + 1, :] * e, axis=1, keepdims=True)
            ag = jnp.where(lane == lo + s, p, ag)
        blocks.append(ag)
    diag = blocks[0] if len(blocks) == 1 else jnp.concatenate(blocks, axis=0)
    return diag if acc is None else acc + diag


def _hgrn_kernel(*refs, c, t_valid, has_s0, nseq):
    if has_s0:
        hq_ref, hf_ref, hig_ref, lb_ref, gn_ref, s0_ref, o_ref, sout_ref, st_ref = refs
    else:
        hq_ref, hf_ref, hig_ref, lb_ref, gn_ref, o_ref, sout_ref, st_ref = refs
    ci = pl.program_id(1)
    last = pl.num_programs(1) - 1
    w = max(c, LANES)

    outs = []
    for g in range(nseq):
        @pl.when(ci == 0)
        def _():
            for h in range(HG_HEADS):
                st_ref[h] = s0_ref[g, h].T if has_s0 else jnp.zeros((HG_DIM, HG_DIM), F32)

        outs.append(_hgrn_chunk(hq_ref, hf_ref, hig_ref, lb_ref, gn_ref, sout_ref, st_ref,
                                g, ci == last, c=c, t_valid=t_valid, w=w))
    for h in range(HG_HEADS):
        oh = outs[0][h] if nseq == 1 else jnp.concatenate([og[h] for og in outs], axis=0)
        o_ref[:, h * HG_DIM:(h + 1) * HG_DIM] = oh.astype(o_ref.dtype)


def _hgrn_chunk(hq_ref, hf_ref, hig_ref, lb_ref, gn_ref, sout_ref, st_ref, g, is_last, *, c, t_valid, w):
    row = lax.broadcasted_iota(I32, (c, HG_DIM), 0)
    r0 = g * c
    outs = []
    for h in range(HG_HEADS):
        lo = h * HG_DIM
        hq = hq_ref[r0:r0 + c, lo:lo + HG_DIM].astype(F32)
        hf = hf_ref[r0:r0 + c, lo:lo + HG_DIM]
        v = hig_ref[r0:r0 + c, lo:lo + HG_DIM].astype(F32)
        hg = hig_ref[r0:r0 + c, HG_WIDTH + lo:HG_WIDTH + lo + HG_DIM].astype(F32)
        lb = lb_ref[:, lo:lo + HG_DIM]
        q = _silu(hq)
        f = lb + (1.0 - lb) * _sigmoid(hf)
        logf = jnp.log(f)
        kk = 1.0 - f
        if t_valid < c:
            valid = row < t_valid
            logf = jnp.where(valid, logf, 0.0)
            kk = jnp.where(valid, kk, 0.0)
            v = jnp.where(valid, v, 0.0)
        b = _cumsum_rows(logf)
        a = _hgrn_intra(q, kk, b, c, w)
        st = st_ref[h]
        bl = b[c - 1:c, :]
        k2 = kk * jnp.exp(bl - b)
        if c < w:
            zpad = jnp.zeros((w - c, HG_DIM), F32)
            vp = jnp.concatenate([v, zpad], axis=0)
            k2 = jnp.concatenate([k2, zpad], axis=0)
        else:
            vp = v
        vb = vp.astype(BF16)
        o = lax.dot_general((q * jnp.exp(b)).astype(BF16), st.astype(BF16), _NT, preferred_element_type=F32)
        o = o + jnp.dot(a.astype(BF16), vb, preferred_element_type=F32)
        outs.append(_rms(o, gn_ref[...]) * _silu(hg))
        st_new = st * jnp.exp(bl) + jnp.dot(vp.T.astype(BF16), k2.astype(BF16), preferred_element_type=F32)
        st_ref[h] = st_new

        @pl.when(is_last)
        def _():
            sout_ref[g, h] = st_new.T
    return outs


def _hgrn(hq, hf, hig, lb, gn, s0, batch, t_pad, t_valid, c, nseq=1):
    rows = hq.shape[0]
    nc = t_pad // c
    assert batch % nseq == 0 and (nseq == 1 or nc == 1)
    has_s0 = s0 is not None
    rb = nseq * c

    def rows_spec(width):
        return pl.BlockSpec((rb, width), lambda b, i: (b * nc + i, 0))

    in_specs = [rows_spec(HG_WIDTH), rows_spec(HG_WIDTH), rows_spec(2 * HG_WIDTH),
                pl.BlockSpec((1, HG_WIDTH), lambda b, i: (0, 0)),
                pl.BlockSpec((1, HG_DIM), lambda b, i: (0, 0))]
    args = [hq, hf, hig, lb.reshape(1, HG_WIDTH), gn.reshape(1, HG_DIM)]
    state_spec = pl.BlockSpec((nseq, HG_HEADS, HG_DIM, HG_DIM), lambda b, i: (b, 0, 0, 0))
    if has_s0:
        in_specs.append(state_spec)
        args.append(s0)
    return pl.pallas_call(
        functools.partial(_hgrn_kernel, c=c, t_valid=min(t_valid, c), has_s0=has_s0, nseq=nseq),
        grid=(batch // nseq, nc),
        in_specs=in_specs,
        out_specs=[rows_spec(HG_WIDTH), state_spec],
        out_shape=[jax.ShapeDtypeStruct((rows, HG_WIDTH), BF16),
                   jax.ShapeDtypeStruct((batch, HG_HEADS, HG_DIM, HG_DIM), F32)],
        scratch_shapes=[pltpu.VMEM((HG_HEADS, HG_DIM, HG_DIM), F32)],
        compiler_params=_cparams(("parallel", "arbitrary"), 32),
        name="hgrn2",
    )(*args)


def _topk_block_mask(gate, n_valid, nrow):
    nb = gate.shape[0]
    cnt = jnp.zeros(gate.shape, I32)
    for n2 in range(nb):
        g2 = gate[n2:n2 + 1, :]
        beats = (g2 > gate) | ((g2 == gate) & (n2 < nrow))
        cnt = cnt + jnp.where(beats, (n_valid > n2).astype(I32), 0)
    return (cnt < MB_TOPK) & (nrow < n_valid)


def _moba_prompt_kernel(slope_ref, qt_ref, kt_ref, vt_ref, o_ref, kb_ref, vb_ref, km_ref, sel_ref, *, nb, npairs):
    hg = pl.program_id(1)
    qi = pl.program_id(2)
    tq = MB_BLOCK
    pw = 2 * tq
    width = npairs * pw
    pair_rows = 2 * MB_DIM

    @pl.when(qi == 0)
    def _():
        pos = jnp.where(lax.broadcasted_iota(I32, (MB_BLOCK, pair_rows), 1) == 0,
                        lax.broadcasted_iota(I32, (MB_BLOCK, pair_rows), 0), 0).astype(BF16)
        for pp in range(npairs):
            means = []
            for j in range(nb):
                kj = kt_ref[pp * pair_rows:(pp + 1) * pair_rows, j * MB_BLOCK:(j + 1) * MB_BLOCK].T
                kb_ref[pp, j * MB_BLOCK:(j + 1) * MB_BLOCK, :pair_rows] = kj.astype(BF16)
                kb_ref[pp, j * MB_BLOCK:(j + 1) * MB_BLOCK, pair_rows:] = pos
                means.append(jnp.sum(kj, axis=0, keepdims=True) * (1.0 / MB_BLOCK))
                vb_ref[pp, j] = vt_ref[pp * pair_rows:(pp + 1) * pair_rows,
                                       j * MB_BLOCK:(j + 1) * MB_BLOCK].astype(BF16)
            km_ref[pp] = jnp.concatenate(means, axis=0)

    second_head = lax.broadcasted_iota(I32, (pair_rows, tq), 0) >= MB_DIM
    first_row = lax.broadcasted_iota(I32, (pair_rows, pw), 0) == 0
    qaugs, gates, slope_parts = [], [], []
    for pp in range(npairs):
        qt2 = qt_ref[pp * pair_rows:(pp + 1) * pair_rows, :]
        zero = jnp.zeros_like(qt2)
        qcat = jnp.concatenate([jnp.where(second_head, zero, qt2), jnp.where(second_head, qt2, zero)], axis=1)
        gates.append(jnp.dot(km_ref[pp].astype(BF16), qcat, preferred_element_type=F32))
        pair_slopes = jnp.concatenate(
            [jnp.full((1, tq), slope_ref[(hg * npairs + pp) * 2 + i], F32) for i in range(2)], axis=1)
        slope_parts.append(pair_slopes)
        srows = jnp.where(first_row, pair_slopes, 0.0).astype(BF16)
        qaugs.append(jnp.concatenate([qcat, srows], axis=0))
    slope_row = slope_parts[0] if npairs == 1 else jnp.concatenate(slope_parts, axis=1)
    gate = gates[0] if npairs == 1 else jnp.concatenate(gates, axis=1)
    nrow = lax.broadcasted_iota(I32, (nb, width), 0)
    sel = _topk_block_mask(gate, qi, nrow).astype(F32)
    for n in range(nb):
        sel_ref[n] = jnp.broadcast_to(sel[n:n + 1, :], (SUBLANES, width))

    def scores(j):
        start = pl.multiple_of(j * MB_BLOCK, MB_BLOCK)
        parts = [jnp.dot(kb_ref[pp, pl.ds(start, MB_BLOCK), :], qaugs[pp], preferred_element_type=F32)
                 for pp in range(npairs)]
        return parts[0] if npairs == 1 else jnp.concatenate(parts, axis=1)

    def values(j, p):
        pb = p.astype(BF16)
        return [jnp.dot(vb_ref[pp, j], pb[:, pp * pw:(pp + 1) * pw], preferred_element_type=F32)
                for pp in range(npairs)]

    lane = lax.broadcasted_iota(I32, (MB_BLOCK, width), 1) & (tq - 1)
    rowk = lax.broadcasted_iota(I32, (MB_BLOCK, width), 0)
    s = jnp.where(rowk <= lane, scores(qi), NEG_INF)
    m = jnp.max(s, axis=0, keepdims=True)
    p = jnp.exp(s - m)
    l = jnp.sum(p, axis=0, keepdims=True)
    accs = values(qi, p)

    def body(j, carry):
        m, l, accs = carry
        off = jnp.full((1, width), (qi - j) * MB_BLOCK, I32).astype(F32) * slope_row
        picked = sel_ref[j][0:1, :] > 0.5
        s = scores(j)
        m_new = jnp.maximum(m, jnp.where(picked, jnp.max(s, axis=0, keepdims=True) - off, NEG_INF))
        alpha = jnp.exp(m - m_new)
        p = jnp.exp(s - jnp.where(picked, m_new + off, float("inf")))
        l = alpha * l + jnp.sum(p, axis=0, keepdims=True)
        pv = values(j, p)
        accs = [alpha[:, pp * pw:(pp + 1) * pw] * accs[pp] + pv[pp] for pp in range(npairs)]
        return m_new, l, accs

    m, l, accs = lax.fori_loop(0, qi, body, (m, l, accs))
    for pp in range(npairs):
        a = accs[pp] / l[:, pp * pw:(pp + 1) * pw]
        o_ref[pp * pair_rows:(pp + 1) * pair_rows, :] = jnp.where(second_head, a[:, tq:], a[:, :tq])


def _moba_prompt(qt, kt, vt, slopes, batch, t, npairs=MOBA_PAIRS):
    nb = t // MB_BLOCK
    rows = batch * t
    gr = 2 * MB_DIM * npairs
    width = npairs * 2 * MB_BLOCK
    return pl.pallas_call(
        functools.partial(_moba_prompt_kernel, nb=nb, npairs=npairs),
        grid=(batch, MB_WIDTH // gr, nb),
        in_specs=[pl.BlockSpec(memory_space=pltpu.SMEM),
                  pl.BlockSpec((gr, MB_BLOCK), lambda b, h, i: (h, b * nb + i)),
                  pl.BlockSpec((None, gr, t), lambda b, h, i: (b, h, 0)),
                  pl.BlockSpec((None, gr, t), lambda b, h, i: (b, h, 0))],
        out_specs=pl.BlockSpec((gr, MB_BLOCK), lambda b, h, i: (h, b * nb + i)),
        out_shape=jax.ShapeDtypeStruct((MB_WIDTH, rows), F32),
        scratch_shapes=[pltpu.VMEM((npairs, t, 4 * MB_DIM), BF16),
                        pltpu.VMEM((npairs, nb, 2 * MB_DIM, MB_BLOCK), BF16),
                        pltpu.VMEM((npairs, nb, 2 * MB_DIM), F32),
                        pltpu.VMEM((nb, SUBLANES, width), F32)],
        compiler_params=_cparams(("parallel", "parallel", "arbitrary"), 48),
        name="moba_prompt",
    )(slopes, qt, kt, vt)


def _moba_sample_kernel(pt_ref, slope_ref, hm_ref, q_ref, kn_ref, vn_ref, *rest, npages, page, t_new, past_len):
    del pt_ref
    k_pages = rest[:npages]
    v_pages = rest[npages:2 * npages]
    o_ref = rest[2 * npages]
    nrows = t_new * MB_HEADS
    pages_per_block = MB_BLOCK // page
    nb = npages // pages_per_block
    hm = hm_ref[...]
    slope = slope_ref[...][:, 0:1]
    row_q = lax.broadcasted_iota(I32, (nrows, 1), 0) >> 3
    pos_q = past_len + row_q

    q = q_ref[...] * (MB_DIM ** -0.5)
    qbd = jnp.concatenate([jnp.broadcast_to(q[t:t + 1, :], (MB_HEADS, MB_WIDTH)) * hm for t in range(t_new)],
                          axis=0).astype(BF16)

    lane = lax.broadcasted_iota(I32, (nrows, page), 1)
    scores = []
    gates = [None] * nb
    for p in range(npages):
        s = jnp.dot(qbd, k_pages[p][0].astype(BF16), preferred_element_type=F32)
        rs = jnp.sum(s, axis=1, keepdims=True)
        n = p // pages_per_block
        gates[n] = rs if gates[n] is None else gates[n] + rs
        scores.append(s - slope * (pos_q - (p * page + lane)).astype(F32))
    sels = []
    for n in range(nb):
        cnt = jnp.zeros((nrows, 1), I32)
        for n2 in range(nb):
            beats = (gates[n2] > gates[n]) | ((gates[n2] == gates[n]) & (n2 < n))
            cnt = cnt + jnp.where(beats, 1, 0)
        sels.append(jnp.where(cnt < MB_TOPK, 1.0, 0.0))

    zrows = jnp.zeros((LANES - SAMPLE_PAD_T, MB_WIDTH), F32)
    knp = jnp.concatenate([kn_ref[...], zrows], axis=0).astype(BF16)
    vnp = jnp.concatenate([vn_ref[...], zrows], axis=0).astype(BF16)
    lane_n = lax.broadcasted_iota(I32, (nrows, LANES), 1)
    s_new = lax.dot_general(qbd, knp, _NT, preferred_element_type=F32)
    s_new = jnp.where(lane_n <= row_q, s_new - slope * (row_q - lane_n).astype(F32), NEG_INF)
    m = jnp.max(s_new, axis=1, keepdims=True)
    for p in range(npages):
        sel_p = jnp.broadcast_to(sels[p // pages_per_block], (nrows, page)) > 0.5
        scores[p] = jnp.where(sel_p, scores[p], NEG_INF)
        m = jnp.maximum(m, jnp.max(scores[p], axis=1, keepdims=True))

    p_new = jnp.exp(s_new - m)
    l = jnp.sum(p_new, axis=1, keepdims=True)
    acc = jnp.dot(p_new.astype(BF16), vnp, preferred_element_type=F32)
    for p in range(npages):
        pr = jnp.exp(scores[p] - m)
        l = l + jnp.sum(pr, axis=1, keepdims=True)
        acc = acc + lax.dot_general(pr.astype(BF16), v_pages[p][0].astype(BF16), _NT,
                                    preferred_element_type=F32)
    acc = acc / l
    rows = [jnp.sum(acc[t * MB_HEADS:(t + 1) * MB_HEADS, :] * hm, axis=0, keepdims=True) for t in range(t_new)]
    rows.append(jnp.zeros((SAMPLE_PAD_T - t_new, MB_WIDTH), F32))
    o_ref[...] = jnp.concatenate(rows, axis=0)


def _moba_sample(mq, mk, mv, pool_kt, pool_vt, page_table, t_new):
    batch, npages = page_table.shape
    page = pool_kt.shape[2]
    past_len = npages * page
    assert past_len % MB_BLOCK == 0 and MB_BLOCK % page == 0 and t_new <= SAMPLE_PAD_T and page == LANES
    nrows = t_new * MB_HEADS
    slopes = np.power(2.0, -8.0 * np.arange(1, MB_HEADS + 1) / MB_HEADS).astype(np.float32)
    slope_rows = jnp.asarray(np.tile(np.tile(slopes, t_new)[:, None], (1, LANES)))
    head_mask = jnp.asarray((np.arange(MB_WIDTH)[None, :] // MB_DIM == np.arange(MB_HEADS)[:, None])
                            .astype(np.float32))
    new_spec = pl.BlockSpec((SAMPLE_PAD_T, MB_WIDTH), lambda b, pt: (b, 0))
    page_specs = [pl.BlockSpec((1, MB_WIDTH, page),
                               functools.partial(lambda b, pt, i: (pt[b * npages + i], 0, 0), i=i))
                  for i in range(npages)]
    grid_spec = pltpu.PrefetchScalarGridSpec(
        num_scalar_prefetch=1,
        grid=(batch,),
        in_specs=[pl.BlockSpec((nrows, LANES), lambda b, pt: (0, 0)),
                  pl.BlockSpec((MB_HEADS, MB_WIDTH), lambda b, pt: (0, 0)),
                  new_spec, new_spec, new_spec] + page_specs + page_specs,
        out_specs=new_spec,
    )
    return pl.pallas_call(
        functools.partial(_moba_sample_kernel, npages=npages, page=page, t_new=t_new, past_len=past_len),
        grid_spec=grid_spec,
        out_shape=jax.ShapeDtypeStruct((batch * SAMPLE_PAD_T, MB_WIDTH), F32),
        compiler_params=_cparams(("parallel",), 40),
        name="moba_sample",
    )(page_table.reshape(-1), slope_rows, head_mask, mq, mk, mv, *([pool_kt] * npages), *([pool_vt] * npages))


def _outproj_math(x, o, att, wo_ref, g, wq_ref):
    mix = jnp.dot(o, wo_ref[:HG_WIDTH, :], preferred_element_type=F32)
    mix = mix + jnp.dot(att.astype(BF16), wo_ref[HG_WIDTH:, :], preferred_element_type=F32)
    x1 = x + mix
    h = _rms(x1, g).astype(BF16)
    return x1, jnp.dot(h, wq_ref[...], preferred_element_type=F32) * (XA_DIM ** -0.5)


def _outproj_kernel(x_ref, o_ref, att_ref, wo_ref, g_ref, wq_ref, x1_ref, q_ref, *, att_transposed):
    att = att_ref[...]
    if att_transposed:
        att = att.T
    x1, q = _outproj_math(x_ref[...], o_ref[...], att, wo_ref, g_ref[...], wq_ref)
    x1_ref[...] = x1
    q_ref[...] = q.astype(q_ref.dtype)


def _outproj(x, o, att, wo_bf, g, wq_bf, att_transposed, q_dtype, tm=ROW_TILE):
    rows, d = x.shape
    att_spec = (pl.BlockSpec((MB_WIDTH, tm), lambda i: (0, i)) if att_transposed
                else pl.BlockSpec((tm, MB_WIDTH), lambda i: (i, 0)))
    return pl.pallas_call(
        functools.partial(_outproj_kernel, att_transposed=att_transposed),
        grid=(rows // tm,),
        in_specs=[pl.BlockSpec((tm, d), lambda i: (i, 0)),
                  pl.BlockSpec((tm, HG_WIDTH), lambda i: (i, 0)),
                  att_spec,
                  pl.BlockSpec((d, d), lambda i: (0, 0)),
                  pl.BlockSpec((1, d), lambda i: (0, 0)),
                  pl.BlockSpec((d, d), lambda i: (0, 0))],
        out_specs=[pl.BlockSpec((tm, d), lambda i: (i, 0)), pl.BlockSpec((tm, d), lambda i: (i, 0))],
        out_shape=[jax.ShapeDtypeStruct((rows, d), F32), jax.ShapeDtypeStruct((rows, d), q_dtype)],
        compiler_params=_cparams(("parallel",), 40),
        name="outproj_q",
    )(x, o, att, wo_bf, g.reshape(1, d), wq_bf)


def _xattn_tiled_kernel(q_ref, k_ref, v_ref, o_ref, *, nt):
    for g in range(k_ref.shape[0]):
        o_ref[g * nt:(g + 1) * nt, :] = _xattn_tiled_one(
            q_ref[g * nt:(g + 1) * nt, :], k_ref[g], v_ref[g], nt).astype(o_ref.dtype)


def _xattn_tiled_one(q, k, v, nt):
    ndt = XA_DIM // LANES
    nc = XA_HEADS * ndt
    half = XA_HEADS * nt
    q = q.astype(F32)
    a = jnp.concatenate([q[:, (h * ndt + dt) * LANES:(h * ndt + dt + 1) * LANES]
                         for dt in range(ndt) for h in range(XA_HEADS)], axis=0).astype(BF16)
    s = lax.dot_general(a, k.astype(BF16), _NT, preferred_element_type=F32)
    width = s.shape[1]
    lane_c = lax.broadcasted_iota(I32, s.shape, 1) & (nc - 1)
    assert nt & (nt - 1) == 0 and nc & (nc - 1) == 0
    row_c = lax.broadcasted_iota(I32, s.shape, 0) >> (nt.bit_length() - 1)
    s = jnp.where(lane_c == row_c, s, 0.0)
    tot = s[:half]
    for dt in range(1, ndt):
        tot = tot + pltpu.roll(s[dt * half:(dt + 1) * half], width - dt * XA_HEADS, 1)
    ok = ((lax.broadcasted_iota(I32, tot.shape, 1) & (nc - 1))
          == (lax.broadcasted_iota(I32, tot.shape, 0) >> (nt.bit_length() - 1)))
    tot = jnp.where(ok, tot, NEG_INF)
    p = jnp.exp(tot - jnp.max(tot, axis=-1, keepdims=True))
    p = p / jnp.sum(p, axis=-1, keepdims=True)
    pe = jnp.concatenate([p] + [pltpu.roll(p, dt * XA_HEADS, 1) for dt in range(1, ndt)], axis=0)
    o = jnp.dot(pe.astype(BF16), v.astype(BF16), preferred_element_type=F32)
    return jnp.concatenate([o[(dt * XA_HEADS + h) * nt:(dt * XA_HEADS + h + 1) * nt, :]
                            for h in range(XA_HEADS) for dt in range(ndt)], axis=1)


def _xattn_math(q, k_ref, v_ref):
    outs = []
    for h in range(XA_HEADS):
        lo = h * XA_DIM
        kh = k_ref[:, lo:lo + XA_DIM].astype(BF16)
        vh = v_ref[:, lo:lo + XA_DIM].astype(BF16)
        s = lax.dot_general(q[:, lo:lo + XA_DIM], kh, _NT, preferred_element_type=F32)
        p = jnp.exp(s - jnp.max(s, axis=-1, keepdims=True))
        p = p / jnp.sum(p, axis=-1, keepdims=True)
        outs.append(jnp.dot(p.astype(BF16), vh, preferred_element_type=F32))
    return outs


def _xattn_kernel(q_ref, k_ref, v_ref, o_ref):
    outs = _xattn_math(q_ref[...].astype(BF16), k_ref, v_ref)
    for h in range(XA_HEADS):
        o_ref[:, h * XA_DIM:(h + 1) * XA_DIM] = outs[h].astype(o_ref.dtype)


def _xattn(q, mem_k, mem_v, batch, t, tq):
    rows, d = q.shape
    nq = t // tq
    if mem_k.ndim == 3:
        assert tq == t and batch % XATTN_SEQS_PER_STEP == 0
        g = XATTN_SEQS_PER_STEP
        kv_spec = pl.BlockSpec((g,) + mem_k.shape[1:], lambda b: (b, 0, 0))
        return pl.pallas_call(
            functools.partial(_xattn_tiled_kernel, nt=t),
            grid=(batch // g,),
            in_specs=[pl.BlockSpec((g * t, d), lambda b: (b, 0)), kv_spec, kv_spec],
            out_specs=pl.BlockSpec((g * t, d), lambda b: (b, 0)),
            out_shape=jax.ShapeDtypeStruct((rows, d), q.dtype),
            compiler_params=_cparams(("parallel",), 40),
            name="xattn_tiled",
        )(q, mem_k, mem_v)
    m = mem_k.shape[0] // batch
    kv_spec = pl.BlockSpec((m, d), lambda b, i: (b, 0))
    return pl.pallas_call(
        _xattn_kernel,
        grid=(batch, nq),
        in_specs=[pl.BlockSpec((tq, d), lambda b, i: (b * nq + i, 0)), kv_spec, kv_spec],
        out_specs=pl.BlockSpec((tq, d), lambda b, i: (b * nq + i, 0)),
        out_shape=jax.ShapeDtypeStruct((rows, d), q.dtype),
        compiler_params=_cparams(("parallel", "arbitrary"), 32),
        name="xattn",
    )(q, mem_k, mem_v)


def _xo_router_kernel(x1_ref, o_ref, wo_ref, g_ref, wr_ref, br_ref, x2_ref, h_ref, r_ref):
    x2 = x1_ref[...] + jnp.dot(o_ref[...].astype(BF16), wo_ref[...], preferred_element_type=F32)
    x2_ref[...] = x2
    h = _rms(x2, g_ref[...])
    h_ref[...] = h
    r_ref[...] = _router_math(h, wr_ref, br_ref)


def _router_math(h, wr_ref, br_ref):
    logits = jnp.dot(h.astype(BF16), wr_ref[...], preferred_element_type=F32) + br_ref[...]
    lane = lax.broadcasted_iota(I32, logits.shape, 1)
    big = jnp.int32(LANES)

    def top1(mask):
        mx = jnp.max(jnp.where(mask, logits, NEG_INF), axis=-1, keepdims=True)
        idx = jnp.min(jnp.where(mask & (logits == mx), lane, big), axis=-1, keepdims=True)
        return mx, idx

    gmask = lane < N_GROUPS
    gmx, gsel = top1(gmask)
    gw = 1.0 / jnp.sum(jnp.where(gmask, jnp.exp(logits - gmx), 0.0), axis=-1, keepdims=True)
    elo = N_GROUPS + gsel * EXP_PER_GROUP
    emask = (lane >= elo) & (lane < elo + EXP_PER_GROUP)
    m1, i1 = top1(emask)
    m2, i2 = top1(emask & (lane != i1))
    e2 = jnp.exp(m2 - m1)
    g1 = gw / (1.0 + e2)
    g2 = gw * e2 / (1.0 + e2)
    out = jnp.where(lane == 0, (i1 - N_GROUPS).astype(F32), 0.0)
    out = jnp.where(lane == 1, (i2 - N_GROUPS).astype(F32), out)
    out = jnp.where(lane == 2, g1, out)
    return jnp.where(lane == 3, g2, out)


def _tail_fused_kernel(x_ref, o_ref, att_ref, wo_ref, gc_ref, wq_ref, k_ref, v_ref, wxo_ref, gf_ref, wr_ref,
                       br_ref, x2_ref, h_ref, r_ref, *, n_tiles):
    i = pl.program_id(0)

    @pl.when(i < n_tiles)
    def _():
        x1, q = _outproj_math(x_ref[...], o_ref[...], att_ref[...].T, wo_ref, gc_ref[...], wq_ref)
        outs = _xattn_math(q.astype(BF16), k_ref, v_ref)
        x2 = x1
        for h in range(XA_HEADS):
            x2 = x2 + jnp.dot(outs[h].astype(BF16), wxo_ref[h * XA_DIM:(h + 1) * XA_DIM, :],
                              preferred_element_type=F32)
        x2_ref[...] = x2
        hn = _rms(x2, gf_ref[...])
        h_ref[...] = hn
        r_ref[...] = _router_math(hn, wr_ref, br_ref)

    @pl.when(i >= n_tiles)
    def _():
        h_ref[...] = jnp.zeros(h_ref.shape, h_ref.dtype)


def _tail_fused(x, o, att_t, mem_k, mem_v, batch, t, wts, h_rows_total, tm=512):
    rows, d = x.shape
    nq = t // tm
    n_tiles = rows // tm
    n_extra = (h_rows_total - rows) // tm
    assert (h_rows_total - rows) % tm == 0
    m = mem_k.shape[0] // batch

    def tile(i):
        return jnp.minimum(i, n_tiles - 1)

    row_spec = pl.BlockSpec((tm, d), lambda i: (tile(i), 0))
    w_spec = pl.BlockSpec((d, d), lambda i: (0, 0))
    g_spec = pl.BlockSpec((1, d), lambda i: (0, 0))
    kv_spec = pl.BlockSpec((m, d), lambda i: (tile(i) // nq, 0))
    return pl.pallas_call(
        functools.partial(_tail_fused_kernel, n_tiles=n_tiles),
        grid=(n_tiles + n_extra,),
        in_specs=[row_spec,
                  pl.BlockSpec((tm, HG_WIDTH), lambda i: (tile(i), 0)),
                  pl.BlockSpec((MB_WIDTH, tm), lambda i: (0, tile(i))),
                  w_spec, g_spec, w_spec, kv_spec, kv_spec, w_spec, g_spec,
                  pl.BlockSpec((d, LANES), lambda i: (0, 0)),
                  pl.BlockSpec((1, LANES), lambda i: (0, 0))],
        out_specs=[row_spec, pl.BlockSpec((tm, d), lambda i: (i, 0)),
                   pl.BlockSpec((tm, LANES), lambda i: (tile(i), 0))],
        out_shape=[jax.ShapeDtypeStruct((rows, d), F32),
                   jax.ShapeDtypeStruct((h_rows_total, d), F32),
                   jax.ShapeDtypeStruct((rows, LANES), F32)],
        compiler_params=_cparams(("arbitrary",), 56),
        name="tail_fused",
    )(x, o, att_t, wts["w_out"], wts["g_cross"].reshape(1, d), wts["w_xq"], mem_k, mem_v, wts["w_xo"],
      wts["g_ffn"].reshape(1, d), wts["w_router"], wts["b_router"])


def _xo_router(x1, o, wxo_bf, g, wr_bf, br, tm=ROW_TILE):
    rows, d = x1.shape
    return pl.pallas_call(
        _xo_router_kernel,
        grid=(rows // tm,),
        in_specs=[pl.BlockSpec((tm, d), lambda i: (i, 0)),
                  pl.BlockSpec((tm, d), lambda i: (i, 0)),
                  pl.BlockSpec((d, d), lambda i: (0, 0)),
                  pl.BlockSpec((1, d), lambda i: (0, 0)),
                  pl.BlockSpec((d, LANES), lambda i: (0, 0)),
                  pl.BlockSpec((1, LANES), lambda i: (0, 0))],
        out_specs=[pl.BlockSpec((tm, d), lambda i: (i, 0)),
                   pl.BlockSpec((tm, d), lambda i: (i, 0)),
                   pl.BlockSpec((tm, LANES), lambda i: (i, 0))],
        out_shape=[jax.ShapeDtypeStruct((rows, d), F32),
                   jax.ShapeDtypeStruct((rows, d), F32),
                   jax.ShapeDtypeStruct((rows, LANES), F32)],
        compiler_params=_cparams(("parallel",), 40),
        name="xo_router",
    )(x1, o, wxo_bf, g.reshape(1, d), wr_bf, br)


def _lane_cumsum(x):
    lane = lax.broadcasted_iota(I32, x.shape, 1)
    s = 1
    while s < LANES:
        x = x + jnp.where(lane >= s, pltpu.roll(x, s, 1), 0.0)
        s *= 2
    return x


def _route_dest_kernel(rp_ref, rs_ref, dest_ref, tab_ref, last_ref, r_buf, cnt_ref, carry_ref, pstart_ref, *,
                       n_prompt_chunks, t_pad, t_valid, tm, rows_cap, nblk_lanes):
    ph = pl.program_id(0)
    c = pl.program_id(1)
    chunk = r_buf.shape[0]
    is_prompt = c < n_prompt_chunks

    @pl.when(is_prompt)
    def _():
        r_buf[...] = rp_ref[...]

    @pl.when(jnp.logical_not(is_prompt))
    def _():
        r_buf[...] = rs_ref[...]

    @pl.when((ph == 0) & (c == 0))
    def _():
        cnt_ref[...] = jnp.zeros(cnt_ref.shape, F32)

    route = r_buf[...]
    lane = lax.broadcasted_iota(I32, (chunk, LANES), 1)
    row = lax.broadcasted_iota(I32, (chunk, LANES), 0)
    lanef = lane.astype(F32)
    valid = (((row & (t_pad - 1)) < t_valid).astype(I32) | is_prompt.astype(I32)) > 0
    oh0 = jnp.where((lanef == route[:, 0:1]) & valid, 1.0, 0.0)
    oh1 = jnp.where((lanef == route[:, 1:2]) & valid, 1.0, 0.0)
    cmat = oh0 + oh1
    csum = jnp.sum(cmat, axis=0, keepdims=True)

    @pl.when(ph == 0)
    def _():
        cnt_ref[...] = cnt_ref[...] + csum

    @pl.when((ph == 1) & (c == 0))
    def _():
        cnt = jnp.broadcast_to(cnt_ref[...], (SUBLANES, LANES))
        padc = jnp.floor((cnt + (tm - 1)) * (1.0 / tm)) * tm
        pend = _lane_cumsum(padc)
        pstart_ref[...] = (pend - padc)[0:1, :]
        carry_ref[...] = jnp.zeros(carry_ref.shape, F32)
        starts = lax.broadcasted_iota(I32, (SUBLANES, nblk_lanes), 1).astype(F32) * tm
        blk = jnp.zeros((SUBLANES, nblk_lanes), F32)
        for e in range(N_EXPERTS):
            blk = blk + jnp.where(pend[:, e:e + 1] <= starts, 1.0, 0.0)
        blk = jnp.minimum(blk, N_EXPERTS - 1.0)
        nused = jnp.broadcast_to(pend[:, N_EXPERTS - 1:N_EXPERTS] * (1.0 / tm), (SUBLANES, nblk_lanes))
        sub = lax.broadcasted_iota(I32, (SUBLANES, nblk_lanes), 0)
        tab_ref[...] = jnp.where(sub == 0, blk, nused)
        last_blk = jnp.where(cnt > 0.0, pend * (1.0 / tm) - 1.0, -1.0)
        sub_e = lax.broadcasted_iota(I32, (SUBLANES, LANES), 0)
        last_ref[...] = jnp.where(sub_e == 0, last_blk, jnp.where(sub_e == 1, pend - padc + cnt, pend))

    @pl.when(ph == 1)
    def _():
        rt = lax.broadcasted_iota(I32, (chunk, chunk), 0)
        cs = lax.broadcasted_iota(I32, (chunk, chunk), 1)
        tri = jnp.where(rt > cs, 1.0, 0.0).astype(BF16)
        before = jnp.dot(tri, cmat.astype(BF16), preferred_element_type=F32) + carry_ref[...]
        base = before + pstart_ref[...]
        d0 = jnp.sum(base * oh0, axis=1, keepdims=True)
        d1 = jnp.sum(base * oh1, axis=1, keepdims=True)
        srow = (c - n_prompt_chunks) * chunk + row
        tshift = t_pad.bit_length() - 1
        padded_idx = (srow >> tshift) * (t_pad - t_valid) + (srow & (t_pad - 1)) - t_valid
        spare = (rows_cap + EXP_TOPK * padded_idx).astype(F32)
        out = jnp.where(lane == 0, d0, d1)
        out = jnp.where(valid, out, spare + lanef)
        dest_ref[...] = jnp.where(lane < EXP_TOPK, out, 0.0)
        carry_ref[...] = carry_ref[...] + csum


def _route_dest(route_p, route_s, t_pad, t_valid, tm, nblk, chunk=ROW_TILE):
    rp, rs = route_p.shape[0], route_s.shape[0]
    assert rp % chunk == 0 and rs % chunk == 0 and chunk % t_pad == 0
    npc, nsc = rp // chunk, rs // chunk
    rows_cap = nblk * tm
    nblk_lanes = -(-nblk // LANES) * LANES
    dest_f, tab, last = pl.pallas_call(
        functools.partial(_route_dest_kernel, n_prompt_chunks=npc, t_pad=t_pad, t_valid=t_valid, tm=tm,
                          rows_cap=rows_cap, nblk_lanes=nblk_lanes),
        grid=(2, npc + nsc),
        in_specs=[pl.BlockSpec((chunk, LANES), lambda ph, c: (jnp.minimum(c, npc - 1), 0)),
                  pl.BlockSpec((chunk, LANES), lambda ph, c: (jnp.maximum(c - npc, 0), 0))],
        out_specs=[pl.BlockSpec((chunk, LANES), lambda ph, c: (c * ph, 0)),
                   pl.BlockSpec((SUBLANES, nblk_lanes), lambda ph, c: (0, 0)),
                   pl.BlockSpec((SUBLANES, LANES), lambda ph, c: (0, 0))],
        out_shape=[jax.ShapeDtypeStruct((rp + rs, LANES), F32),
                   jax.ShapeDtypeStruct((SUBLANES, nblk_lanes), F32),
                   jax.ShapeDtypeStruct((SUBLANES, LANES), F32)],
        scratch_shapes=[pltpu.VMEM((chunk, LANES), F32), pltpu.VMEM((1, LANES), F32),
                        pltpu.VMEM((1, LANES), F32), pltpu.VMEM((1, LANES), F32)],
        compiler_params=_cparams(("arbitrary", "arbitrary"), 32),
        name="route_dest",
    )(route_p, route_s)
    dest = dest_f[:, :EXP_TOPK].astype(I32).reshape(-1)
    blk_e = tab[0, :nblk].astype(I32)
    nused = tab[1, :1].astype(I32)
    last_blk = last[0, :N_EXPERTS].astype(I32)
    pad_rows = last[1:3, :N_EXPERTS].astype(I32).reshape(-1)
    return dest, blk_e, nused, last_blk, pad_rows


def _scatter_rows_kernel(dest_ref, last_ref, nused_ref, hp_ref, hs_ref, xs_ref, sbuf, sem, zsem, *,
                         tm, n_prompt_tiles, n_tiles, nblk):
    i = pl.program_id(0)
    slot = i % 2

    def wait_tile(s):
        for _ in range(EXP_TOPK):
            pltpu.make_async_copy(sbuf.at[s], xs_ref.at[pl.ds(0, tm)], sem.at[s]).wait()

    def clear_block(b):
        return pltpu.make_async_copy(sbuf.at[1], xs_ref.at[pl.ds(b * tm, tm)], zsem)

    @pl.when(i == 0)
    def _():
        sbuf[1] = jnp.zeros(sbuf.shape[1:], sbuf.dtype)
        nused = nused_ref[0]
        for e in range(N_EXPERTS):
            @pl.when(last_ref[e] >= 0)
            def _():
                clear_block(last_ref[e]).start()

        def start_unused(b, carry):
            clear_block(b).start()
            return carry

        def wait_one(b, carry):
            clear_block(0).wait()
            return carry

        lax.fori_loop(nused, nblk, start_unused, 0)
        for e in range(N_EXPERTS):
            @pl.when(last_ref[e] >= 0)
            def _():
                clear_block(0).wait()
        lax.fori_loop(nused, nblk, wait_one, 0)

    @pl.when(i >= 2)
    def _():
        wait_tile(slot)

    @pl.when(i < n_prompt_tiles)
    def _():
        sbuf[slot] = hp_ref[...]

    @pl.when(i >= n_prompt_tiles)
    def _():
        sbuf[slot] = hs_ref[...]

    base = EXP_TOPK * i * tm
    for r in range(tm):
        for k in range(EXP_TOPK):
            pltpu.make_async_copy(sbuf.at[slot, pl.ds(r, 1)],
                                  xs_ref.at[pl.ds(dest_ref[base + EXP_TOPK * r + k], 1)],
                                  sem.at[slot]).start(priority=k % 2)

    @pl.when(i == n_tiles - 1)
    def _():
        wait_tile(slot)
        wait_tile(1 - slot)


def _scatter_rows(dest, last_blk, nused, h_p, h_s, nblk, n_spare, tm=ROW_TILE):
    d = h_p.shape[1]
    npt, nst = h_p.shape[0] // tm, h_s.shape[0] // tm
    assert npt + nst >= 2 and MOE_TILE == tm
    grid_spec = pltpu.PrefetchScalarGridSpec(
        num_scalar_prefetch=3,
        grid=(npt + nst,),
        in_specs=[pl.BlockSpec((tm, d), lambda i, de, lb, nu: (jnp.minimum(i, npt - 1), 0)),
                  pl.BlockSpec((tm, d), lambda i, de, lb, nu: (jnp.maximum(i - npt, 0), 0))],
        out_specs=pl.BlockSpec(memory_space=pl.ANY),
        scratch_shapes=[pltpu.VMEM((2, tm, d), F32), pltpu.SemaphoreType.DMA((2,)), pltpu.SemaphoreType.DMA(())],
    )
    return pl.pallas_call(
        functools.partial(_scatter_rows_kernel, tm=tm, n_prompt_tiles=npt, n_tiles=npt + nst, nblk=nblk),
        grid_spec=grid_spec,
        out_shape=jax.ShapeDtypeStruct((nblk * tm + n_spare, d), F32),
        compiler_params=_cparams(("arbitrary",), 32),
        name="scatter_rows",
    )(dest, last_blk, nused, h_p, h_s)


def _moe_kernel(blk_e_ref, nused_ref, x_ref, w1_hbm, w3_hbm, w2_hbm, o_ref, wf_ref, wb_ref, sem):
    i = pl.program_id(0)
    nused = nused_ref[0]
    e = blk_e_ref[i]
    w_hbm = (w1_hbm, w3_hbm, w2_hbm)

    def start_fetch(expert):
        for k in range(3):
            pltpu.make_async_copy(w_hbm[k].at[expert], wf_ref.at[k], sem.at[k]).start()

    @pl.when(i == 0)
    def _():
        start_fetch(e)

    @pl.when((i < nused) & ((i == 0) | (e != blk_e_ref[jnp.maximum(i - 1, 0)])))
    def _():
        for k in range(3):
            pltpu.make_async_copy(w_hbm[k].at[e], wf_ref.at[k], sem.at[k]).wait()
            wb_ref[k] = wf_ref[k].astype(BF16)
        nxt = lax.while_loop(lambda j: (j < nused) & (blk_e_ref[jnp.minimum(j, nused - 1)] == e),
                             lambda j: j + 1, i + 1)

        @pl.when(nxt < nused)
        def _():
            start_fetch(blk_e_ref[nxt])

    @pl.when(i < nused)
    def _():
        x = x_ref[...].astype(BF16)
        a = jnp.dot(x, wb_ref[0], preferred_element_type=F32)
        b = jnp.dot(x, wb_ref[1], preferred_element_type=F32)
        hmid = (_silu(a) * b).astype(BF16)
        o_ref[...] = jnp.dot(hmid, wb_ref[2], preferred_element_type=F32)

    @pl.when(i >= nused)
    def _():
        o_ref[...] = jnp.zeros(o_ref.shape, o_ref.dtype)


def _moe_experts(xs, blk_e, nused, w1, w3, w2, tm=MOE_TILE):
    nblk = blk_e.shape[0]
    d = xs.shape[1]
    wspec = pl.BlockSpec(memory_space=pl.ANY)
    grid_spec = pltpu.PrefetchScalarGridSpec(
        num_scalar_prefetch=2,
        grid=(nblk,),
        in_specs=[pl.BlockSpec((tm, d), lambda i, be, nu: (jnp.minimum(i, nu[0] - 1), 0)), wspec, wspec, wspec],
        out_specs=pl.BlockSpec((tm, d), lambda i, be, nu: (i, 0)),
        scratch_shapes=[pltpu.VMEM((3, d, d), F32), pltpu.VMEM((3, d, d), BF16), pltpu.SemaphoreType.DMA((3,))],
    )
    return pl.pallas_call(
        _moe_kernel,
        grid_spec=grid_spec,
        out_shape=jax.ShapeDtypeStruct((nblk * tm, d), F32),
        compiler_params=_cparams(("arbitrary",), 40),
        name="moe_experts",
    )(blk_e, nused, xs, w1, w3, w2)


def _combine_kernel(dest_ref, x2_ref, r_ref, g_ref, outs_hbm, y_ref, buf, sem, *, tm):
    i = pl.program_id(0)
    n = pl.num_programs(0)
    n_rows = outs_hbm.shape[0]

    def start_tile(tile, slot):
        base = EXP_TOPK * tile * tm
        for r in range(tm):
            for k in range(EXP_TOPK):
                src = dest_ref[base + EXP_TOPK * r + k]
                src = jnp.where(src >= n_rows, src - n_rows, src)
                pltpu.make_async_copy(outs_hbm.at[pl.ds(src, 1)], buf.at[slot, k, pl.ds(r, 1)],
                                      sem.at[slot]).start(priority=k % 2)

    @pl.when(i == 0)
    def _():
        start_tile(0, 0)

    @pl.when(i + 1 < n)
    def _():
        start_tile(i + 1, (i + 1) % 2)

    slot = i % 2
    for k in range(EXP_TOPK):
        pltpu.make_async_copy(outs_hbm.at[pl.ds(0, tm)], buf.at[slot, k], sem.at[slot]).wait()
    route = r_ref[...]
    y = buf[slot, 0] * route[:, 2:3] + buf[slot, 1] * route[:, 3:4]
    y_ref[...] = _rms(x2_ref[...] + y, g_ref[...])


def _combine(x2, route, g_final, outs, dest, tm=ROW_TILE):
    rows, d = x2.shape
    grid_spec = pltpu.PrefetchScalarGridSpec(
        num_scalar_prefetch=1,
        grid=(rows // tm,),
        in_specs=[pl.BlockSpec((tm, d), lambda i, de: (i, 0)),
                  pl.BlockSpec((tm, LANES), lambda i, de: (i, 0)),
                  pl.BlockSpec((1, d), lambda i, de: (0, 0)),
                  pl.BlockSpec(memory_space=pl.ANY)],
        out_specs=pl.BlockSpec((tm, d), lambda i, de: (i, 0)),
        scratch_shapes=[pltpu.VMEM((2, EXP_TOPK, tm, d), F32), pltpu.SemaphoreType.DMA((2,))],
    )
    return pl.pallas_call(
        functools.partial(_combine_kernel, tm=tm),
        grid_spec=grid_spec,
        out_shape=jax.ShapeDtypeStruct((rows, d), F32),
        compiler_params=_cparams(("arbitrary",), 32),
        name="moe_combine",
    )(dest, x2, route, g_final.reshape(1, d), outs)


def _layer_tail(x, o, att, att_transposed, mem_k, mem_v, batch, t, xq_tile, wts):
    q_dtype = BF16 if xq_tile % (2 * SUBLANES) == 0 else F32
    x1, q = _outproj(x, o, att, wts["w_out"], wts["g_cross"], wts["w_xq"], att_transposed, q_dtype)
    xo = _xattn(q, mem_k, mem_v, batch, t, xq_tile)
    x2, h, route = _xo_router(x1, xo, wts["w_xo"], wts["g_ffn"], wts["w_router"], wts["b_router"])
    return x2, h, route


def _invert_rows_kernel(dest_ref, pad_ref, nused_ref, inv_ref, *, n_assign, rows_cap, tm, trash_base, n_trash):
    def put(a, carry):
        inv_ref[dest_ref[a]] = a
        return carry

    lax.fori_loop(0, n_assign, put, 0, unroll=8)

    def fill(p, carry):
        inv_ref[p] = trash_base + (p & (n_trash - 1))
        return carry

    for e in range(N_EXPERTS):
        lax.fori_loop(pad_ref[e], pad_ref[N_EXPERTS + e], fill, 0)
    lax.fori_loop(nused_ref[0] * tm, rows_cap, fill, 0)


def _invert_rows(dest, pad_rows, nused, rows_cap, n_spare, trash_base, n_trash, tm=MOE_TILE):
    n_assign = dest.shape[0]
    smem = pl.BlockSpec(memory_space=pltpu.SMEM)
    return pl.pallas_call(
        functools.partial(_invert_rows_kernel, n_assign=n_assign, rows_cap=rows_cap, tm=tm,
                          trash_base=trash_base, n_trash=n_trash),
        in_specs=[smem, smem, smem],
        out_specs=smem,
        out_shape=jax.ShapeDtypeStruct((rows_cap + n_spare,), I32),
        name="invert_rows",
    )(dest, pad_rows, nused)


def _moe_direct_kernel(blk_e_ref, nused_ref, inv_ref, h_hbm, w1_hbm, w3_hbm, w2_hbm, y2_hbm,
                       xbuf, obuf, wf_ref, wb_ref, gsem, ssem, wsem, zsem, *, tm, n_tokens, zero_runs):
    i = pl.program_id(0)
    nused = nused_ref[0]
    slot = i % 2
    e = blk_e_ref[i]
    w_hbm = (w1_hbm, w3_hbm, w2_hbm)
    trash_base = EXP_TOPK * n_tokens
    thirds = [(0, tm // 3), (tm // 3, 2 * (tm // 3)), (2 * (tm // 3), tm)]

    def start_fetch(expert):
        for k in range(3):
            pltpu.make_async_copy(w_hbm[k].at[expert], wf_ref.at[k], wsem.at[k]).start()

    def gather_rows(blk, s, lo, hi):
        for r in range(lo, hi):
            a = inv_ref[blk * tm + r]
            tok = jnp.where(a >= n_tokens, a - n_tokens, a)
            tok = jnp.where(a >= trash_base, r, tok)
            pltpu.make_async_copy(h_hbm.at[pl.ds(tok, 1)], xbuf.at[s, pl.ds(r, 1)], gsem.at[s]).start(priority=0)

    def scatter_rows(blk, s, lo, hi):
        for r in range(lo, hi):
            pltpu.make_async_copy(obuf.at[s, pl.ds(r, 1)], y2_hbm.at[pl.ds(inv_ref[blk * tm + r], 1)],
                                  ssem.at[s]).start(priority=1)

    def wait_rows(buf, sem, s):
        if buf is xbuf:
            pltpu.make_async_copy(h_hbm.at[pl.ds(0, tm)], buf.at[s], sem.at[s]).wait()
        else:
            pltpu.make_async_copy(buf.at[s], y2_hbm.at[pl.ds(0, tm)], sem.at[s]).wait()

    @pl.when(i == 0)
    def _():
        start_fetch(e)
        gather_rows(0, 0, 0, tm)
        obuf[1] = jnp.zeros(obuf.shape[1:], obuf.dtype)
        for start, size in zero_runs:
            pltpu.make_async_copy(obuf.at[1, pl.ds(0, size)], y2_hbm.at[pl.ds(start, size)], zsem).start()
        for start, size in zero_runs:
            pltpu.make_async_copy(obuf.at[1, pl.ds(0, size)], y2_hbm.at[pl.ds(start, size)], zsem).wait()

    @pl.when((i < nused) & ((i == 0) | (e != blk_e_ref[jnp.maximum(i - 1, 0)])))
    def _():
        for k in range(3):
            pltpu.make_async_copy(w_hbm[k].at[e], wf_ref.at[k], wsem.at[k]).wait()
            wb_ref[k] = wf_ref[k].astype(BF16)
        nxt = lax.while_loop(lambda j: (j < nused) & (blk_e_ref[jnp.minimum(j, nused - 1)] == e),
                             lambda j: j + 1, i + 1)

        @pl.when(nxt < nused)
        def _():
            start_fetch(blk_e_ref[nxt])

    @pl.when((i >= 1) & (i < nused))
    def _():
        wait_rows(obuf, ssem, slot)

    @pl.when(i < nused)
    def _():
        wait_rows(xbuf, gsem, slot)
        nb = jnp.minimum(i + 1, nused - 1)
        pb = jnp.maximum(i - 1, 0)
        x = xbuf[slot].astype(BF16)
        gather_rows(nb, 1 - slot, *thirds[0])
        scatter_rows(pb, 1 - slot, *thirds[0])
        a = jnp.dot(x, wb_ref[0], preferred_element_type=F32)
        gather_rows(nb, 1 - slot, *thirds[1])
        scatter_rows(pb, 1 - slot, *thirds[1])
        b = jnp.dot(x, wb_ref[1], preferred_element_type=F32)
        gather_rows(nb, 1 - slot, *thirds[2])
        scatter_rows(pb, 1 - slot, *thirds[2])
        hmid = (_silu(a) * b).astype(BF16)
        obuf[slot] = jnp.dot(hmid, wb_ref[2], preferred_element_type=F32)

    @pl.when(i == nused - 1)
    def _():
        wait_rows(xbuf, gsem, 1 - slot)
        wait_rows(obuf, ssem, 1 - slot)
        scatter_rows(i, slot, 0, tm)
        wait_rows(obuf, ssem, slot)


def _moe_direct(h_all, blk_e, nused, inv, w1, w3, w2, n_tokens, zero_runs, n_trash, tm=MOE_TILE):
    nblk = blk_e.shape[0]
    d = h_all.shape[1]
    any_spec = pl.BlockSpec(memory_space=pl.ANY)
    grid_spec = pltpu.PrefetchScalarGridSpec(
        num_scalar_prefetch=3,
        grid=(nblk,),
        in_specs=[any_spec, any_spec, any_spec, any_spec],
        out_specs=any_spec,
        scratch_shapes=[pltpu.VMEM((2, tm, d), F32), pltpu.VMEM((2, tm, d), F32),
                        pltpu.VMEM((3, d, d), F32), pltpu.VMEM((3, d, d), BF16),
                        pltpu.SemaphoreType.DMA((2,)), pltpu.SemaphoreType.DMA((2,)),
                        pltpu.SemaphoreType.DMA((3,)), pltpu.SemaphoreType.DMA(())],
    )
    return pl.pallas_call(
        functools.partial(_moe_direct_kernel, tm=tm, n_tokens=n_tokens, zero_runs=zero_runs),
        grid_spec=grid_spec,
        out_shape=jax.ShapeDtypeStruct((EXP_TOPK * n_tokens + n_trash, d), F32),
        compiler_params=_cparams(("arbitrary",), 40),
        name="moe_direct",
    )(blk_e, nused, inv, h_all, w1, w3, w2)


def _combine_stream_kernel(x2_ref, r_ref, g_ref, ya_ref, yb_ref, y_ref):
    route = r_ref[...]
    y = ya_ref[...] * route[:, 2:3] + yb_ref[...] * route[:, 3:4]
    y_ref[...] = _rms(x2_ref[...] + y, g_ref[...])


def _combine_stream(x2, route, g_final, y2, first_tile, n_tokens, tm=ROW_TILE):
    rows, d = x2.shape
    slot_tiles = n_tokens // tm
    return pl.pallas_call(
        _combine_stream_kernel,
        grid=(rows // tm,),
        in_specs=[pl.BlockSpec((tm, d), lambda i: (i, 0)),
                  pl.BlockSpec((tm, LANES), lambda i: (i, 0)),
                  pl.BlockSpec((1, d), lambda i: (0, 0)),
                  pl.BlockSpec((tm, d), lambda i: (i + first_tile, 0)),
                  pl.BlockSpec((tm, d), lambda i: (i + first_tile + slot_tiles, 0))],
        out_specs=pl.BlockSpec((tm, d), lambda i: (i, 0)),
        out_shape=jax.ShapeDtypeStruct((rows, d), F32),
        compiler_params=_cparams(("parallel",), 40),
        name="combine_stream",
    )(x2, route, g_final.reshape(1, d), y2, y2)


def _moe_and_final(x2_p, h_p, route_p, x2_s, h_s, route_s, n_sample_tokens, t_pad, t_valid, wts):
    rp, d = h_p.shape
    rs = h_s.shape[0]
    tm = MOE_TILE
    n_assign = (rp + n_sample_tokens) * EXP_TOPK
    nblk = -(-(n_assign + N_EXPERTS * (tm - 1)) // tm)
    dest, blk_e, nused, last_blk, _ = _route_dest(route_p, route_s, t_pad, t_valid, tm, nblk)
    n_spare = EXP_TOPK * (rs // t_pad) * (t_pad - t_valid)
    xs = _scatter_rows(dest, last_blk, nused, h_p, h_s, nblk, n_spare)
    outs = _moe_experts(xs, blk_e, nused, wts["w1"], wts["w3"], wts["w2"])
    y_p = _combine(x2_p, route_p, wts["g_final"], outs, dest[:EXP_TOPK * rp])
    y_s = _combine(x2_s, route_s, wts["g_final"], outs, dest[EXP_TOPK * rp:])
    return y_p, y_s


def kernel(x_prompt, x_sample, state_hgrn, cache_moba_k, cache_moba_v, cache_mem_k, cache_mem_v, page_table,
           mem_prompt, g_mix, w_in, hg_lb, hg_norm, w_out, g_cross, g_mem, w_xq, w_xk, w_xv, w_xo, g_ffn,
           w_grp, b_grp, w_exp, b_exp, w1, w3, w2, g_final):
    depth = g_mix.shape[0]
    assert depth == 1
    bp, tp, d = x_prompt.shape
    bs, ts, _ = x_sample.shape
    mem_len = mem_prompt.shape[1]
    l = 0

    lb = jnp.cumsum(jax.nn.softmax(hg_lb.astype(F32), axis=0), axis=0)[l]
    slopes = jnp.asarray(np.power(2.0, -8.0 * np.arange(1, MB_HEADS + 1) / MB_HEADS).astype(np.float32))
    n_router = N_GROUPS + N_EXPERTS
    w_router = jnp.pad(jnp.concatenate([w_grp[l], w_exp[l]], axis=1), ((0, 0), (0, LANES - n_router)))
    b_router = jnp.pad(jnp.concatenate([b_grp[l], b_exp[l]]), (0, LANES - n_router)).reshape(1, LANES)
    wts = {
        "w_out": w_out[l].astype(BF16), "g_cross": g_cross[l], "w_xq": w_xq[l].astype(BF16),
        "w_xo": w_xo[l].astype(BF16), "g_ffn": g_ffn[l], "w_router": w_router.astype(BF16),
        "b_router": b_router.astype(F32), "w1": w1[l], "w3": w3[l], "w2": w2[l], "g_final": g_final,
    }
    w_in_bf = w_in[l].astype(BF16)
    o_hq, o_mq, o_mk, o_mv = 0, 4 * HG_WIDTH, 4 * HG_WIDTH + MB_WIDTH, 4 * HG_WIDTH + 2 * MB_WIDTH
    o_end = o_mv + MB_WIDTH

    xp = x_prompt.reshape(bp * tp, d)
    w_kv = jnp.concatenate([w_xk[l], w_xv[l]], axis=1).astype(BF16)
    memk_p, memv_p = _rms_proj(mem_prompt.reshape(bp * mem_len, d), g_mem[l], w_kv,
                               [(0, d, [(1.0, False)]), (d, 2 * d, [(1.0, False)])], [F32, F32])
    o_hf, o_hi = o_hq + HG_WIDTH, o_hq + 2 * HG_WIDTH
    hq_p, hf_p, hig_p, qt_p, kt_p, vt_p = _rms_proj(
        xp, g_mix[l], w_in_bf,
        [(o_hq, o_hf, [(1.0, False)]), (o_hf, o_hi, [(1.0, False)]), (o_hi, o_mq, [(1.0, False)]),
         (o_mq, o_mk, [(MB_DIM ** -0.5, True)]), (o_mk, o_mv, [(1.0, "batched")]),
         (o_mv, o_end, [(1.0, "batched")])],
        [BF16, F32, BF16, BF16, F32, F32], seq_len=tp)
    o_p, s_p = _hgrn(hq_p, hf_p, hig_p, lb, hg_norm[l], None, bp, tp, tp, 128)
    att_p = _moba_prompt(qt_p, kt_p, vt_p, slopes, bp, tp)
    x2_p, h_p, route_p = _tail_fused(xp, o_p, att_p, memk_p, memv_p, bp, tp, wts, bp * tp)

    tpad = SAMPLE_PAD_T
    xs = jnp.pad(x_sample, ((0, 0), (0, tpad - ts), (0, 0))).reshape(bs * tpad, d)
    hq_s, hf_s, hig_s, mq_s, mk_s, mv_s = _rms_proj(
        xs, g_mix[l], w_in_bf,
        [(o_hq, o_hf, [(1.0, False)]), (o_hf, o_hi, [(1.0, False)]), (o_hi, o_mq, [(1.0, False)]),
         (o_mq, o_mk, [(1.0, False)]), (o_mk, o_mv, [(1.0, False)]), (o_mv, o_end, [(1.0, False)])],
        [F32, F32, F32, F32, F32, F32])
    o_s, s_s = _hgrn(hq_s, hf_s, hig_s, lb, hg_norm[l], state_hgrn[l], bs, tpad, ts, tpad, nseq=HGRN_SEQS_PER_STEP)
    mk4 = mk_s.reshape(bs, tpad, MB_HEADS, MB_DIM)
    mv4 = mv_s.reshape(bs, tpad, MB_HEADS, MB_DIM)
    n_phys, page = cache_moba_k.shape[1], cache_moba_k.shape[2]
    pool_kt = jnp.transpose(cache_moba_k[l], (0, 2, 3, 1)).reshape(n_phys, MB_WIDTH, page)
    pool_vt = jnp.transpose(cache_moba_v[l], (0, 2, 3, 1)).reshape(n_phys, MB_WIDTH, page)
    att_s = _moba_sample(mq_s, mk_s, mv_s, pool_kt, pool_vt, page_table, ts)

    def mem_rows(c):
        c = c[l].reshape(bs, mem_len, XA_HEADS, XA_DIM // LANES, LANES)
        return jnp.transpose(c, (0, 1, 3, 2, 4)).reshape(bs, mem_len * d // LANES, LANES)

    memk_s, memv_s = mem_rows(cache_mem_k), mem_rows(cache_mem_v)
    x2_s, h_s, route_s = _layer_tail(xs, o_s, att_s, False, memk_s, memv_s, bs, tpad, tpad, wts)
    y_p, y_s = _moe_and_final(x2_p, h_p, route_p, x2_s, h_s, route_s, bs * ts, tpad, ts, wts)

    return (y_p.reshape(bp, tp, d),
            y_s.reshape(bs, tpad, d)[:, :ts],
            s_p.reshape(1, bp, HG_HEADS, HG_DIM, HG_DIM),
            jnp.transpose(kt_p.reshape(1, bp, MB_HEADS, MB_DIM, tp), (0, 1, 4, 2, 3)),
            jnp.transpose(vt_p.reshape(1, bp, MB_HEADS, MB_DIM, tp), (0, 1, 4, 2, 3)),
            memk_p.reshape(1, bp, mem_len, XA_HEADS, XA_DIM),
            memv_p.reshape(1, bp, mem_len, XA_HEADS, XA_DIM),
            s_s.reshape(1, bs, HG_HEADS, HG_DIM, HG_DIM),
            mk4[:, :ts].reshape(1, bs, ts, MB_HEADS, MB_DIM),
            mv4[:, :ts].reshape(1, bs, ts, MB_HEADS, MB_DIM))
```
